```python
import jax, jax.numpy as jnp
from jax import lax
import numpy as np

D_MODEL = 1024
BATCH = 32
SEQ = 256
DEPTH = 1
DEC_BATCH = 4
DEC_SEQ = 1024
PAST_LEN = 256

GRID_W = 64
D_CONV = D_MODEL // 2
CONV_GROUPS = 8
CONV_WIDTH = 3
GLA_HEADS = 4
GLA_DV = (D_MODEL - D_CONV) // GLA_HEADS
GLA_DK = GLA_DV // 2
GLA_DK_TOT = GLA_HEADS * GLA_DK
GLA_DV_TOT = GLA_HEADS * GLA_DV
GLA_LOW_RANK = 16
GLA_TAU = 16.0
GLA_CHUNK = 64
N_EXPERTS = 16
EC_CAPACITY_FACTOR = 2
D_EXPERT = 1024
N_MOD = 6
EPS = 1e-6
SPLIT_SIZES = (D_CONV, D_CONV, D_CONV, GLA_DK_TOT, GLA_DK_TOT, GLA_DV_TOT, GLA_DV_TOT, 2 * GLA_LOW_RANK)
P_TOT = 3 * D_CONV + 2 * GLA_DK_TOT + 2 * GLA_DV_TOT + 2 * GLA_LOW_RANK

kernel_name = "hybrid_conv_gla_ec_moe_diffusion_step"


def _rmsnorm(x, g):
    xf = x.astype(jnp.float32)
    r = lax.rsqrt(jnp.mean(xf * xf, axis=-1, keepdims=True) + EPS)
    return (xf * r).astype(x.dtype) * g


def _centred_conv3(u, w, bias):
    pad = [(0, 0)] * (u.ndim - 2) + [(1, 1), (0, 0)]
    up = jnp.pad(u, pad)
    L = u.shape[-2]
    return up[..., :L, :] * w[0] + up[..., 1:L + 1, :] * w[1] + up[..., 2:, :] * w[2] + bias


def _gla_chunked(q, k, v, log_a, s0):
    Bn, H, T, DK = q.shape
    DV = v.shape[-1]
    C = GLA_CHUNK
    N = T // C
    f32 = jnp.float32
    q = q.astype(f32).reshape(Bn, H, N, C, DK) * (DK ** -0.5)
    k = k.astype(f32).reshape(Bn, H, N, C, DK)
    v = v.astype(f32).reshape(Bn, H, N, C, DV)
    b = jnp.cumsum(log_a.astype(f32).reshape(Bn, H, N, C, DK), axis=3)
    b_last = b[:, :, :, -1:, :]
    q_d = q * jnp.exp(b)
    k_d = k * jnp.exp(-b)
    k_s = k * jnp.exp(b_last - b)
    mask = jnp.tril(jnp.ones((C, C), dtype=bool))
    att = jnp.where(mask, jnp.einsum('bhncd,bhnsd->bhncs', q_d, k_d), 0.0)
    o_intra = jnp.einsum('bhncs,bhnsv->bhncv', att, v)
    kv = jnp.einsum('bhncd,bhncv->bhndv', k_s, v)
    decay = jnp.exp(b_last[:, :, :, 0, :])

    def step(s, inp):
        q_n, dec_n, kv_n = inp
        o = jnp.einsum('bhcd,bhdv->bhcv', q_n, s)
        s = dec_n[..., None] * s + kv_n
        return s, o

    s_fin, o_inter = lax.scan(step, s0.astype(f32),
                              (jnp.moveaxis(q_d, 2, 0), jnp.moveaxis(decay, 2, 0), jnp.moveaxis(kv, 2, 0)))
    o = o_intra + jnp.moveaxis(o_inter, 0, 2)
    return o.reshape(Bn, H, T, DV), s_fin


def _gla_bidir(q, k, v, la_f, la_b, s_f, s_b):
    o_f, sf = _gla_chunked(q, k, v, la_f, s_f)
    fl = lambda t: jnp.flip(t, axis=2)
    o_b, sb = _gla_chunked(fl(q), fl(k), fl(v), fl(la_b), s_b)
    return o_f + fl(o_b), sf, sb


def _mixer(h, s_f, s_b, grid, w_in, w_conv, b_conv, w_a_up_f, b_a_f, w_a_up_b, b_a_b, g_gla_norm, w_out):
    Bn, T, _ = h.shape
    p = h @ w_in
    offs = []
    acc = 0
    for s in SPLIT_SIZES[:-1]:
        acc += s
        offs.append(acc)
    xb, xc, xv, q, k, v, og, a_low = jnp.split(p, offs, axis=-1)
    u = xc * xv
    if grid:
        rows = T // GRID_W
        y_a = _centred_conv3(u.reshape(Bn, rows, GRID_W, D_CONV), w_conv, b_conv).reshape(Bn, T, D_CONV)
    else:
        y_a = _centred_conv3(u, w_conv, b_conv)
    y_a = xb * y_a
    heads = lambda t, d: jnp.transpose(t.reshape(Bn, T, GLA_HEADS, d), (0, 2, 1, 3))
    la_f = jax.nn.log_sigmoid((a_low[..., :GLA_LOW_RANK] @ w_a_up_f + b_a_f).astype(jnp.float32)) / GLA_TAU
    la_b = jax.nn.log_sigmoid((a_low[..., GLA_LOW_RANK:] @ w_a_up_b + b_a_b).astype(jnp.float32)) / GLA_TAU
    o, sf, sb = _gla_bidir(heads(q, GLA_DK), heads(k, GLA_DK), heads(v, GLA_DV),
                           heads(la_f, GLA_DK), heads(la_b, GLA_DK), s_f, s_b)
    o = o * lax.rsqrt(jnp.mean(o * o, axis=-1, keepdims=True) + EPS) * g_gla_norm[None, :, None, :].astype(jnp.float32)
    o = jnp.transpose(o, (0, 2, 1, 3)).reshape(Bn, T, GLA_DV_TOT).astype(h.dtype)
    y_b = o * jax.nn.silu(og)
    return jnp.concatenate([y_a, y_b], axis=-1) @ w_out, sf, sb


def _ec_moe(h, w_router, w_gate, w_up, w_down):
    Bn, T, _ = h.shape
    cap = EC_CAPACITY_FACTOR * T // N_EXPERTS
    probs = jax.nn.softmax((h @ w_router).astype(jnp.float32), axis=-1)
    gates, idx = lax.top_k(jnp.swapaxes(probs, 1, 2), cap)
    bidx = jnp.arange(Bn)[:, None, None]
    xs = h[bidx, idx]
    a = jax.nn.silu(jnp.einsum('becd,edf->becf', xs, w_gate)) * jnp.einsum('becd,edf->becf', xs, w_up)
    y = jnp.einsum('becf,efd->becd', a, w_down) * gates[..., None].astype(h.dtype)
    return jnp.zeros_like(h).at[bidx, idx].add(y)


def _layer(x, mod, s_f, s_b, grid, g_norm1, g_norm2, w_in, w_conv, b_conv, w_a_up_f, b_a_f,
           w_a_up_b, b_a_b, g_gla_norm, w_out, w_router, w_gate, w_up, w_down):
    shift1, scale1, gate1, shift2, scale2, gate2 = jnp.split(mod, N_MOD, axis=-1)
    h = _rmsnorm(x, g_norm1) * (1.0 + scale1) + shift1
    m, sf, sb = _mixer(h, s_f, s_b, grid, w_in, w_conv, b_conv, w_a_up_f, b_a_f, w_a_up_b, b_a_b, g_gla_norm, w_out)
    x = x + gate1 * m
    h = _rmsnorm(x, g_norm2) * (1.0 + scale2) + shift2
    x = x + gate2 * _ec_moe(h, w_router, w_gate, w_up, w_down)
    return x, sf, sb


def setup_inputs(seed: int = 0) -> dict:
    key = jax.random.key(seed)
    ks = jax.random.split(key, 24)
    nrm = lambda k, shape, s: jax.random.normal(k, shape, jnp.float32) * s
    D = D_MODEL
    return {
        "x_prompt": nrm(ks[0], (BATCH, SEQ, D), 1.0),
        "x_sample": nrm(ks[1], (DEC_BATCH, DEC_SEQ, D), 1.0),
        "state_gla_fwd": nrm(ks[2], (DEC_BATCH, DEPTH, GLA_HEADS, GLA_DK, GLA_DV), 0.5),
        "state_gla_bwd": nrm(ks[3], (DEC_BATCH, DEPTH, GLA_HEADS, GLA_DK, GLA_DV), 0.5),
        "c": nrm(ks[4], (DEC_BATCH, D), 1.0),
        "c_ctx": nrm(ks[5], (D,), 1.0),
        "w_mod": nrm(ks[6], (DEPTH, D, N_MOD * D), 0.5 * D ** -0.5),
        "b_mod": nrm(ks[7], (DEPTH, N_MOD * D), 0.02),
        "g_norm1": 1.0 + nrm(ks[8], (DEPTH, D), 0.02),
        "g_norm2": 1.0 + nrm(ks[9], (DEPTH, D), 0.02),
        "w_in": nrm(ks[10], (DEPTH, D, P_TOT), D ** -0.5),
        "w_conv": nrm(ks[11], (DEPTH, CONV_WIDTH, D_CONV), CONV_WIDTH ** -0.5),
        "b_conv": nrm(ks[12], (DEPTH, D_CONV), 0.02),
        "w_a_up_f": nrm(ks[13], (DEPTH, GLA_LOW_RANK, GLA_DK_TOT), GLA_LOW_RANK ** -0.5),
        "b_a_f": 1.0 + nrm(ks[14], (DEPTH, GLA_DK_TOT), 0.5),
        "w_a_up_b": nrm(ks[15], (DEPTH, GLA_LOW_RANK, GLA_DK_TOT), GLA_LOW_RANK ** -0.5),
        "b_a_b": 1.0 + nrm(ks[16], (DEPTH, GLA_DK_TOT), 0.5),
        "g_gla_norm": 1.0 + nrm(ks[17], (DEPTH, GLA_HEADS, GLA_DV), 0.02),
        "w_out": nrm(ks[18], (DEPTH, D, D), D ** -0.5),
        "w_router": nrm(ks[19], (DEPTH, D, N_EXPERTS), D ** -0.5),
        "w_gate": nrm(ks[20], (DEPTH, N_EXPERTS, D, D_EXPERT), D ** -0.5),
        "w_up": nrm(ks[21], (DEPTH, N_EXPERTS, D, D_EXPERT), D ** -0.5),
        "w_down": nrm(ks[22], (DEPTH, N_EXPERTS, D_EXPERT, D), D_EXPERT ** -0.5),
        "g_final": 1.0 + nrm(ks[23], (D,), 0.02),
    }


def reference(x_prompt, x_sample, state_gla_fwd, state_gla_bwd, c, c_ctx, w_mod, b_mod, g_norm1, g_norm2,
              w_in, w_conv, b_conv, w_a_up_f, b_a_f, w_a_up_b, b_a_b, g_gla_norm, w_out, w_router,
              w_gate, w_up, w_down, g_final):
    n_ctx = x_prompt.shape[0]
    zero_state = jnp.zeros((n_ctx, GLA_HEADS, GLA_DK, GLA_DV), jnp.float32)
    xp = x_prompt
    xs = x_sample
    new_f, new_b = [], []
    for l in range(DEPTH):
        lw = (g_norm1[l], g_norm2[l], w_in[l], w_conv[l], b_conv[l], w_a_up_f[l], b_a_f[l],
              w_a_up_b[l], b_a_b[l], g_gla_norm[l], w_out[l], w_router[l], w_gate[l], w_up[l], w_down[l])
        mod_ctx = (jax.nn.silu(c_ctx) @ w_mod[l] + b_mod[l])[None, None, :]
        xp, sf, sb = _layer(xp, mod_ctx, zero_state, zero_state, False, *lw)
        new_f.append(sf)
        new_b.append(sb)
        mod_lat = (jax.nn.silu(c) @ w_mod[l] + b_mod[l])[:, None, :]
        xs, _, _ = _layer(xs, mod_lat, state_gla_fwd[:, l], state_gla_bwd[:, l], True, *lw)
    y_prompt = _rmsnorm(xp, g_final)
    y_sample = _rmsnorm(xs, g_final)
    new_state_gla_fwd = jnp.stack(new_f, axis=1)
    new_state_gla_bwd = jnp.stack(new_b, axis=1)
    return (y_prompt, y_sample, new_state_gla_fwd, new_state_gla_bwd)
```

```python
import functools

import jax
import jax.numpy as jnp
from jax import lax
from jax.experimental import pallas as pl
from jax.experimental.pallas import tpu as pltpu

F32 = jnp.float32
BF16 = jnp.bfloat16
I32 = jnp.int32

D_MODEL = 1024
D_CONV = D_MODEL // 2
GRID_W = 64
GLA_HEADS = 4
GLA_DK = 64
GLA_DV = 128
GLA_DK_TOT = GLA_HEADS * GLA_DK
GLA_DV_TOT = GLA_HEADS * GLA_DV
GLA_LOW_RANK = 16
GLA_TAU = 16.0
GLA_CHUNK = 64
N_EXPERTS = 16
EC_CAPACITY_FACTOR = 2
D_EXPERT = 1024
N_MOD = 6
EPS = 1e-6

OFF_XB = 0
OFF_XC = D_CONV
OFF_XV = 2 * D_CONV
OFF_Q = 3 * D_CONV
OFF_K = OFF_Q + GLA_DK_TOT
OFF_V = OFF_K + GLA_DK_TOT
OFF_OG = OFF_V + GLA_DV_TOT
OFF_ALOW = OFF_OG + GLA_DV_TOT
P_TOT = OFF_ALOW + 2 * GLA_LOW_RANK

LANES = 128
P_PAD = -(-P_TOT // LANES) * LANES
ROW_TILE = 256
TOKEN_TILE = 512
DISPATCH_ROWS = 512
VMEM_LIMIT = 56 * 1024 * 1024


def _dot(a, b):
    return jnp.dot(a, b, preferred_element_type=F32)


def _dot_nt(a, b):
    return lax.dot_general(a, b, (((1,), (1,)), ((), ())), preferred_element_type=F32)


def _dot_tn(a, b):
    return lax.dot_general(a, b, (((0,), (0,)), ((), ())), preferred_element_type=F32)


def _split(a):
    hi = a.astype(BF16)
    lo = (a - hi.astype(F32)).astype(BF16)
    return hi, lo


def _silu(x):
    return x * jax.nn.sigmoid(x)


def _modulated_norm(x, g, scale, shift):
    r = lax.rsqrt(jnp.mean(x * x, axis=-1, keepdims=True) + EPS)
    return (x * r) * g * (1.0 + scale) + shift


def _params(n_axes):
    return pltpu.CompilerParams(dimension_semantics=("arbitrary",) * n_axes,
                                vmem_limit_bytes=VMEM_LIMIT)


def _mod_kernel(c_ref, w_ref, b_ref, o_ref):
    s_hi, s_lo = _split(_silu(c_ref[...]))
    w_hi, w_lo = _split(w_ref[...])
    o_ref[...] = _dot(s_hi, w_hi) + _dot(s_hi, w_lo) + _dot(s_lo, w_hi) + b_ref[...]


def _mod_call(c_rows, w_mod, b_mod):
    rows, d = c_rows.shape
    n = w_mod.shape[1]
    tn = D_MODEL
    return pl.pallas_call(
        _mod_kernel,
        grid=(n // tn,),
        in_specs=[pl.BlockSpec((rows, d), lambda j: (0, 0)),
                  pl.BlockSpec((d, tn), lambda j: (0, j)),
                  pl.BlockSpec((1, tn), lambda j: (0, j))],
        out_specs=pl.BlockSpec((rows, tn), lambda j: (0, j)),
        out_shape=jax.ShapeDtypeStruct((rows, n), F32),
        compiler_params=_params(1),
        name="mod",
    )(c_rows, w_mod, b_mod)


def _inproj_kernel(x_ref, mod_ref, g_ref, w_ref, p_ref):
    m = mod_ref[0]
    h = _modulated_norm(x_ref[...], g_ref[...], m[1:2], m[0:1])
    p_ref[...] = _dot(h.astype(BF16), w_ref[...])


def _inproj_call(x2d, mod3, g1, w_in_bf, mod_row_of_tile):
    n_tok = x2d.shape[0]
    return pl.pallas_call(
        _inproj_kernel,
        grid=(n_tok // TOKEN_TILE,),
        in_specs=[pl.BlockSpec((TOKEN_TILE, D_MODEL), lambda i: (i, 0)),
                  pl.BlockSpec((1, N_MOD, D_MODEL), lambda i: (mod_row_of_tile(i), 0, 0)),
                  pl.BlockSpec((1, D_MODEL), lambda i: (0, 0)),
                  pl.BlockSpec((D_MODEL, P_PAD), lambda i: (0, 0))],
        out_specs=pl.BlockSpec((TOKEN_TILE, P_PAD), lambda i: (i, 0)),
        out_shape=jax.ShapeDtypeStruct((n_tok, P_PAD), F32),
        compiler_params=_params(1),
        name="inproj",
    )(x2d, mod3, g1, w_in_bf)


def _for_row_tiles(seq_len, fn):
    n = seq_len // ROW_TILE
    if n == 1:
        fn(0)
    else:
        def body(i, carry):
            fn(pl.multiple_of(i * ROW_TILE, ROW_TILE))
            return carry
        lax.fori_loop(0, n, body, 0)


def _chunk_scan(la, pos, forward):
    rows = la.shape[0]
    b = la
    sh = 1
    while sh < GLA_CHUNK:
        if forward:
            b = b + jnp.where(pos >= sh, pltpu.roll(b, sh, 0), 0.0)
        else:
            b = b + jnp.where(pos < GLA_CHUNK - sh, pltpu.roll(b, rows - sh, 0), 0.0)
        sh *= 2
    return b


def _mixer_kernel(*refs, seq_len, period, has_state_in, has_state_out):
    refs = list(refs)
    x_ref, p_ref, mod_ref = refs[:3]
    del refs[:3]
    if has_state_in:
        sf0_ref, sb0_ref = refs[:2]
        del refs[:2]
    wconv_ref, bconv_ref, wup_ref, bup_ref, ggla_ref, wout_ref, x1_ref = refs[:7]
    del refs[:7]
    if has_state_out:
        sf_ref, sb_ref = refs[:2]
        del refs[:2]
    qd_ref, kd_ref, ks_ref, v_ref, b_ref, o_ref, ya_ref = refs

    c = GLA_CHUNK
    n_chunks = seq_len // c
    m = mod_ref[0]

    def stage1(r0):
        rows = pl.ds(r0, ROW_TILE)
        row_i = lax.broadcasted_iota(I32, (ROW_TILE, 1), 0)
        pos = row_i & (period - 1)
        u = p_ref[rows, OFF_XC:OFF_XC + D_CONV] * p_ref[rows, OFF_XV:OFF_XV + D_CONV]
        u_prev = jnp.where(pos == 0, 0.0, pltpu.roll(u, 1, 0))
        u_next = jnp.where(pos == period - 1, 0.0, pltpu.roll(u, ROW_TILE - 1, 0))
        conv = u_prev * wconv_ref[0:1, :] + u * wconv_ref[1:2, :] + u_next * wconv_ref[2:3, :] + bconv_ref[...]
        ya_ref[rows, :] = (p_ref[rows, OFF_XB:OFF_XB + D_CONV] * conv).astype(BF16)

        z = _dot(p_ref[rows, OFF_ALOW:P_PAD].astype(BF16), wup_ref[...]) + bup_ref[...]
        la = (jnp.minimum(z, 0.0) - jnp.log1p(jnp.exp(-jnp.abs(z)))) * (1.0 / GLA_TAU)
        posc = row_i & (c - 1)
        pre = _chunk_scan(la, posc, True)
        suf = _chunk_scan(la, posc, False)
        q = p_ref[rows, OFF_Q:OFF_Q + GLA_DK_TOT] * (GLA_DK ** -0.5)
        k = p_ref[rows, OFF_K:OFF_K + GLA_DK_TOT]
        for d in range(2):
            cols = slice(d * GLA_DK_TOT, (d + 1) * GLA_DK_TOT)
            cum, rest = (pre, suf) if d == 0 else (suf, pre)
            bq = cum[:, cols]
            bk = rest[:, cols] - la[:, cols]
            qd_ref[d, rows, :] = (q * jnp.exp(bq)).astype(BF16)
            kd_ref[d, rows, :] = (k * jnp.exp(-bq)).astype(BF16)
            ks_ref[d, rows, :] = (k * jnp.exp(bk)).astype(BF16)
            b_ref[d, rows, :] = bq
        v_ref[rows, :] = p_ref[rows, OFF_V:OFF_V + GLA_DV_TOT].astype(BF16)

    _for_row_tiles(seq_len, stage1)

    lane_i = lax.broadcasted_iota(I32, (1, 2 * GLA_DK), 1)
    st_row = lax.broadcasted_iota(I32, (2 * GLA_DV, 1), 0)
    blockdiag = (st_row >= GLA_DV) == (lane_i >= GLA_DK)
    ci = lax.broadcasted_iota(I32, (c, 1), 0)
    cj = lax.broadcasted_iota(I32, (1, c), 1)

    def gla_pass(d, pair, st0):
        kl = slice(pair * 2 * GLA_DK, (pair + 1) * 2 * GLA_DK)
        vl = slice(pair * 2 * GLA_DV, (pair + 1) * 2 * GLA_DV)
        causal = (cj <= ci) if d == 0 else (cj >= ci)

        def body(i, st):
            n = i if d == 0 else n_chunks - 1 - i
            r0 = pl.multiple_of(n * c, c)
            rows = pl.ds(r0, c)
            qd = qd_ref[d, rows, kl]
            kd = kd_ref[d, rows, kl]
            ks = ks_ref[d, rows, kl]
            v = v_ref[rows, vl]
            intra = []
            for hh in range(2):
                qh = jnp.where((lane_i >= GLA_DK) == (hh == 1), qd, jnp.zeros_like(qd))
                att = jnp.where(causal, _dot_nt(qh, kd), 0.0)
                intra.append(_dot(att.astype(BF16), v[:, hh * GLA_DV:(hh + 1) * GLA_DV]))
            o = _dot_nt(qd, st.astype(BF16)) + jnp.concatenate(intra, axis=1)
            if d == 0:
                o_ref[rows, vl] = o
            else:
                o_ref[rows, vl] = o_ref[rows, vl] + o
            kv_t = jnp.where(blockdiag, _dot_tn(v, ks), 0.0)
            if d == 0:
                edge = b_ref[d, pl.ds(pl.multiple_of(r0 + c - 8, 8), 8), kl][7:8]
            else:
                edge = b_ref[d, pl.ds(r0, 8), kl][0:1]
            decay = jnp.exp(edge)
            return decay * st + kv_t

        return lax.fori_loop(0, n_chunks, body, st0)

    for d in range(2):
        for pair in range(GLA_HEADS // 2):
            h0 = 2 * pair
            if has_state_in:
                s_ref = sf0_ref if d == 0 else sb0_ref
                zero = jnp.zeros((GLA_DK, GLA_DV), F32)
                top = jnp.concatenate([s_ref[h0], zero], axis=1)
                bot = jnp.concatenate([zero, s_ref[h0 + 1]], axis=1)
                st0 = jnp.transpose(jnp.concatenate([top, bot], axis=0))
            else:
                st0 = jnp.zeros((2 * GLA_DV, 2 * GLA_DK), F32)
            st = gla_pass(d, pair, st0)
            if has_state_out:
                s_pair = jnp.transpose(st)
                out_ref = sf_ref if d == 0 else sb_ref
                out_ref[h0] = s_pair[0:GLA_DK, 0:GLA_DV]
                out_ref[h0 + 1] = s_pair[GLA_DK:2 * GLA_DK, GLA_DV:2 * GLA_DV]

    def stage3(r0):
        rows = pl.ds(r0, ROW_TILE)
        heads = []
        for h in range(GLA_HEADS):
            hl = slice(h * GLA_DV, (h + 1) * GLA_DV)
            oh = o_ref[rows, hl]
            r = lax.rsqrt(jnp.mean(oh * oh, axis=-1, keepdims=True) + EPS)
            heads.append(oh * r * ggla_ref[:, hl])
        y_b = jnp.concatenate(heads, axis=1) * _silu(p_ref[rows, OFF_OG:OFF_OG + GLA_DV_TOT])
        y = jnp.concatenate([ya_ref[rows, :], y_b.astype(BF16)], axis=1)
        x1_ref[rows, :] = x_ref[rows, :] + m[2:3] * _dot(y, wout_ref[...])

    _for_row_tiles(seq_len, stage3)


def _mixer_call(x, p, mod3, mod_row_of_seq, states, weights, *, period, has_state_out):
    n_seq, seq_len, _ = x.shape
    has_state_in = states is not None
    kernel = functools.partial(_mixer_kernel, seq_len=seq_len, period=period,
                               has_state_in=has_state_in, has_state_out=has_state_out)
    state_spec = pl.BlockSpec((None, None, GLA_HEADS, GLA_DK, GLA_DV), lambda b: (b, 0, 0, 0, 0))
    const2 = lambda b: (0, 0)
    in_specs = [pl.BlockSpec((None, seq_len, D_MODEL), lambda b: (b, 0, 0)),
                pl.BlockSpec((seq_len, P_PAD), lambda b: (b, 0), pipeline_mode=pl.Buffered(1)),
                pl.BlockSpec((1, N_MOD, D_MODEL), lambda b: (mod_row_of_seq(b), 0, 0))]
    args = [x, p, mod3]
    if has_state_in:
        in_specs += [state_spec, state_spec]
        args += list(states)
    in_specs += [pl.BlockSpec(w.shape, const2) for w in weights]
    args += list(weights)
    out_specs = [pl.BlockSpec((None, seq_len, D_MODEL), lambda b: (b, 0, 0))]
    out_shape = [jax.ShapeDtypeStruct((n_seq, seq_len, D_MODEL), F32)]
    if has_state_out:
        out_specs += [state_spec, state_spec]
        out_shape += [jax.ShapeDtypeStruct((n_seq, 1, GLA_HEADS, GLA_DK, GLA_DV), F32)] * 2
    scratch = [pltpu.VMEM((2, seq_len, GLA_DK_TOT), BF16),
               pltpu.VMEM((2, seq_len, GLA_DK_TOT), BF16),
               pltpu.VMEM((2, seq_len, GLA_DK_TOT), BF16),
               pltpu.VMEM((seq_len, GLA_DV_TOT), BF16),
               pltpu.VMEM((2, seq_len, GLA_DK_TOT), F32),
               pltpu.VMEM((seq_len, GLA_DV_TOT), F32),
               pltpu.VMEM((seq_len, D_CONV), BF16)]
    return pl.pallas_call(
        kernel,
        grid=(n_seq,),
        in_specs=in_specs,
        out_specs=out_specs,
        out_shape=out_shape,
        scratch_shapes=scratch,
        compiler_params=_params(1),
        name="mixer",
    )(*args)


def _onehot_rows(rank_ref, experts, cap):
    slot = lax.broadcasted_iota(I32, (cap, 1), 0)
    return jnp.concatenate([rank_ref[e:e + 1, :] == slot for e in experts], axis=0)


def _route_kernel(x1_ref, mod_ref, g2_ref, wrh_ref, wrl_ref, xs_ref, rank_ref, pt_ref,
                  h2_ref, kn_ref, kt_ref, *, seq_len, cap):
    m = mod_ref[0]
    wrh = wrh_ref[...]
    wrl = wrl_ref[...]
    for t in range(seq_len // ROW_TILE):
        rows = slice(t * ROW_TILE, (t + 1) * ROW_TILE)
        h2 = _modulated_norm(x1_ref[rows, :], g2_ref[...], m[4:5], m[3:4])
        hi, lo = _split(h2)
        h2_ref[rows, :] = hi
        pt_ref[:, rows] = _dot_nt(wrh, hi) + _dot_nt(wrh, lo) + _dot_nt(wrl, hi)

    logits = pt_ref[...]
    ex = jnp.exp(logits - jnp.max(logits, axis=0, keepdims=True))
    probs = ex / jnp.sum(ex, axis=0, keepdims=True)
    pt_ref[...] = probs
    keys = pltpu.bitcast(probs, I32)
    kt_ref[...] = keys
    pad = jnp.zeros((LANES - N_EXPERTS, seq_len), I32)
    kn_ref[...] = jnp.transpose(jnp.concatenate([keys, pad], axis=0))

    tok_l = lax.broadcasted_iota(I32, (1, seq_len), 1)
    tok_s = lax.broadcasted_iota(I32, (LANES, 1), 0)
    for e in range(N_EXPERTS):
        krow = kt_ref[e:e + 1, :]

        def block(sb, acc):
            s0 = pl.multiple_of(sb * LANES, LANES)
            kcol = kn_ref[pl.ds(s0, LANES), e:e + 1]
            earlier = jnp.where(tok_s + s0 < tok_l, 1, 0)
            beats = kcol > krow - earlier
            return acc + jnp.sum(jnp.where(beats, 1.0, 0.0), axis=0, keepdims=True)

        cnt = lax.fori_loop(0, seq_len // LANES, block, jnp.zeros((1, seq_len), F32))
        rank_ref[e:e + 1, :] = cnt.astype(I32)

    group = DISPATCH_ROWS // cap
    for gi in range(N_EXPERTS // group):
        oh = _onehot_rows(rank_ref, range(gi * group, (gi + 1) * group), cap)
        ohb = jnp.where(oh, 1.0, 0.0).astype(BF16)
        xs_ref[gi * DISPATCH_ROWS:(gi + 1) * DISPATCH_ROWS, :] = _dot(ohb, h2_ref[...]).astype(BF16)


def _route_call(x1, mod3, mod_row_of_seq, g2, wr_hi, wr_lo):
    n_seq, seq_len, _ = x1.shape
    cap = EC_CAPACITY_FACTOR * seq_len // N_EXPERTS
    kernel = functools.partial(_route_kernel, seq_len=seq_len, cap=cap)
    const2 = lambda b: (0, 0)
    et_spec = pl.BlockSpec((None, N_EXPERTS, seq_len), lambda b: (b, 0, 0))
    return pl.pallas_call(
        kernel,
        grid=(n_seq,),
        in_specs=[pl.BlockSpec((None, seq_len, D_MODEL), lambda b: (b, 0, 0)),
                  pl.BlockSpec((1, N_MOD, D_MODEL), lambda b: (mod_row_of_seq(b), 0, 0)),
                  pl.BlockSpec((1, D_MODEL), const2),
                  pl.BlockSpec((N_EXPERTS, D_MODEL), const2),
                  pl.BlockSpec((N_EXPERTS, D_MODEL), const2)],
        out_specs=[pl.BlockSpec((None, N_EXPERTS * cap, D_MODEL), lambda b: (b, 0, 0)), et_spec, et_spec],
        out_shape=[jax.ShapeDtypeStruct((n_seq, N_EXPERTS * cap, D_MODEL), BF16),
                   jax.ShapeDtypeStruct((n_seq, N_EXPERTS, seq_len), I32),
                   jax.ShapeDtypeStruct((n_seq, N_EXPERTS, seq_len), F32)],
        scratch_shapes=[pltpu.VMEM((seq_len, D_MODEL), BF16),
                        pltpu.VMEM((seq_len, LANES), I32),
                        pltpu.VMEM((N_EXPERTS, seq_len), I32)],
        compiler_params=_params(1),
        name="route",
    )(x1, mod3, g2, wr_hi, wr_lo)


def _experts_kernel(xc_ref, xl_ref, wg_ref, wu_ref, wd_ref, yc_ref, yl_ref):
    j = pl.program_id(1)
    wg = wg_ref[...].astype(BF16)
    wu = wu_ref[...].astype(BF16)
    wd = wd_ref[...].astype(BF16)

    def run(x_ref, y_ref):
        n_seq, cap, _ = x_ref.shape
        seqs = DISPATCH_ROWS // cap
        for s0 in range(0, n_seq, seqs):
            x = x_ref[s0:s0 + seqs].reshape(DISPATCH_ROWS, D_MODEL)
            a = (_silu(_dot(x, wg)) * _dot(x, wu)).astype(BF16)
            y = _dot(a, wd).reshape(seqs, cap, D_MODEL)

            @pl.when(j == 0)
            def _():
                y_ref[s0:s0 + seqs] = y

            @pl.when(j != 0)
            def _():
                y_ref[s0:s0 + seqs] = y_ref[s0:s0 + seqs] + y

    run(xc_ref, yc_ref)
    run(xl_ref, yl_ref)


def _experts_call(xs_ctx, xs_lat, w_gate, w_up, w_down):
    tf = D_EXPERT // 2

    def tok_spec(xs):
        n_seq, _, cap, _ = xs.shape
        return pl.BlockSpec((n_seq, None, cap, D_MODEL), lambda e, j: (0, e, 0, 0))

    return pl.pallas_call(
        _experts_kernel,
        grid=(N_EXPERTS, D_EXPERT // tf),
        in_specs=[tok_spec(xs_ctx), tok_spec(xs_lat),
                  pl.BlockSpec((None, D_MODEL, tf), lambda e, j: (e, 0, j)),
                  pl.BlockSpec((None, D_MODEL, tf), lambda e, j: (e, 0, j)),
                  pl.BlockSpec((None, tf, D_MODEL), lambda e, j: (e, j, 0))],
        out_specs=[tok_spec(xs_ctx), tok_spec(xs_lat)],
        out_shape=[jax.ShapeDtypeStruct(xs_ctx.shape, F32), jax.ShapeDtypeStruct(xs_lat.shape, F32)],
        compiler_params=_params(2),
        name="experts",
    )(xs_ctx, xs_lat, w_gate, w_up, w_down)


def _combine_kernel(x1_ref, y_ref, rank_ref, pt_ref, mod_ref, gf_ref, o_ref,
                    moe_ref, oh_ref, yh_ref, yl_ref, *, seq_len, cap):
    m = mod_ref[0]
    group = DISPATCH_ROWS // cap
    slot = lax.broadcasted_iota(I32, (cap, 1), 0)
    for gi in range(N_EXPERTS // group):
        for k in range(group):
            e = gi * group + k
            oh = rank_ref[e:e + 1, :] == slot
            gate = jnp.sum(jnp.where(oh, pt_ref[e:e + 1, :], 0.0), axis=1, keepdims=True)
            hi, lo = _split(y_ref[e] * gate)
            rows = slice(k * cap, (k + 1) * cap)
            oh_ref[rows, :] = jnp.where(oh, 1.0, 0.0).astype(BF16)
            yh_ref[rows, :] = hi
            yl_ref[rows, :] = lo
        for t in range(seq_len // ROW_TILE):
            rows = slice(t * ROW_TILE, (t + 1) * ROW_TILE)
            oh_t = oh_ref[:, rows]
            part = _dot_tn(oh_t, yh_ref[...]) + _dot_tn(oh_t, yl_ref[...])
            if gi == 0:
                moe_ref[rows, :] = part
            else:
                moe_ref[rows, :] = moe_ref[rows, :] + part

    def finish(r0):
        rows = pl.ds(r0, ROW_TILE)
        x2 = x1_ref[rows, :] + m[5:6] * moe_ref[rows, :]
        r = lax.rsqrt(jnp.mean(x2 * x2, axis=-1, keepdims=True) + EPS)
        o_ref[rows, :] = (x2 * r) * gf_ref[...]

    _for_row_tiles(seq_len, finish)


def _combine_call(x1, y, rank, probs, mod3, mod_row_of_seq, g_final):
    n_seq, seq_len, _ = x1.shape
    cap = y.shape[2]
    kernel = functools.partial(_combine_kernel, seq_len=seq_len, cap=cap)
    et_spec = pl.BlockSpec((None, N_EXPERTS, seq_len), lambda b: (b, 0, 0))
    seq_spec = pl.BlockSpec((None, seq_len, D_MODEL), lambda b: (b, 0, 0))
    return pl.pallas_call(
        kernel,
        grid=(n_seq,),
        in_specs=[seq_spec,
                  pl.BlockSpec((None, N_EXPERTS, cap, D_MODEL), lambda b: (b, 0, 0, 0)),
                  et_spec, et_spec,
                  pl.BlockSpec((1, N_MOD, D_MODEL), lambda b: (mod_row_of_seq(b), 0, 0)),
                  pl.BlockSpec((1, D_MODEL), lambda b: (0, 0))],
        out_specs=seq_spec,
        out_shape=jax.ShapeDtypeStruct((n_seq, seq_len, D_MODEL), F32),
        scratch_shapes=[pltpu.VMEM((seq_len, D_MODEL), F32),
                        pltpu.VMEM((DISPATCH_ROWS, seq_len), BF16),
                        pltpu.VMEM((DISPATCH_ROWS, D_MODEL), BF16),
                        pltpu.VMEM((DISPATCH_ROWS, D_MODEL), BF16)],
        compiler_params=_params(1),
        name="combine",
    )(x1, y, rank, probs, mod3, g_final)


def kernel(x_prompt, x_sample, state_gla_fwd, state_gla_bwd, c, c_ctx, w_mod, b_mod, g_norm1, g_norm2,
           w_in, w_conv, b_conv, w_a_up_f, b_a_f, w_a_up_b, b_a_b, g_gla_norm, w_out, w_router,
           w_gate, w_up, w_down, g_final):
    assert w_mod.shape[0] == 1, "single trunk layer"
    n_ctx, ctx_len, _ = x_prompt.shape
    n_lat, lat_len, _ = x_sample.shape
    ctx_cap = EC_CAPACITY_FACTOR * ctx_len // N_EXPERTS
    lat_cap = EC_CAPACITY_FACTOR * lat_len // N_EXPERTS

    c_rows = jnp.concatenate([c_ctx[None, :], c, jnp.zeros((8 - 1 - n_lat, D_MODEL), F32)], axis=0)
    mod3 = _mod_call(c_rows, w_mod[0], b_mod).reshape(8, N_MOD, D_MODEL)

    w_in_bf = jnp.pad(w_in[0], ((0, 0), (0, P_PAD - P_TOT))).astype(BF16)
    w_up_gate = jnp.zeros((P_PAD - OFF_ALOW, 2 * GLA_DK_TOT), F32)
    w_up_gate = w_up_gate.at[:GLA_LOW_RANK, :GLA_DK_TOT].set(w_a_up_f[0])
    w_up_gate = w_up_gate.at[GLA_LOW_RANK:2 * GLA_LOW_RANK, GLA_DK_TOT:].set(w_a_up_b[0]).astype(BF16)
    b_up_gate = jnp.concatenate([b_a_f[0], b_a_b[0]])[None, :]
    mixer_weights = (w_conv[0], b_conv, w_up_gate, b_up_gate,
                     g_gla_norm[0].reshape(1, GLA_DV_TOT), w_out[0].astype(BF16))
    wr_t = jnp.transpose(w_router[0])
    wr_hi = wr_t.astype(BF16)
    wr_lo = (wr_t - wr_hi.astype(F32)).astype(BF16)

    ctx_row = lambda b: 0
    lat_row = lambda b: b + 1
    lat_tiles = lat_len // TOKEN_TILE

    p_ctx = _inproj_call(x_prompt.reshape(n_ctx * ctx_len, D_MODEL), mod3, g_norm1, w_in_bf, lambda i: 0)
    p_lat = _inproj_call(x_sample.reshape(n_lat * lat_len, D_MODEL), mod3, g_norm1, w_in_bf,
                         lambda i: i // lat_tiles + 1)

    x1_ctx, new_f, new_b = _mixer_call(x_prompt, p_ctx, mod3, ctx_row, None, mixer_weights,
                                       period=ctx_len, has_state_out=True)
    (x1_lat,) = _mixer_call(x_sample, p_lat, mod3, lat_row, (state_gla_fwd, state_gla_bwd), mixer_weights,
                            period=GRID_W, has_state_out=False)

    xs_ctx, rank_ctx, pr_ctx = _route_call(x1_ctx, mod3, ctx_row, g_norm2, wr_hi, wr_lo)
    xs_lat, rank_lat, pr_lat = _route_call(x1_lat, mod3, lat_row, g_norm2, wr_hi, wr_lo)

    y_ctx, y_lat = _experts_call(xs_ctx.reshape(n_ctx, N_EXPERTS, ctx_cap, D_MODEL),
                                 xs_lat.reshape(n_lat, N_EXPERTS, lat_cap, D_MODEL),
                                 w_gate[0], w_up[0], w_down[0])

    g_fin = g_final[None, :]
    y_prompt = _combine_call(x1_ctx, y_ctx, rank_ctx, pr_ctx, mod3, ctx_row, g_fin)
    y_sample = _combine_call(x1_lat, y_lat, rank_lat, pr_lat, mod3, lat_row, g_fin)
    return y_prompt, y_sample, new_f, new_b
```

```python
import functools

import jax
import jax.numpy as jnp
from jax import lax
from jax.experimental import pallas as pl
from jax.experimental.pallas import tpu as pltpu

F32 = jnp.float32
BF16 = jnp.bfloat16
I32 = jnp.int32

D_MODEL = 1024
D_CONV = D_MODEL // 2
GRID_W = 64
GLA_HEADS = 4
GLA_DK = 64
GLA_DV = 128
GLA_DK_TOT = GLA_HEADS * GLA_DK
GLA_DV_TOT = GLA_HEADS * GLA_DV
GLA_LOW_RANK = 16
GLA_TAU = 16.0
GLA_CHUNK = 64
N_EXPERTS = 16
EC_CAPACITY_FACTOR = 2
D_EXPERT = 1024
N_MOD = 6
EPS = 1e-6

OFF_XB = 0
OFF_XC = D_CONV
OFF_XV = 2 * D_CONV
OFF_Q = 3 * D_CONV
OFF_K = OFF_Q + GLA_DK_TOT
OFF_V = OFF_K + GLA_DK_TOT
OFF_OG = OFF_V + GLA_DV_TOT
OFF_ALOW = OFF_OG + GLA_DV_TOT
P_TOT = OFF_ALOW + 2 * GLA_LOW_RANK

LANES = 128
P_PAD = -(-P_TOT // LANES) * LANES
ROW_TILE = 256
TOKEN_TILE = 512
DISPATCH_ROWS = 512
VMEM_LIMIT = 56 * 1024 * 1024


def _dot(a, b):
    return jnp.dot(a, b, preferred_element_type=F32)


def _dot_nt(a, b):
    return lax.dot_general(a, b, (((1,), (1,)), ((), ())), preferred_element_type=F32)


def _dot_tn(a, b):
    return lax.dot_general(a, b, (((0,), (0,)), ((), ())), preferred_element_type=F32)


def _split(a):
    hi = a.astype(BF16)
    lo = (a - hi.astype(F32)).astype(BF16)
    return hi, lo


def _silu(x):
    return x * jax.nn.sigmoid(x)


def _modulated_norm(x, g, scale, shift):
    r = lax.rsqrt(jnp.mean(x * x, axis=-1, keepdims=True) + EPS)
    return (x * r) * g * (1.0 + scale) + shift


def _params(n_axes):
    return pltpu.CompilerParams(dimension_semantics=("arbitrary",) * n_axes,
                                vmem_limit_bytes=VMEM_LIMIT)


def _mod_kernel(c_ref, w_ref, b_ref, o_ref):
    s_hi, s_lo = _split(_silu(c_ref[...]))
    w_hi, w_lo = _split(w_ref[...])
    o_ref[...] = _dot(s_hi, w_hi) + _dot(s_hi, w_lo) + _dot(s_lo, w_hi) + b_ref[...]


def _mod_call(c_rows, w_mod, b_mod):
    rows, d = c_rows.shape
    n = w_mod.shape[1]
    tn = D_MODEL
    return pl.pallas_call(
        _mod_kernel,
        grid=(n // tn,),
        in_specs=[pl.BlockSpec((rows, d), lambda j: (0, 0)),
                  pl.BlockSpec((d, tn), lambda j: (0, j)),
                  pl.BlockSpec((1, tn), lambda j: (0, j))],
        out_specs=pl.BlockSpec((rows, tn), lambda j: (0, j)),
        out_shape=jax.ShapeDtypeStruct((rows, n), F32),
        compiler_params=_params(1),
        name="mod",
    )(c_rows, w_mod, b_mod)


def _inproj_kernel(x_ref, mod_ref, g_ref, w_ref, p_ref):
    m = mod_ref[0]
    h = _modulated_norm(x_ref[...], g_ref[...], m[1:2], m[0:1])
    p_ref[...] = _dot(h.astype(BF16), w_ref[...])


def _inproj_call(x2d, mod3, g1, w_in_bf, mod_row_of_tile):
    n_tok = x2d.shape[0]
    return pl.pallas_call(
        _inproj_kernel,
        grid=(n_tok // TOKEN_TILE,),
        in_specs=[pl.BlockSpec((TOKEN_TILE, D_MODEL), lambda i: (i, 0)),
                  pl.BlockSpec((1, N_MOD, D_MODEL), lambda i: (mod_row_of_tile(i), 0, 0)),
                  pl.BlockSpec((1, D_MODEL), lambda i: (0, 0)),
                  pl.BlockSpec((D_MODEL, P_PAD), lambda i: (0, 0))],
        out_specs=pl.BlockSpec((TOKEN_TILE, P_PAD), lambda i: (i, 0)),
        out_shape=jax.ShapeDtypeStruct((n_tok, P_PAD), F32),
        compiler_params=_params(1),
        name="inproj",
    )(x2d, mod3, g1, w_in_bf)


def _for_row_tiles(seq_len, fn):
    n = seq_len // ROW_TILE
    if n == 1:
        fn(0)
    else:
        def body(i, carry):
            fn(i)
            return carry
        lax.fori_loop(0, n, body, 0)


def _tile_rows(tile, offset=0, size=ROW_TILE):
    if isinstance(tile, int):
        return pl.ds(tile * ROW_TILE + offset, size)
    return pl.ds(pl.multiple_of(tile * ROW_TILE + offset, size), size)


def _chunk_scan(la, pos, forward):
    rows = la.shape[0]
    b = la
    sh = 1
    while sh < GLA_CHUNK:
        if forward:
            b = b + jnp.where(pos >= sh, pltpu.roll(b, sh, 0), 0.0)
        else:
            b = b + jnp.where(pos < GLA_CHUNK - sh, pltpu.roll(b, rows - sh, 0), 0.0)
        sh *= 2
    return b


def _mixer_kernel(*refs, seq_len, period, has_state_in, has_state_out):
    refs = list(refs)
    x_ref, p_ref, mod_ref = refs[:3]
    del refs[:3]
    if has_state_in:
        s0_refs = refs[:2]
        del refs[:2]
    wconv_ref, bconv_ref, wup_ref, bup_ref, ggla_ref, wout_ref, x1_ref = refs[:7]
    del refs[:7]
    if has_state_out:
        sout_refs = refs[:2]
        del refs[:2]
    qd_ref, kd_ref, kst_ref, dect_ref, v_ref, s_ref, sst_ref, o_ref, ya_ref = refs

    c = GLA_CHUNK
    tile_chunks = ROW_TILE // c
    n_tiles = seq_len // ROW_TILE
    n_pairs = GLA_HEADS // 2
    pair_k = 2 * GLA_DK
    pair_v = 2 * GLA_DV
    m = mod_ref[0]

    def stage1(ti):
        rows = _tile_rows(ti)
        row_i = lax.broadcasted_iota(I32, (ROW_TILE, 1), 0)
        pos = row_i & (period - 1)
        u = p_ref[rows, OFF_XC:OFF_XC + D_CONV] * p_ref[rows, OFF_XV:OFF_XV + D_CONV]
        u_prev = jnp.where(pos == 0, 0.0, pltpu.roll(u, 1, 0))
        u_next = jnp.where(pos == period - 1, 0.0, pltpu.roll(u, ROW_TILE - 1, 0))
        conv = u_prev * wconv_ref[0:1, :] + u * wconv_ref[1:2, :] + u_next * wconv_ref[2:3, :] + bconv_ref[...]
        ya_ref[rows, :] = (p_ref[rows, OFF_XB:OFF_XB + D_CONV] * conv).astype(BF16)

        z = _dot(p_ref[rows, OFF_ALOW:P_PAD].astype(BF16), wup_ref[...]) + bup_ref[...]
        la = (jnp.minimum(z, 0.0) - jnp.log1p(jnp.exp(-jnp.abs(z)))) * (1.0 / GLA_TAU)
        posc = row_i & (c - 1)
        pre = _chunk_scan(la, posc, True)
        suf = _chunk_scan(la, posc, False)
        q = p_ref[rows, OFF_Q:OFF_Q + GLA_DK_TOT] * (GLA_DK ** -0.5)
        k = p_ref[rows, OFF_K:OFF_K + GLA_DK_TOT]
        for d in range(2):
            cols = slice(d * GLA_DK_TOT, (d + 1) * GLA_DK_TOT)
            cum, rest = (pre, suf) if d == 0 else (suf, pre)
            bq = cum[:, cols]
            bk = rest[:, cols] - la[:, cols]
            qd_ref[d, rows, :] = (q * jnp.exp(bq)).astype(BF16)
            kd_ref[d, rows, :] = (k * jnp.exp(-bq)).astype(BF16)
            kst_ref[d, ti] = jnp.transpose(k * jnp.exp(bk)).astype(BF16)
            edge = c - 1 if d == 0 else 0
            totals = [bq[n * c + edge:n * c + edge + 1] for n in range(tile_chunks)]
            totals.append(jnp.zeros((LANES - tile_chunks, GLA_DK_TOT), F32))
            dect_ref[d, ti] = jnp.transpose(jnp.exp(jnp.concatenate(totals, axis=0)))
        v_ref[rows, :] = p_ref[rows, OFF_V:OFF_V + GLA_DV_TOT].astype(BF16)

    _for_row_tiles(seq_len, stage1)

    for d in range(2):
        for pair in range(n_pairs):
            if has_state_in:
                zero = jnp.zeros((GLA_DK, GLA_DV), F32)
                top = jnp.concatenate([s0_refs[d][2 * pair], zero], axis=1)
                bot = jnp.concatenate([zero, s0_refs[d][2 * pair + 1]], axis=1)
                s_ref[d, pair] = jnp.concatenate([top, bot], axis=0)
            else:
                s_ref[d, pair] = jnp.zeros((pair_k, pair_v), F32)

    def scan_tile(i):
        upper_lane = lax.broadcasted_iota(I32, (1, LANES), 1) >= GLA_DK
        qi = lax.broadcasted_iota(I32, (LANES, 1), 0)
        kj = lax.broadcasted_iota(I32, (1, 2 * LANES), 1) & (LANES - 1)
        same_chunk = (qi & c) == (kj & c)
        causal = (same_chunk & (kj <= qi), same_chunk & (kj >= qi))
        for pair in range(n_pairs):
            kl = slice(pair * pair_k, (pair + 1) * pair_k)
            vl = slice(pair * pair_v, (pair + 1) * pair_v)
            for blk in range(ROW_TILE // LANES):
                rows = _tile_rows(i, blk * LANES, LANES)
                att = None
                for d in range(2):
                    kd = kd_ref[d, rows, kl]
                    zk = jnp.zeros_like(kd)
                    keys = jnp.concatenate([jnp.where(upper_lane, zk, kd), jnp.where(upper_lane, kd, zk)], axis=0)
                    a = jnp.where(causal[d], _dot_nt(qd_ref[d, rows, kl], keys), 0.0)
                    att = a if att is None else att + a
                v = v_ref[rows, vl]
                zv = jnp.zeros((LANES, GLA_DV), BF16)
                v_bd = jnp.concatenate([jnp.concatenate([v[:, :GLA_DV], zv], axis=1),
                                        jnp.concatenate([zv, v[:, GLA_DV:]], axis=1)], axis=0)
                o_ref[rows, vl] = _dot(att.astype(BF16), v_bd)
        key_row = lax.broadcasted_iota(I32, (pair_k, 1), 0)
        val_col = lax.broadcasted_iota(I32, (1, pair_v), 1)
        blockdiag = (key_row >= GLA_DK) == (val_col >= GLA_DV)
        for d in range(2):
            tile = i if d == 0 else n_tiles - 1 - i
            chunks = range(tile_chunks)
            for pair in range(n_pairs):
                kr = slice(pair * pair_k, (pair + 1) * pair_k)
                vl = slice(pair * pair_v, (pair + 1) * pair_v)
                s = s_ref[d, pair]
                for c4 in (chunks if d == 0 else reversed(chunks)):
                    blk, half = divmod(c4, 2)
                    kst = kst_ref[d, tile, kr, blk * LANES:(blk + 1) * LANES]
                    kst = jnp.where(upper_lane if half else ~upper_lane, kst, jnp.zeros_like(kst))
                    kv = jnp.where(blockdiag, _dot(kst, v_ref[_tile_rows(tile, blk * LANES, LANES), vl]), 0.0)
                    sst_ref[pair, tile * tile_chunks + c4, d * pair_k:(d + 1) * pair_k, :] = s.astype(BF16)
                    s = dect_ref[d, tile, kr, c4:c4 + 1] * s + kv
                s_ref[d, pair] = s

    _for_row_tiles(seq_len, scan_tile)

    if has_state_out:
        for d in range(2):
            for pair in range(n_pairs):
                s = s_ref[d, pair]
                sout_refs[d][2 * pair] = s[0:GLA_DK, 0:GLA_DV]
                sout_refs[d][2 * pair + 1] = s[GLA_DK:, GLA_DV:]

    def stage3(i):
        for pair in range(n_pairs):
            kl = slice(pair * pair_k, (pair + 1) * pair_k)
            vl = slice(pair * pair_v, (pair + 1) * pair_v)
            for c4 in range(tile_chunks):
                crow = _tile_rows(i, c4 * c, c)
                q2 = jnp.concatenate([qd_ref[0, crow, kl], qd_ref[1, crow, kl]], axis=1)
                o_ref[crow, vl] = o_ref[crow, vl] + _dot(q2, sst_ref[pair, i * tile_chunks + c4])
        rows = _tile_rows(i)
        heads = []
        for h in range(GLA_HEADS):
            hl = slice(h * GLA_DV, (h + 1) * GLA_DV)
            oh = o_ref[rows, hl]
            r = lax.rsqrt(jnp.mean(oh * oh, axis=-1, keepdims=True) + EPS)
            heads.append(oh * r * ggla_ref[:, hl])
        y_b = jnp.concatenate(heads, axis=1) * _silu(p_ref[rows, OFF_OG:OFF_OG + GLA_DV_TOT])
        y = jnp.concatenate([ya_ref[rows, :], y_b.astype(BF16)], axis=1)
        x1_ref[rows, :] = x_ref[rows, :] + m[2:3] * _dot(y, wout_ref[...])

    _for_row_tiles(seq_len, stage3)


def _mixer_call(x, p, mod3, mod_row_of_seq, states, weights, *, period, has_state_out):
    n_seq, seq_len, _ = x.shape
    has_state_in = states is not None
    kernel = functools.partial(_mixer_kernel, seq_len=seq_len, period=period,
                               has_state_in=has_state_in, has_state_out=has_state_out)
    state_spec = pl.BlockSpec((None, None, GLA_HEADS, GLA_DK, GLA_DV), lambda b: (b, 0, 0, 0, 0))
    const2 = lambda b: (0, 0)
    in_specs = [pl.BlockSpec((None, seq_len, D_MODEL), lambda b: (b, 0, 0)),
                pl.BlockSpec((seq_len, P_PAD), lambda b: (b, 0), pipeline_mode=pl.Buffered(1)),
                pl.BlockSpec((1, N_MOD, D_MODEL), lambda b: (mod_row_of_seq(b), 0, 0))]
    args = [x, p, mod3]
    if has_state_in:
        in_specs += [state_spec, state_spec]
        args += list(states)
    in_specs += [pl.BlockSpec(w.shape, const2) for w in weights]
    args += list(weights)
    out_specs = [pl.BlockSpec((None, seq_len, D_MODEL), lambda b: (b, 0, 0))]
    out_shape = [jax.ShapeDtypeStruct((n_seq, seq_len, D_MODEL), F32)]
    if has_state_out:
        out_specs += [state_spec, state_spec]
        out_shape += [jax.ShapeDtypeStruct((n_seq, 1, GLA_HEADS, GLA_DK, GLA_DV), F32)] * 2
    n_tiles = seq_len // ROW_TILE
    n_pairs = GLA_HEADS // 2
    scratch = [pltpu.VMEM((2, seq_len, GLA_DK_TOT), BF16),
               pltpu.VMEM((2, seq_len, GLA_DK_TOT), BF16),
               pltpu.VMEM((2, n_tiles, GLA_DK_TOT, ROW_TILE), BF16),
               pltpu.VMEM((2, n_tiles, GLA_DK_TOT, LANES), F32),
               pltpu.VMEM((seq_len, GLA_DV_TOT), BF16),
               pltpu.VMEM((2, n_pairs, 2 * GLA_DK, 2 * GLA_DV), F32),
               pltpu.VMEM((n_pairs, seq_len // GLA_CHUNK, 4 * GLA_DK, 2 * GLA_DV), BF16),
               pltpu.VMEM((seq_len, GLA_DV_TOT), F32),
               pltpu.VMEM((seq_len, D_CONV), BF16)]
    return pl.pallas_call(
        kernel,
        grid=(n_seq,),
        in_specs=in_specs,
        out_specs=out_specs,
        out_shape=out_shape,
        scratch_shapes=scratch,
        compiler_params=_params(1),
        name="mixer",
    )(*args)


def _onehot_rows(rank_ref, experts, cap):
    slot = lax.broadcasted_iota(I32, (cap, 1), 0)
    return jnp.concatenate([rank_ref[e:e + 1, :] == slot for e in experts], axis=0)


def _route_kernel(x1_ref, mod_ref, g2_ref, wrh_ref, wrl_ref, xs_ref, rank_ref, pt_ref,
                  h2_ref, kn_ref, kt_ref, *, seq_len, cap):
    m = mod_ref[0]
    wrh = wrh_ref[...]
    wrl = wrl_ref[...]
    for t in range(seq_len // ROW_TILE):
        rows = slice(t * ROW_TILE, (t + 1) * ROW_TILE)
        h2 = _modulated_norm(x1_ref[rows, :], g2_ref[...], m[4:5], m[3:4])
        hi, lo = _split(h2)
        h2_ref[rows, :] = hi
        pt_ref[:, rows] = _dot_nt(wrh, hi) + _dot_nt(wrh, lo) + _dot_nt(wrl, hi)

    logits = pt_ref[...]
    ex = jnp.exp(logits - jnp.max(logits, axis=0, keepdims=True))
    probs = ex / jnp.sum(ex, axis=0, keepdims=True)
    pt_ref[...] = probs
    keys = pltpu.bitcast(probs, I32)
    kt_ref[...] = keys
    pad = jnp.zeros((LANES - N_EXPERTS, seq_len), I32)
    kn_ref[...] = jnp.transpose(jnp.concatenate([keys, pad], axis=0))

    tok_l = lax.broadcasted_iota(I32, (1, seq_len), 1)
    tok_s = lax.broadcasted_iota(I32, (LANES, 1), 0)
    for e in range(N_EXPERTS):
        krow = kt_ref[e:e + 1, :]

        def block(sb, acc):
            s0 = pl.multiple_of(sb * LANES, LANES)
            kcol = kn_ref[pl.ds(s0, LANES), e:e + 1]
            earlier = jnp.where(tok_s + s0 < tok_l, 1, 0)
            beats = kcol > krow - earlier
            return acc + jnp.sum(jnp.where(beats, 1.0, 0.0), axis=0, keepdims=True)

        cnt = lax.fori_loop(0, seq_len // LANES, block, jnp.zeros((1, seq_len), F32))
        rank_ref[e:e + 1, :] = cnt.astype(I32)

    group = DISPATCH_ROWS // cap
    for gi in range(N_EXPERTS // group):
        oh = _onehot_rows(rank_ref, range(gi * group, (gi + 1) * group), cap)
        ohb = jnp.where(oh, 1.0, 0.0).astype(BF16)
        xs_ref[gi * DISPATCH_ROWS:(gi + 1) * DISPATCH_ROWS, :] = _dot(ohb, h2_ref[...]).astype(BF16)


def _route_call(x1, mod3, mod_row_of_seq, g2, wr_hi, wr_lo):
    n_seq, seq_len, _ = x1.shape
    cap = EC_CAPACITY_FACTOR * seq_len // N_EXPERTS
    kernel = functools.partial(_route_kernel, seq_len=seq_len, cap=cap)
    const2 = lambda b: (0, 0)
    et_spec = pl.BlockSpec((None, N_EXPERTS, seq_len), lambda b: (b, 0, 0))
    return pl.pallas_call(
        kernel,
        grid=(n_seq,),
        in_specs=[pl.BlockSpec((None, seq_len, D_MODEL), lambda b: (b, 0, 0)),
                  pl.BlockSpec((1, N_MOD, D_MODEL), lambda b: (mod_row_of_seq(b), 0, 0)),
                  pl.BlockSpec((1, D_MODEL), const2),
                  pl.BlockSpec((N_EXPERTS, D_MODEL), const2),
                  pl.BlockSpec((N_EXPERTS, D_MODEL), const2)],
        out_specs=[pl.BlockSpec((None, N_EXPERTS * cap, D_MODEL), lambda b: (b, 0, 0)), et_spec, et_spec],
        out_shape=[jax.ShapeDtypeStruct((n_seq, N_EXPERTS * cap, D_MODEL), BF16),
                   jax.ShapeDtypeStruct((n_seq, N_EXPERTS, seq_len), I32),
                   jax.ShapeDtypeStruct((n_seq, N_EXPERTS, seq_len), F32)],
        scratch_shapes=[pltpu.VMEM((seq_len, D_MODEL), BF16),
                        pltpu.VMEM((seq_len, LANES), I32),
                        pltpu.VMEM((N_EXPERTS, seq_len), I32)],
        compiler_params=_params(1),
        name="route",
    )(x1, mod3, g2, wr_hi, wr_lo)


def _experts_kernel(xc_ref, xl_ref, wg_ref, wu_ref, wd_ref, yc_ref, yl_ref):
    j = pl.program_id(1)
    wg = wg_ref[...].astype(BF16)
    wu = wu_ref[...].astype(BF16)
    wd = wd_ref[...].astype(BF16)

    def run(x_ref, y_ref):
        n_seq, cap, _ = x_ref.shape
        seqs = DISPATCH_ROWS // cap
        for s0 in range(0, n_seq, seqs):
            x = x_ref[s0:s0 + seqs].reshape(DISPATCH_ROWS, D_MODEL)
            a = (_silu(_dot(x, wg)) * _dot(x, wu)).astype(BF16)
            y = _dot(a, wd).reshape(seqs, cap, D_MODEL)

            @pl.when(j == 0)
            def _():
                y_ref[s0:s0 + seqs] = y

            @pl.when(j != 0)
            def _():
                y_ref[s0:s0 + seqs] = y_ref[s0:s0 + seqs] + y

    run(xc_ref, yc_ref)
    run(xl_ref, yl_ref)


def _experts_call(xs_ctx, xs_lat, w_gate, w_up, w_down):
    tf = D_EXPERT // 2

    def tok_spec(xs):
        n_seq, _, cap, _ = xs.shape
        return pl.BlockSpec((n_seq, None, cap, D_MODEL), lambda e, j: (0, e, 0, 0))

    return pl.pallas_call(
        _experts_kernel,
        grid=(N_EXPERTS, D_EXPERT // tf),
        in_specs=[tok_spec(xs_ctx), tok_spec(xs_lat),
                  pl.BlockSpec((None, D_MODEL, tf), lambda e, j: (e, 0, j)),
                  pl.BlockSpec((None, D_MODEL, tf), lambda e, j: (e, 0, j)),
                  pl.BlockSpec((None, tf, D_MODEL), lambda e, j: (e, j, 0))],
        out_specs=[tok_spec(xs_ctx), tok_spec(xs_lat)],
        out_shape=[jax.ShapeDtypeStruct(xs_ctx.shape, F32), jax.ShapeDtypeStruct(xs_lat.shape, F32)],
        compiler_params=_params(2),
        name="experts",
    )(xs_ctx, xs_lat, w_gate, w_up, w_down)


def _combine_kernel(x1_ref, y_ref, rank_ref, pt_ref, mod_ref, gf_ref, o_ref,
                    moe_ref, oh_ref, yh_ref, yl_ref, *, seq_len, cap):
    m = mod_ref[0]
    group = DISPATCH_ROWS // cap
    slot = lax.broadcasted_iota(I32, (cap, 1), 0)
    for gi in range(N_EXPERTS // group):
        for k in range(group):
            e = gi * group + k
            oh = rank_ref[e:e + 1, :] == slot
            gate = jnp.sum(jnp.where(oh, pt_ref[e:e + 1, :], 0.0), axis=1, keepdims=True)
            hi, lo = _split(y_ref[e] * gate)
            rows = slice(k * cap, (k + 1) * cap)
            oh_ref[rows, :] = jnp.where(oh, 1.0, 0.0).astype(BF16)
            yh_ref[rows, :] = hi
            yl_ref[rows, :] = lo
        for t in range(seq_len // ROW_TILE):
            rows = slice(t * ROW_TILE, (t + 1) * ROW_TILE)
            oh_t = oh_ref[:, rows]
            part = _dot_tn(oh_t, yh_ref[...]) + _dot_tn(oh_t, yl_ref[...])
            if gi == 0:
                moe_ref[rows, :] = part
            else:
                moe_ref[rows, :] = moe_ref[rows, :] + part

    def finish(ti):
        rows = _tile_rows(ti)
        x2 = x1_ref[rows, :] + m[5:6] * moe_ref[rows, :]
        r = lax.rsqrt(jnp.mean(x2 * x2, axis=-1, keepdims=True) + EPS)
        o_ref[rows, :] = (x2 * r) * gf_ref[...]

    _for_row_tiles(seq_len, finish)


def _combine_call(x1, y, rank, probs, mod3, mod_row_of_seq, g_final):
    n_seq, seq_len, _ = x1.shape
    cap = y.shape[2]
    kernel = functools.partial(_combine_kernel, seq_len=seq_len, cap=cap)
    et_spec = pl.BlockSpec((None, N_EXPERTS, seq_len), lambda b: (b, 0, 0))
    seq_spec = pl.BlockSpec((None, seq_len, D_MODEL), lambda b: (b, 0, 0))
    return pl.pallas_call(
        kernel,
        grid=(n_seq,),
        in_specs=[seq_spec,
                  pl.BlockSpec((None, N_EXPERTS, cap, D_MODEL), lambda b: (b, 0, 0, 0)),
                  et_spec, et_spec,
                  pl.BlockSpec((1, N_MOD, D_MODEL), lambda b: (mod_row_of_seq(b), 0, 0)),
                  pl.BlockSpec((1, D_MODEL), lambda b: (0, 0))],
        out_specs=seq_spec,
        out_shape=jax.ShapeDtypeStruct((n_seq, seq_len, D_MODEL), F32),
        scratch_shapes=[pltpu.VMEM((seq_len, D_MODEL), F32),
                        pltpu.VMEM((DISPATCH_ROWS, seq_len), BF16),
                        pltpu.VMEM((DISPATCH_ROWS, D_MODEL), BF16),
                        pltpu.VMEM((DISPATCH_ROWS, D_MODEL), BF16)],
        compiler_params=_params(1),
        name="combine",
    )(x1, y, rank, probs, mod3, g_final)


def kernel(x_prompt, x_sample, state_gla_fwd, state_gla_bwd, c, c_ctx, w_mod, b_mod, g_norm1, g_norm2,
           w_in, w_conv, b_conv, w_a_up_f, b_a_f, w_a_up_b, b_a_b, g_gla_norm, w_out, w_router,
           w_gate, w_up, w_down, g_final):
    assert w_mod.shape[0] == 1, "single trunk layer"
    n_ctx, ctx_len, _ = x_prompt.shape
    n_lat, lat_len, _ = x_sample.shape
    ctx_cap = EC_CAPACITY_FACTOR * ctx_len // N_EXPERTS
    lat_cap = EC_CAPACITY_FACTOR * lat_len // N_EXPERTS

    c_rows = jnp.concatenate([c_ctx[None, :], c, jnp.zeros((8 - 1 - n_lat, D_MODEL), F32)], axis=0)
    mod3 = _mod_call(c_rows, w_mod[0], b_mod).reshape(8, N_MOD, D_MODEL)

    w_in_bf = jnp.pad(w_in[0], ((0, 0), (0, P_PAD - P_TOT))).astype(BF16)
    w_up_gate = jnp.zeros((P_PAD - OFF_ALOW, 2 * GLA_DK_TOT), F32)
    w_up_gate = w_up_gate.at[:GLA_LOW_RANK, :GLA_DK_TOT].set(w_a_up_f[0])
    w_up_gate = w_up_gate.at[GLA_LOW_RANK:2 * GLA_LOW_RANK, GLA_DK_TOT:].set(w_a_up_b[0]).astype(BF16)
    b_up_gate = jnp.concatenate([b_a_f[0], b_a_b[0]])[None, :]
    mixer_weights = (w_conv[0], b_conv, w_up_gate, b_up_gate,
                     g_gla_norm[0].reshape(1, GLA_DV_TOT), w_out[0].astype(BF16))
    wr_t = jnp.transpose(w_router[0])
    wr_hi = wr_t.astype(BF16)
    wr_lo = (wr_t - wr_hi.astype(F32)).astype(BF16)

    ctx_row = lambda b: 0
    lat_row = lambda b: b + 1
    lat_tiles = lat_len // TOKEN_TILE

    p_ctx = _inproj_call(x_prompt.reshape(n_ctx * ctx_len, D_MODEL), mod3, g_norm1, w_in_bf, lambda i: 0)
    p_lat = _inproj_call(x_sample.reshape(n_lat * lat_len, D_MODEL), mod3, g_norm1, w_in_bf,
                         lambda i: i // lat_tiles + 1)

    x1_ctx, new_f, new_b = _mixer_call(x_prompt, p_ctx, mod3, ctx_row, None, mixer_weights,
                                       period=ctx_len, has_state_out=True)
    (x1_lat,) = _mixer_call(x_sample, p_lat, mod3, lat_row, (state_gla_fwd, state_gla_bwd), mixer_weights,
                            period=GRID_W, has_state_out=False)

    xs_ctx, rank_ctx, pr_ctx = _route_call(x1_ctx, mod3, ctx_row, g_norm2, wr_hi, wr_lo)
    xs_lat, rank_lat, pr_lat = _route_call(x1_lat, mod3, lat_row, g_norm2, wr_hi, wr_lo)

    y_ctx, y_lat = _experts_call(xs_ctx.reshape(n_ctx, N_EXPERTS, ctx_cap, D_MODEL),
                                 xs_lat.reshape(n_lat, N_EXPERTS, lat_cap, D_MODEL),
                                 w_gate[0], w_up[0], w_down[0])

    g_fin = g_final[None, :]
    y_prompt = _combine_call(x1_ctx, y_ctx, rank_ctx, pr_ctx, mod3, ctx_row, g_fin)
    y_sample = _combine_call(x1_lat, y_lat, rank_lat, pr_lat, mod3, lat_row, g_fin)
    return y_prompt, y_sample, new_f, new_b
```

```python
import functools

import jax
import jax.numpy as jnp
from jax import lax
from jax.experimental import pallas as pl
from jax.experimental.pallas import tpu as pltpu

F32 = jnp.float32
BF16 = jnp.bfloat16
I32 = jnp.int32

D_MODEL = 1024
D_CONV = D_MODEL // 2
GRID_W = 64
GLA_HEADS = 4
GLA_DK = 64
GLA_DV = 128
GLA_DK_TOT = GLA_HEADS * GLA_DK
GLA_DV_TOT = GLA_HEADS * GLA_DV
GLA_LOW_RANK = 16
GLA_TAU = 16.0
GLA_CHUNK = 64
N_EXPERTS = 16
EC_CAPACITY_FACTOR = 2
D_EXPERT = 1024
N_MOD = 6
EPS = 1e-6

OFF_XB = 0
OFF_XC = D_CONV
OFF_XV = 2 * D_CONV
OFF_Q = 3 * D_CONV
OFF_K = OFF_Q + GLA_DK_TOT
OFF_V = OFF_K + GLA_DK_TOT
OFF_OG = OFF_V + GLA_DV_TOT
OFF_ALOW = OFF_OG + GLA_DV_TOT
P_TOT = OFF_ALOW + 2 * GLA_LOW_RANK

LANES = 128
P_PAD = -(-P_TOT // LANES) * LANES
ROW_TILE = 256
TOKEN_TILE = 512
DISPATCH_ROWS = 512
VMEM_LIMIT = 56 * 1024 * 1024


def _dot(a, b):
    return jnp.dot(a, b, preferred_element_type=F32)


def _dot_nt(a, b):
    return lax.dot_general(a, b, (((1,), (1,)), ((), ())), preferred_element_type=F32)


def _dot_tn(a, b):
    return lax.dot_general(a, b, (((0,), (0,)), ((), ())), preferred_element_type=F32)


def _split(a):
    hi = a.astype(BF16)
    lo = (a - hi.astype(F32)).astype(BF16)
    return hi, lo


def _silu(x):
    return x * jax.nn.sigmoid(x)


def _modulated_norm(x, g, scale, shift):
    r = lax.rsqrt(jnp.mean(x * x, axis=-1, keepdims=True) + EPS)
    return (x * r) * g * (1.0 + scale) + shift


def _params(n_axes):
    return pltpu.CompilerParams(dimension_semantics=("arbitrary",) * n_axes,
                                vmem_limit_bytes=VMEM_LIMIT)


def _mod_kernel(c_ref, w_ref, b_ref, o_ref):
    s_hi, s_lo = _split(_silu(c_ref[...]))
    w_hi, w_lo = _split(w_ref[...])
    o_ref[...] = _dot(s_hi, w_hi) + _dot(s_hi, w_lo) + _dot(s_lo, w_hi) + b_ref[...]


def _mod_call(c_rows, w_mod, b_mod):
    rows, d = c_rows.shape
    n = w_mod.shape[1]
    tn = D_MODEL
    return pl.pallas_call(
        _mod_kernel,
        grid=(n // tn,),
        in_specs=[pl.BlockSpec((rows, d), lambda j: (0, 0)),
                  pl.BlockSpec((d, tn), lambda j: (0, j)),
                  pl.BlockSpec((1, tn), lambda j: (0, j))],
        out_specs=pl.BlockSpec((rows, tn), lambda j: (0, j)),
        out_shape=jax.ShapeDtypeStruct((rows, n), F32),
        compiler_params=_params(1),
        name="mod",
    )(c_rows, w_mod, b_mod)


def _for_row_tiles(seq_len, fn):
    n = seq_len // ROW_TILE
    if n == 1:
        fn(0)
    else:
        def body(i, carry):
            fn(i)
            return carry
        lax.fori_loop(0, n, body, 0)


def _tile_rows(tile, offset=0, size=ROW_TILE):
    if isinstance(tile, int):
        return pl.ds(tile * ROW_TILE + offset, size)
    return pl.ds(pl.multiple_of(tile * ROW_TILE + offset, size), size)


def _mixer_kernel(*refs, seq_len, period, has_state_in, has_state_out):
    refs = list(refs)
    x_ref, mod_ref = refs[:2]
    del refs[:2]
    if has_state_in:
        s0_refs = refs[:2]
        del refs[:2]
    g1_ref, win_ref, wconv_ref, bconv_ref, wup_ref, bup_ref, ggla_ref, wout_ref, x1_ref = refs[:9]
    del refs[:9]
    if has_state_out:
        sout_refs = refs[:2]
        del refs[:2]
    p_ref, og_ref, qd_ref, kd_ref, kst_ref, dect_ref, v_ref, s_ref, sst_ref, o_ref, ya_ref = refs

    c = GLA_CHUNK
    tile_chunks = ROW_TILE // c
    n_tiles = seq_len // ROW_TILE
    n_pairs = GLA_HEADS // 2
    pair_k = 2 * GLA_DK
    pair_v = 2 * GLA_DV
    m = mod_ref[0]

    def stage1(ti):
        rows = _tile_rows(ti)
        h = _modulated_norm(x_ref[rows, :], g1_ref[...], m[1:2], m[0:1])
        p_ref[...] = _dot(h.astype(BF16), win_ref[...])
        og_ref[rows, :] = p_ref[:, OFF_OG:OFF_OG + GLA_DV_TOT]

        row_i = lax.broadcasted_iota(I32, (ROW_TILE, 1), 0)
        pos = row_i & (period - 1)
        u = p_ref[:, OFF_XC:OFF_XC + D_CONV] * p_ref[:, OFF_XV:OFF_XV + D_CONV]
        u_prev = jnp.where(pos == 0, 0.0, pltpu.roll(u, 1, 0))
        u_next = jnp.where(pos == period - 1, 0.0, pltpu.roll(u, ROW_TILE - 1, 0))
        conv = u_prev * wconv_ref[0:1, :] + u * wconv_ref[1:2, :] + u_next * wconv_ref[2:3, :] + bconv_ref[...]
        ya_ref[rows, :] = (p_ref[:, OFF_XB:OFF_XB + D_CONV] * conv).astype(BF16)

        z = _dot(p_ref[:, OFF_ALOW:P_PAD].astype(BF16), wup_ref[...]) + bup_ref[...]
        la = (jnp.minimum(z, 0.0) - jnp.log(1.0 + jnp.exp(-jnp.abs(z)))) * (1.0 / GLA_TAU)
        col_j = lax.broadcasted_iota(I32, (1, ROW_TILE), 1)
        same_chunk = (row_i & -c) == (col_j & -c)
        lower = jnp.where(same_chunk & (col_j <= row_i), 1.0, 0.0).astype(BF16)
        block = jnp.where(same_chunk, 1.0, 0.0).astype(BF16)
        la_parts = jnp.concatenate(_split(la), axis=1)
        n_gate = 2 * GLA_DK_TOT
        pre = _dot(lower, la_parts)
        pre = pre[:, :n_gate] + pre[:, n_gate:]
        tot = _dot(block, la_parts)
        tot = tot[:, :n_gate] + tot[:, n_gate:]
        q = p_ref[:, OFF_Q:OFF_Q + GLA_DK_TOT] * (GLA_DK ** -0.5)
        k = p_ref[:, OFF_K:OFF_K + GLA_DK_TOT]
        for d in range(2):
            cols = slice(d * GLA_DK_TOT, (d + 1) * GLA_DK_TOT)
            if d == 0:
                bq = pre[:, cols]
                bk = tot[:, cols] - bq
            else:
                bk = pre[:, cols] - la[:, cols]
                bq = tot[:, cols] - bk
            qd_ref[d, rows, :] = (q * jnp.exp(bq)).astype(BF16)
            kd_ref[d, rows, :] = (k * jnp.exp(-bq)).astype(BF16)
            kst_ref[d, ti] = jnp.transpose(k * jnp.exp(bk)).astype(BF16)
            totals = [tot[n * c:n * c + 1, cols] for n in range(tile_chunks)]
            totals.append(jnp.zeros((LANES - tile_chunks, GLA_DK_TOT), F32))
            dect_ref[d, ti] = jnp.transpose(jnp.exp(jnp.concatenate(totals, axis=0)))
        v_ref[rows, :] = p_ref[:, OFF_V:OFF_V + GLA_DV_TOT].astype(BF16)

    _for_row_tiles(seq_len, stage1)

    for d in range(2):
        for pair in range(n_pairs):
            if has_state_in:
                zero = jnp.zeros((GLA_DK, GLA_DV), F32)
                top = jnp.concatenate([s0_refs[d][2 * pair], zero], axis=1)
                bot = jnp.concatenate([zero, s0_refs[d][2 * pair + 1]], axis=1)
                s_ref[d, pair] = jnp.concatenate([top, bot], axis=0)
            else:
                s_ref[d, pair] = jnp.zeros((pair_k, pair_v), F32)

    def scan_tile(i):
        upper_lane = lax.broadcasted_iota(I32, (1, LANES), 1) >= GLA_DK
        qi = lax.broadcasted_iota(I32, (LANES, 1), 0)
        kj = lax.broadcasted_iota(I32, (1, 2 * LANES), 1) & (LANES - 1)
        same_chunk = (qi & c) == (kj & c)
        causal = (same_chunk & (kj <= qi), same_chunk & (kj >= qi))
        for pair in range(n_pairs):
            kl = slice(pair * pair_k, (pair + 1) * pair_k)
            vl = slice(pair * pair_v, (pair + 1) * pair_v)
            for blk in range(ROW_TILE // LANES):
                rows = _tile_rows(i, blk * LANES, LANES)
                att = None
                for d in range(2):
                    kd = kd_ref[d, rows, kl]
                    zk = jnp.zeros_like(kd)
                    keys = jnp.concatenate([jnp.where(upper_lane, zk, kd), jnp.where(upper_lane, kd, zk)], axis=0)
                    a = jnp.where(causal[d], _dot_nt(qd_ref[d, rows, kl], keys), 0.0)
                    att = a if att is None else att + a
                v = v_ref[rows, vl]
                zv = jnp.zeros((LANES, GLA_DV), BF16)
                v_bd = jnp.concatenate([jnp.concatenate([v[:, :GLA_DV], zv], axis=1),
                                        jnp.concatenate([zv, v[:, GLA_DV:]], axis=1)], axis=0)
                o_ref[rows, vl] = _dot(att.astype(BF16), v_bd)
        key_row = lax.broadcasted_iota(I32, (pair_k, 1), 0)
        val_col = lax.broadcasted_iota(I32, (1, pair_v), 1)
        blockdiag = (key_row >= GLA_DK) == (val_col >= GLA_DV)
        for d in range(2):
            tile = i if d == 0 else n_tiles - 1 - i
            chunks = range(tile_chunks)
            for pair in range(n_pairs):
                kr = slice(pair * pair_k, (pair + 1) * pair_k)
                vl = slice(pair * pair_v, (pair + 1) * pair_v)
                s = s_ref[d, pair]
                for c4 in (chunks if d == 0 else reversed(chunks)):
                    blk, half = divmod(c4, 2)
                    kst = kst_ref[d, tile, kr, blk * LANES:(blk + 1) * LANES]
                    kst = jnp.where(upper_lane if half else ~upper_lane, kst, jnp.zeros_like(kst))
                    kv = jnp.where(blockdiag, _dot(kst, v_ref[_tile_rows(tile, blk * LANES, LANES), vl]), 0.0)
                    sst_ref[pair, tile * tile_chunks + c4, d * pair_k:(d + 1) * pair_k, :] = s.astype(BF16)
                    s = dect_ref[d, tile, kr, c4:c4 + 1] * s + kv
                s_ref[d, pair] = s

    _for_row_tiles(seq_len, scan_tile)

    if has_state_out:
        for d in range(2):
            for pair in range(n_pairs):
                s = s_ref[d, pair]
                sout_refs[d][2 * pair] = s[0:GLA_DK, 0:GLA_DV]
                sout_refs[d][2 * pair + 1] = s[GLA_DK:, GLA_DV:]

    def stage3(i):
        for pair in range(n_pairs):
            kl = slice(pair * pair_k, (pair + 1) * pair_k)
            vl = slice(pair * pair_v, (pair + 1) * pair_v)
            for c4 in range(tile_chunks):
                crow = _tile_rows(i, c4 * c, c)
                q2 = jnp.concatenate([qd_ref[0, crow, kl], qd_ref[1, crow, kl]], axis=1)
                o_ref[crow, vl] = o_ref[crow, vl] + _dot(q2, sst_ref[pair, i * tile_chunks + c4])
        rows = _tile_rows(i)
        heads = []
        for h in range(GLA_HEADS):
            hl = slice(h * GLA_DV, (h + 1) * GLA_DV)
            oh = o_ref[rows, hl]
            r = lax.rsqrt(jnp.mean(oh * oh, axis=-1, keepdims=True) + EPS)
            heads.append(oh * r * ggla_ref[:, hl])
        y_b = jnp.concatenate(heads, axis=1) * _silu(og_ref[rows, :])
        y = jnp.concatenate([ya_ref[rows, :], y_b.astype(BF16)], axis=1)
        x1_ref[rows, :] = x_ref[rows, :] + m[2:3] * _dot(y, wout_ref[...])

    _for_row_tiles(seq_len, stage3)


def _mixer_call(x, mod3, mod_row_of_seq, states, weights, *, period, has_state_out):
    n_seq, seq_len, _ = x.shape
    has_state_in = states is not None
    kernel = functools.partial(_mixer_kernel, seq_len=seq_len, period=period,
                               has_state_in=has_state_in, has_state_out=has_state_out)
    state_spec = pl.BlockSpec((None, None, GLA_HEADS, GLA_DK, GLA_DV), lambda b: (b, 0, 0, 0, 0))
    const2 = lambda b: (0, 0)
    in_specs = [pl.BlockSpec((None, seq_len, D_MODEL), lambda b: (b, 0, 0)),
                pl.BlockSpec((1, N_MOD, D_MODEL), lambda b: (mod_row_of_seq(b), 0, 0))]
    args = [x, mod3]
    if has_state_in:
        in_specs += [state_spec, state_spec]
        args += list(states)
    in_specs += [pl.BlockSpec(w.shape, const2, pipeline_mode=pl.Buffered(1)) for w in weights]
    args += list(weights)
    out_specs = [pl.BlockSpec((None, seq_len, D_MODEL), lambda b: (b, 0, 0))]
    out_shape = [jax.ShapeDtypeStruct((n_seq, seq_len, D_MODEL), F32)]
    if has_state_out:
        out_specs += [state_spec, state_spec]
        out_shape += [jax.ShapeDtypeStruct((n_seq, 1, GLA_HEADS, GLA_DK, GLA_DV), F32)] * 2
    n_tiles = seq_len // ROW_TILE
    n_pairs = GLA_HEADS // 2
    scratch = [pltpu.VMEM((ROW_TILE, P_PAD), F32),
               pltpu.VMEM((seq_len, GLA_DV_TOT), F32),
               pltpu.VMEM((2, seq_len, GLA_DK_TOT), BF16),
               pltpu.VMEM((2, seq_len, GLA_DK_TOT), BF16),
               pltpu.VMEM((2, n_tiles, GLA_DK_TOT, ROW_TILE), BF16),
               pltpu.VMEM((2, n_tiles, GLA_DK_TOT, LANES), F32),
               pltpu.VMEM((seq_len, GLA_DV_TOT), BF16),
               pltpu.VMEM((2, n_pairs, 2 * GLA_DK, 2 * GLA_DV), F32),
               pltpu.VMEM((n_pairs, seq_len // GLA_CHUNK, 4 * GLA_DK, 2 * GLA_DV), BF16),
               pltpu.VMEM((seq_len, GLA_DV_TOT), F32),
               pltpu.VMEM((seq_len, D_CONV), BF16)]
    return pl.pallas_call(
        kernel,
        grid=(n_seq,),
        in_specs=in_specs,
        out_specs=out_specs,
        out_shape=out_shape,
        scratch_shapes=scratch,
        compiler_params=_params(1),
        name="mixer",
    )(*args)


SUBLANES = 8


def _onehot_rows(rank_ref, experts, cap):
    slot = lax.broadcasted_iota(I32, (cap, 1), 0)
    return jnp.concatenate([rank_ref[e, 0:1, :] == slot for e in experts], axis=0)


def _route_kernel(x1_ref, mod_ref, g2_ref, wrh_ref, wrl_ref, xs_ref, rank_ref, pt_ref,
                  h2_ref, kn_ref, kcol_ref, krow_ref, *, seq_len, cap):
    m = mod_ref[0]
    wrh = wrh_ref[...]
    wrl = wrl_ref[...]
    for t in range(seq_len // ROW_TILE):
        rows = slice(t * ROW_TILE, (t + 1) * ROW_TILE)
        h2 = _modulated_norm(x1_ref[rows, :], g2_ref[...], m[4:5], m[3:4])
        hi, lo = _split(h2)
        h2_ref[rows, :] = hi
        pt_ref[:, rows] = _dot_nt(wrh, hi) + _dot_nt(wrh, lo) + _dot_nt(wrl, hi)

    logits = pt_ref[...]
    ex = jnp.exp(logits - jnp.max(logits, axis=0, keepdims=True))
    probs = ex / jnp.sum(ex, axis=0, keepdims=True)
    pt_ref[...] = probs
    pad = jnp.zeros((LANES - N_EXPERTS, seq_len), F32)
    kn_ref[...] = jnp.transpose(jnp.concatenate([probs, pad], axis=0))
    for e in range(N_EXPERTS):
        kcol_ref[e] = jnp.broadcast_to(kn_ref[:, e:e + 1], (seq_len, LANES))
        krow_ref[e] = jnp.broadcast_to(probs[e:e + 1, :], (SUBLANES, seq_len))

    n_blk = seq_len // LANES

    def rank_expert(e, carry):
        krow = krow_ref[e][0:1, :]
        counts = [jnp.zeros((SUBLANES, LANES), F32) for _ in range(n_blk)]
        for sb in range(n_blk):
            kcol = kcol_ref[e, sb * LANES:(sb + 1) * LANES, :]
            for tb in range(n_blk):
                bar = krow[:, tb * LANES:(tb + 1) * LANES]
                if sb < tb:
                    beats = jnp.where(kcol >= bar, 1.0, 0.0)
                elif sb > tb:
                    beats = jnp.where(kcol > bar, 1.0, 0.0)
                else:
                    earlier = (lax.broadcasted_iota(I32, (LANES, 1), 0) < lax.broadcasted_iota(I32, (1, LANES), 1))
                    beats = jnp.where(kcol > bar, 1.0, 0.0) + jnp.where((kcol == bar) & earlier, 1.0, 0.0)
                counts[tb] = counts[tb] + jnp.sum(beats.reshape(LANES // SUBLANES, SUBLANES, LANES), axis=0)
        cnt = jnp.concatenate([jnp.sum(a, axis=0, keepdims=True) for a in counts], axis=1)
        rank_ref[e] = jnp.broadcast_to(cnt.astype(I32), (SUBLANES, seq_len))
        return carry

    lax.fori_loop(0, N_EXPERTS, rank_expert, 0)

    group = DISPATCH_ROWS // cap
    for gi in range(N_EXPERTS // group):
        oh = _onehot_rows(rank_ref, range(gi * group, (gi + 1) * group), cap)
        ohb = jnp.where(oh, 1.0, 0.0).astype(BF16)
        xs_ref[gi * DISPATCH_ROWS:(gi + 1) * DISPATCH_ROWS, :] = _dot(ohb, h2_ref[...]).astype(BF16)


def _rank_spec(seq_len):
    return pl.BlockSpec((None, N_EXPERTS, SUBLANES, seq_len), lambda b: (b, 0, 0, 0))


def _probs_spec(seq_len):
    return pl.BlockSpec((None, N_EXPERTS, seq_len), lambda b: (b, 0, 0))


def _route_call(x1, mod3, mod_row_of_seq, g2, wr_hi, wr_lo):
    n_seq, seq_len, _ = x1.shape
    cap = EC_CAPACITY_FACTOR * seq_len // N_EXPERTS
    kernel = functools.partial(_route_kernel, seq_len=seq_len, cap=cap)
    const2 = lambda b: (0, 0)
    return pl.pallas_call(
        kernel,
        grid=(n_seq,),
        in_specs=[pl.BlockSpec((None, seq_len, D_MODEL), lambda b: (b, 0, 0)),
                  pl.BlockSpec((1, N_MOD, D_MODEL), lambda b: (mod_row_of_seq(b), 0, 0)),
                  pl.BlockSpec((1, D_MODEL), const2),
                  pl.BlockSpec((N_EXPERTS, D_MODEL), const2),
                  pl.BlockSpec((N_EXPERTS, D_MODEL), const2)],
        out_specs=[pl.BlockSpec((None, N_EXPERTS * cap, D_MODEL), lambda b: (b, 0, 0)),
                   _rank_spec(seq_len), _probs_spec(seq_len)],
        out_shape=[jax.ShapeDtypeStruct((n_seq, N_EXPERTS * cap, D_MODEL), BF16),
                   jax.ShapeDtypeStruct((n_seq, N_EXPERTS, SUBLANES, seq_len), I32),
                   jax.ShapeDtypeStruct((n_seq, N_EXPERTS, seq_len), F32)],
        scratch_shapes=[pltpu.VMEM((seq_len, D_MODEL), BF16),
                        pltpu.VMEM((seq_len, LANES), F32),
                        pltpu.VMEM((N_EXPERTS, seq_len, LANES), F32),
                        pltpu.VMEM((N_EXPERTS, SUBLANES, seq_len), F32)],
        compiler_params=_params(1),
        name="route",
    )(x1, mod3, g2, wr_hi, wr_lo)


def _experts_kernel(xc_ref, xl_ref, wg_ref, wu_ref, wd_ref, yc_ref, yl_ref):
    j = pl.program_id(1)
    wg = wg_ref[...].astype(BF16)
    wu = wu_ref[...].astype(BF16)
    wd = wd_ref[...].astype(BF16)

    def run(x_ref, y_ref):
        n_seq, cap, _ = x_ref.shape
        seqs = DISPATCH_ROWS // cap
        for s0 in range(0, n_seq, seqs):
            x = x_ref[s0:s0 + seqs].reshape(DISPATCH_ROWS, D_MODEL)
            a = (_silu(_dot(x, wg)) * _dot(x, wu)).astype(BF16)
            y = _dot(a, wd).reshape(seqs, cap, D_MODEL)

            @pl.when(j == 0)
            def _():
                y_ref[s0:s0 + seqs] = y

            @pl.when(j != 0)
            def _():
                y_ref[s0:s0 + seqs] = y_ref[s0:s0 + seqs] + y

    run(xc_ref, yc_ref)
    run(xl_ref, yl_ref)


def _experts_call(xs_ctx, xs_lat, w_gate, w_up, w_down):
    tf = D_EXPERT // 2

    def tok_spec(xs):
        n_seq, _, cap, _ = xs.shape
        return pl.BlockSpec((n_seq, None, cap, D_MODEL), lambda e, j: (0, e, 0, 0))

    return pl.pallas_call(
        _experts_kernel,
        grid=(N_EXPERTS, D_EXPERT // tf),
        in_specs=[tok_spec(xs_ctx), tok_spec(xs_lat),
                  pl.BlockSpec((None, D_MODEL, tf), lambda e, j: (e, 0, j)),
                  pl.BlockSpec((None, D_MODEL, tf), lambda e, j: (e, 0, j)),
                  pl.BlockSpec((None, tf, D_MODEL), lambda e, j: (e, j, 0))],
        out_specs=[tok_spec(xs_ctx), tok_spec(xs_lat)],
        out_shape=[jax.ShapeDtypeStruct(xs_ctx.shape, F32), jax.ShapeDtypeStruct(xs_lat.shape, F32)],
        compiler_params=_params(2),
        name="experts",
    )(xs_ctx, xs_lat, w_gate, w_up, w_down)


def _combine_kernel(x1_ref, y_ref, rank_ref, pt_ref, mod_ref, gf_ref, o_ref,
                    moe_ref, oh_ref, yh_ref, yl_ref, *, seq_len, cap):
    m = mod_ref[0]
    group = DISPATCH_ROWS // cap
    slot = lax.broadcasted_iota(I32, (cap, 1), 0)
    for gi in range(N_EXPERTS // group):
        for k in range(group):
            e = gi * group + k
            oh = rank_ref[e, 0:1, :] == slot
            gate = jnp.sum(jnp.where(oh, pt_ref[e:e + 1, :], 0.0), axis=1, keepdims=True)
            hi, lo = _split(y_ref[e] * gate)
            rows = slice(k * cap, (k + 1) * cap)
            oh_ref[rows, :] = jnp.where(oh, 1.0, 0.0).astype(BF16)
            yh_ref[rows, :] = hi
            yl_ref[rows, :] = lo
        for t in range(seq_len // ROW_TILE):
            rows = slice(t * ROW_TILE, (t + 1) * ROW_TILE)
            oh_t = oh_ref[:, rows]
            part = _dot_tn(oh_t, yh_ref[...]) + _dot_tn(oh_t, yl_ref[...])
            if gi == 0:
                moe_ref[rows, :] = part
            else:
                moe_ref[rows, :] = moe_ref[rows, :] + part

    def finish(ti):
        rows = _tile_rows(ti)
        x2 = x1_ref[rows, :] + m[5:6] * moe_ref[rows, :]
        r = lax.rsqrt(jnp.mean(x2 * x2, axis=-1, keepdims=True) + EPS)
        o_ref[rows, :] = (x2 * r) * gf_ref[...]

    _for_row_tiles(seq_len, finish)


def _combine_call(x1, y, rank, probs, mod3, mod_row_of_seq, g_final):
    n_seq, seq_len, _ = x1.shape
    cap = y.shape[2]
    kernel = functools.partial(_combine_kernel, seq_len=seq_len, cap=cap)
    seq_spec = pl.BlockSpec((None, seq_len, D_MODEL), lambda b: (b, 0, 0))
    return pl.pallas_call(
        kernel,
        grid=(n_seq,),
        in_specs=[seq_spec,
                  pl.BlockSpec((None, N_EXPERTS, cap, D_MODEL), lambda b: (b, 0, 0, 0)),
                  _rank_spec(seq_len), _probs_spec(seq_len),
                  pl.BlockSpec((1, N_MOD, D_MODEL), lambda b: (mod_row_of_seq(b), 0, 0)),
                  pl.BlockSpec((1, D_MODEL), lambda b: (0, 0))],
        out_specs=seq_spec,
        out_shape=jax.ShapeDtypeStruct((n_seq, seq_len, D_MODEL), F32),
        scratch_shapes=[pltpu.VMEM((seq_len, D_MODEL), F32),
                        pltpu.VMEM((DISPATCH_ROWS, seq_len), BF16),
                        pltpu.VMEM((DISPATCH_ROWS, D_MODEL), BF16),
                        pltpu.VMEM((DISPATCH_ROWS, D_MODEL), BF16)],
        compiler_params=_params(1),
        name="combine",
    )(x1, y, rank, probs, mod3, g_final)


def kernel(x_prompt, x_sample, state_gla_fwd, state_gla_bwd, c, c_ctx, w_mod, b_mod, g_norm1, g_norm2,
           w_in, w_conv, b_conv, w_a_up_f, b_a_f, w_a_up_b, b_a_b, g_gla_norm, w_out, w_router,
           w_gate, w_up, w_down, g_final):
    assert w_mod.shape[0] == 1, "single trunk layer"
    n_ctx, ctx_len, _ = x_prompt.shape
    n_lat, lat_len, _ = x_sample.shape
    ctx_cap = EC_CAPACITY_FACTOR * ctx_len // N_EXPERTS
    lat_cap = EC_CAPACITY_FACTOR * lat_len // N_EXPERTS

    c_rows = jnp.concatenate([c_ctx[None, :], c, jnp.zeros((8 - 1 - n_lat, D_MODEL), F32)], axis=0)
    mod3 = _mod_call(c_rows, w_mod[0], b_mod).reshape(8, N_MOD, D_MODEL)

    w_in_bf = jnp.pad(w_in[0], ((0, 0), (0, P_PAD - P_TOT))).astype(BF16)
    w_up_gate = jnp.zeros((P_PAD - OFF_ALOW, 2 * GLA_DK_TOT), F32)
    w_up_gate = w_up_gate.at[:GLA_LOW_RANK, :GLA_DK_TOT].set(w_a_up_f[0])
    w_up_gate = w_up_gate.at[GLA_LOW_RANK:2 * GLA_LOW_RANK, GLA_DK_TOT:].set(w_a_up_b[0]).astype(BF16)
    b_up_gate = jnp.concatenate([b_a_f[0], b_a_b[0]])[None, :]
    mixer_weights = (g_norm1, w_in_bf, w_conv[0], b_conv, w_up_gate, b_up_gate,
                     g_gla_norm[0].reshape(1, GLA_DV_TOT), w_out[0].astype(BF16))
    wr_t = jnp.transpose(w_router[0])
    wr_hi = wr_t.astype(BF16)
    wr_lo = (wr_t - wr_hi.astype(F32)).astype(BF16)

    ctx_row = lambda b: 0
    lat_row = lambda b: b + 1

    x1_ctx, new_f, new_b = _mixer_call(x_prompt, mod3, ctx_row, None, mixer_weights,
                                       period=ctx_len, has_state_out=True)
    (x1_lat,) = _mixer_call(x_sample, mod3, lat_row, (state_gla_fwd, state_gla_bwd), mixer_weights,
                            period=GRID_W, has_state_out=False)

    xs_ctx, rank_ctx, pr_ctx = _route_call(x1_ctx, mod3, ctx_row, g_norm2, wr_hi, wr_lo)
    xs_lat, rank_lat, pr_lat = _route_call(x1_lat, mod3, lat_row, g_norm2, wr_hi, wr_lo)

    y_ctx, y_lat = _experts_call(xs_ctx.reshape(n_ctx, N_EXPERTS, ctx_cap, D_MODEL),
                                 xs_lat.reshape(n_lat, N_EXPERTS, lat_cap, D_MODEL),
                                 w_gate[0], w_up[0], w_down[0])

    g_fin = g_final[None, :]
    y_prompt = _combine_call(x1_ctx, y_ctx, rank_ctx, pr_ctx, mod3, ctx_row, g_fin)
    y_sample = _combine_call(x1_lat, y_lat, rank_lat, pr_lat, mod3, lat_row, g_fin)
    return y_prompt, y_sample, new_f, new_b
```

```python
import functools

import jax
import jax.numpy as jnp
from jax import lax
from jax.experimental import pallas as pl
from jax.experimental.pallas import tpu as pltpu

F32 = jnp.float32
BF16 = jnp.bfloat16
I32 = jnp.int32

D_MODEL = 1024
D_CONV = D_MODEL // 2
GRID_W = 64
GLA_HEADS = 4
GLA_DK = 64
GLA_DV = 128
GLA_DK_TOT = GLA_HEADS * GLA_DK
GLA_DV_TOT = GLA_HEADS * GLA_DV
GLA_LOW_RANK = 16
GLA_TAU = 16.0
GLA_CHUNK = 64
N_EXPERTS = 16
EC_CAPACITY_FACTOR = 2
D_EXPERT = 1024
N_MOD = 6
EPS = 1e-6

OFF_XB = 0
OFF_XC = D_CONV
OFF_XV = 2 * D_CONV
OFF_Q = 3 * D_CONV
OFF_K = OFF_Q + GLA_DK_TOT
OFF_V = OFF_K + GLA_DK_TOT
OFF_OG = OFF_V + GLA_DV_TOT
OFF_ALOW = OFF_OG + GLA_DV_TOT
P_TOT = OFF_ALOW + 2 * GLA_LOW_RANK

LANES = 128
P_PAD = -(-P_TOT // LANES) * LANES
ROW_TILE = 256
TOKEN_TILE = 512
DISPATCH_ROWS = 512
VMEM_LIMIT = 56 * 1024 * 1024


def _dot(a, b):
    return jnp.dot(a, b, preferred_element_type=F32)


def _dot_nt(a, b):
    return lax.dot_general(a, b, (((1,), (1,)), ((), ())), preferred_element_type=F32)


def _dot_tn(a, b):
    return lax.dot_general(a, b, (((0,), (0,)), ((), ())), preferred_element_type=F32)


def _split(a):
    hi = a.astype(BF16)
    lo = (a - hi.astype(F32)).astype(BF16)
    return hi, lo


def _silu(x):
    return x * jax.nn.sigmoid(x)


def _modulated_norm(x, g, scale, shift):
    r = lax.rsqrt(jnp.mean(x * x, axis=-1, keepdims=True) + EPS)
    return (x * r) * g * (1.0 + scale) + shift


def _params(n_axes):
    return pltpu.CompilerParams(dimension_semantics=("arbitrary",) * n_axes,
                                vmem_limit_bytes=VMEM_LIMIT)


def _mod_kernel(c_ref, w_ref, b_ref, o_ref):
    s_hi, s_lo = _split(_silu(c_ref[...]))
    w_hi, w_lo = _split(w_ref[...])
    o_ref[...] = _dot(s_hi, w_hi) + _dot(s_hi, w_lo) + _dot(s_lo, w_hi) + b_ref[...]


def _mod_call(c_rows, w_mod, b_mod):
    rows, d = c_rows.shape
    n = w_mod.shape[1]
    tn = D_MODEL
    return pl.pallas_call(
        _mod_kernel,
        grid=(n // tn,),
        in_specs=[pl.BlockSpec((rows, d), lambda j: (0, 0)),
                  pl.BlockSpec((d, tn), lambda j: (0, j)),
                  pl.BlockSpec((1, tn), lambda j: (0, j))],
        out_specs=pl.BlockSpec((rows, tn), lambda j: (0, j)),
        out_shape=jax.ShapeDtypeStruct((rows, n), F32),
        compiler_params=_params(1),
        name="mod",
    )(c_rows, w_mod, b_mod)


def _for_row_tiles(seq_len, fn):
    n = seq_len // ROW_TILE
    if n == 1:
        fn(0)
    else:
        def body(i, carry):
            fn(i)
            return carry
        lax.fori_loop(0, n, body, 0)


def _tile_rows(tile, offset=0, size=ROW_TILE):
    if isinstance(tile, int):
        return pl.ds(tile * ROW_TILE + offset, size)
    return pl.ds(pl.multiple_of(tile * ROW_TILE + offset, size), size)


def _mixer_kernel(*refs, seq_len, period, has_state_in, has_state_out):
    refs = list(refs)
    x_ref, mod_ref = refs[:2]
    del refs[:2]
    if has_state_in:
        s0_refs = refs[:2]
        del refs[:2]
    g1_ref, win_ref, wconv_ref, bconv_ref, wup_ref, bup_ref, ggla_ref, wout_ref, x1_ref = refs[:9]
    del refs[:9]
    if has_state_out:
        sout_refs = refs[:2]
        del refs[:2]
    p_ref, og_ref, qd_ref, kd_ref, kst_ref, dect_ref, v_ref, s_ref, sst_ref, o_ref, ya_ref = refs

    c = GLA_CHUNK
    tile_chunks = ROW_TILE // c
    n_tiles = seq_len // ROW_TILE
    n_pairs = GLA_HEADS // 2
    pair_k = 2 * GLA_DK
    pair_v = 2 * GLA_DV
    m = mod_ref[0]

    def stage1(ti):
        rows = _tile_rows(ti)
        h = _modulated_norm(x_ref[rows, :], g1_ref[...], m[1:2], m[0:1])
        p_ref[...] = _dot(h.astype(BF16), win_ref[...])
        og_ref[rows, :] = p_ref[:, OFF_OG:OFF_OG + GLA_DV_TOT]

        row_i = lax.broadcasted_iota(I32, (ROW_TILE, 1), 0)
        pos = row_i & (period - 1)
        u = p_ref[:, OFF_XC:OFF_XC + D_CONV] * p_ref[:, OFF_XV:OFF_XV + D_CONV]
        u_prev = jnp.where(pos == 0, 0.0, pltpu.roll(u, 1, 0))
        u_next = jnp.where(pos == period - 1, 0.0, pltpu.roll(u, ROW_TILE - 1, 0))
        conv = u_prev * wconv_ref[0:1, :] + u * wconv_ref[1:2, :] + u_next * wconv_ref[2:3, :] + bconv_ref[...]
        ya_ref[rows, :] = (p_ref[:, OFF_XB:OFF_XB + D_CONV] * conv).astype(BF16)

        z = _dot(p_ref[:, OFF_ALOW:P_PAD].astype(BF16), wup_ref[...]) + bup_ref[...]
        la = (jnp.minimum(z, 0.0) - jnp.log(1.0 + jnp.exp(-jnp.abs(z)))) * (1.0 / GLA_TAU)
        col_j = lax.broadcasted_iota(I32, (1, ROW_TILE), 1)
        same_chunk = (row_i & -c) == (col_j & -c)
        lower = jnp.where(same_chunk & (col_j <= row_i), 1.0, 0.0).astype(BF16)
        block = jnp.where(same_chunk, 1.0, 0.0).astype(BF16)
        la_parts = jnp.concatenate(_split(la), axis=1)
        n_gate = 2 * GLA_DK_TOT
        pre = _dot(lower, la_parts)
        pre = pre[:, :n_gate] + pre[:, n_gate:]
        tot = _dot(block, la_parts)
        tot = tot[:, :n_gate] + tot[:, n_gate:]
        q = p_ref[:, OFF_Q:OFF_Q + GLA_DK_TOT] * (GLA_DK ** -0.5)
        k = p_ref[:, OFF_K:OFF_K + GLA_DK_TOT]
        for d in range(2):
            cols = slice(d * GLA_DK_TOT, (d + 1) * GLA_DK_TOT)
            if d == 0:
                bq = pre[:, cols]
                bk = tot[:, cols] - bq
            else:
                bk = pre[:, cols] - la[:, cols]
                bq = tot[:, cols] - bk
            qd_ref[d, rows, :] = (q * jnp.exp(bq)).astype(BF16)
            kd_ref[d, rows, :] = (k * jnp.exp(-bq)).astype(BF16)
            kst_ref[d, ti] = jnp.transpose(k * jnp.exp(bk)).astype(BF16)
            totals = [tot[n * c:n * c + 1, cols] for n in range(tile_chunks)]
            totals.append(jnp.zeros((LANES - tile_chunks, GLA_DK_TOT), F32))
            dect_ref[d, ti] = jnp.transpose(jnp.exp(jnp.concatenate(totals, axis=0)))
        v_ref[rows, :] = p_ref[:, OFF_V:OFF_V + GLA_DV_TOT].astype(BF16)

    _for_row_tiles(seq_len, stage1)

    for d in range(2):
        for pair in range(n_pairs):
            if has_state_in:
                zero = jnp.zeros((GLA_DK, GLA_DV), F32)
                top = jnp.concatenate([s0_refs[d][2 * pair], zero], axis=1)
                bot = jnp.concatenate([zero, s0_refs[d][2 * pair + 1]], axis=1)
                s_ref[d, pair] = jnp.concatenate([top, bot], axis=0)
            else:
                s_ref[d, pair] = jnp.zeros((pair_k, pair_v), F32)

    def scan_tile(i):
        upper_lane = lax.broadcasted_iota(I32, (1, LANES), 1) >= GLA_DK
        qi = lax.broadcasted_iota(I32, (LANES, 1), 0)
        kj = lax.broadcasted_iota(I32, (1, 2 * LANES), 1) & (LANES - 1)
        same_chunk = (qi & c) == (kj & c)
        causal = (same_chunk & (kj <= qi), same_chunk & (kj >= qi))
        for pair in range(n_pairs):
            kl = slice(pair * pair_k, (pair + 1) * pair_k)
            vl = slice(pair * pair_v, (pair + 1) * pair_v)
            for blk in range(ROW_TILE // LANES):
                rows = _tile_rows(i, blk * LANES, LANES)
                att = None
                for d in range(2):
                    kd = kd_ref[d, rows, kl]
                    zk = jnp.zeros_like(kd)
                    keys = jnp.concatenate([jnp.where(upper_lane, zk, kd), jnp.where(upper_lane, kd, zk)], axis=0)
                    a = jnp.where(causal[d], _dot_nt(qd_ref[d, rows, kl], keys), 0.0)
                    att = a if att is None else att + a
                v = v_ref[rows, vl]
                zv = jnp.zeros((LANES, GLA_DV), BF16)
                v_bd = jnp.concatenate([jnp.concatenate([v[:, :GLA_DV], zv], axis=1),
                                        jnp.concatenate([zv, v[:, GLA_DV:]], axis=1)], axis=0)
                o_ref[rows, vl] = _dot(att.astype(BF16), v_bd)
        key_row = lax.broadcasted_iota(I32, (pair_k, 1), 0)
        val_col = lax.broadcasted_iota(I32, (1, pair_v), 1)
        blockdiag = (key_row >= GLA_DK) == (val_col >= GLA_DV)
        for d in range(2):
            tile = i if d == 0 else n_tiles - 1 - i
            chunks = range(tile_chunks)
            for pair in range(n_pairs):
                kr = slice(pair * pair_k, (pair + 1) * pair_k)
                vl = slice(pair * pair_v, (pair + 1) * pair_v)
                s = s_ref[d, pair]
                for c4 in (chunks if d == 0 else reversed(chunks)):
                    blk, half = divmod(c4, 2)
                    kst = kst_ref[d, tile, kr, blk * LANES:(blk + 1) * LANES]
                    kst = jnp.where(upper_lane if half else ~upper_lane, kst, jnp.zeros_like(kst))
                    kv = jnp.where(blockdiag, _dot(kst, v_ref[_tile_rows(tile, blk * LANES, LANES), vl]), 0.0)
                    sst_ref[pair, tile * tile_chunks + c4, d * pair_k:(d + 1) * pair_k, :] = s.astype(BF16)
                    s = dect_ref[d, tile, kr, c4:c4 + 1] * s + kv
                s_ref[d, pair] = s

    _for_row_tiles(seq_len, scan_tile)

    if has_state_out:
        for d in range(2):
            for pair in range(n_pairs):
                s = s_ref[d, pair]
                sout_refs[d][2 * pair] = s[0:GLA_DK, 0:GLA_DV]
                sout_refs[d][2 * pair + 1] = s[GLA_DK:, GLA_DV:]

    def stage3(i):
        for pair in range(n_pairs):
            kl = slice(pair * pair_k, (pair + 1) * pair_k)
            vl = slice(pair * pair_v, (pair + 1) * pair_v)
            for c4 in range(tile_chunks):
                crow = _tile_rows(i, c4 * c, c)
                q2 = jnp.concatenate([qd_ref[0, crow, kl], qd_ref[1, crow, kl]], axis=1)
                o_ref[crow, vl] = o_ref[crow, vl] + _dot(q2, sst_ref[pair, i * tile_chunks + c4])
        rows = _tile_rows(i)
        heads = []
        for h in range(GLA_HEADS):
            hl = slice(h * GLA_DV, (h + 1) * GLA_DV)
            oh = o_ref[rows, hl]
            r = lax.rsqrt(jnp.mean(oh * oh, axis=-1, keepdims=True) + EPS)
            heads.append(oh * r * ggla_ref[:, hl])
        y_b = jnp.concatenate(heads, axis=1) * _silu(og_ref[rows, :])
        y = jnp.concatenate([ya_ref[rows, :], y_b.astype(BF16)], axis=1)
        x1_ref[rows, :] = x_ref[rows, :] + m[2:3] * _dot(y, wout_ref[...])

    _for_row_tiles(seq_len, stage3)


def _mixer_call(x, mod3, mod_row_of_seq, states, weights, *, period, has_state_out):
    n_seq, seq_len, _ = x.shape
    has_state_in = states is not None
    kernel = functools.partial(_mixer_kernel, seq_len=seq_len, period=period,
                               has_state_in=has_state_in, has_state_out=has_state_out)
    state_spec = pl.BlockSpec((None, None, GLA_HEADS, GLA_DK, GLA_DV), lambda b: (b, 0, 0, 0, 0))
    const2 = lambda b: (0, 0)
    in_specs = [pl.BlockSpec((None, seq_len, D_MODEL), lambda b: (b, 0, 0)),
                pl.BlockSpec((1, N_MOD, D_MODEL), lambda b: (mod_row_of_seq(b), 0, 0))]
    args = [x, mod3]
    if has_state_in:
        in_specs += [state_spec, state_spec]
        args += list(states)
    in_specs += [pl.BlockSpec(w.shape, const2, pipeline_mode=pl.Buffered(1)) for w in weights]
    args += list(weights)
    out_specs = [pl.BlockSpec((None, seq_len, D_MODEL), lambda b: (b, 0, 0))]
    out_shape = [jax.ShapeDtypeStruct((n_seq, seq_len, D_MODEL), F32)]
    if has_state_out:
        out_specs += [state_spec, state_spec]
        out_shape += [jax.ShapeDtypeStruct((n_seq, 1, GLA_HEADS, GLA_DK, GLA_DV), F32)] * 2
    n_tiles = seq_len // ROW_TILE
    n_pairs = GLA_HEADS // 2
    scratch = [pltpu.VMEM((ROW_TILE, P_PAD), F32),
               pltpu.VMEM((seq_len, GLA_DV_TOT), F32),
               pltpu.VMEM((2, seq_len, GLA_DK_TOT), BF16),
               pltpu.VMEM((2, seq_len, GLA_DK_TOT), BF16),
               pltpu.VMEM((2, n_tiles, GLA_DK_TOT, ROW_TILE), BF16),
               pltpu.VMEM((2, n_tiles, GLA_DK_TOT, LANES), F32),
               pltpu.VMEM((seq_len, GLA_DV_TOT), BF16),
               pltpu.VMEM((2, n_pairs, 2 * GLA_DK, 2 * GLA_DV), F32),
               pltpu.VMEM((n_pairs, seq_len // GLA_CHUNK, 4 * GLA_DK, 2 * GLA_DV), BF16),
               pltpu.VMEM((seq_len, GLA_DV_TOT), F32),
               pltpu.VMEM((seq_len, D_CONV), BF16)]
    return pl.pallas_call(
        kernel,
        grid=(n_seq,),
        in_specs=in_specs,
        out_specs=out_specs,
        out_shape=out_shape,
        scratch_shapes=scratch,
        compiler_params=_params(1),
        name="mixer",
    )(*args)


SUBLANES = 8


def _route_kernel(x1_ref, mod_ref, g2_ref, wrh_ref, wrl_ref, xs_ref, rank_ref, gate_ref,
                  pt_ref, h2_ref, kn_ref, kcol_ref, krow_ref, *, seq_len, cap):
    m = mod_ref[0]
    wrh = wrh_ref[...]
    wrl = wrl_ref[...]
    for t in range(seq_len // ROW_TILE):
        rows = slice(t * ROW_TILE, (t + 1) * ROW_TILE)
        h2 = _modulated_norm(x1_ref[rows, :], g2_ref[...], m[4:5], m[3:4])
        hi, lo = _split(h2)
        h2_ref[rows, :] = hi
        pt_ref[:, rows] = _dot_nt(wrh, hi) + _dot_nt(wrh, lo) + _dot_nt(wrl, hi)

    logits = pt_ref[...]
    ex = jnp.exp(logits - jnp.max(logits, axis=0, keepdims=True))
    probs = ex / jnp.sum(ex, axis=0, keepdims=True)
    pt_ref[...] = probs
    pad = jnp.zeros((LANES - N_EXPERTS, seq_len), F32)
    kn_ref[...] = jnp.transpose(jnp.concatenate([probs, pad], axis=0))
    for e in range(N_EXPERTS):
        kcol_ref[e] = jnp.broadcast_to(kn_ref[:, e:e + 1], (seq_len, LANES))
        krow_ref[e] = jnp.broadcast_to(probs[e:e + 1, :], (SUBLANES, seq_len))

    n_blk = seq_len // LANES

    def rank_expert(e, carry):
        krow = krow_ref[e][0:1, :]
        counts = [jnp.zeros((SUBLANES, LANES), F32) for _ in range(n_blk)]
        for sb in range(n_blk):
            kcol = kcol_ref[e, sb * LANES:(sb + 1) * LANES, :]
            for tb in range(n_blk):
                bar = krow[:, tb * LANES:(tb + 1) * LANES]
                if sb < tb:
                    beats = jnp.where(kcol >= bar, 1.0, 0.0)
                elif sb > tb:
                    beats = jnp.where(kcol > bar, 1.0, 0.0)
                else:
                    earlier = (lax.broadcasted_iota(I32, (LANES, 1), 0) < lax.broadcasted_iota(I32, (1, LANES), 1))
                    beats = jnp.where(kcol > bar, 1.0, 0.0) + jnp.where((kcol == bar) & earlier, 1.0, 0.0)
                counts[tb] = counts[tb] + jnp.sum(beats.reshape(LANES // SUBLANES, SUBLANES, LANES), axis=0)
        cnt = jnp.concatenate([jnp.sum(a, axis=0, keepdims=True) for a in counts], axis=1)
        rank_ref[e] = jnp.broadcast_to(cnt.astype(I32), (SUBLANES, seq_len))
        return carry

    lax.fori_loop(0, N_EXPERTS, rank_expert, 0)

    group = DISPATCH_ROWS // cap
    slot = lax.broadcasted_iota(I32, (cap, 1), 0)
    for gi in range(N_EXPERTS // group):
        picks = []
        for e in range(gi * group, (gi + 1) * group):
            oh = rank_ref[e, 0:1, :] == slot
            gate = jnp.sum(jnp.where(oh, pt_ref[e:e + 1, :], 0.0), axis=1, keepdims=True)
            gate_ref[e * cap:(e + 1) * cap, :] = jnp.broadcast_to(gate, (cap, LANES))
            picks.append(oh)
        ohb = jnp.where(jnp.concatenate(picks, axis=0), 1.0, 0.0).astype(BF16)
        xs_ref[gi * DISPATCH_ROWS:(gi + 1) * DISPATCH_ROWS, :] = _dot(ohb, h2_ref[...]).astype(BF16)


def _rank_spec(seq_len):
    return pl.BlockSpec((None, N_EXPERTS, SUBLANES, seq_len), lambda b: (b, 0, 0, 0))


def _route_call(x1, mod3, mod_row_of_seq, g2, wr_hi, wr_lo):
    n_seq, seq_len, _ = x1.shape
    cap = EC_CAPACITY_FACTOR * seq_len // N_EXPERTS
    kernel = functools.partial(_route_kernel, seq_len=seq_len, cap=cap)
    const2 = lambda b: (0, 0)
    return pl.pallas_call(
        kernel,
        grid=(n_seq,),
        in_specs=[pl.BlockSpec((None, seq_len, D_MODEL), lambda b: (b, 0, 0)),
                  pl.BlockSpec((1, N_MOD, D_MODEL), lambda b: (mod_row_of_seq(b), 0, 0)),
                  pl.BlockSpec((1, D_MODEL), const2),
                  pl.BlockSpec((N_EXPERTS, D_MODEL), const2),
                  pl.BlockSpec((N_EXPERTS, D_MODEL), const2)],
        out_specs=[pl.BlockSpec((None, N_EXPERTS * cap, D_MODEL), lambda b: (b, 0, 0)),
                   _rank_spec(seq_len),
                   pl.BlockSpec((None, N_EXPERTS * cap, LANES), lambda b: (b, 0, 0))],
        out_shape=[jax.ShapeDtypeStruct((n_seq, N_EXPERTS * cap, D_MODEL), BF16),
                   jax.ShapeDtypeStruct((n_seq, N_EXPERTS, SUBLANES, seq_len), I32),
                   jax.ShapeDtypeStruct((n_seq, N_EXPERTS * cap, LANES), F32)],
        scratch_shapes=[pltpu.VMEM((N_EXPERTS, seq_len), F32),
                        pltpu.VMEM((seq_len, D_MODEL), BF16),
                        pltpu.VMEM((seq_len, LANES), F32),
                        pltpu.VMEM((N_EXPERTS, seq_len, LANES), F32),
                        pltpu.VMEM((N_EXPERTS, SUBLANES, seq_len), F32)],
        compiler_params=_params(1),
        name="route",
    )(x1, mod3, g2, wr_hi, wr_lo)


def _experts_kernel(xc_ref, xl_ref, gc_ref, gl_ref, wg_ref, wu_ref, wd_ref, yc_ref, yl_ref,
                    wgb_ref, wub_ref, wdb_ref):
    wgb_ref[...] = wg_ref[...].astype(BF16)
    wub_ref[...] = wu_ref[...].astype(BF16)
    wdb_ref[...] = wd_ref[...].astype(BF16)

    def run(x_ref, g_ref, y_ref):
        n_seq, cap, _ = x_ref.shape
        seqs = DISPATCH_ROWS // cap
        for s0 in range(0, n_seq, seqs):
            x = x_ref[s0:s0 + seqs].reshape(DISPATCH_ROWS, D_MODEL)
            a = (_silu(_dot(x, wgb_ref[...])) * _dot(x, wub_ref[...])).astype(BF16)
            gate = g_ref[s0:s0 + seqs].reshape(DISPATCH_ROWS, LANES)
            y = _dot(a, wdb_ref[...]) * jnp.concatenate([gate] * (D_MODEL // LANES), axis=1)
            y_ref[s0:s0 + seqs] = y.astype(BF16).reshape(seqs, cap, D_MODEL)

    run(xc_ref, gc_ref, yc_ref)
    run(xl_ref, gl_ref, yl_ref)


def _experts_call(xs_ctx, xs_lat, gates_ctx, gates_lat, w_gate, w_up, w_down):
    def slot_spec(a):
        n_seq, _, cap, width = a.shape
        return pl.BlockSpec((n_seq, None, cap, width), lambda e: (0, e, 0, 0))

    w_spec = pl.BlockSpec((None, D_MODEL, D_EXPERT), lambda e: (e, 0, 0))
    return pl.pallas_call(
        _experts_kernel,
        grid=(N_EXPERTS,),
        in_specs=[slot_spec(xs_ctx), slot_spec(xs_lat), slot_spec(gates_ctx), slot_spec(gates_lat),
                  w_spec, w_spec, pl.BlockSpec((None, D_EXPERT, D_MODEL), lambda e: (e, 0, 0))],
        out_specs=[slot_spec(xs_ctx), slot_spec(xs_lat)],
        out_shape=[jax.ShapeDtypeStruct(xs_ctx.shape, BF16), jax.ShapeDtypeStruct(xs_lat.shape, BF16)],
        scratch_shapes=[pltpu.VMEM((D_MODEL, D_EXPERT), BF16),
                        pltpu.VMEM((D_MODEL, D_EXPERT), BF16),
                        pltpu.VMEM((D_EXPERT, D_MODEL), BF16)],
        compiler_params=_params(1),
        name="experts",
    )(xs_ctx, xs_lat, gates_ctx, gates_lat, w_gate, w_up, w_down)


def _combine_kernel(x1_ref, y_ref, rank_ref, mod_ref, gf_ref, o_ref, oh_ref, *, seq_len, cap):
    m = mod_ref[0]
    slot = lax.broadcasted_iota(I32, (cap, 1), 0)
    for e in range(N_EXPERTS):
        oh_ref[e * cap:(e + 1) * cap, :] = jnp.where(rank_ref[e, 0:1, :] == slot, 1.0, 0.0).astype(BF16)
    for t in range(seq_len // ROW_TILE):
        rows = slice(t * ROW_TILE, (t + 1) * ROW_TILE)
        moe = _dot_tn(oh_ref[:, rows], y_ref[...])
        x2 = x1_ref[rows, :] + m[5:6] * moe
        r = lax.rsqrt(jnp.mean(x2 * x2, axis=-1, keepdims=True) + EPS)
        o_ref[rows, :] = (x2 * r) * gf_ref[...]


def _combine_call(x1, y, rank, mod3, mod_row_of_seq, g_final):
    n_seq, seq_len, _ = x1.shape
    n_slots = y.shape[1]
    kernel = functools.partial(_combine_kernel, seq_len=seq_len, cap=n_slots // N_EXPERTS)
    seq_spec = pl.BlockSpec((None, seq_len, D_MODEL), lambda b: (b, 0, 0))
    return pl.pallas_call(
        kernel,
        grid=(n_seq,),
        in_specs=[seq_spec,
                  pl.BlockSpec((None, n_slots, D_MODEL), lambda b: (b, 0, 0)),
                  _rank_spec(seq_len),
                  pl.BlockSpec((1, N_MOD, D_MODEL), lambda b: (mod_row_of_seq(b), 0, 0)),
                  pl.BlockSpec((1, D_MODEL), lambda b: (0, 0))],
        out_specs=seq_spec,
        out_shape=jax.ShapeDtypeStruct((n_seq, seq_len, D_MODEL), F32),
        scratch_shapes=[pltpu.VMEM((n_slots, seq_len), BF16)],
        compiler_params=_params(1),
        name="combine",
    )(x1, y, rank, mod3, g_final)


def kernel(x_prompt, x_sample, state_gla_fwd, state_gla_bwd, c, c_ctx, w_mod, b_mod, g_norm1, g_norm2,
           w_in, w_conv, b_conv, w_a_up_f, b_a_f, w_a_up_b, b_a_b, g_gla_norm, w_out, w_router,
           w_gate, w_up, w_down, g_final):
    assert w_mod.shape[0] == 1, "single trunk layer"
    n_ctx, ctx_len, _ = x_prompt.shape
    n_lat, lat_len, _ = x_sample.shape
    ctx_cap = EC_CAPACITY_FACTOR * ctx_len // N_EXPERTS
    lat_cap = EC_CAPACITY_FACTOR * lat_len // N_EXPERTS

    c_rows = jnp.concatenate([c_ctx[None, :], c, jnp.zeros((8 - 1 - n_lat, D_MODEL), F32)], axis=0)
    mod3 = _mod_call(c_rows, w_mod[0], b_mod).reshape(8, N_MOD, D_MODEL)

    w_in_bf = jnp.pad(w_in[0], ((0, 0), (0, P_PAD - P_TOT))).astype(BF16)
    w_up_gate = jnp.zeros((P_PAD - OFF_ALOW, 2 * GLA_DK_TOT), F32)
    w_up_gate = w_up_gate.at[:GLA_LOW_RANK, :GLA_DK_TOT].set(w_a_up_f[0])
    w_up_gate = w_up_gate.at[GLA_LOW_RANK:2 * GLA_LOW_RANK, GLA_DK_TOT:].set(w_a_up_b[0]).astype(BF16)
    b_up_gate = jnp.concatenate([b_a_f[0], b_a_b[0]])[None, :]
    mixer_weights = (g_norm1, w_in_bf, w_conv[0], b_conv, w_up_gate, b_up_gate,
                     g_gla_norm[0].reshape(1, GLA_DV_TOT), w_out[0].astype(BF16))
    wr_t = jnp.transpose(w_router[0])
    wr_hi = wr_t.astype(BF16)
    wr_lo = (wr_t - wr_hi.astype(F32)).astype(BF16)

    ctx_row = lambda b: 0
    lat_row = lambda b: b + 1

    x1_ctx, new_f, new_b = _mixer_call(x_prompt, mod3, ctx_row, None, mixer_weights,
                                       period=ctx_len, has_state_out=True)
    (x1_lat,) = _mixer_call(x_sample, mod3, lat_row, (state_gla_fwd, state_gla_bwd), mixer_weights,
                            period=GRID_W, has_state_out=False)

    xs_ctx, rank_ctx, gates_ctx = _route_call(x1_ctx, mod3, ctx_row, g_norm2, wr_hi, wr_lo)
    xs_lat, rank_lat, gates_lat = _route_call(x1_lat, mod3, lat_row, g_norm2, wr_hi, wr_lo)

    per_expert = lambda a, n, cap: a.reshape(n, N_EXPERTS, cap, a.shape[-1])
    y_ctx, y_lat = _experts_call(per_expert(xs_ctx, n_ctx, ctx_cap), per_expert(xs_lat, n_lat, lat_cap),
                                 per_expert(gates_ctx, n_ctx, ctx_cap), per_expert(gates_lat, n_lat, lat_cap),
                                 w_gate[0], w_up[0], w_down[0])

    g_fin = g_final[None, :]
    y_prompt = _combine_call(x1_ctx, y_ctx.reshape(xs_ctx.shape), rank_ctx, mod3, ctx_row, g_fin)
    y_sample = _combine_call(x1_lat, y_lat.reshape(xs_lat.shape), rank_lat, mod3, lat_row, g_fin)
    return y_prompt, y_sample, new_f, new_b
```

```python
import functools

import jax
import jax.numpy as jnp
from jax import lax
from jax.experimental import pallas as pl
from jax.experimental.pallas import tpu as pltpu

F32 = jnp.float32
BF16 = jnp.bfloat16
I32 = jnp.int32

D_MODEL = 1024
D_CONV = D_MODEL // 2
GRID_W = 64
GLA_HEADS = 4
GLA_DK = 64
GLA_DV = 128
GLA_DK_TOT = GLA_HEADS * GLA_DK
GLA_DV_TOT = GLA_HEADS * GLA_DV
GLA_LOW_RANK = 16
GLA_TAU = 16.0
GLA_CHUNK = 64
N_EXPERTS = 16
EC_CAPACITY_FACTOR = 2
D_EXPERT = 1024
N_MOD = 6
EPS = 1e-6

OFF_XB = 0
OFF_XC = D_CONV
OFF_XV = 2 * D_CONV
OFF_Q = 3 * D_CONV
OFF_K = OFF_Q + GLA_DK_TOT
OFF_V = OFF_K + GLA_DK_TOT
OFF_OG = OFF_V + GLA_DV_TOT
OFF_ALOW = OFF_OG + GLA_DV_TOT
P_TOT = OFF_ALOW + 2 * GLA_LOW_RANK

LANES = 128
P_PAD = -(-P_TOT // LANES) * LANES
ROW_TILE = 256
TOKEN_TILE = 512
DISPATCH_ROWS = 512
VMEM_LIMIT = 56 * 1024 * 1024


def _dot(a, b):
    return jnp.dot(a, b, preferred_element_type=F32)


def _dot_nt(a, b):
    return lax.dot_general(a, b, (((1,), (1,)), ((), ())), preferred_element_type=F32)


def _dot_tn(a, b):
    return lax.dot_general(a, b, (((0,), (0,)), ((), ())), preferred_element_type=F32)


def _split(a):
    hi = a.astype(BF16)
    lo = (a - hi.astype(F32)).astype(BF16)
    return hi, lo


def _silu(x):
    return x * jax.nn.sigmoid(x)


def _modulated_norm(x, g, scale, shift):
    r = lax.rsqrt(jnp.mean(x * x, axis=-1, keepdims=True) + EPS)
    return (x * r) * (g * (1.0 + scale)) + shift


def _params(n_axes):
    return pltpu.CompilerParams(dimension_semantics=("arbitrary",) * n_axes,
                                vmem_limit_bytes=VMEM_LIMIT)


def _mod_kernel(c_ref, w_ref, b_ref, o_ref):
    rows = c_ref.shape[0]
    s = _silu(c_ref[...])
    s_hi, s_lo = _split(jnp.concatenate([s, s], axis=0))
    upper = lax.broadcasted_iota(I32, (2 * rows, 1), 0) < rows
    w_hi, w_lo = _split(w_ref[...])
    by_hi = _dot(jnp.where(upper, s_hi, s_lo), w_hi)
    o_ref[...] = by_hi[:rows] + by_hi[rows:] + _dot(s_hi[:rows], w_lo) + b_ref[...]


def _mod_call(c_rows, w_mod, b_mod):
    rows, d = c_rows.shape
    n = w_mod.shape[1]
    tn = D_MODEL
    return pl.pallas_call(
        _mod_kernel,
        grid=(n // tn,),
        in_specs=[pl.BlockSpec((rows, d), lambda j: (0, 0)),
                  pl.BlockSpec((d, tn), lambda j: (0, j)),
                  pl.BlockSpec((1, tn), lambda j: (0, j))],
        out_specs=pl.BlockSpec((rows, tn), lambda j: (0, j)),
        out_shape=jax.ShapeDtypeStruct((rows, n), F32),
        compiler_params=_params(1),
        name="mod",
    )(c_rows, w_mod, b_mod)


def _for_row_tiles(seq_len, phases):
    n = seq_len // ROW_TILE
    if n == 1:
        yield from phases(0)
    else:
        def body(i, carry):
            for _ in phases(i):
                pass
            return carry
        lax.fori_loop(0, n, body, 0)
        yield


def _tile_rows(tile, offset=0, size=ROW_TILE):
    if isinstance(tile, int):
        return pl.ds(tile * ROW_TILE + offset, size)
    return pl.ds(pl.multiple_of(tile * ROW_TILE + offset, size), size)


def _mixer_kernel(*refs, seqs_per_step, has_state_in, has_state_out, stages_w_in, **static):
    refs = list(refs)
    n_in = 2 + (2 if has_state_in else 0)
    per_seq_in, refs = [refs[0]] + refs[2:n_in], [refs[1]] + refs[n_in:]
    mod_ref, weights, refs = refs[0], refs[1:9], refs[9:]
    n_out = 1 + (2 if has_state_out else 0)
    per_seq_out, refs = refs[:n_out], refs[n_out:]
    if stages_w_in:
        win_f32_ref, win_bf_ref, scratch = weights[1], refs[0], refs[1:]
        weights = weights[:1] + [win_bf_ref] + weights[2:]

        @pl.when(pl.program_id(0) == 0)
        def _():
            win_bf_ref[:, P_TOT:] = jnp.zeros((D_MODEL, P_PAD - P_TOT), BF16)

            def cast_rows(i, carry):
                r = pl.ds(pl.multiple_of(i * LANES, LANES), LANES)
                win_bf_ref[r, :P_TOT] = win_f32_ref[r, :].astype(BF16)
                return carry
            lax.fori_loop(0, D_MODEL // LANES, cast_rows, 0)
    else:
        scratch = refs
    programs = []
    for j in range(seqs_per_step):
        ins = [r.at[j] for r in per_seq_in]
        outs = [r.at[j] for r in per_seq_out]
        programs.append(_mixer_sequence(ins[0], mod_ref, ins[1:], weights, outs[0], outs[1:],
                                        [r.at[j] for r in scratch], **static))
    while programs:
        programs = [p for p in programs if next(p, "done") != "done"]


def _mixer_sequence(x_ref, mod_ref, s0_refs, weights, x1_ref, sout_refs, scratch, *, seq_len, period):
    has_state_in = bool(s0_refs)
    has_state_out = bool(sout_refs)
    g1_ref, win_ref, wconv_ref, bconv_ref, wup_ref, bup_ref, ggla_ref, wout_ref = weights
    og_ref, qd_ref, kd_ref, kst_ref, dect_ref, v_ref, s_ref, sst_ref, o_ref, ya_ref = scratch

    c = GLA_CHUNK
    tile_chunks = ROW_TILE // c
    n_tiles = seq_len // ROW_TILE
    n_pairs = GLA_HEADS // 2
    pair_k = 2 * GLA_DK
    pair_v = 2 * GLA_DV
    m = mod_ref[0]

    def stage1(ti):
        rows = _tile_rows(ti)
        h = _modulated_norm(x_ref[rows, :], g1_ref[...], m[1:2], m[0:1]).astype(BF16)
        row_i = lax.broadcasted_iota(I32, (ROW_TILE, 1), 0)

        p_gate = _dot(h, win_ref[:, OFF_OG:P_PAD])
        og_ref[rows, :] = p_gate[:, :GLA_DV_TOT]
        z = _dot(p_gate[:, GLA_DV_TOT:].astype(BF16), wup_ref[...]) + bup_ref[...]
        la = (jnp.minimum(z, 0.0) - jnp.log(1.0 + jnp.exp(-jnp.abs(z)))) * (1.0 / GLA_TAU)
        col_j = lax.broadcasted_iota(I32, (1, ROW_TILE), 1)
        same_chunk = (row_i & -c) == (col_j & -c)
        lower = jnp.where(same_chunk & (col_j <= row_i), 1.0, 0.0).astype(BF16)
        la_parts = jnp.concatenate(_split(la), axis=1)
        n_gate = 2 * GLA_DK_TOT
        pre = _dot(lower, la_parts)
        pre = pre[:, :n_gate] + pre[:, n_gate:]
        tot = jnp.concatenate([jnp.broadcast_to(pre[(n + 1) * c - 1:(n + 1) * c], (c, n_gate))
                               for n in range(tile_chunks)], axis=0)
        yield
        p_qkv = _dot(h, win_ref[:, OFF_Q:OFF_OG])
        q = p_qkv[:, :GLA_DK_TOT] * (GLA_DK ** -0.5)
        k = p_qkv[:, GLA_DK_TOT:2 * GLA_DK_TOT]
        v_ref[rows, :] = p_qkv[:, 2 * GLA_DK_TOT:].astype(BF16)
        for d in range(2):
            cols = slice(d * GLA_DK_TOT, (d + 1) * GLA_DK_TOT)
            if d == 0:
                bq = pre[:, cols]
                bk = tot[:, cols] - bq
            else:
                bk = pre[:, cols] - la[:, cols]
                bq = tot[:, cols] - bk
            qd_ref[d, rows, :] = (q * jnp.exp(bq)).astype(BF16)
            kd_ref[d, rows, :] = (k * jnp.exp(-bq)).astype(BF16)
            kst_ref[d, ti] = jnp.transpose(k * jnp.exp(bk)).astype(BF16)
            totals = [tot[n * c:n * c + 1, cols] for n in range(tile_chunks)]
            totals.append(jnp.zeros((LANES - tile_chunks, GLA_DK_TOT), F32))
            dect_ref[d, ti] = jnp.transpose(jnp.exp(jnp.concatenate(totals, axis=0)))
        yield
        p_conv = _dot(h, win_ref[:, :OFF_Q])
        pos = row_i & (period - 1)
        u = p_conv[:, OFF_XC:OFF_XC + D_CONV] * p_conv[:, OFF_XV:OFF_XV + D_CONV]
        u_prev = jnp.where(pos == 0, 0.0, pltpu.roll(u, 1, 0))
        u_next = jnp.where(pos == period - 1, 0.0, pltpu.roll(u, ROW_TILE - 1, 0))
        conv = u_prev * wconv_ref[0:1, :] + u * wconv_ref[1:2, :] + u_next * wconv_ref[2:3, :] + bconv_ref[...]
        ya_ref[rows, :] = (p_conv[:, OFF_XB:OFF_XB + D_CONV] * conv).astype(BF16)
        yield

    yield from _for_row_tiles(seq_len, stage1)

    for d in range(2):
        for pair in range(n_pairs):
            if has_state_in:
                zero = jnp.zeros((GLA_DK, GLA_DV), F32)
                top = jnp.concatenate([s0_refs[d][2 * pair], zero], axis=1)
                bot = jnp.concatenate([zero, s0_refs[d][2 * pair + 1]], axis=1)
                s_ref[d, pair] = jnp.concatenate([top, bot], axis=0)
            else:
                s_ref[d, pair] = jnp.zeros((pair_k, pair_v), F32)

    def scan_tile(i):
        upper_lane = lax.broadcasted_iota(I32, (1, LANES), 1) >= GLA_DK
        qi = lax.broadcasted_iota(I32, (LANES, 1), 0)
        kj = lax.broadcasted_iota(I32, (1, 2 * LANES), 1) & (LANES - 1)
        same_chunk = (qi & c) == (kj & c)
        causal = (same_chunk & (kj <= qi), same_chunk & (kj >= qi))
        for pair in range(n_pairs):
            kl = slice(pair * pair_k, (pair + 1) * pair_k)
            vl = slice(pair * pair_v, (pair + 1) * pair_v)
            for blk in range(ROW_TILE // LANES):
                rows = _tile_rows(i, blk * LANES, LANES)
                att = None
                for d in range(2):
                    kd = kd_ref[d, rows, kl]
                    zk = jnp.zeros_like(kd)
                    keys = jnp.concatenate([jnp.where(upper_lane, zk, kd), jnp.where(upper_lane, kd, zk)], axis=0)
                    a = jnp.where(causal[d], _dot_nt(qd_ref[d, rows, kl], keys), 0.0)
                    att = a if att is None else att + a
                v = v_ref[rows, vl]
                zv = jnp.zeros((LANES, GLA_DV), BF16)
                v_bd = jnp.concatenate([jnp.concatenate([v[:, :GLA_DV], zv], axis=1),
                                        jnp.concatenate([zv, v[:, GLA_DV:]], axis=1)], axis=0)
                o_ref[rows, vl] = _dot(att.astype(BF16), v_bd)
        yield
        key_row = lax.broadcasted_iota(I32, (pair_k, 1), 0)
        val_col = lax.broadcasted_iota(I32, (1, pair_v), 1)
        blockdiag = (key_row >= GLA_DK) == (val_col >= GLA_DV)
        for d in range(2):
            tile = i if d == 0 else n_tiles - 1 - i
            chunks = range(tile_chunks)
            for pair in range(n_pairs):
                kr = slice(pair * pair_k, (pair + 1) * pair_k)
                vl = slice(pair * pair_v, (pair + 1) * pair_v)
                s = s_ref[d, pair]
                for c4 in (chunks if d == 0 else reversed(chunks)):
                    blk, half = divmod(c4, 2)
                    kst = kst_ref[d, tile, kr, blk * LANES:(blk + 1) * LANES]
                    kst = jnp.where(upper_lane if half else ~upper_lane, kst, jnp.zeros_like(kst))
                    kv = jnp.where(blockdiag, _dot(kst, v_ref[_tile_rows(tile, blk * LANES, LANES), vl]), 0.0)
                    sst_ref[pair, tile * tile_chunks + c4, d * pair_k:(d + 1) * pair_k, :] = s.astype(BF16)
                    s = dect_ref[d, tile, kr, c4:c4 + 1] * s + kv
                s_ref[d, pair] = s
        yield

    yield from _for_row_tiles(seq_len, scan_tile)

    if has_state_out:
        for d in range(2):
            for pair in range(n_pairs):
                s = s_ref[d, pair]
                sout_refs[d][2 * pair] = s[0:GLA_DK, 0:GLA_DV]
                sout_refs[d][2 * pair + 1] = s[GLA_DK:, GLA_DV:]

    def stage3(i):
        for pair in range(n_pairs):
            kl = slice(pair * pair_k, (pair + 1) * pair_k)
            vl = slice(pair * pair_v, (pair + 1) * pair_v)
            for c4 in range(tile_chunks):
                crow = _tile_rows(i, c4 * c, c)
                q2 = jnp.concatenate([qd_ref[0, crow, kl], qd_ref[1, crow, kl]], axis=1)
                o_ref[crow, vl] = o_ref[crow, vl] + _dot(q2, sst_ref[pair, i * tile_chunks + c4])
        yield
        rows = _tile_rows(i)
        heads = []
        for h in range(GLA_HEADS):
            hl = slice(h * GLA_DV, (h + 1) * GLA_DV)
            oh = o_ref[rows, hl]
            r = lax.rsqrt(jnp.mean(oh * oh, axis=-1, keepdims=True) + EPS)
            heads.append(oh * r * ggla_ref[:, hl])
        y_b = jnp.concatenate(heads, axis=1) * _silu(og_ref[rows, :])
        y = jnp.concatenate([ya_ref[rows, :], y_b.astype(BF16)], axis=1)
        x1_ref[rows, :] = x_ref[rows, :] + m[2:3] * _dot(y, wout_ref[...])
        yield

    yield from _for_row_tiles(seq_len, stage3)


def _mixer_call(x, mod3, mod_row_of_step, states, weights, *, period, has_state_out, seqs_per_step):
    n_seq, seq_len, _ = x.shape
    sps = seqs_per_step
    has_state_in = states is not None
    stages_w_in = weights[1].dtype == F32
    kernel = functools.partial(_mixer_kernel, seqs_per_step=sps, seq_len=seq_len, period=period,
                               has_state_in=has_state_in, has_state_out=has_state_out, stages_w_in=stages_w_in)
    state_spec = pl.BlockSpec((sps, None, GLA_HEADS, GLA_DK, GLA_DV), lambda b: (b, 0, 0, 0, 0))
    const2 = lambda b: (0, 0)
    in_specs = [pl.BlockSpec((sps, seq_len, D_MODEL), lambda b: (b, 0, 0)),
                pl.BlockSpec((1, N_MOD, D_MODEL), lambda b: (mod_row_of_step(b), 0, 0))]
    args = [x, mod3]
    if has_state_in:
        in_specs += [state_spec, state_spec]
        args += list(states)
    in_specs += [pl.BlockSpec(w.shape, const2, pipeline_mode=pl.Buffered(1)) for w in weights]
    args += list(weights)
    out_specs = [pl.BlockSpec((sps, seq_len, D_MODEL), lambda b: (b, 0, 0))]
    out_shape = [jax.ShapeDtypeStruct((n_seq, seq_len, D_MODEL), F32)]
    if has_state_out:
        out_specs += [state_spec, state_spec]
        out_shape += [jax.ShapeDtypeStruct((n_seq, 1, GLA_HEADS, GLA_DK, GLA_DV), F32)] * 2
    if stages_w_in:
        out_specs.append(pl.BlockSpec((D_MODEL, P_PAD), const2))
        out_shape.append(jax.ShapeDtypeStruct((D_MODEL, P_PAD), BF16))
    n_tiles = seq_len // ROW_TILE
    n_pairs = GLA_HEADS // 2
    per_seq = lambda shape, dtype: pltpu.VMEM((sps,) + shape, dtype)
    scratch = [per_seq((seq_len, GLA_DV_TOT), F32),
               per_seq((2, seq_len, GLA_DK_TOT), BF16),
               per_seq((2, seq_len, GLA_DK_TOT), BF16),
               per_seq((2, n_tiles, GLA_DK_TOT, ROW_TILE), BF16),
               per_seq((2, n_tiles, GLA_DK_TOT, LANES), F32),
               per_seq((seq_len, GLA_DV_TOT), BF16),
               per_seq((2, n_pairs, 2 * GLA_DK, 2 * GLA_DV), F32),
               per_seq((n_pairs, seq_len // GLA_CHUNK, 4 * GLA_DK, 2 * GLA_DV), BF16),
               per_seq((seq_len, GLA_DV_TOT), F32),
               per_seq((seq_len, D_CONV), BF16)]
    return pl.pallas_call(
        kernel,
        grid=(n_seq // sps,),
        in_specs=in_specs,
        out_specs=out_specs,
        out_shape=out_shape,
        scratch_shapes=scratch,
        compiler_params=_params(1),
        name="mixer",
    )(*args)


SUBLANES = 8


def _route_kernel(x1_ref, mod_ref, g2_ref, wrh_ref, wrl_ref, xs_ref, rank_ref, gate_ref,
                  pt_ref, h2_ref, kn_ref, kcol_ref, krow_ref, *, seq_len, cap):
    m = mod_ref[0]
    wrh = wrh_ref[...]
    wrl = wrl_ref[...]
    for t in range(seq_len // ROW_TILE):
        rows = slice(t * ROW_TILE, (t + 1) * ROW_TILE)
        h2 = _modulated_norm(x1_ref[rows, :], g2_ref[...], m[4:5], m[3:4])
        hi, lo = _split(h2)
        h2_ref[rows, :] = hi
        pt_ref[:, rows] = _dot_nt(wrh, hi) + _dot_nt(wrh, lo) + _dot_nt(wrl, hi)

    logits = pt_ref[...]
    ex = jnp.exp(logits - jnp.max(logits, axis=0, keepdims=True))
    probs = ex / jnp.sum(ex, axis=0, keepdims=True)
    pt_ref[...] = probs
    pad = jnp.zeros((LANES - N_EXPERTS, seq_len), F32)
    kn_ref[...] = jnp.transpose(jnp.concatenate([probs, pad], axis=0))
    for e in range(N_EXPERTS):
        kcol_ref[e] = jnp.broadcast_to(kn_ref[:, e:e + 1], (seq_len, LANES))
        krow_ref[e] = jnp.broadcast_to(probs[e:e + 1, :], (SUBLANES, seq_len))

    n_blk = seq_len // LANES

    def rank_expert(e, carry):
        krow = krow_ref[e][0:1, :]
        counts = [jnp.zeros((SUBLANES, LANES), F32) for _ in range(n_blk)]
        for sb in range(n_blk):
            kcol = kcol_ref[e, sb * LANES:(sb + 1) * LANES, :]
            for tb in range(n_blk):
                bar = krow[:, tb * LANES:(tb + 1) * LANES]
                if sb < tb:
                    beats = jnp.where(kcol >= bar, 1.0, 0.0)
                elif sb > tb:
                    beats = jnp.where(kcol > bar, 1.0, 0.0)
                else:
                    earlier = (lax.broadcasted_iota(I32, (LANES, 1), 0) < lax.broadcasted_iota(I32, (1, LANES), 1))
                    beats = jnp.where(kcol > bar, 1.0, 0.0) + jnp.where((kcol == bar) & earlier, 1.0, 0.0)
                counts[tb] = counts[tb] + jnp.sum(beats.reshape(LANES // SUBLANES, SUBLANES, LANES), axis=0)
        cnt = jnp.concatenate([jnp.sum(a, axis=0, keepdims=True) for a in counts], axis=1)
        rank_ref[e] = jnp.broadcast_to(cnt.astype(I32), (SUBLANES, seq_len))
        return carry

    lax.fori_loop(0, N_EXPERTS, rank_expert, 0)

    group = DISPATCH_ROWS // cap
    slot = lax.broadcasted_iota(I32, (cap, 1), 0)
    for gi in range(N_EXPERTS // group):
        picks = []
        for e in range(gi * group, (gi + 1) * group):
            oh = rank_ref[e, 0:1, :] == slot
            gate = jnp.sum(jnp.where(oh, pt_ref[e:e + 1, :], 0.0), axis=1, keepdims=True)
            gate_ref[e * cap:(e + 1) * cap, :] = jnp.broadcast_to(gate, (cap, LANES))
            picks.append(oh)
        ohb = jnp.where(jnp.concatenate(picks, axis=0), 1.0, 0.0).astype(BF16)
        xs_ref[gi * DISPATCH_ROWS:(gi + 1) * DISPATCH_ROWS, :] = _dot(ohb, h2_ref[...]).astype(BF16)


def _rank_spec(seq_len):
    return pl.BlockSpec((None, N_EXPERTS, SUBLANES, seq_len), lambda b: (b, 0, 0, 0))


def _route_call(x1, mod3, mod_row_of_seq, g2, wr_hi, wr_lo):
    n_seq, seq_len, _ = x1.shape
    cap = EC_CAPACITY_FACTOR * seq_len // N_EXPERTS
    kernel = functools.partial(_route_kernel, seq_len=seq_len, cap=cap)
    const2 = lambda b: (0, 0)
    return pl.pallas_call(
        kernel,
        grid=(n_seq,),
        in_specs=[pl.BlockSpec((None, seq_len, D_MODEL), lambda b: (b, 0, 0)),
                  pl.BlockSpec((1, N_MOD, D_MODEL), lambda b: (mod_row_of_seq(b), 0, 0)),
                  pl.BlockSpec((1, D_MODEL), const2),
                  pl.BlockSpec((N_EXPERTS, D_MODEL), const2),
                  pl.BlockSpec((N_EXPERTS, D_MODEL), const2)],
        out_specs=[pl.BlockSpec((None, N_EXPERTS * cap, D_MODEL), lambda b: (b, 0, 0)),
                   _rank_spec(seq_len),
                   pl.BlockSpec((None, N_EXPERTS * cap, LANES), lambda b: (b, 0, 0))],
        out_shape=[jax.ShapeDtypeStruct((n_seq, N_EXPERTS * cap, D_MODEL), BF16),
                   jax.ShapeDtypeStruct((n_seq, N_EXPERTS, SUBLANES, seq_len), I32),
                   jax.ShapeDtypeStruct((n_seq, N_EXPERTS * cap, LANES), F32)],
        scratch_shapes=[pltpu.VMEM((N_EXPERTS, seq_len), F32),
                        pltpu.VMEM((seq_len, D_MODEL), BF16),
                        pltpu.VMEM((seq_len, LANES), F32),
                        pltpu.VMEM((N_EXPERTS, seq_len, LANES), F32),
                        pltpu.VMEM((N_EXPERTS, SUBLANES, seq_len), F32)],
        compiler_params=_params(1),
        name="route",
    )(x1, mod3, g2, wr_hi, wr_lo)


def _experts_kernel(xc_ref, xl_ref, gc_ref, gl_ref, wg_ref, wu_ref, wd_ref, yc_ref, yl_ref,
                    wgb_ref, wub_ref, wdb_ref):
    wgb_ref[...] = wg_ref[...].astype(BF16)
    wub_ref[...] = wu_ref[...].astype(BF16)
    wdb_ref[...] = wd_ref[...].astype(BF16)

    def run(x_ref, g_ref, y_ref):
        n_seq, cap, _ = x_ref.shape
        seqs = DISPATCH_ROWS // cap
        for s0 in range(0, n_seq, seqs):
            x = x_ref[s0:s0 + seqs].reshape(DISPATCH_ROWS, D_MODEL)
            a = (_silu(_dot(x, wgb_ref[...])) * _dot(x, wub_ref[...])).astype(BF16)
            gate = g_ref[s0:s0 + seqs].reshape(DISPATCH_ROWS, LANES)
            y = _dot(a, wdb_ref[...]) * jnp.concatenate([gate] * (D_MODEL // LANES), axis=1)
            y_ref[s0:s0 + seqs] = y.astype(BF16).reshape(seqs, cap, D_MODEL)

    run(xc_ref, gc_ref, yc_ref)
    run(xl_ref, gl_ref, yl_ref)


def _experts_call(xs_ctx, xs_lat, gates_ctx, gates_lat, w_gate, w_up, w_down):
    def slot_spec(a):
        n_seq, _, cap, width = a.shape
        return pl.BlockSpec((n_seq, None, cap, width), lambda e: (0, e, 0, 0))

    w_spec = pl.BlockSpec((None, D_MODEL, D_EXPERT), lambda e: (e, 0, 0))
    return pl.pallas_call(
        _experts_kernel,
        grid=(N_EXPERTS,),
        in_specs=[slot_spec(xs_ctx), slot_spec(xs_lat), slot_spec(gates_ctx), slot_spec(gates_lat),
                  w_spec, w_spec, pl.BlockSpec((None, D_EXPERT, D_MODEL), lambda e: (e, 0, 0))],
        out_specs=[slot_spec(xs_ctx), slot_spec(xs_lat)],
        out_shape=[jax.ShapeDtypeStruct(xs_ctx.shape, BF16), jax.ShapeDtypeStruct(xs_lat.shape, BF16)],
        scratch_shapes=[pltpu.VMEM((D_MODEL, D_EXPERT), BF16),
                        pltpu.VMEM((D_MODEL, D_EXPERT), BF16),
                        pltpu.VMEM((D_EXPERT, D_MODEL), BF16)],
        compiler_params=_params(1),
        name="experts",
    )(xs_ctx, xs_lat, gates_ctx, gates_lat, w_gate, w_up, w_down)


def _combine_kernel(x1_ref, y_ref, rank_ref, mod_ref, gf_ref, o_ref, oh_ref, *, seq_len, cap):
    m = mod_ref[0]
    slot = lax.broadcasted_iota(I32, (cap, 1), 0)
    for e in range(N_EXPERTS):
        oh_ref[e * cap:(e + 1) * cap, :] = jnp.where(rank_ref[e, 0:1, :] == slot, 1.0, 0.0).astype(BF16)
    for t in range(seq_len // ROW_TILE):
        rows = slice(t * ROW_TILE, (t + 1) * ROW_TILE)
        moe = _dot_tn(oh_ref[:, rows], y_ref[...])
        x2 = x1_ref[rows, :] + m[5:6] * moe
        r = lax.rsqrt(jnp.mean(x2 * x2, axis=-1, keepdims=True) + EPS)
        o_ref[rows, :] = (x2 * r) * gf_ref[...]


def _combine_call(x1, y, rank, mod3, mod_row_of_seq, g_final):
    n_seq, seq_len, _ = x1.shape
    n_slots = y.shape[1]
    kernel = functools.partial(_combine_kernel, seq_len=seq_len, cap=n_slots // N_EXPERTS)
    seq_spec = pl.BlockSpec((None, seq_len, D_MODEL), lambda b: (b, 0, 0))
    return pl.pallas_call(
        kernel,
        grid=(n_seq,),
        in_specs=[seq_spec,
                  pl.BlockSpec((None, n_slots, D_MODEL), lambda b: (b, 0, 0)),
                  _rank_spec(seq_len),
                  pl.BlockSpec((1, N_MOD, D_MODEL), lambda b: (mod_row_of_seq(b), 0, 0)),
                  pl.BlockSpec((1, D_MODEL), lambda b: (0, 0))],
        out_specs=seq_spec,
        out_shape=jax.ShapeDtypeStruct((n_seq, seq_len, D_MODEL), F32),
        scratch_shapes=[pltpu.VMEM((n_slots, seq_len), BF16)],
        compiler_params=_params(1),
        name="combine",
    )(x1, y, rank, mod3, g_final)


def kernel(x_prompt, x_sample, state_gla_fwd, state_gla_bwd, c, c_ctx, w_mod, b_mod, g_norm1, g_norm2,
           w_in, w_conv, b_conv, w_a_up_f, b_a_f, w_a_up_b, b_a_b, g_gla_norm, w_out, w_router,
           w_gate, w_up, w_down, g_final):
    assert w_mod.shape[0] == 1, "single trunk layer"
    n_ctx, ctx_len, _ = x_prompt.shape
    n_lat, lat_len, _ = x_sample.shape
    ctx_cap = EC_CAPACITY_FACTOR * ctx_len // N_EXPERTS
    lat_cap = EC_CAPACITY_FACTOR * lat_len // N_EXPERTS

    c_rows = jnp.concatenate([c_ctx[None, :], c, jnp.zeros((8 - 1 - n_lat, D_MODEL), F32)], axis=0)
    mod3 = _mod_call(c_rows, w_mod[0], b_mod).reshape(8, N_MOD, D_MODEL)

    w_up_gate = jnp.zeros((P_PAD - OFF_ALOW, 2 * GLA_DK_TOT), F32)
    w_up_gate = w_up_gate.at[:GLA_LOW_RANK, :GLA_DK_TOT].set(w_a_up_f[0])
    w_up_gate = w_up_gate.at[GLA_LOW_RANK:2 * GLA_LOW_RANK, GLA_DK_TOT:].set(w_a_up_b[0]).astype(BF16)
    b_up_gate = jnp.concatenate([b_a_f[0], b_a_b[0]])[None, :]
    mixer_weights = [g_norm1, w_in[0], w_conv[0], b_conv, w_up_gate, b_up_gate,
                     g_gla_norm[0].reshape(1, GLA_DV_TOT), w_out[0].astype(BF16)]
    wr_t = jnp.transpose(w_router[0])
    wr_hi = wr_t.astype(BF16)
    wr_lo = (wr_t - wr_hi.astype(F32)).astype(BF16)

    ctx_row = lambda b: 0
    lat_row = lambda b: b + 1

    x1_ctx, new_f, new_b, w_in_bf = _mixer_call(x_prompt, mod3, ctx_row, None, mixer_weights,
                                                period=ctx_len, has_state_out=True, seqs_per_step=2)
    mixer_weights[1] = w_in_bf
    (x1_lat,) = _mixer_call(x_sample, mod3, lat_row, (state_gla_fwd, state_gla_bwd), mixer_weights,
                            period=GRID_W, has_state_out=False, seqs_per_step=1)

    xs_ctx, rank_ctx, gates_ctx = _route_call(x1_ctx, mod3, ctx_row, g_norm2, wr_hi, wr_lo)
    xs_lat, rank_lat, gates_lat = _route_call(x1_lat, mod3, lat_row, g_norm2, wr_hi, wr_lo)

    per_expert = lambda a, n, cap: a.reshape(n, N_EXPERTS, cap, a.shape[-1])
    y_ctx, y_lat = _experts_call(per_expert(xs_ctx, n_ctx, ctx_cap), per_expert(xs_lat, n_lat, lat_cap),
                                 per_expert(gates_ctx, n_ctx, ctx_cap), per_expert(gates_lat, n_lat, lat_cap),
                                 w_gate[0], w_up[0], w_down[0])

    g_fin = g_final[None, :]
    y_prompt = _combine_call(x1_ctx, y_ctx.reshape(xs_ctx.shape), rank_ctx, mod3, ctx_row, g_fin)
    y_sample = _combine_call(x1_lat, y_lat.reshape(xs_lat.shape), rank_lat, mod3, lat_row, g_fin)
    return y_prompt, y_sample, new_f, new_b
```

```python
import functools

import jax
import jax.numpy as jnp
from jax import lax
from jax.experimental import pallas as pl
from jax.experimental.pallas import tpu as pltpu

F32 = jnp.float32
BF16 = jnp.bfloat16
I32 = jnp.int32

D_MODEL = 1024
D_CONV = D_MODEL // 2
GRID_W = 64
GLA_HEADS = 4
GLA_DK = 64
GLA_DV = 128
GLA_DK_TOT = GLA_HEADS * GLA_DK
GLA_DV_TOT = GLA_HEADS * GLA_DV
GLA_LOW_RANK = 16
GLA_TAU = 16.0
GLA_CHUNK = 64
N_EXPERTS = 16
EC_CAPACITY_FACTOR = 2
D_EXPERT = 1024
N_MOD = 6
EPS = 1e-6

OFF_XB = 0
OFF_XC = D_CONV
OFF_XV = 2 * D_CONV
OFF_Q = 3 * D_CONV
OFF_K = OFF_Q + GLA_DK_TOT
OFF_V = OFF_K + GLA_DK_TOT
OFF_OG = OFF_V + GLA_DV_TOT
OFF_ALOW = OFF_OG + GLA_DV_TOT
P_TOT = OFF_ALOW + 2 * GLA_LOW_RANK

LANES = 128
P_PAD = -(-P_TOT // LANES) * LANES
ROW_TILE = 256
TOKEN_TILE = 512
DISPATCH_ROWS = 512
VMEM_LIMIT = 56 * 1024 * 1024


def _dot(a, b):
    return jnp.dot(a, b, preferred_element_type=F32)


def _dot_nt(a, b):
    return lax.dot_general(a, b, (((1,), (1,)), ((), ())), preferred_element_type=F32)


def _dot_tn(a, b):
    return lax.dot_general(a, b, (((0,), (0,)), ((), ())), preferred_element_type=F32)


def _split(a):
    hi = a.astype(BF16)
    lo = (a - hi.astype(F32)).astype(BF16)
    return hi, lo


def _silu(x):
    return x * jax.nn.sigmoid(x)


def _modulated_norm(x, g, scale, shift):
    r = lax.rsqrt(jnp.mean(x * x, axis=-1, keepdims=True) + EPS)
    return (x * r) * (g * (1.0 + scale)) + shift


def _params(n_axes):
    return pltpu.CompilerParams(dimension_semantics=("arbitrary",) * n_axes,
                                vmem_limit_bytes=VMEM_LIMIT)


def _mod_kernel(c_ref, w_ref, b_ref, o_ref):
    rows = c_ref.shape[0]
    s = _silu(c_ref[...])
    s_hi, s_lo = _split(jnp.concatenate([s, s], axis=0))
    upper = lax.broadcasted_iota(I32, (2 * rows, 1), 0) < rows
    w_hi, w_lo = _split(w_ref[...])
    by_hi = _dot(jnp.where(upper, s_hi, s_lo), w_hi)
    o_ref[...] = by_hi[:rows] + by_hi[rows:] + _dot(s_hi[:rows], w_lo) + b_ref[...]


def _mod_call(c_rows, w_mod, b_mod):
    rows, d = c_rows.shape
    n = w_mod.shape[1]
    tn = D_MODEL
    return pl.pallas_call(
        _mod_kernel,
        grid=(n // tn,),
        in_specs=[pl.BlockSpec((rows, d), lambda j: (0, 0)),
                  pl.BlockSpec((d, tn), lambda j: (0, j)),
                  pl.BlockSpec((1, tn), lambda j: (0, j))],
        out_specs=pl.BlockSpec((rows, tn), lambda j: (0, j)),
        out_shape=jax.ShapeDtypeStruct((rows, n), F32),
        compiler_params=_params(1),
        name="mod",
    )(c_rows, w_mod, b_mod)


def _for_row_tiles(seq_len, phases):
    n = seq_len // ROW_TILE
    if n == 1:
        yield from phases(0)
    else:
        def body(i, carry):
            for _ in phases(i):
                pass
            return carry
        lax.fori_loop(0, n, body, 0)
        yield


def _tile_rows(tile, offset=0, size=ROW_TILE):
    if isinstance(tile, int):
        return pl.ds(tile * ROW_TILE + offset, size)
    return pl.ds(pl.multiple_of(tile * ROW_TILE + offset, size), size)


def _mixer_kernel(*refs, seqs_per_step, has_state_in, has_state_out, stages_w_in, **static):
    refs = list(refs)
    n_in = 2 + (2 if has_state_in else 0)
    per_seq_in, refs = [refs[0]] + refs[2:n_in], [refs[1]] + refs[n_in:]
    mod_ref, weights, refs = refs[0], refs[1:9], refs[9:]
    n_out = 1 + (2 if has_state_out else 0)
    per_seq_out, refs = refs[:n_out], refs[n_out:]
    if stages_w_in:
        win_f32_ref, win_bf_ref, scratch = weights[1], refs[0], refs[1:]
        weights = weights[:1] + [win_bf_ref] + weights[2:]

        @pl.when(pl.program_id(0) == 0)
        def _():
            for c0 in range(0, P_PAD, LANES):
                n = min(LANES, P_TOT - c0)
                cols = win_f32_ref[c0:c0 + n, :]
                if n < LANES:
                    cols = jnp.concatenate([cols, jnp.zeros((LANES - n, D_MODEL), F32)], axis=0)
                win_bf_ref[:, c0:c0 + LANES] = jnp.transpose(cols).astype(BF16)
    else:
        scratch = refs
    programs = []
    for j in range(seqs_per_step):
        ins = [r.at[j] for r in per_seq_in]
        outs = [r.at[j] for r in per_seq_out]
        programs.append(_mixer_sequence(ins[0], mod_ref, ins[1:], weights, outs[0], outs[1:],
                                        [r.at[j] for r in scratch], **static))
    while programs:
        programs = [p for p in programs if next(p, "done") != "done"]


def _mixer_sequence(x_ref, mod_ref, s0_refs, weights, x1_ref, sout_refs, scratch, *, seq_len, period):
    has_state_in = bool(s0_refs)
    has_state_out = bool(sout_refs)
    g1_ref, win_ref, wconv_ref, bconv_ref, wup_ref, bup_ref, ggla_ref, wout_ref = weights
    og_ref, qd_ref, kd_ref, kst_ref, dect_ref, v_ref, s_ref, sst_ref, o_ref, ya_ref = scratch

    c = GLA_CHUNK
    tile_chunks = ROW_TILE // c
    n_tiles = seq_len // ROW_TILE
    n_pairs = GLA_HEADS // 2
    pair_k = 2 * GLA_DK
    pair_v = 2 * GLA_DV
    m = mod_ref[0]

    def stage1(ti):
        rows = _tile_rows(ti)
        h = _modulated_norm(x_ref[rows, :], g1_ref[...], m[1:2], m[0:1]).astype(BF16)
        row_i = lax.broadcasted_iota(I32, (ROW_TILE, 1), 0)

        p_gate = _dot(h, win_ref[:, OFF_OG:P_PAD])
        og_ref[rows, :] = p_gate[:, :GLA_DV_TOT]
        z = _dot(p_gate[:, GLA_DV_TOT:].astype(BF16), wup_ref[...]) + bup_ref[...]
        la = (jnp.minimum(z, 0.0) - jnp.log(1.0 + jnp.exp(-jnp.abs(z)))) * (1.0 / GLA_TAU)
        col_j = lax.broadcasted_iota(I32, (1, ROW_TILE), 1)
        same_chunk = (row_i & -c) == (col_j & -c)
        lower = jnp.where(same_chunk & (col_j <= row_i), 1.0, 0.0).astype(BF16)
        la_parts = jnp.concatenate(_split(la), axis=1)
        n_gate = 2 * GLA_DK_TOT
        pre = _dot(lower, la_parts)
        pre = pre[:, :n_gate] + pre[:, n_gate:]
        tot = jnp.concatenate([jnp.broadcast_to(pre[(n + 1) * c - 1:(n + 1) * c], (c, n_gate))
                               for n in range(tile_chunks)], axis=0)
        yield
        p_qkv = _dot(h, win_ref[:, OFF_Q:OFF_OG])
        q = p_qkv[:, :GLA_DK_TOT] * (GLA_DK ** -0.5)
        k = p_qkv[:, GLA_DK_TOT:2 * GLA_DK_TOT]
        v_ref[rows, :] = p_qkv[:, 2 * GLA_DK_TOT:].astype(BF16)
        for d in range(2):
            cols = slice(d * GLA_DK_TOT, (d + 1) * GLA_DK_TOT)
            if d == 0:
                bq = pre[:, cols]
                bk = tot[:, cols] - bq
            else:
                bk = pre[:, cols] - la[:, cols]
                bq = tot[:, cols] - bk
            qd_ref[d, rows, :] = (q * jnp.exp(bq)).astype(BF16)
            kd_ref[d, rows, :] = (k * jnp.exp(-bq)).astype(BF16)
            kst_ref[d, ti] = jnp.transpose(k * jnp.exp(bk)).astype(BF16)
            totals = [tot[n * c:n * c + 1, cols] for n in range(tile_chunks)]
            totals.append(jnp.zeros((LANES - tile_chunks, GLA_DK_TOT), F32))
            dect_ref[d, ti] = jnp.transpose(jnp.exp(jnp.concatenate(totals, axis=0)))
        yield
        p_conv = _dot(h, win_ref[:, :OFF_Q])
        pos = row_i & (period - 1)
        u = p_conv[:, OFF_XC:OFF_XC + D_CONV] * p_conv[:, OFF_XV:OFF_XV + D_CONV]
        u_prev = jnp.where(pos == 0, 0.0, pltpu.roll(u, 1, 0))
        u_next = jnp.where(pos == period - 1, 0.0, pltpu.roll(u, ROW_TILE - 1, 0))
        conv = u_prev * wconv_ref[0:1, :] + u * wconv_ref[1:2, :] + u_next * wconv_ref[2:3, :] + bconv_ref[...]
        ya_ref[rows, :] = (p_conv[:, OFF_XB:OFF_XB + D_CONV] * conv).astype(BF16)
        yield

    yield from _for_row_tiles(seq_len, stage1)

    for d in range(2):
        for pair in range(n_pairs):
            if has_state_in:
                zero = jnp.zeros((GLA_DK, GLA_DV), F32)
                top = jnp.concatenate([s0_refs[d][2 * pair], zero], axis=1)
                bot = jnp.concatenate([zero, s0_refs[d][2 * pair + 1]], axis=1)
                s_ref[d, pair] = jnp.concatenate([top, bot], axis=0)
            else:
                s_ref[d, pair] = jnp.zeros((pair_k, pair_v), F32)

    def scan_tile(i):
        upper_lane = lax.broadcasted_iota(I32, (1, LANES), 1) >= GLA_DK
        qi = lax.broadcasted_iota(I32, (LANES, 1), 0)
        kj = lax.broadcasted_iota(I32, (1, 2 * LANES), 1) & (LANES - 1)
        same_chunk = (qi & c) == (kj & c)
        causal = (same_chunk & (kj <= qi), same_chunk & (kj >= qi))
        for pair in range(n_pairs):
            kl = slice(pair * pair_k, (pair + 1) * pair_k)
            vl = slice(pair * pair_v, (pair + 1) * pair_v)
            for blk in range(ROW_TILE // LANES):
                rows = _tile_rows(i, blk * LANES, LANES)
                att = None
                for d in range(2):
                    kd = kd_ref[d, rows, kl]
                    zk = jnp.zeros_like(kd)
                    keys = jnp.concatenate([jnp.where(upper_lane, zk, kd), jnp.where(upper_lane, kd, zk)], axis=0)
                    a = jnp.where(causal[d], _dot_nt(qd_ref[d, rows, kl], keys), 0.0)
                    att = a if att is None else att + a
                v = v_ref[rows, vl]
                zv = jnp.zeros((LANES, GLA_DV), BF16)
                v_bd = jnp.concatenate([jnp.concatenate([v[:, :GLA_DV], zv], axis=1),
                                        jnp.concatenate([zv, v[:, GLA_DV:]], axis=1)], axis=0)
                o_ref[rows, vl] = _dot(att.astype(BF16), v_bd)
        yield
        key_row = lax.broadcasted_iota(I32, (pair_k, 1), 0)
        val_col = lax.broadcasted_iota(I32, (1, pair_v), 1)
        blockdiag = (key_row >= GLA_DK) == (val_col >= GLA_DV)
        for d in range(2):
            tile = i if d == 0 else n_tiles - 1 - i
            chunks = range(tile_chunks)
            for pair in range(n_pairs):
                kr = slice(pair * pair_k, (pair + 1) * pair_k)
                vl = slice(pair * pair_v, (pair + 1) * pair_v)
                s = s_ref[d, pair]
                for c4 in (chunks if d == 0 else reversed(chunks)):
                    blk, half = divmod(c4, 2)
                    kst = kst_ref[d, tile, kr, blk * LANES:(blk + 1) * LANES]
                    kst = jnp.where(upper_lane if half else ~upper_lane, kst, jnp.zeros_like(kst))
                    kv = jnp.where(blockdiag, _dot(kst, v_ref[_tile_rows(tile, blk * LANES, LANES), vl]), 0.0)
                    sst_ref[pair, tile * tile_chunks + c4, d * pair_k:(d + 1) * pair_k, :] = s.astype(BF16)
                    s = dect_ref[d, tile, kr, c4:c4 + 1] * s + kv
                s_ref[d, pair] = s
        yield

    yield from _for_row_tiles(seq_len, scan_tile)

    if has_state_out:
        for d in range(2):
            for pair in range(n_pairs):
                s = s_ref[d, pair]
                sout_refs[d][2 * pair] = s[0:GLA_DK, 0:GLA_DV]
                sout_refs[d][2 * pair + 1] = s[GLA_DK:, GLA_DV:]

    def stage3(i):
        for pair in range(n_pairs):
            kl = slice(pair * pair_k, (pair + 1) * pair_k)
            vl = slice(pair * pair_v, (pair + 1) * pair_v)
            for c4 in range(tile_chunks):
                crow = _tile_rows(i, c4 * c, c)
                q2 = jnp.concatenate([qd_ref[0, crow, kl], qd_ref[1, crow, kl]], axis=1)
                o_ref[crow, vl] = o_ref[crow, vl] + _dot(q2, sst_ref[pair, i * tile_chunks + c4])
        yield
        rows = _tile_rows(i)
        heads = []
        for h in range(GLA_HEADS):
            hl = slice(h * GLA_DV, (h + 1) * GLA_DV)
            oh = o_ref[rows, hl]
            r = lax.rsqrt(jnp.mean(oh * oh, axis=-1, keepdims=True) + EPS)
            heads.append(oh * r * ggla_ref[:, hl])
        y_b = jnp.concatenate(heads, axis=1) * _silu(og_ref[rows, :])
        y = jnp.concatenate([ya_ref[rows, :], y_b.astype(BF16)], axis=1)
        x1_ref[rows, :] = x_ref[rows, :] + m[2:3] * _dot(y, wout_ref[...])
        yield

    yield from _for_row_tiles(seq_len, stage3)


def _mixer_call(x, mod3, mod_row_of_step, states, weights, *, period, has_state_out, seqs_per_step):
    n_seq, seq_len, _ = x.shape
    sps = seqs_per_step
    has_state_in = states is not None
    stages_w_in = weights[1].dtype == F32
    kernel = functools.partial(_mixer_kernel, seqs_per_step=sps, seq_len=seq_len, period=period,
                               has_state_in=has_state_in, has_state_out=has_state_out, stages_w_in=stages_w_in)
    state_spec = pl.BlockSpec((sps, None, GLA_HEADS, GLA_DK, GLA_DV), lambda b: (b, 0, 0, 0, 0))
    const2 = lambda b: (0, 0)
    in_specs = [pl.BlockSpec((sps, seq_len, D_MODEL), lambda b: (b, 0, 0)),
                pl.BlockSpec((1, N_MOD, D_MODEL), lambda b: (mod_row_of_step(b), 0, 0))]
    args = [x, mod3]
    if has_state_in:
        in_specs += [state_spec, state_spec]
        args += list(states)
    in_specs += [pl.BlockSpec(w.shape, lambda b, nd=w.ndim: (0,) * nd, pipeline_mode=pl.Buffered(1))
                 for w in weights]
    args += list(weights)
    out_specs = [pl.BlockSpec((sps, seq_len, D_MODEL), lambda b: (b, 0, 0))]
    out_shape = [jax.ShapeDtypeStruct((n_seq, seq_len, D_MODEL), F32)]
    if has_state_out:
        out_specs += [state_spec, state_spec]
        out_shape += [jax.ShapeDtypeStruct((n_seq, 1, GLA_HEADS, GLA_DK, GLA_DV), F32)] * 2
    if stages_w_in:
        out_specs.append(pl.BlockSpec((D_MODEL, P_PAD), const2))
        out_shape.append(jax.ShapeDtypeStruct((D_MODEL, P_PAD), BF16))
    n_tiles = seq_len // ROW_TILE
    n_pairs = GLA_HEADS // 2
    per_seq = lambda shape, dtype: pltpu.VMEM((sps,) + shape, dtype)
    scratch = [per_seq((seq_len, GLA_DV_TOT), F32),
               per_seq((2, seq_len, GLA_DK_TOT), BF16),
               per_seq((2, seq_len, GLA_DK_TOT), BF16),
               per_seq((2, n_tiles, GLA_DK_TOT, ROW_TILE), BF16),
               per_seq((2, n_tiles, GLA_DK_TOT, LANES), F32),
               per_seq((seq_len, GLA_DV_TOT), BF16),
               per_seq((2, n_pairs, 2 * GLA_DK, 2 * GLA_DV), F32),
               per_seq((n_pairs, seq_len // GLA_CHUNK, 4 * GLA_DK, 2 * GLA_DV), BF16),
               per_seq((seq_len, GLA_DV_TOT), F32),
               per_seq((seq_len, D_CONV), BF16)]
    return pl.pallas_call(
        kernel,
        grid=(n_seq // sps,),
        in_specs=in_specs,
        out_specs=out_specs,
        out_shape=out_shape,
        scratch_shapes=scratch,
        compiler_params=_params(1),
        name="mixer",
    )(*args)


SUBLANES = 8


def _route_kernel(x1_ref, mod_ref, g2_ref, wrh_ref, wrl_ref, xs_ref, rank_ref, gate_ref,
                  pt_ref, h2_ref, kn_ref, kcol_ref, krow_ref, *, seq_len, cap):
    m = mod_ref[0]
    wrh = wrh_ref[...]
    wrl = wrl_ref[...]
    for t in range(seq_len // ROW_TILE):
        rows = slice(t * ROW_TILE, (t + 1) * ROW_TILE)
        h2 = _modulated_norm(x1_ref[rows, :], g2_ref[...], m[4:5], m[3:4])
        hi, lo = _split(h2)
        h2_ref[rows, :] = hi
        pt_ref[:, rows] = _dot_nt(wrh, hi) + _dot_nt(wrh, lo) + _dot_nt(wrl, hi)

    logits = pt_ref[...]
    ex = jnp.exp(logits - jnp.max(logits, axis=0, keepdims=True))
    probs = ex / jnp.sum(ex, axis=0, keepdims=True)
    pt_ref[...] = probs
    pad = jnp.zeros((LANES - N_EXPERTS, seq_len), F32)
    kn_ref[...] = jnp.transpose(jnp.concatenate([probs, pad], axis=0))
    for e in range(N_EXPERTS):
        kcol_ref[e] = jnp.broadcast_to(kn_ref[:, e:e + 1], (seq_len, LANES))
        krow_ref[e] = jnp.broadcast_to(probs[e:e + 1, :], (SUBLANES, seq_len))

    n_blk = seq_len // LANES

    def rank_expert(e, carry):
        krow = krow_ref[e][0:1, :]
        counts = [jnp.zeros((SUBLANES, LANES), F32) for _ in range(n_blk)]
        for sb in range(n_blk):
            kcol = kcol_ref[e, sb * LANES:(sb + 1) * LANES, :]
            for tb in range(n_blk):
                bar = krow[:, tb * LANES:(tb + 1) * LANES]
                if sb < tb:
                    beats = jnp.where(kcol >= bar, 1.0, 0.0)
                elif sb > tb:
                    beats = jnp.where(kcol > bar, 1.0, 0.0)
                else:
                    earlier = (lax.broadcasted_iota(I32, (LANES, 1), 0) < lax.broadcasted_iota(I32, (1, LANES), 1))
                    beats = jnp.where(kcol > bar, 1.0, 0.0) + jnp.where((kcol == bar) & earlier, 1.0, 0.0)
                counts[tb] = counts[tb] + jnp.sum(beats.reshape(LANES // SUBLANES, SUBLANES, LANES), axis=0)
        cnt = jnp.concatenate([jnp.sum(a, axis=0, keepdims=True) for a in counts], axis=1)
        rank_ref[e] = jnp.broadcast_to(cnt.astype(I32), (SUBLANES, seq_len))
        return carry

    lax.fori_loop(0, N_EXPERTS, rank_expert, 0)

    group = DISPATCH_ROWS // cap
    slot = lax.broadcasted_iota(I32, (cap, 1), 0)
    for gi in range(N_EXPERTS // group):
        picks = []
        for e in range(gi * group, (gi + 1) * group):
            oh = rank_ref[e, 0:1, :] == slot
            gate = jnp.sum(jnp.where(oh, pt_ref[e:e + 1, :], 0.0), axis=1, keepdims=True)
            gate_ref[e * cap:(e + 1) * cap, :] = jnp.broadcast_to(gate, (cap, LANES))
            picks.append(oh)
        ohb = jnp.where(jnp.concatenate(picks, axis=0), 1.0, 0.0).astype(BF16)
        xs_ref[gi * DISPATCH_ROWS:(gi + 1) * DISPATCH_ROWS, :] = _dot(ohb, h2_ref[...]).astype(BF16)


def _rank_spec(seq_len):
    return pl.BlockSpec((None, N_EXPERTS, SUBLANES, seq_len), lambda b: (b, 0, 0, 0))


def _route_call(x1, mod3, mod_row_of_seq, g2, wr_hi, wr_lo):
    n_seq, seq_len, _ = x1.shape
    cap = EC_CAPACITY_FACTOR * seq_len // N_EXPERTS
    kernel = functools.partial(_route_kernel, seq_len=seq_len, cap=cap)
    const2 = lambda b: (0, 0)
    return pl.pallas_call(
        kernel,
        grid=(n_seq,),
        in_specs=[pl.BlockSpec((None, seq_len, D_MODEL), lambda b: (b, 0, 0)),
                  pl.BlockSpec((1, N_MOD, D_MODEL), lambda b: (mod_row_of_seq(b), 0, 0)),
                  pl.BlockSpec((1, D_MODEL), const2),
                  pl.BlockSpec((N_EXPERTS, D_MODEL), const2),
                  pl.BlockSpec((N_EXPERTS, D_MODEL), const2)],
        out_specs=[pl.BlockSpec((None, N_EXPERTS * cap, D_MODEL), lambda b: (b, 0, 0)),
                   _rank_spec(seq_len),
                   pl.BlockSpec((None, N_EXPERTS * cap, LANES), lambda b: (b, 0, 0))],
        out_shape=[jax.ShapeDtypeStruct((n_seq, N_EXPERTS * cap, D_MODEL), BF16),
                   jax.ShapeDtypeStruct((n_seq, N_EXPERTS, SUBLANES, seq_len), I32),
                   jax.ShapeDtypeStruct((n_seq, N_EXPERTS * cap, LANES), F32)],
        scratch_shapes=[pltpu.VMEM((N_EXPERTS, seq_len), F32),
                        pltpu.VMEM((seq_len, D_MODEL), BF16),
                        pltpu.VMEM((seq_len, LANES), F32),
                        pltpu.VMEM((N_EXPERTS, seq_len, LANES), F32),
                        pltpu.VMEM((N_EXPERTS, SUBLANES, seq_len), F32)],
        compiler_params=_params(1),
        name="route",
    )(x1, mod3, g2, wr_hi, wr_lo)


def _experts_kernel(xc_ref, xl_ref, gc_ref, gl_ref, wg_ref, wu_ref, wd_ref, yc_ref, yl_ref,
                    wgb_ref, wub_ref, wdb_ref):
    wgb_ref[...] = wg_ref[...].astype(BF16)
    wub_ref[...] = wu_ref[...].astype(BF16)
    wdb_ref[...] = wd_ref[...].astype(BF16)

    def run(x_ref, g_ref, y_ref):
        n_seq, cap, _ = x_ref.shape
        seqs = DISPATCH_ROWS // cap
        for s0 in range(0, n_seq, seqs):
            x = x_ref[s0:s0 + seqs].reshape(DISPATCH_ROWS, D_MODEL)
            a = (_silu(_dot(x, wgb_ref[...])) * _dot(x, wub_ref[...])).astype(BF16)
            gate = g_ref[s0:s0 + seqs].reshape(DISPATCH_ROWS, LANES)
            y = _dot(a, wdb_ref[...]) * jnp.concatenate([gate] * (D_MODEL // LANES), axis=1)
            y_ref[s0:s0 + seqs] = y.astype(BF16).reshape(seqs, cap, D_MODEL)

    run(xc_ref, gc_ref, yc_ref)
    run(xl_ref, gl_ref, yl_ref)


def _experts_call(xs_ctx, xs_lat, gates_ctx, gates_lat, w_gate, w_up, w_down):
    def slot_spec(a):
        n_seq, _, cap, width = a.shape
        return pl.BlockSpec((n_seq, None, cap, width), lambda e: (0, e, 0, 0))

    w_spec = pl.BlockSpec((None, D_MODEL, D_EXPERT), lambda e: (e, 0, 0))
    return pl.pallas_call(
        _experts_kernel,
        grid=(N_EXPERTS,),
        in_specs=[slot_spec(xs_ctx), slot_spec(xs_lat), slot_spec(gates_ctx), slot_spec(gates_lat),
                  w_spec, w_spec, pl.BlockSpec((None, D_EXPERT, D_MODEL), lambda e: (e, 0, 0))],
        out_specs=[slot_spec(xs_ctx), slot_spec(xs_lat)],
        out_shape=[jax.ShapeDtypeStruct(xs_ctx.shape, BF16), jax.ShapeDtypeStruct(xs_lat.shape, BF16)],
        scratch_shapes=[pltpu.VMEM((D_MODEL, D_EXPERT), BF16),
                        pltpu.VMEM((D_MODEL, D_EXPERT), BF16),
                        pltpu.VMEM((D_EXPERT, D_MODEL), BF16)],
        compiler_params=_params(1),
        name="experts",
    )(xs_ctx, xs_lat, gates_ctx, gates_lat, w_gate, w_up, w_down)


def _combine_kernel(x1_ref, y_ref, rank_ref, mod_ref, gf_ref, o_ref, oh_ref, *, seq_len, cap):
    m = mod_ref[0]
    slot = lax.broadcasted_iota(I32, (cap, 1), 0)
    for e in range(N_EXPERTS):
        oh_ref[e * cap:(e + 1) * cap, :] = jnp.where(rank_ref[e, 0:1, :] == slot, 1.0, 0.0).astype(BF16)
    for t in range(seq_len // ROW_TILE):
        rows = slice(t * ROW_TILE, (t + 1) * ROW_TILE)
        moe = _dot_tn(oh_ref[:, rows], y_ref[...])
        x2 = x1_ref[rows, :] + m[5:6] * moe
        r = lax.rsqrt(jnp.mean(x2 * x2, axis=-1, keepdims=True) + EPS)
        o_ref[rows, :] = (x2 * r) * gf_ref[...]


def _combine_call(x1, y, rank, mod3, mod_row_of_seq, g_final):
    n_seq, seq_len, _ = x1.shape
    n_slots = y.shape[1]
    kernel = functools.partial(_combine_kernel, seq_len=seq_len, cap=n_slots // N_EXPERTS)
    seq_spec = pl.BlockSpec((None, seq_len, D_MODEL), lambda b: (b, 0, 0))
    return pl.pallas_call(
        kernel,
        grid=(n_seq,),
        in_specs=[seq_spec,
                  pl.BlockSpec((None, n_slots, D_MODEL), lambda b: (b, 0, 0)),
                  _rank_spec(seq_len),
                  pl.BlockSpec((1, N_MOD, D_MODEL), lambda b: (mod_row_of_seq(b), 0, 0)),
                  pl.BlockSpec((1, D_MODEL), lambda b: (0, 0))],
        out_specs=seq_spec,
        out_shape=jax.ShapeDtypeStruct((n_seq, seq_len, D_MODEL), F32),
        scratch_shapes=[pltpu.VMEM((n_slots, seq_len), BF16)],
        compiler_params=_params(1),
        name="combine",
    )(x1, y, rank, mod3, g_final)


def kernel(x_prompt, x_sample, state_gla_fwd, state_gla_bwd, c, c_ctx, w_mod, b_mod, g_norm1, g_norm2,
           w_in, w_conv, b_conv, w_a_up_f, b_a_f, w_a_up_b, b_a_b, g_gla_norm, w_out, w_router,
           w_gate, w_up, w_down, g_final):
    assert w_mod.shape[0] == 1, "single trunk layer"
    n_ctx, ctx_len, _ = x_prompt.shape
    n_lat, lat_len, _ = x_sample.shape
    ctx_cap = EC_CAPACITY_FACTOR * ctx_len // N_EXPERTS
    lat_cap = EC_CAPACITY_FACTOR * lat_len // N_EXPERTS

    c_rows = jnp.concatenate([c_ctx[None, :], c, jnp.zeros((8 - 1 - n_lat, D_MODEL), F32)], axis=0)
    mod3 = _mod_call(c_rows, w_mod[0], b_mod).reshape(8, N_MOD, D_MODEL)

    w_up_gate = jnp.zeros((P_PAD - OFF_ALOW, 2 * GLA_DK_TOT), F32)
    w_up_gate = w_up_gate.at[:GLA_LOW_RANK, :GLA_DK_TOT].set(w_a_up_f[0])
    w_up_gate = w_up_gate.at[GLA_LOW_RANK:2 * GLA_LOW_RANK, GLA_DK_TOT:].set(w_a_up_b[0]).astype(BF16)
    b_up_gate = jnp.concatenate([b_a_f[0], b_a_b[0]])[None, :]
    mixer_weights = [g_norm1, jnp.transpose(w_in[0]), w_conv[0], b_conv, w_up_gate, b_up_gate,
                     g_gla_norm[0].reshape(1, GLA_DV_TOT), w_out[0].astype(BF16)]
    wr_t = jnp.transpose(w_router[0])
    wr_hi = wr_t.astype(BF16)
    wr_lo = (wr_t - wr_hi.astype(F32)).astype(BF16)

    ctx_row = lambda b: 0
    lat_row = lambda b: b + 1

    x1_ctx, new_f, new_b, w_in_bf = _mixer_call(x_prompt, mod3, ctx_row, None, mixer_weights,
                                                period=ctx_len, has_state_out=True, seqs_per_step=2)
    mixer_weights[1] = w_in_bf
    (x1_lat,) = _mixer_call(x_sample, mod3, lat_row, (state_gla_fwd, state_gla_bwd), mixer_weights,
                            period=GRID_W, has_state_out=False, seqs_per_step=1)

    xs_ctx, rank_ctx, gates_ctx = _route_call(x1_ctx, mod3, ctx_row, g_norm2, wr_hi, wr_lo)
    xs_lat, rank_lat, gates_lat = _route_call(x1_lat, mod3, lat_row, g_norm2, wr_hi, wr_lo)

    per_expert = lambda a, n, cap: a.reshape(n, N_EXPERTS, cap, a.shape[-1])
    y_ctx, y_lat = _experts_call(per_expert(xs_ctx, n_ctx, ctx_cap), per_expert(xs_lat, n_lat, lat_cap),
                                 per_expert(gates_ctx, n_ctx, ctx_cap), per_expert(gates_lat, n_lat, lat_cap),
                                 w_gate[0], w_up[0], w_down[0])

    g_fin = g_final[None, :]
    y_prompt = _combine_call(x1_ctx, y_ctx.reshape(xs_ctx.shape), rank_ctx, mod3, ctx_row, g_fin)
    y_sample = _combine_call(x1_lat, y_lat.reshape(xs_lat.shape), rank_lat, mod3, lat_row, g_fin)
    return y_prompt, y_sample, new_f, new_b
```

```python
import functools

import jax
import jax.numpy as jnp
from jax import lax
from jax.experimental import pallas as pl
from jax.experimental.pallas import tpu as pltpu

F32 = jnp.float32
BF16 = jnp.bfloat16
I32 = jnp.int32

D_MODEL = 1024
D_CONV = D_MODEL // 2
GRID_W = 64
GLA_HEADS = 4
GLA_DK = 64
GLA_DV = 128
GLA_DK_TOT = GLA_HEADS * GLA_DK
GLA_DV_TOT = GLA_HEADS * GLA_DV
GLA_LOW_RANK = 16
GLA_TAU = 16.0
GLA_CHUNK = 64
N_EXPERTS = 16
EC_CAPACITY_FACTOR = 2
D_EXPERT = 1024
N_MOD = 6
EPS = 1e-6

OFF_XB = 0
OFF_XC = D_CONV
OFF_XV = 2 * D_CONV
OFF_Q = 3 * D_CONV
OFF_K = OFF_Q + GLA_DK_TOT
OFF_V = OFF_K + GLA_DK_TOT
OFF_OG = OFF_V + GLA_DV_TOT
OFF_ALOW = OFF_OG + GLA_DV_TOT
P_TOT = OFF_ALOW + 2 * GLA_LOW_RANK

LANES = 128
P_PAD = -(-P_TOT // LANES) * LANES
ROW_TILE = 256
TOKEN_TILE = 512
DISPATCH_ROWS = 512
VMEM_LIMIT = 56 * 1024 * 1024


def _dot(a, b):
    return jnp.dot(a, b, preferred_element_type=F32)


def _dot_nt(a, b):
    return lax.dot_general(a, b, (((1,), (1,)), ((), ())), preferred_element_type=F32)


def _dot_tn(a, b):
    return lax.dot_general(a, b, (((0,), (0,)), ((), ())), preferred_element_type=F32)


def _split(a):
    hi = a.astype(BF16)
    lo = (a - hi.astype(F32)).astype(BF16)
    return hi, lo


def _silu(x):
    return x * jax.nn.sigmoid(x)


def _modulated_norm(x, g, scale, shift):
    r = lax.rsqrt(jnp.mean(x * x, axis=-1, keepdims=True) + EPS)
    return (x * r) * (g * (1.0 + scale)) + shift


def _params(n_axes):
    return pltpu.CompilerParams(dimension_semantics=("arbitrary",) * n_axes,
                                vmem_limit_bytes=VMEM_LIMIT)


def _mod_kernel(c_ref, w_ref, b_ref, o_ref):
    rows = c_ref.shape[0]
    s = _silu(c_ref[...])
    s_hi, s_lo = _split(jnp.concatenate([s, s], axis=0))
    upper = lax.broadcasted_iota(I32, (2 * rows, 1), 0) < rows
    w_hi, w_lo = _split(w_ref[...])
    by_hi = _dot(jnp.where(upper, s_hi, s_lo), w_hi)
    o_ref[...] = by_hi[:rows] + by_hi[rows:] + _dot(s_hi[:rows], w_lo) + b_ref[...]


def _mod_call(c_rows, w_mod, b_mod):
    rows, d = c_rows.shape
    n = w_mod.shape[1]
    tn = D_MODEL
    return pl.pallas_call(
        _mod_kernel,
        grid=(n // tn,),
        in_specs=[pl.BlockSpec((rows, d), lambda j: (0, 0)),
                  pl.BlockSpec((d, tn), lambda j: (0, j)),
                  pl.BlockSpec((1, tn), lambda j: (0, j))],
        out_specs=pl.BlockSpec((rows, tn), lambda j: (0, j)),
        out_shape=jax.ShapeDtypeStruct((rows, n), F32),
        compiler_params=_params(1),
        name="mod",
    )(c_rows, w_mod, b_mod)


def _for_row_tiles(seq_len, phases):
    n = seq_len // ROW_TILE
    if n == 1:
        yield from phases(0)
    else:
        def body(i, carry):
            for _ in phases(i):
                pass
            return carry
        lax.fori_loop(0, n, body, 0)
        yield


def _tile_rows(tile, offset=0, size=ROW_TILE):
    if isinstance(tile, int):
        return pl.ds(tile * ROW_TILE + offset, size)
    return pl.ds(pl.multiple_of(tile * ROW_TILE + offset, size), size)


def _mixer_kernel(*refs, seqs_per_step, has_state_in, has_state_out, stages_w_in, **static):
    refs = list(refs)
    n_in = 2 + (2 if has_state_in else 0)
    per_seq_in, refs = [refs[0]] + refs[2:n_in], [refs[1]] + refs[n_in:]
    mod_ref, weights, refs = refs[0], refs[1:9], refs[9:]
    n_out = 1 + (2 if has_state_out else 0)
    per_seq_out, refs = refs[:n_out], refs[n_out:]
    if stages_w_in:
        win_f32_ref, win_bf_ref, scratch = weights[1], refs[0], refs[1:]
        weights = weights[:1] + [win_bf_ref] + weights[2:]

        @pl.when(pl.program_id(0) == 0)
        def _():
            for c0 in range(0, P_PAD, LANES):
                n = min(LANES, P_TOT - c0)
                cols = win_f32_ref[c0:c0 + n, :]
                if n < LANES:
                    cols = jnp.concatenate([cols, jnp.zeros((LANES - n, D_MODEL), F32)], axis=0)
                win_bf_ref[:, c0:c0 + LANES] = jnp.transpose(cols).astype(BF16)
    else:
        scratch = refs
    programs = []
    for j in range(seqs_per_step):
        ins = [r.at[j] for r in per_seq_in]
        outs = [r.at[j] for r in per_seq_out]
        programs.append(_mixer_sequence(ins[0], mod_ref, ins[1:], weights, outs[0], outs[1:],
                                        [r.at[j] for r in scratch], **static))
    while programs:
        programs = [p for p in programs if next(p, "done") != "done"]


def _mixer_sequence(x_ref, mod_ref, s0_refs, weights, x1_ref, sout_refs, scratch, *, seq_len, period):
    has_state_in = bool(s0_refs)
    has_state_out = bool(sout_refs)
    g1_ref, win_ref, wconv_ref, bconv_ref, wup_ref, bup_ref, ggla_ref, wout_ref = weights
    og_ref, qd_ref, kd_ref, kst_ref, dect_ref, v_ref, s_ref, sst_ref, o_ref, ya_ref = scratch

    c = GLA_CHUNK
    tile_chunks = ROW_TILE // c
    n_tiles = seq_len // ROW_TILE
    n_pairs = GLA_HEADS // 2
    pair_k = 2 * GLA_DK
    pair_v = 2 * GLA_DV
    m = mod_ref[0]

    def stage1(ti):
        rows = _tile_rows(ti)
        h = _modulated_norm(x_ref[rows, :], g1_ref[...], m[1:2], m[0:1]).astype(BF16)
        row_i = lax.broadcasted_iota(I32, (ROW_TILE, 1), 0)

        p_gate = _dot(h, win_ref[:, OFF_OG:P_PAD])
        og_ref[rows, :] = p_gate[:, :GLA_DV_TOT]
        z = _dot(p_gate[:, GLA_DV_TOT:].astype(BF16), wup_ref[...]) + bup_ref[...]
        la = (jnp.minimum(z, 0.0) - jnp.log(1.0 + jnp.exp(-jnp.abs(z)))) * (1.0 / GLA_TAU)
        col_j = lax.broadcasted_iota(I32, (1, ROW_TILE), 1)
        same_chunk = (row_i & -c) == (col_j & -c)
        lower = jnp.where(same_chunk & (col_j <= row_i), 1.0, 0.0).astype(BF16)
        la_parts = jnp.concatenate(_split(la), axis=1)
        n_gate = 2 * GLA_DK_TOT
        pre = _dot(lower, la_parts)
        pre = pre[:, :n_gate] + pre[:, n_gate:]
        tot = jnp.concatenate([jnp.broadcast_to(pre[(n + 1) * c - 1:(n + 1) * c], (c, n_gate))
                               for n in range(tile_chunks)], axis=0)
        yield
        p_qkv = _dot(h, win_ref[:, OFF_Q:OFF_OG])
        q = p_qkv[:, :GLA_DK_TOT] * (GLA_DK ** -0.5)
        k = p_qkv[:, GLA_DK_TOT:2 * GLA_DK_TOT]
        v_ref[rows, :] = p_qkv[:, 2 * GLA_DK_TOT:].astype(BF16)
        for d in range(2):
            cols = slice(d * GLA_DK_TOT, (d + 1) * GLA_DK_TOT)
            if d == 0:
                bq = pre[:, cols]
                bk = tot[:, cols] - bq
            else:
                bk = pre[:, cols] - la[:, cols]
                bq = tot[:, cols] - bk
            qd_ref[d, rows, :] = (q * jnp.exp(bq)).astype(BF16)
            kd_ref[d, rows, :] = (k * jnp.exp(-bq)).astype(BF16)
            kst_ref[d, ti] = jnp.transpose(k * jnp.exp(bk)).astype(BF16)
            totals = [tot[n * c:n * c + 1, cols] for n in range(tile_chunks)]
            totals.append(jnp.zeros((LANES - tile_chunks, GLA_DK_TOT), F32))
            dect_ref[d, ti] = jnp.transpose(jnp.exp(jnp.concatenate(totals, axis=0)))
        yield
        p_conv = _dot(h, win_ref[:, :OFF_Q])
        pos = row_i & (period - 1)
        u = p_conv[:, OFF_XC:OFF_XC + D_CONV] * p_conv[:, OFF_XV:OFF_XV + D_CONV]
        u_prev = jnp.where(pos == 0, 0.0, pltpu.roll(u, 1, 0))
        u_next = jnp.where(pos == period - 1, 0.0, pltpu.roll(u, ROW_TILE - 1, 0))
        conv = u_prev * wconv_ref[0:1, :] + u * wconv_ref[1:2, :] + u_next * wconv_ref[2:3, :] + bconv_ref[...]
        ya_ref[rows, :] = (p_conv[:, OFF_XB:OFF_XB + D_CONV] * conv).astype(BF16)
        yield

    yield from _for_row_tiles(seq_len, stage1)

    for d in range(2):
        for pair in range(n_pairs):
            if has_state_in:
                zero = jnp.zeros((GLA_DK, GLA_DV), F32)
                top = jnp.concatenate([s0_refs[d][2 * pair], zero], axis=1)
                bot = jnp.concatenate([zero, s0_refs[d][2 * pair + 1]], axis=1)
                s_ref[d, pair] = jnp.concatenate([top, bot], axis=0)
            else:
                s_ref[d, pair] = jnp.zeros((pair_k, pair_v), F32)

    def scan_tile(i):
        upper_lane = lax.broadcasted_iota(I32, (1, LANES), 1) >= GLA_DK
        qi = lax.broadcasted_iota(I32, (LANES, 1), 0)
        kj = lax.broadcasted_iota(I32, (1, 2 * LANES), 1) & (LANES - 1)
        same_chunk = (qi & c) == (kj & c)
        causal = (same_chunk & (kj <= qi), same_chunk & (kj >= qi))
        for pair in range(n_pairs):
            kl = slice(pair * pair_k, (pair + 1) * pair_k)
            vl = slice(pair * pair_v, (pair + 1) * pair_v)
            for blk in range(ROW_TILE // LANES):
                rows = _tile_rows(i, blk * LANES, LANES)
                att = None
                for d in range(2):
                    kd = kd_ref[d, rows, kl]
                    zk = jnp.zeros_like(kd)
                    keys = jnp.concatenate([jnp.where(upper_lane, zk, kd), jnp.where(upper_lane, kd, zk)], axis=0)
                    a = jnp.where(causal[d], _dot_nt(qd_ref[d, rows, kl], keys), 0.0)
                    att = a if att is None else att + a
                v = v_ref[rows, vl]
                zv = jnp.zeros((LANES, GLA_DV), BF16)
                v_bd = jnp.concatenate([jnp.concatenate([v[:, :GLA_DV], zv], axis=1),
                                        jnp.concatenate([zv, v[:, GLA_DV:]], axis=1)], axis=0)
                o_ref[rows, vl] = _dot(att.astype(BF16), v_bd)
        yield
        key_row = lax.broadcasted_iota(I32, (pair_k, 1), 0)
        val_col = lax.broadcasted_iota(I32, (1, pair_v), 1)
        blockdiag = (key_row >= GLA_DK) == (val_col >= GLA_DV)
        for d in range(2):
            tile = i if d == 0 else n_tiles - 1 - i
            chunks = range(tile_chunks)
            for pair in range(n_pairs):
                kr = slice(pair * pair_k, (pair + 1) * pair_k)
                vl = slice(pair * pair_v, (pair + 1) * pair_v)
                s = s_ref[d, pair]
                for c4 in (chunks if d == 0 else reversed(chunks)):
                    blk, half = divmod(c4, 2)
                    kst = kst_ref[d, tile, kr, blk * LANES:(blk + 1) * LANES]
                    kst = jnp.where(upper_lane if half else ~upper_lane, kst, jnp.zeros_like(kst))
                    kv = jnp.where(blockdiag, _dot(kst, v_ref[_tile_rows(tile, blk * LANES, LANES), vl]), 0.0)
                    sst_ref[pair, tile * tile_chunks + c4, d * pair_k:(d + 1) * pair_k, :] = s.astype(BF16)
                    s = dect_ref[d, tile, kr, c4:c4 + 1] * s + kv
                s_ref[d, pair] = s
        yield

    yield from _for_row_tiles(seq_len, scan_tile)

    if has_state_out:
        for d in range(2):
            for pair in range(n_pairs):
                s = s_ref[d, pair]
                sout_refs[d][2 * pair] = s[0:GLA_DK, 0:GLA_DV]
                sout_refs[d][2 * pair + 1] = s[GLA_DK:, GLA_DV:]

    def stage3(i):
        for pair in range(n_pairs):
            kl = slice(pair * pair_k, (pair + 1) * pair_k)
            vl = slice(pair * pair_v, (pair + 1) * pair_v)
            for c4 in range(tile_chunks):
                crow = _tile_rows(i, c4 * c, c)
                q2 = jnp.concatenate([qd_ref[0, crow, kl], qd_ref[1, crow, kl]], axis=1)
                o_ref[crow, vl] = o_ref[crow, vl] + _dot(q2, sst_ref[pair, i * tile_chunks + c4])
        yield
        rows = _tile_rows(i)
        heads = []
        for h in range(GLA_HEADS):
            hl = slice(h * GLA_DV, (h + 1) * GLA_DV)
            oh = o_ref[rows, hl]
            r = lax.rsqrt(jnp.mean(oh * oh, axis=-1, keepdims=True) + EPS)
            heads.append(oh * r * ggla_ref[:, hl])
        y_b = jnp.concatenate(heads, axis=1) * _silu(og_ref[rows, :])
        y = jnp.concatenate([ya_ref[rows, :], y_b.astype(BF16)], axis=1)
        x1_ref[rows, :] = x_ref[rows, :] + m[2:3] * _dot(y, wout_ref[...])
        yield

    yield from _for_row_tiles(seq_len, stage3)


def _mixer_call(x, mod3, mod_row_of_step, states, weights, *, period, has_state_out, seqs_per_step):
    n_seq, seq_len, _ = x.shape
    sps = seqs_per_step
    has_state_in = states is not None
    stages_w_in = weights[1].dtype == F32
    kernel = functools.partial(_mixer_kernel, seqs_per_step=sps, seq_len=seq_len, period=period,
                               has_state_in=has_state_in, has_state_out=has_state_out, stages_w_in=stages_w_in)
    state_spec = pl.BlockSpec((sps, None, GLA_HEADS, GLA_DK, GLA_DV), lambda b: (b, 0, 0, 0, 0))
    const2 = lambda b: (0, 0)
    in_specs = [pl.BlockSpec((sps, seq_len, D_MODEL), lambda b: (b, 0, 0)),
                pl.BlockSpec((1, N_MOD, D_MODEL), lambda b: (mod_row_of_step(b), 0, 0))]
    args = [x, mod3]
    if has_state_in:
        in_specs += [state_spec, state_spec]
        args += list(states)
    in_specs += [pl.BlockSpec(w.shape, lambda b, nd=w.ndim: (0,) * nd, pipeline_mode=pl.Buffered(1))
                 for w in weights]
    args += list(weights)
    out_specs = [pl.BlockSpec((sps, seq_len, D_MODEL), lambda b: (b, 0, 0))]
    out_shape = [jax.ShapeDtypeStruct((n_seq, seq_len, D_MODEL), F32)]
    if has_state_out:
        out_specs += [state_spec, state_spec]
        out_shape += [jax.ShapeDtypeStruct((n_seq, 1, GLA_HEADS, GLA_DK, GLA_DV), F32)] * 2
    if stages_w_in:
        out_specs.append(pl.BlockSpec((D_MODEL, P_PAD), const2))
        out_shape.append(jax.ShapeDtypeStruct((D_MODEL, P_PAD), BF16))
    n_tiles = seq_len // ROW_TILE
    n_pairs = GLA_HEADS // 2
    per_seq = lambda shape, dtype: pltpu.VMEM((sps,) + shape, dtype)
    scratch = [per_seq((seq_len, GLA_DV_TOT), F32),
               per_seq((2, seq_len, GLA_DK_TOT), BF16),
               per_seq((2, seq_len, GLA_DK_TOT), BF16),
               per_seq((2, n_tiles, GLA_DK_TOT, ROW_TILE), BF16),
               per_seq((2, n_tiles, GLA_DK_TOT, LANES), F32),
               per_seq((seq_len, GLA_DV_TOT), BF16),
               per_seq((2, n_pairs, 2 * GLA_DK, 2 * GLA_DV), F32),
               per_seq((n_pairs, seq_len // GLA_CHUNK, 4 * GLA_DK, 2 * GLA_DV), BF16),
               per_seq((seq_len, GLA_DV_TOT), F32),
               per_seq((seq_len, D_CONV), BF16)]
    return pl.pallas_call(
        kernel,
        grid=(n_seq // sps,),
        in_specs=in_specs,
        out_specs=out_specs,
        out_shape=out_shape,
        scratch_shapes=scratch,
        compiler_params=_params(1),
        name="mixer",
    )(*args)


SUBLANES = 8


ROW_TOKENS = LANES // N_EXPERTS


def _route_kernel(x1_ref, mod_ref, g2_ref, wr2_ref, wrh_ref, xs_ref, rank_ref, gate_ref,
                  pt_ref, h2_ref, spread_ref, bar_ref, cnt_ref, *, seq_len, cap):
    m = mod_ref[0]
    n_rows = seq_len // ROW_TOKENS
    for t in range(seq_len // ROW_TILE):
        rows = slice(t * ROW_TILE, (t + 1) * ROW_TILE)
        h2 = _modulated_norm(x1_ref[rows, :], g2_ref[...], m[4:5], m[3:4])
        hi, lo = _split(h2)
        h2_ref[rows, :] = hi
        by_hi = _dot_nt(wr2_ref[...], hi)
        pt_ref[:, rows] = by_hi[:N_EXPERTS] + by_hi[N_EXPERTS:] + _dot_nt(wrh_ref[...], lo)

    logits = pt_ref[...]
    ex = jnp.exp(logits - jnp.max(logits, axis=0, keepdims=True))
    probs = ex / jnp.sum(ex, axis=0, keepdims=True)
    pt_ref[...] = probs
    pad = jnp.zeros((LANES - N_EXPERTS, seq_len), F32)
    p_tok = jnp.transpose(jnp.concatenate([probs, pad], axis=0))

    spread = p_tok
    sh = N_EXPERTS
    while sh < LANES:
        spread = spread + pltpu.roll(spread, sh, 1)
        sh *= 2
    spread_ref[...] = spread
    sub_j = lax.broadcasted_iota(I32, (SUBLANES, 1), 0)
    lane_g = lax.broadcasted_iota(I32, (1, LANES), 1) >> (N_EXPERTS.bit_length() - 1)
    lane_j = (ROW_TOKENS - lane_g) & (ROW_TOKENS - 1)
    own_group = jnp.where(lane_j == sub_j, spread.reshape(n_rows, SUBLANES, LANES), 0.0)
    bar = jnp.sum(own_group, axis=1, keepdims=True)
    bar_ref[...] = jnp.broadcast_to(bar, (n_rows, SUBLANES, LANES))
    cnt_ref[...] = jnp.zeros((n_rows, SUBLANES, LANES), F32)

    rows_per_block = LANES // ROW_TOKENS

    def count_block(g_s, g_t, relation):
        if isinstance(g_s, int):
            s_rows = slice(g_s * LANES, (g_s + 1) * LANES)
        else:
            s_rows = pl.ds(pl.multiple_of(g_s * LANES, LANES), LANES)
        s_blk = spread_ref[s_rows, :]
        for tl in range(rows_per_block):
            r = g_t * rows_per_block + tl
            bar_r = bar_ref[r]
            acc = cnt_ref[r]
            for sl in range(rows_per_block):
                s_vreg = s_blk[sl * SUBLANES:(sl + 1) * SUBLANES]
                order = relation if relation != "same" else ("before" if sl < tl else "after" if sl > tl else "tie")
                if order == "before":
                    acc = acc + jnp.where(s_vreg >= bar_r, 1.0, 0.0)
                elif order == "after":
                    acc = acc + jnp.where(s_vreg > bar_r, 1.0, 0.0)
                else:
                    acc = acc + jnp.where(s_vreg > bar_r, 1.0, 0.0) \
                        + jnp.where((s_vreg == bar_r) & (sub_j < lane_j), 1.0, 0.0)
            cnt_ref[r] = acc

    n_grp = seq_len // LANES
    if n_grp <= 2:
        for g_t in range(n_grp):
            for g_s in range(n_grp):
                count_block(g_s, g_t, "before" if g_s < g_t else "after" if g_s > g_t else "same")
    else:
        def per_target_group(g_t, carry):
            def before(g_s, c):
                count_block(g_s, g_t, "before")
                return c

            def after(g_s, c):
                count_block(g_s, g_t, "after")
                return c
            lax.fori_loop(0, g_t, before, 0)
            count_block(g_t, g_t, "same")
            lax.fori_loop(g_t + 1, n_grp, after, 0)
            return carry
        lax.fori_loop(0, n_grp, per_target_group, 0)

    counts = jnp.sum(cnt_ref[...], axis=1, keepdims=True)
    counts = jnp.broadcast_to(counts, (n_rows, SUBLANES, LANES)).reshape(seq_len, LANES)
    rank_tok = pltpu.roll(counts, 0, 1, stride=N_EXPERTS, stride_axis=0)
    rank_ref[...] = jnp.transpose(rank_tok)[:N_EXPERTS].astype(I32)

    group = DISPATCH_ROWS // cap
    slot = lax.broadcasted_iota(I32, (cap, 1), 0)
    half = D_MODEL // 2
    for gi in range(N_EXPERTS // group):
        picks = []
        for e in range(gi * group, (gi + 1) * group):
            oh = rank_ref[e:e + 1, :] == slot
            gate = jnp.sum(jnp.where(oh, pt_ref[e:e + 1, :], 0.0), axis=1, keepdims=True)
            gate_ref[e * cap:(e + 1) * cap, :] = jnp.broadcast_to(gate, (cap, LANES))
            picks.append(oh)
        ohb = jnp.where(jnp.concatenate(picks, axis=0), 1.0, 0.0).astype(BF16)
        out_rows = slice(gi * DISPATCH_ROWS, (gi + 1) * DISPATCH_ROWS)
        xs_ref[out_rows, :half] = _dot(ohb, h2_ref[:, :half]).astype(BF16)
        xs_ref[out_rows, half:] = _dot(ohb, h2_ref[:, half:]).astype(BF16)


def _rank_spec(seq_len):
    return pl.BlockSpec((None, N_EXPERTS, seq_len), lambda b: (b, 0, 0))


def _route_call(x1, mod3, mod_row_of_seq, g2, wr_both, wr_hi):
    n_seq, seq_len, _ = x1.shape
    cap = EC_CAPACITY_FACTOR * seq_len // N_EXPERTS
    kernel = functools.partial(_route_kernel, seq_len=seq_len, cap=cap)
    const2 = lambda b: (0, 0)
    rank_rows = (seq_len // ROW_TOKENS, SUBLANES, LANES)
    return pl.pallas_call(
        kernel,
        grid=(n_seq,),
        in_specs=[pl.BlockSpec((None, seq_len, D_MODEL), lambda b: (b, 0, 0)),
                  pl.BlockSpec((1, N_MOD, D_MODEL), lambda b: (mod_row_of_seq(b), 0, 0)),
                  pl.BlockSpec((1, D_MODEL), const2),
                  pl.BlockSpec((2 * N_EXPERTS, D_MODEL), const2),
                  pl.BlockSpec((N_EXPERTS, D_MODEL), const2)],
        out_specs=[pl.BlockSpec((None, N_EXPERTS * cap, D_MODEL), lambda b: (b, 0, 0)),
                   _rank_spec(seq_len),
                   pl.BlockSpec((None, N_EXPERTS * cap, LANES), lambda b: (b, 0, 0))],
        out_shape=[jax.ShapeDtypeStruct((n_seq, N_EXPERTS * cap, D_MODEL), BF16),
                   jax.ShapeDtypeStruct((n_seq, N_EXPERTS, seq_len), I32),
                   jax.ShapeDtypeStruct((n_seq, N_EXPERTS * cap, LANES), F32)],
        scratch_shapes=[pltpu.VMEM((N_EXPERTS, seq_len), F32),
                        pltpu.VMEM((seq_len, D_MODEL), BF16),
                        pltpu.VMEM((seq_len, LANES), F32),
                        pltpu.VMEM(rank_rows, F32),
                        pltpu.VMEM(rank_rows, F32)],
        compiler_params=_params(1),
        name="route",
    )(x1, mod3, g2, wr_both, wr_hi)


def _experts_kernel(xc_ref, xl_ref, gc_ref, gl_ref, wg_ref, wu_ref, wd_ref, yc_ref, yl_ref,
                    wgb_ref, wub_ref, wdb_ref):
    wgb_ref[...] = wg_ref[...].astype(BF16)
    wub_ref[...] = wu_ref[...].astype(BF16)
    wdb_ref[...] = wd_ref[...].astype(BF16)

    def run(x_ref, g_ref, y_ref):
        n_seq, cap, _ = x_ref.shape
        seqs = DISPATCH_ROWS // cap
        for s0 in range(0, n_seq, seqs):
            x = x_ref[s0:s0 + seqs].reshape(DISPATCH_ROWS, D_MODEL)
            a = (_silu(_dot(x, wgb_ref[...])) * _dot(x, wub_ref[...])).astype(BF16)
            gate = g_ref[s0:s0 + seqs].reshape(DISPATCH_ROWS, LANES)
            y = _dot(a, wdb_ref[...]) * jnp.concatenate([gate] * (D_MODEL // LANES), axis=1)
            y_ref[s0:s0 + seqs] = y.astype(BF16).reshape(seqs, cap, D_MODEL)

    run(xc_ref, gc_ref, yc_ref)
    run(xl_ref, gl_ref, yl_ref)


def _experts_call(xs_ctx, xs_lat, gates_ctx, gates_lat, w_gate, w_up, w_down):
    def slot_spec(a):
        n_seq, _, cap, width = a.shape
        return pl.BlockSpec((n_seq, None, cap, width), lambda e: (0, e, 0, 0))

    w_spec = pl.BlockSpec((None, D_MODEL, D_EXPERT), lambda e: (e, 0, 0))
    return pl.pallas_call(
        _experts_kernel,
        grid=(N_EXPERTS,),
        in_specs=[slot_spec(xs_ctx), slot_spec(xs_lat), slot_spec(gates_ctx), slot_spec(gates_lat),
                  w_spec, w_spec, pl.BlockSpec((None, D_EXPERT, D_MODEL), lambda e: (e, 0, 0))],
        out_specs=[slot_spec(xs_ctx), slot_spec(xs_lat)],
        out_shape=[jax.ShapeDtypeStruct(xs_ctx.shape, BF16), jax.ShapeDtypeStruct(xs_lat.shape, BF16)],
        scratch_shapes=[pltpu.VMEM((D_MODEL, D_EXPERT), BF16),
                        pltpu.VMEM((D_MODEL, D_EXPERT), BF16),
                        pltpu.VMEM((D_EXPERT, D_MODEL), BF16)],
        compiler_params=_params(1),
        name="experts",
    )(xs_ctx, xs_lat, gates_ctx, gates_lat, w_gate, w_up, w_down)


def _combine_kernel(x1_ref, y_ref, rank_ref, mod_ref, gf_ref, o_ref, oh_ref, *, seqs_per_step, seq_len, cap):
    m = mod_ref[0]
    slot = lax.broadcasted_iota(I32, (cap, 1), 0)
    for j in range(seqs_per_step):
        for e in range(N_EXPERTS):
            oh_ref[j, e * cap:(e + 1) * cap, :] = jnp.where(rank_ref[j, e:e + 1, :] == slot, 1.0, 0.0).astype(BF16)
        for t in range(seq_len // ROW_TILE):
            rows = slice(t * ROW_TILE, (t + 1) * ROW_TILE)
            moe = _dot_tn(oh_ref[j, :, rows], y_ref[j])
            x2 = x1_ref[j, rows, :] + m[5:6] * moe
            r = lax.rsqrt(jnp.mean(x2 * x2, axis=-1, keepdims=True) + EPS)
            o_ref[j, rows, :] = (x2 * r) * gf_ref[...]


def _combine_call(x1, y, rank, mod3, mod_row_of_step, g_final, *, seqs_per_step):
    n_seq, seq_len, _ = x1.shape
    n_slots = y.shape[1]
    sps = seqs_per_step
    kernel = functools.partial(_combine_kernel, seqs_per_step=sps, seq_len=seq_len, cap=n_slots // N_EXPERTS)
    seq_spec = pl.BlockSpec((sps, seq_len, D_MODEL), lambda b: (b, 0, 0))
    return pl.pallas_call(
        kernel,
        grid=(n_seq // sps,),
        in_specs=[seq_spec,
                  pl.BlockSpec((sps, n_slots, D_MODEL), lambda b: (b, 0, 0)),
                  pl.BlockSpec((sps, N_EXPERTS, seq_len), lambda b: (b, 0, 0)),
                  pl.BlockSpec((1, N_MOD, D_MODEL), lambda b: (mod_row_of_step(b), 0, 0)),
                  pl.BlockSpec((1, D_MODEL), lambda b: (0, 0))],
        out_specs=seq_spec,
        out_shape=jax.ShapeDtypeStruct((n_seq, seq_len, D_MODEL), F32),
        scratch_shapes=[pltpu.VMEM((sps, n_slots, seq_len), BF16)],
        compiler_params=_params(1),
        name="combine",
    )(x1, y, rank, mod3, g_final)


def kernel(x_prompt, x_sample, state_gla_fwd, state_gla_bwd, c, c_ctx, w_mod, b_mod, g_norm1, g_norm2,
           w_in, w_conv, b_conv, w_a_up_f, b_a_f, w_a_up_b, b_a_b, g_gla_norm, w_out, w_router,
           w_gate, w_up, w_down, g_final):
    assert w_mod.shape[0] == 1, "single trunk layer"
    n_ctx, ctx_len, _ = x_prompt.shape
    n_lat, lat_len, _ = x_sample.shape
    ctx_cap = EC_CAPACITY_FACTOR * ctx_len // N_EXPERTS
    lat_cap = EC_CAPACITY_FACTOR * lat_len // N_EXPERTS

    c_rows = jnp.concatenate([c_ctx[None, :], c, jnp.zeros((8 - 1 - n_lat, D_MODEL), F32)], axis=0)
    mod3 = _mod_call(c_rows, w_mod[0], b_mod).reshape(8, N_MOD, D_MODEL)

    w_up_gate = jnp.zeros((P_PAD - OFF_ALOW, 2 * GLA_DK_TOT), F32)
    w_up_gate = w_up_gate.at[:GLA_LOW_RANK, :GLA_DK_TOT].set(w_a_up_f[0])
    w_up_gate = w_up_gate.at[GLA_LOW_RANK:2 * GLA_LOW_RANK, GLA_DK_TOT:].set(w_a_up_b[0]).astype(BF16)
    b_up_gate = jnp.concatenate([b_a_f[0], b_a_b[0]])[None, :]
    mixer_weights = [g_norm1, jnp.transpose(w_in[0]), w_conv[0], b_conv, w_up_gate, b_up_gate,
                     g_gla_norm[0].reshape(1, GLA_DV_TOT), w_out[0].astype(BF16)]
    wr_t = jnp.transpose(w_router[0])
    wr_hi = wr_t.astype(BF16)
    wr_lo = (wr_t - wr_hi.astype(F32)).astype(BF16)

    ctx_row = lambda b: 0
    lat_row = lambda b: b + 1

    x1_ctx, new_f, new_b, w_in_bf = _mixer_call(x_prompt, mod3, ctx_row, None, mixer_weights,
                                                period=ctx_len, has_state_out=True, seqs_per_step=2)
    mixer_weights[1] = w_in_bf
    (x1_lat,) = _mixer_call(x_sample, mod3, lat_row, (state_gla_fwd, state_gla_bwd), mixer_weights,
                            period=GRID_W, has_state_out=False, seqs_per_step=1)

    wr_both = jnp.concatenate([wr_hi, wr_lo], axis=0)
    xs_ctx, rank_ctx, gates_ctx = _route_call(x1_ctx, mod3, ctx_row, g_norm2, wr_both, wr_hi)
    xs_lat, rank_lat, gates_lat = _route_call(x1_lat, mod3, lat_row, g_norm2, wr_both, wr_hi)

    per_expert = lambda a, n, cap: a.reshape(n, N_EXPERTS, cap, a.shape[-1])
    y_ctx, y_lat = _experts_call(per_expert(xs_ctx, n_ctx, ctx_cap), per_expert(xs_lat, n_lat, lat_cap),
                                 per_expert(gates_ctx, n_ctx, ctx_cap), per_expert(gates_lat, n_lat, lat_cap),
                                 w_gate[0], w_up[0], w_down[0])

    g_fin = g_final[None, :]
    y_prompt = _combine_call(x1_ctx, y_ctx.reshape(xs_ctx.shape), rank_ctx, mod3, ctx_row, g_fin, seqs_per_step=4)
    y_sample = _combine_call(x1_lat, y_lat.reshape(xs_lat.shape), rank_lat, mod3, lat_row, g_fin, seqs_per_step=1)
    return y_prompt, y_sample, new_f, new_b
```

```python
import functools

import jax
import jax.numpy as jnp
from jax import lax
from jax.experimental import pallas as pl
from jax.experimental.pallas import tpu as pltpu

F32 = jnp.float32
BF16 = jnp.bfloat16
I32 = jnp.int32

D_MODEL = 1024
D_CONV = D_MODEL // 2
GRID_W = 64
GLA_HEADS = 4
GLA_DK = 64
GLA_DV = 128
GLA_DK_TOT = GLA_HEADS * GLA_DK
GLA_DV_TOT = GLA_HEADS * GLA_DV
GLA_LOW_RANK = 16
GLA_TAU = 16.0
GLA_CHUNK = 64
N_EXPERTS = 16
EC_CAPACITY_FACTOR = 2
D_EXPERT = 1024
N_MOD = 6
EPS = 1e-6
LOG2_E = 1.4426950408889634

OFF_XB = 0
OFF_XC = D_CONV
OFF_XV = 2 * D_CONV
OFF_Q = 3 * D_CONV
OFF_K = OFF_Q + GLA_DK_TOT
OFF_V = OFF_K + GLA_DK_TOT
OFF_OG = OFF_V + GLA_DV_TOT
OFF_ALOW = OFF_OG + GLA_DV_TOT
P_TOT = OFF_ALOW + 2 * GLA_LOW_RANK

LANES = 128
P_PAD = -(-P_TOT // LANES) * LANES
ROW_TILE = 256
TOKEN_TILE = 512
DISPATCH_ROWS = 512
VMEM_LIMIT = 56 * 1024 * 1024


def _dot(a, b):
    return jnp.dot(a, b, preferred_element_type=F32)


def _dot_nt(a, b):
    return lax.dot_general(a, b, (((1,), (1,)), ((), ())), preferred_element_type=F32)


def _dot_tn(a, b):
    return lax.dot_general(a, b, (((0,), (0,)), ((), ())), preferred_element_type=F32)


def _split(a):
    hi = a.astype(BF16)
    lo = (a - hi.astype(F32)).astype(BF16)
    return hi, lo


def _silu(x):
    return x * jax.nn.sigmoid(x)


def _modulated_norm(x, g, scale, shift):
    r = lax.rsqrt(jnp.mean(x * x, axis=-1, keepdims=True) + EPS)
    return (x * r) * (g * (1.0 + scale)) + shift


def _params(n_axes):
    return pltpu.CompilerParams(dimension_semantics=("arbitrary",) * n_axes,
                                vmem_limit_bytes=VMEM_LIMIT)


def _mod_kernel(c_ref, w_ref, b_ref, o_ref):
    rows = c_ref.shape[0]
    s = _silu(c_ref[...])
    s_hi, s_lo = _split(jnp.concatenate([s, s], axis=0))
    upper = lax.broadcasted_iota(I32, (2 * rows, 1), 0) < rows
    w_hi, w_lo = _split(w_ref[...])
    by_hi = _dot(jnp.where(upper, s_hi, s_lo), w_hi)
    o_ref[...] = by_hi[:rows] + by_hi[rows:] + _dot(s_hi[:rows], w_lo) + b_ref[...]


def _mod_call(c_rows, w_mod, b_mod):
    rows, d = c_rows.shape
    n = w_mod.shape[1]
    tn = 2 * D_MODEL
    return pl.pallas_call(
        _mod_kernel,
        grid=(n // tn,),
        in_specs=[pl.BlockSpec((rows, d), lambda j: (0, 0)),
                  pl.BlockSpec((d, tn), lambda j: (0, j)),
                  pl.BlockSpec((1, tn), lambda j: (0, j))],
        out_specs=pl.BlockSpec((rows, tn), lambda j: (0, j)),
        out_shape=jax.ShapeDtypeStruct((rows, n), F32),
        compiler_params=_params(1),
        name="mod",
    )(c_rows, w_mod, b_mod)


def _for_row_tiles(seq_len, phases):
    n = seq_len // ROW_TILE
    if n == 1:
        yield from phases(0)
    else:
        def body(i, carry):
            for _ in phases(i):
                pass
            return carry
        lax.fori_loop(0, n, body, 0)
        yield


def _tile_rows(tile, offset=0, size=ROW_TILE):
    if isinstance(tile, int):
        return pl.ds(tile * ROW_TILE + offset, size)
    return pl.ds(pl.multiple_of(tile * ROW_TILE + offset, size), size)


def _mixer_kernel(*refs, seqs_per_step, has_state_in, has_state_out, stages_w_in, **static):
    refs = list(refs)
    n_in = 2 + (2 if has_state_in else 0)
    per_seq_in, refs = [refs[0]] + refs[2:n_in], [refs[1]] + refs[n_in:]
    mod_ref, weights, refs = refs[0], refs[1:9], refs[9:]
    n_out = 1 + (2 if has_state_out else 0)
    per_seq_out, refs = refs[:n_out], refs[n_out:]
    if stages_w_in:
        win_f32_ref, win_bf_ref, scratch = weights[1], refs[0], refs[1:]
        weights = weights[:1] + [win_bf_ref] + weights[2:]

        @pl.when(pl.program_id(0) == 0)
        def _():
            for c0 in range(0, P_PAD, LANES):
                n = min(LANES, P_TOT - c0)
                cols = win_f32_ref[c0:c0 + n, :]
                if n < LANES:
                    cols = jnp.concatenate([cols, jnp.zeros((LANES - n, D_MODEL), F32)], axis=0)
                win_bf_ref[:, c0:c0 + LANES] = jnp.transpose(cols).astype(BF16)
    else:
        scratch = refs
    programs = []
    for j in range(seqs_per_step):
        ins = [r.at[j] for r in per_seq_in]
        outs = [r.at[j] for r in per_seq_out]
        programs.append(_mixer_sequence(ins[0], mod_ref, ins[1:], weights, outs[0], outs[1:],
                                        [r.at[j] for r in scratch], **static))
    started = 0
    while programs:
        started = min(started + 1, len(programs))
        running = [p for p in programs[:started] if next(p, "done") != "done"]
        programs = running + programs[started:]
        started = len(running)


def _mixer_sequence(x_ref, mod_ref, s0_refs, weights, x1_ref, sout_refs, scratch, *, seq_len, period):
    has_state_in = bool(s0_refs)
    has_state_out = bool(sout_refs)
    g1_ref, win_ref, wconv_ref, bconv_ref, wup_ref, bup_ref, ggla_ref, wout_ref = weights
    og_ref, qd_ref, kd_ref, kst_ref, dect_ref, v_ref, s_ref, sst_ref, o_ref, ya_ref = scratch

    c = GLA_CHUNK
    tile_chunks = ROW_TILE // c
    n_tiles = seq_len // ROW_TILE
    n_pairs = GLA_HEADS // 2
    pair_k = 2 * GLA_DK
    pair_v = 2 * GLA_DV
    m = mod_ref[0]

    def stage1(ti):
        rows = _tile_rows(ti)
        h = _modulated_norm(x_ref[rows, :], g1_ref[...], m[1:2], m[0:1]).astype(BF16)
        row_i = lax.broadcasted_iota(I32, (ROW_TILE, 1), 0)
        yield
        p_gate = _dot(h, win_ref[:, OFF_OG:P_PAD])
        og_ref[rows, :] = p_gate[:, :GLA_DV_TOT]
        yield
        z = _dot(p_gate[:, GLA_DV_TOT:].astype(BF16), wup_ref[...]) + bup_ref[...]
        la = (jnp.minimum(z, 0.0) - jnp.log(1.0 + jnp.exp(-jnp.abs(z)))) * (LOG2_E / GLA_TAU)
        col_j = lax.broadcasted_iota(I32, (1, ROW_TILE), 1)
        same_chunk = (row_i & -c) == (col_j & -c)
        lower = jnp.where(same_chunk & (col_j <= row_i), 1.0, 0.0).astype(BF16)
        la_parts = jnp.concatenate(_split(la), axis=1)
        n_gate = 2 * GLA_DK_TOT
        yield
        pre = _dot(lower, la_parts)
        pre = pre[:, :n_gate] + pre[:, n_gate:]
        tot = jnp.concatenate([jnp.broadcast_to(pre[(n + 1) * c - 1:(n + 1) * c], (c, n_gate))
                               for n in range(tile_chunks)], axis=0)
        p_qkv = _dot(h, win_ref[:, OFF_Q:OFF_OG])
        yield
        q = p_qkv[:, :GLA_DK_TOT] * (GLA_DK ** -0.5)
        k = p_qkv[:, GLA_DK_TOT:2 * GLA_DK_TOT]
        v_ref[rows, :] = p_qkv[:, 2 * GLA_DK_TOT:].astype(BF16)
        for d in range(2):
            cols = slice(d * GLA_DK_TOT, (d + 1) * GLA_DK_TOT)
            if d == 0:
                bq = pre[:, cols]
                bk = tot[:, cols] - bq
            else:
                bk = pre[:, cols] - la[:, cols]
                bq = tot[:, cols] - bk
            qd_ref[d, rows, :] = (q * jnp.exp2(bq)).astype(BF16)
            kd_ref[d, rows, :] = (k * jnp.exp2(-bq)).astype(BF16)
            kst_ref[d, ti] = jnp.transpose(k * jnp.exp2(bk)).astype(BF16)
            totals = [tot[n * c:n * c + 1, cols] for n in range(tile_chunks)]
            totals.append(jnp.zeros((LANES - tile_chunks, GLA_DK_TOT), F32))
            dect_ref[d, ti] = jnp.transpose(jnp.exp2(jnp.concatenate(totals, axis=0)))
        yield
        p_conv = _dot(h, win_ref[:, :OFF_Q])
        yield
        pos = row_i & (period - 1)
        u = p_conv[:, OFF_XC:OFF_XC + D_CONV] * p_conv[:, OFF_XV:OFF_XV + D_CONV]
        u_prev = jnp.where(pos == 0, 0.0, pltpu.roll(u, 1, 0))
        u_next = jnp.where(pos == period - 1, 0.0, pltpu.roll(u, ROW_TILE - 1, 0))
        conv = u_prev * wconv_ref[0:1, :] + u * wconv_ref[1:2, :] + u_next * wconv_ref[2:3, :] + bconv_ref[...]
        ya_ref[rows, :] = (p_conv[:, OFF_XB:OFF_XB + D_CONV] * conv).astype(BF16)
        yield

    yield from _for_row_tiles(seq_len, stage1)

    for d in range(2):
        for pair in range(n_pairs):
            if has_state_in:
                zero = jnp.zeros((GLA_DK, GLA_DV), F32)
                top = jnp.concatenate([s0_refs[d][2 * pair], zero], axis=1)
                bot = jnp.concatenate([zero, s0_refs[d][2 * pair + 1]], axis=1)
                s_ref[d, pair] = jnp.concatenate([top, bot], axis=0)
            else:
                s_ref[d, pair] = jnp.zeros((pair_k, pair_v), F32)

    def scan_tile(i):
        upper_lane = lax.broadcasted_iota(I32, (1, LANES), 1) >= GLA_DK
        qi = lax.broadcasted_iota(I32, (LANES, 1), 0)
        kj = lax.broadcasted_iota(I32, (1, 2 * LANES), 1) & (LANES - 1)
        same_chunk = (qi & c) == (kj & c)
        causal = (same_chunk & (kj <= qi), same_chunk & (kj >= qi))
        for pair in range(n_pairs):
            kl = slice(pair * pair_k, (pair + 1) * pair_k)
            vl = slice(pair * pair_v, (pair + 1) * pair_v)
            for blk in range(ROW_TILE // LANES):
                rows = _tile_rows(i, blk * LANES, LANES)
                att = None
                for d in range(2):
                    kd = kd_ref[d, rows, kl]
                    zk = jnp.zeros_like(kd)
                    keys = jnp.concatenate([jnp.where(upper_lane, zk, kd), jnp.where(upper_lane, kd, zk)], axis=0)
                    a = jnp.where(causal[d], _dot_nt(qd_ref[d, rows, kl], keys), 0.0)
                    att = a if att is None else att + a
                v = v_ref[rows, vl]
                zv = jnp.zeros((LANES, GLA_DV), BF16)
                v_bd = jnp.concatenate([jnp.concatenate([v[:, :GLA_DV], zv], axis=1),
                                        jnp.concatenate([zv, v[:, GLA_DV:]], axis=1)], axis=0)
                o_ref[rows, vl] = _dot(att.astype(BF16), v_bd)
        yield
        key_row = lax.broadcasted_iota(I32, (pair_k, 1), 0)
        val_col = lax.broadcasted_iota(I32, (1, pair_v), 1)
        blockdiag = (key_row >= GLA_DK) == (val_col >= GLA_DV)
        for d in range(2):
            tile = i if d == 0 else n_tiles - 1 - i
            chunks = range(tile_chunks)
            for pair in range(n_pairs):
                kr = slice(pair * pair_k, (pair + 1) * pair_k)
                vl = slice(pair * pair_v, (pair + 1) * pair_v)
                s = s_ref[d, pair]
                for c4 in (chunks if d == 0 else reversed(chunks)):
                    blk, half = divmod(c4, 2)
                    kst = kst_ref[d, tile, kr, blk * LANES:(blk + 1) * LANES]
                    kst = jnp.where(upper_lane if half else ~upper_lane, kst, jnp.zeros_like(kst))
                    kv = jnp.where(blockdiag, _dot(kst, v_ref[_tile_rows(tile, blk * LANES, LANES), vl]), 0.0)
                    sst_ref[pair, tile * tile_chunks + c4, d * pair_k:(d + 1) * pair_k, :] = s.astype(BF16)
                    s = dect_ref[d, tile, kr, c4:c4 + 1] * s + kv
                s_ref[d, pair] = s
        yield

    yield from _for_row_tiles(seq_len, scan_tile)

    if has_state_out:
        for d in range(2):
            for pair in range(n_pairs):
                s = s_ref[d, pair]
                sout_refs[d][2 * pair] = s[0:GLA_DK, 0:GLA_DV]
                sout_refs[d][2 * pair + 1] = s[GLA_DK:, GLA_DV:]

    def stage3(i):
        for pair in range(n_pairs):
            kl = slice(pair * pair_k, (pair + 1) * pair_k)
            vl = slice(pair * pair_v, (pair + 1) * pair_v)
            for c4 in range(tile_chunks):
                crow = _tile_rows(i, c4 * c, c)
                q2 = jnp.concatenate([qd_ref[0, crow, kl], qd_ref[1, crow, kl]], axis=1)
                o_ref[crow, vl] = o_ref[crow, vl] + _dot(q2, sst_ref[pair, i * tile_chunks + c4])
        yield
        rows = _tile_rows(i)
        heads = []
        for h in range(GLA_HEADS):
            hl = slice(h * GLA_DV, (h + 1) * GLA_DV)
            oh = o_ref[rows, hl]
            r = lax.rsqrt(jnp.mean(oh * oh, axis=-1, keepdims=True) + EPS)
            heads.append(oh * r * ggla_ref[:, hl])
        y_b = jnp.concatenate(heads, axis=1) * _silu(og_ref[rows, :])
        y = jnp.concatenate([ya_ref[rows, :], y_b.astype(BF16)], axis=1)
        x1_ref[rows, :] = x_ref[rows, :] + m[2:3] * _dot(y, wout_ref[...])
        yield

    yield from _for_row_tiles(seq_len, stage3)


def _mixer_call(x, mod3, mod_row_of_step, states, weights, *, period, has_state_out, seqs_per_step):
    n_seq, seq_len, _ = x.shape
    sps = seqs_per_step
    has_state_in = states is not None
    stages_w_in = weights[1].dtype == F32
    kernel = functools.partial(_mixer_kernel, seqs_per_step=sps, seq_len=seq_len, period=period,
                               has_state_in=has_state_in, has_state_out=has_state_out, stages_w_in=stages_w_in)
    state_spec = pl.BlockSpec((sps, None, GLA_HEADS, GLA_DK, GLA_DV), lambda b: (b, 0, 0, 0, 0))
    const2 = lambda b: (0, 0)
    in_specs = [pl.BlockSpec((sps, seq_len, D_MODEL), lambda b: (b, 0, 0)),
                pl.BlockSpec((1, N_MOD, D_MODEL), lambda b: (mod_row_of_step(b), 0, 0))]
    args = [x, mod3]
    if has_state_in:
        in_specs += [state_spec, state_spec]
        args += list(states)
    in_specs += [pl.BlockSpec(w.shape, lambda b, nd=w.ndim: (0,) * nd, pipeline_mode=pl.Buffered(1))
                 for w in weights]
    args += list(weights)
    out_specs = [pl.BlockSpec((sps, seq_len, D_MODEL), lambda b: (b, 0, 0))]
    out_shape = [jax.ShapeDtypeStruct((n_seq, seq_len, D_MODEL), F32)]
    if has_state_out:
        out_specs += [state_spec, state_spec]
        out_shape += [jax.ShapeDtypeStruct((n_seq, 1, GLA_HEADS, GLA_DK, GLA_DV), F32)] * 2
    if stages_w_in:
        out_specs.append(pl.BlockSpec((D_MODEL, P_PAD), const2))
        out_shape.append(jax.ShapeDtypeStruct((D_MODEL, P_PAD), BF16))
    n_tiles = seq_len // ROW_TILE
    n_pairs = GLA_HEADS // 2
    per_seq = lambda shape, dtype: pltpu.VMEM((sps,) + shape, dtype)
    scratch = [per_seq((seq_len, GLA_DV_TOT), F32),
               per_seq((2, seq_len, GLA_DK_TOT), BF16),
               per_seq((2, seq_len, GLA_DK_TOT), BF16),
               per_seq((2, n_tiles, GLA_DK_TOT, ROW_TILE), BF16),
               per_seq((2, n_tiles, GLA_DK_TOT, LANES), F32),
               per_seq((seq_len, GLA_DV_TOT), BF16),
               per_seq((2, n_pairs, 2 * GLA_DK, 2 * GLA_DV), F32),
               per_seq((n_pairs, seq_len // GLA_CHUNK, 4 * GLA_DK, 2 * GLA_DV), BF16),
               per_seq((seq_len, GLA_DV_TOT), F32),
               per_seq((seq_len, D_CONV), BF16)]
    return pl.pallas_call(
        kernel,
        grid=(n_seq // sps,),
        in_specs=in_specs,
        out_specs=out_specs,
        out_shape=out_shape,
        scratch_shapes=scratch,
        compiler_params=_params(1),
        name="mixer",
    )(*args)


SUBLANES = 8


ROW_TOKENS = LANES // N_EXPERTS


def _route_kernel(x1_ref, mod_ref, g2_ref, wr2_ref, xs_ref, rank_ref, gate_ref,
                  pt_ref, h2_ref, spread_ref, bar_ref, cnt_ref, *, seq_len, cap):
    m = mod_ref[0]
    n_rows = seq_len // ROW_TOKENS
    for t in range(seq_len // ROW_TILE):
        rows = slice(t * ROW_TILE, (t + 1) * ROW_TILE)
        h2 = _modulated_norm(x1_ref[rows, :], g2_ref[...], m[4:5], m[3:4])
        hi = h2.astype(BF16)
        h2_ref[rows, :] = hi
        by_hi = _dot_nt(wr2_ref[...], hi)
        pt_ref[:, rows] = by_hi[:N_EXPERTS] + by_hi[N_EXPERTS:]

    logits = pt_ref[...]
    ex = jnp.exp(logits - jnp.max(logits, axis=0, keepdims=True))
    probs = ex / jnp.sum(ex, axis=0, keepdims=True)
    pt_ref[...] = probs
    pad = jnp.zeros((LANES - N_EXPERTS, seq_len), F32)
    p_tok = jnp.transpose(jnp.concatenate([probs, pad], axis=0))

    spread = p_tok
    sh = N_EXPERTS
    while sh < LANES:
        spread = spread + pltpu.roll(spread, sh, 1)
        sh *= 2
    spread_ref[...] = spread
    sub_j = lax.broadcasted_iota(I32, (SUBLANES, 1), 0)
    lane_g = lax.broadcasted_iota(I32, (1, LANES), 1) >> (N_EXPERTS.bit_length() - 1)
    lane_j = (ROW_TOKENS - lane_g) & (ROW_TOKENS - 1)
    own_group = jnp.where(lane_j == sub_j, spread.reshape(n_rows, SUBLANES, LANES), 0.0)
    bar = jnp.sum(own_group, axis=1, keepdims=True)
    bar_ref[...] = jnp.broadcast_to(bar, (n_rows, SUBLANES, LANES))
    cnt_ref[...] = jnp.zeros((n_rows, SUBLANES, LANES), F32)

    rows_per_block = LANES // ROW_TOKENS

    def count_block(g_s, g_t, relation):
        if isinstance(g_s, int):
            s_rows = slice(g_s * LANES, (g_s + 1) * LANES)
        else:
            s_rows = pl.ds(pl.multiple_of(g_s * LANES, LANES), LANES)
        s_blk = spread_ref[s_rows, :]
        for tl in range(rows_per_block):
            r = g_t * rows_per_block + tl
            bar_r = bar_ref[r]
            acc = cnt_ref[r]
            for sl in range(rows_per_block):
                s_vreg = s_blk[sl * SUBLANES:(sl + 1) * SUBLANES]
                order = relation if relation != "same" else ("before" if sl < tl else "after" if sl > tl else "tie")
                if order == "before":
                    acc = acc + jnp.where(s_vreg >= bar_r, 1.0, 0.0)
                elif order == "after":
                    acc = acc + jnp.where(s_vreg > bar_r, 1.0, 0.0)
                else:
                    acc = acc + jnp.where(s_vreg > bar_r, 1.0, 0.0) \
                        + jnp.where((s_vreg == bar_r) & (sub_j < lane_j), 1.0, 0.0)
            cnt_ref[r] = acc

    n_grp = seq_len // LANES
    if n_grp <= 2:
        for g_t in range(n_grp):
            for g_s in range(n_grp):
                count_block(g_s, g_t, "before" if g_s < g_t else "after" if g_s > g_t else "same")
    else:
        def per_target_group(g_t, carry):
            def before(g_s, c):
                count_block(g_s, g_t, "before")
                return c

            def after(g_s, c):
                count_block(g_s, g_t, "after")
                return c
            lax.fori_loop(0, g_t, before, 0)
            count_block(g_t, g_t, "same")
            lax.fori_loop(g_t + 1, n_grp, after, 0)
            return carry
        lax.fori_loop(0, n_grp, per_target_group, 0)

    counts = jnp.sum(cnt_ref[...], axis=1, keepdims=True)
    counts = jnp.broadcast_to(counts, (n_rows, SUBLANES, LANES)).reshape(seq_len, LANES)
    rank_tok = pltpu.roll(counts, 0, 1, stride=N_EXPERTS, stride_axis=0)
    rank_ref[...] = jnp.transpose(rank_tok)[:N_EXPERTS].astype(I32)

    group = DISPATCH_ROWS // cap
    slot = lax.broadcasted_iota(I32, (cap, 1), 0)
    half = D_MODEL // 2
    for gi in range(N_EXPERTS // group):
        picks = []
        for e in range(gi * group, (gi + 1) * group):
            oh = rank_ref[e:e + 1, :] == slot
            gate = jnp.sum(jnp.where(oh, pt_ref[e:e + 1, :], 0.0), axis=1, keepdims=True)
            gate_ref[e * cap:(e + 1) * cap, :] = jnp.broadcast_to(gate, (cap, LANES))
            picks.append(oh)
        ohb = jnp.where(jnp.concatenate(picks, axis=0), 1.0, 0.0).astype(BF16)
        out_rows = slice(gi * DISPATCH_ROWS, (gi + 1) * DISPATCH_ROWS)
        xs_ref[out_rows, :half] = _dot(ohb, h2_ref[:, :half]).astype(BF16)
        xs_ref[out_rows, half:] = _dot(ohb, h2_ref[:, half:]).astype(BF16)


def _rank_spec(seq_len):
    return pl.BlockSpec((None, N_EXPERTS, seq_len), lambda b: (b, 0, 0))


def _route_call(x1, mod3, mod_row_of_seq, g2, wr_both):
    n_seq, seq_len, _ = x1.shape
    cap = EC_CAPACITY_FACTOR * seq_len // N_EXPERTS
    kernel = functools.partial(_route_kernel, seq_len=seq_len, cap=cap)
    const2 = lambda b: (0, 0)
    rank_rows = (seq_len // ROW_TOKENS, SUBLANES, LANES)
    return pl.pallas_call(
        kernel,
        grid=(n_seq,),
        in_specs=[pl.BlockSpec((None, seq_len, D_MODEL), lambda b: (b, 0, 0)),
                  pl.BlockSpec((1, N_MOD, D_MODEL), lambda b: (mod_row_of_seq(b), 0, 0)),
                  pl.BlockSpec((1, D_MODEL), const2),
                  pl.BlockSpec((2 * N_EXPERTS, D_MODEL), const2)],
        out_specs=[pl.BlockSpec((None, N_EXPERTS * cap, D_MODEL), lambda b: (b, 0, 0)),
                   _rank_spec(seq_len),
                   pl.BlockSpec((None, N_EXPERTS * cap, LANES), lambda b: (b, 0, 0))],
        out_shape=[jax.ShapeDtypeStruct((n_seq, N_EXPERTS * cap, D_MODEL), BF16),
                   jax.ShapeDtypeStruct((n_seq, N_EXPERTS, seq_len), I32),
                   jax.ShapeDtypeStruct((n_seq, N_EXPERTS * cap, LANES), F32)],
        scratch_shapes=[pltpu.VMEM((N_EXPERTS, seq_len), F32),
                        pltpu.VMEM((seq_len, D_MODEL), BF16),
                        pltpu.VMEM((seq_len, LANES), F32),
                        pltpu.VMEM(rank_rows, F32),
                        pltpu.VMEM(rank_rows, F32)],
        compiler_params=_params(1),
        name="route",
    )(x1, mod3, g2, wr_both)


def _experts_kernel(xc_ref, xl_ref, gc_ref, gl_ref, wg_ref, wu_ref, wd_ref, yc_ref, yl_ref,
                    wgb_ref, wub_ref, wdb_ref):
    wgb_ref[...] = wg_ref[...].astype(BF16)
    wub_ref[...] = wu_ref[...].astype(BF16)
    wdb_ref[...] = wd_ref[...].astype(BF16)

    def run(x_ref, g_ref, y_ref):
        n_seq, cap, _ = x_ref.shape
        seqs = DISPATCH_ROWS // cap
        for s0 in range(0, n_seq, seqs):
            x = x_ref[s0:s0 + seqs].reshape(DISPATCH_ROWS, D_MODEL)
            a = (_silu(_dot(x, wgb_ref[...])) * _dot(x, wub_ref[...])).astype(BF16)
            gate = g_ref[s0:s0 + seqs].reshape(DISPATCH_ROWS, LANES)
            y = _dot(a, wdb_ref[...]) * jnp.concatenate([gate] * (D_MODEL // LANES), axis=1)
            y_ref[s0:s0 + seqs] = y.astype(BF16).reshape(seqs, cap, D_MODEL)

    run(xc_ref, gc_ref, yc_ref)
    run(xl_ref, gl_ref, yl_ref)


def _experts_call(xs_ctx, xs_lat, gates_ctx, gates_lat, w_gate, w_up, w_down):
    def slot_spec(a):
        n_seq, _, cap, width = a.shape
        return pl.BlockSpec((n_seq, None, cap, width), lambda e: (0, e, 0, 0))

    w_spec = pl.BlockSpec((None, D_MODEL, D_EXPERT), lambda e: (e, 0, 0))
    return pl.pallas_call(
        _experts_kernel,
        grid=(N_EXPERTS,),
        in_specs=[slot_spec(xs_ctx), slot_spec(xs_lat), slot_spec(gates_ctx), slot_spec(gates_lat),
                  w_spec, w_spec, pl.BlockSpec((None, D_EXPERT, D_MODEL), lambda e: (e, 0, 0))],
        out_specs=[slot_spec(xs_ctx), slot_spec(xs_lat)],
        out_shape=[jax.ShapeDtypeStruct(xs_ctx.shape, BF16), jax.ShapeDtypeStruct(xs_lat.shape, BF16)],
        scratch_shapes=[pltpu.VMEM((D_MODEL, D_EXPERT), BF16),
                        pltpu.VMEM((D_MODEL, D_EXPERT), BF16),
                        pltpu.VMEM((D_EXPERT, D_MODEL), BF16)],
        compiler_params=_params(1),
        name="experts",
    )(xs_ctx, xs_lat, gates_ctx, gates_lat, w_gate, w_up, w_down)


def _combine_kernel(x1_ref, y_ref, rank_ref, mod_ref, gf_ref, o_ref, oh_ref, *, seqs_per_step, seq_len, cap):
    m = mod_ref[0]
    slot = lax.broadcasted_iota(I32, (cap, 1), 0)
    for j in range(seqs_per_step):
        for e in range(N_EXPERTS):
            oh_ref[j, e * cap:(e + 1) * cap, :] = jnp.where(rank_ref[j, e:e + 1, :] == slot, 1.0, 0.0).astype(BF16)
        for t in range(seq_len // ROW_TILE):
            rows = slice(t * ROW_TILE, (t + 1) * ROW_TILE)
            moe = _dot_tn(oh_ref[j, :, rows], y_ref[j])
            x2 = x1_ref[j, rows, :] + m[5:6] * moe
            r = lax.rsqrt(jnp.mean(x2 * x2, axis=-1, keepdims=True) + EPS)
            o_ref[j, rows, :] = (x2 * r) * gf_ref[...]


def _combine_call(x1, y, rank, mod3, mod_row_of_step, g_final, *, seqs_per_step):
    n_seq, seq_len, _ = x1.shape
    n_slots = y.shape[1]
    sps = seqs_per_step
    kernel = functools.partial(_combine_kernel, seqs_per_step=sps, seq_len=seq_len, cap=n_slots // N_EXPERTS)
    seq_spec = pl.BlockSpec((sps, seq_len, D_MODEL), lambda b: (b, 0, 0))
    return pl.pallas_call(
        kernel,
        grid=(n_seq // sps,),
        in_specs=[seq_spec,
                  pl.BlockSpec((sps, n_slots, D_MODEL), lambda b: (b, 0, 0)),
                  pl.BlockSpec((sps, N_EXPERTS, seq_len), lambda b: (b, 0, 0)),
                  pl.BlockSpec((1, N_MOD, D_MODEL), lambda b: (mod_row_of_step(b), 0, 0)),
                  pl.BlockSpec((1, D_MODEL), lambda b: (0, 0))],
        out_specs=seq_spec,
        out_shape=jax.ShapeDtypeStruct((n_seq, seq_len, D_MODEL), F32),
        scratch_shapes=[pltpu.VMEM((sps, n_slots, seq_len), BF16)],
        compiler_params=_params(1),
        name="combine",
    )(x1, y, rank, mod3, g_final)


def kernel(x_prompt, x_sample, state_gla_fwd, state_gla_bwd, c, c_ctx, w_mod, b_mod, g_norm1, g_norm2,
           w_in, w_conv, b_conv, w_a_up_f, b_a_f, w_a_up_b, b_a_b, g_gla_norm, w_out, w_router,
           w_gate, w_up, w_down, g_final):
    assert w_mod.shape[0] == 1, "single trunk layer"
    n_ctx, ctx_len, _ = x_prompt.shape
    n_lat, lat_len, _ = x_sample.shape
    ctx_cap = EC_CAPACITY_FACTOR * ctx_len // N_EXPERTS
    lat_cap = EC_CAPACITY_FACTOR * lat_len // N_EXPERTS

    c_rows = jnp.concatenate([c_ctx[None, :], c, jnp.zeros((8 - 1 - n_lat, D_MODEL), F32)], axis=0)
    mod3 = _mod_call(c_rows, w_mod[0], b_mod).reshape(8, N_MOD, D_MODEL)

    w_up_gate = jnp.zeros((P_PAD - OFF_ALOW, 2 * GLA_DK_TOT), F32)
    w_up_gate = w_up_gate.at[:GLA_LOW_RANK, :GLA_DK_TOT].set(w_a_up_f[0])
    w_up_gate = w_up_gate.at[GLA_LOW_RANK:2 * GLA_LOW_RANK, GLA_DK_TOT:].set(w_a_up_b[0]).astype(BF16)
    b_up_gate = jnp.concatenate([b_a_f[0], b_a_b[0]])[None, :]
    mixer_weights = [g_norm1, jnp.transpose(w_in[0]), w_conv[0], b_conv, w_up_gate, b_up_gate,
                     g_gla_norm[0].reshape(1, GLA_DV_TOT), w_out[0].astype(BF16)]
    wr_t = jnp.transpose(w_router[0])
    wr_hi = wr_t.astype(BF16)
    wr_lo = (wr_t - wr_hi.astype(F32)).astype(BF16)

    ctx_row = lambda b: 0
    lat_row = lambda b: b + 1

    x1_ctx, new_f, new_b, w_in_bf = _mixer_call(x_prompt, mod3, ctx_row, None, mixer_weights,
                                                period=ctx_len, has_state_out=True, seqs_per_step=2)
    mixer_weights[1] = w_in_bf
    (x1_lat,) = _mixer_call(x_sample, mod3, lat_row, (state_gla_fwd, state_gla_bwd), mixer_weights,
                            period=GRID_W, has_state_out=False, seqs_per_step=1)

    wr_both = jnp.concatenate([wr_hi, wr_lo], axis=0)
    xs_ctx, rank_ctx, gates_ctx = _route_call(x1_ctx, mod3, ctx_row, g_norm2, wr_both)
    xs_lat, rank_lat, gates_lat = _route_call(x1_lat, mod3, lat_row, g_norm2, wr_both)

    per_expert = lambda a, n, cap: a.reshape(n, N_EXPERTS, cap, a.shape[-1])
    y_ctx, y_lat = _experts_call(per_expert(xs_ctx, n_ctx, ctx_cap), per_expert(xs_lat, n_lat, lat_cap),
                                 per_expert(gates_ctx, n_ctx, ctx_cap), per_expert(gates_lat, n_lat, lat_cap),
                                 w_gate[0], w_up[0], w_down[0])

    g_fin = g_final[None, :]
    y_prompt = _combine_call(x1_ctx, y_ctx.reshape(xs_ctx.shape), rank_ctx, mod3, ctx_row, g_fin, seqs_per_step=4)
    y_sample = _combine_call(x1_lat, y_lat.reshape(xs_lat.shape), rank_lat, mod3, lat_row, g_fin, seqs_per_step=1)
    return y_prompt, y_sample, new_f, new_b
```

```python
import functools

import jax
import jax.numpy as jnp
from jax import lax
from jax.experimental import pallas as pl
from jax.experimental.pallas import tpu as pltpu

F32 = jnp.float32
BF16 = jnp.bfloat16
I32 = jnp.int32

D_MODEL = 1024
D_CONV = D_MODEL // 2
GRID_W = 64
GLA_HEADS = 4
GLA_DK = 64
GLA_DV = 128
GLA_DK_TOT = GLA_HEADS * GLA_DK
GLA_DV_TOT = GLA_HEADS * GLA_DV
GLA_LOW_RANK = 16
GLA_TAU = 16.0
GLA_CHUNK = 64
N_EXPERTS = 16
EC_CAPACITY_FACTOR = 2
D_EXPERT = 1024
N_MOD = 6
EPS = 1e-6
LOG2_E = 1.4426950408889634

OFF_XB = 0
OFF_XC = D_CONV
OFF_XV = 2 * D_CONV
OFF_Q = 3 * D_CONV
OFF_K = OFF_Q + GLA_DK_TOT
OFF_V = OFF_K + GLA_DK_TOT
OFF_OG = OFF_V + GLA_DV_TOT
OFF_ALOW = OFF_OG + GLA_DV_TOT
P_TOT = OFF_ALOW + 2 * GLA_LOW_RANK

LANES = 128
P_PAD = -(-P_TOT // LANES) * LANES
ROW_TILE = 256
TOKEN_TILE = 512
DISPATCH_ROWS = 512
VMEM_LIMIT = 56 * 1024 * 1024


def _dot(a, b):
    return jnp.dot(a, b, preferred_element_type=F32)


def _dot_nt(a, b):
    return lax.dot_general(a, b, (((1,), (1,)), ((), ())), preferred_element_type=F32)


def _dot_tn(a, b):
    return lax.dot_general(a, b, (((0,), (0,)), ((), ())), preferred_element_type=F32)


def _split(a):
    hi = a.astype(BF16)
    lo = (a - hi.astype(F32)).astype(BF16)
    return hi, lo


def _silu(x):
    return x * jax.nn.sigmoid(x)


def _modulated_norm(x, g, scale, shift):
    r = lax.rsqrt(jnp.mean(x * x, axis=-1, keepdims=True) + EPS)
    return (x * r) * (g * (1.0 + scale)) + shift


def _params(n_axes):
    return pltpu.CompilerParams(dimension_semantics=("arbitrary",) * n_axes,
                                vmem_limit_bytes=VMEM_LIMIT)


def _mod_kernel(c_ref, w_ref, b_ref, o_ref):
    rows = c_ref.shape[0]
    s = _silu(c_ref[...])
    s_hi, s_lo = _split(jnp.concatenate([s, s], axis=0))
    upper = lax.broadcasted_iota(I32, (2 * rows, 1), 0) < rows
    w_hi, w_lo = _split(w_ref[...])
    by_hi = _dot(jnp.where(upper, s_hi, s_lo), w_hi)
    part = by_hi[:rows] + by_hi[rows:] + _dot(s_hi[:rows], w_lo)

    @pl.when(pl.program_id(0) == 0)
    def _():
        o_ref[...] = part + b_ref[...]

    @pl.when(pl.program_id(0) != 0)
    def _():
        o_ref[...] = o_ref[...] + part


def _mod_call(c_rows, w_mod, b_mod):
    rows, d = c_rows.shape
    n = w_mod.shape[1]
    tk = D_MODEL // 4
    return pl.pallas_call(
        _mod_kernel,
        grid=(d // tk,),
        in_specs=[pl.BlockSpec((rows, tk), lambda k: (0, k)),
                  pl.BlockSpec((tk, n), lambda k: (k, 0)),
                  pl.BlockSpec((1, n), lambda k: (0, 0))],
        out_specs=pl.BlockSpec((rows, n), lambda k: (0, 0)),
        out_shape=jax.ShapeDtypeStruct((rows, n), F32),
        compiler_params=_params(1),
        name="mod",
    )(c_rows, w_mod, b_mod)


def _for_row_tiles(seq_len, phases):
    n = seq_len // ROW_TILE
    if n == 1:
        yield from phases(0)
    else:
        def body(i, carry):
            for _ in phases(i):
                pass
            return carry
        lax.fori_loop(0, n, body, 0)
        yield


def _tile_rows(tile, offset=0, size=ROW_TILE):
    if isinstance(tile, int):
        return pl.ds(tile * ROW_TILE + offset, size)
    return pl.ds(pl.multiple_of(tile * ROW_TILE + offset, size), size)


def _mixer_kernel(*refs, seqs_per_step, has_state_in, has_state_out, stages_w_in, **static):
    refs = list(refs)
    n_in = 2 + (2 if has_state_in else 0)
    per_seq_in, refs = [refs[0]] + refs[2:n_in], [refs[1]] + refs[n_in:]
    mod_ref, weights, refs = refs[0], refs[1:9], refs[9:]
    n_out = 1 + (2 if has_state_out else 0)
    per_seq_out, refs = refs[:n_out], refs[n_out:]
    if stages_w_in:
        win_f32_ref, win_bf_ref, scratch = weights[1], refs[0], refs[1:]
        weights = weights[:1] + [win_bf_ref] + weights[2:]

        @pl.when(pl.program_id(0) == 0)
        def _():
            for c0 in range(0, P_PAD, LANES):
                n = min(LANES, P_TOT - c0)
                cols = win_f32_ref[c0:c0 + n, :]
                if n < LANES:
                    cols = jnp.concatenate([cols, jnp.zeros((LANES - n, D_MODEL), F32)], axis=0)
                win_bf_ref[:, c0:c0 + LANES] = jnp.transpose(cols).astype(BF16)
    else:
        scratch = refs
    programs = []
    for j in range(seqs_per_step):
        ins = [r.at[j] for r in per_seq_in]
        outs = [r.at[j] for r in per_seq_out]
        programs.append(_mixer_sequence(ins[0], mod_ref, ins[1:], weights, outs[0], outs[1:],
                                        [r.at[j] for r in scratch], **static))
    started = 0
    while programs:
        started = min(started + 1, len(programs))
        running = [p for p in programs[:started] if next(p, "done") != "done"]
        programs = running + programs[started:]
        started = len(running)


def _mixer_sequence(x_ref, mod_ref, s0_refs, weights, x1_ref, sout_refs, scratch, *, seq_len, period):
    has_state_in = bool(s0_refs)
    has_state_out = bool(sout_refs)
    g1_ref, win_ref, wconv_ref, bconv_ref, wup_ref, bup_ref, ggla_ref, wout_ref = weights
    og_ref, qd_ref, kd_ref, kst_ref, dect_ref, v_ref, s_ref, sst_ref, o_ref, ya_ref = scratch

    c = GLA_CHUNK
    tile_chunks = ROW_TILE // c
    n_tiles = seq_len // ROW_TILE
    n_pairs = GLA_HEADS // 2
    pair_k = 2 * GLA_DK
    pair_v = 2 * GLA_DV
    m = mod_ref[0]

    def stage1(ti):
        rows = _tile_rows(ti)
        h = _modulated_norm(x_ref[rows, :], g1_ref[...], m[1:2], m[0:1]).astype(BF16)
        row_i = lax.broadcasted_iota(I32, (ROW_TILE, 1), 0)
        yield
        p_gate = _dot(h, win_ref[:, OFF_OG:P_PAD])
        og_ref[rows, :] = p_gate[:, :GLA_DV_TOT]
        yield
        z = _dot(p_gate[:, GLA_DV_TOT:].astype(BF16), wup_ref[...]) + bup_ref[...]
        la = (jnp.minimum(z, 0.0) - jnp.log(1.0 + jnp.exp(-jnp.abs(z)))) * (LOG2_E / GLA_TAU)
        col_j = lax.broadcasted_iota(I32, (1, ROW_TILE), 1)
        same_chunk = (row_i & -c) == (col_j & -c)
        lower = jnp.where(same_chunk & (col_j <= row_i), 1.0, 0.0).astype(BF16)
        la_parts = jnp.concatenate(_split(la), axis=1)
        n_gate = 2 * GLA_DK_TOT
        yield
        pre = _dot(lower, la_parts)
        pre = pre[:, :n_gate] + pre[:, n_gate:]
        tot = jnp.concatenate([jnp.broadcast_to(pre[(n + 1) * c - 1:(n + 1) * c], (c, n_gate))
                               for n in range(tile_chunks)], axis=0)
        p_qkv = _dot(h, win_ref[:, OFF_Q:OFF_OG])
        yield
        q = p_qkv[:, :GLA_DK_TOT] * (GLA_DK ** -0.5)
        k = p_qkv[:, GLA_DK_TOT:2 * GLA_DK_TOT]
        v_ref[rows, :] = p_qkv[:, 2 * GLA_DK_TOT:].astype(BF16)
        for d in range(2):
            cols = slice(d * GLA_DK_TOT, (d + 1) * GLA_DK_TOT)
            if d == 0:
                bq = pre[:, cols]
                bk = tot[:, cols] - bq
            else:
                bk = pre[:, cols] - la[:, cols]
                bq = tot[:, cols] - bk
            qd_ref[d, rows, :] = (q * jnp.exp2(bq)).astype(BF16)
            kd_ref[d, rows, :] = (k * jnp.exp2(-bq)).astype(BF16)
            kst_ref[d, ti] = jnp.transpose(k * jnp.exp2(bk)).astype(BF16)
            totals = [tot[n * c:n * c + 1, cols] for n in range(tile_chunks)]
            totals.append(jnp.zeros((LANES - tile_chunks, GLA_DK_TOT), F32))
            dect_ref[d, ti] = jnp.transpose(jnp.exp2(jnp.concatenate(totals, axis=0)))
        yield
        p_conv = _dot(h, win_ref[:, :OFF_Q])
        yield
        pos = row_i & (period - 1)
        u = p_conv[:, OFF_XC:OFF_XC + D_CONV] * p_conv[:, OFF_XV:OFF_XV + D_CONV]
        u_prev = jnp.where(pos == 0, 0.0, pltpu.roll(u, 1, 0))
        u_next = jnp.where(pos == period - 1, 0.0, pltpu.roll(u, ROW_TILE - 1, 0))
        conv = u_prev * wconv_ref[0:1, :] + u * wconv_ref[1:2, :] + u_next * wconv_ref[2:3, :] + bconv_ref[...]
        ya_ref[rows, :] = (p_conv[:, OFF_XB:OFF_XB + D_CONV] * conv).astype(BF16)
        yield

    yield from _for_row_tiles(seq_len, stage1)

    for d in range(2):
        for pair in range(n_pairs):
            if has_state_in:
                zero = jnp.zeros((GLA_DK, GLA_DV), F32)
                top = jnp.concatenate([s0_refs[d][2 * pair], zero], axis=1)
                bot = jnp.concatenate([zero, s0_refs[d][2 * pair + 1]], axis=1)
                s_ref[d, pair] = jnp.concatenate([top, bot], axis=0)
            else:
                s_ref[d, pair] = jnp.zeros((pair_k, pair_v), F32)

    def scan_tile(i):
        upper_lane = lax.broadcasted_iota(I32, (1, LANES), 1) >= GLA_DK
        qi = lax.broadcasted_iota(I32, (LANES, 1), 0)
        kj = lax.broadcasted_iota(I32, (1, 2 * LANES), 1) & (LANES - 1)
        same_chunk = (qi & c) == (kj & c)
        causal = (same_chunk & (kj <= qi), same_chunk & (kj >= qi))
        for pair in range(n_pairs):
            kl = slice(pair * pair_k, (pair + 1) * pair_k)
            vl = slice(pair * pair_v, (pair + 1) * pair_v)
            for blk in range(ROW_TILE // LANES):
                rows = _tile_rows(i, blk * LANES, LANES)
                att = None
                for d in range(2):
                    kd = kd_ref[d, rows, kl]
                    zk = jnp.zeros_like(kd)
                    keys = jnp.concatenate([jnp.where(upper_lane, zk, kd), jnp.where(upper_lane, kd, zk)], axis=0)
                    a = jnp.where(causal[d], _dot_nt(qd_ref[d, rows, kl], keys), 0.0)
                    att = a if att is None else att + a
                v = v_ref[rows, vl]
                zv = jnp.zeros((LANES, GLA_DV), BF16)
                v_bd = jnp.concatenate([jnp.concatenate([v[:, :GLA_DV], zv], axis=1),
                                        jnp.concatenate([zv, v[:, GLA_DV:]], axis=1)], axis=0)
                o_ref[rows, vl] = _dot(att.astype(BF16), v_bd)
        yield
        key_row = lax.broadcasted_iota(I32, (pair_k, 1), 0)
        val_col = lax.broadcasted_iota(I32, (1, pair_v), 1)
        blockdiag = (key_row >= GLA_DK) == (val_col >= GLA_DV)
        for d in range(2):
            tile = i if d == 0 else n_tiles - 1 - i
            chunks = range(tile_chunks)
            for pair in range(n_pairs):
                kr = slice(pair * pair_k, (pair + 1) * pair_k)
                vl = slice(pair * pair_v, (pair + 1) * pair_v)
                s = s_ref[d, pair]
                for c4 in (chunks if d == 0 else reversed(chunks)):
                    blk, half = divmod(c4, 2)
                    kst = kst_ref[d, tile, kr, blk * LANES:(blk + 1) * LANES]
                    kst = jnp.where(upper_lane if half else ~upper_lane, kst, jnp.zeros_like(kst))
                    kv = jnp.where(blockdiag, _dot(kst, v_ref[_tile_rows(tile, blk * LANES, LANES), vl]), 0.0)
                    sst_ref[pair, tile * tile_chunks + c4, d * pair_k:(d + 1) * pair_k, :] = s.astype(BF16)
                    s = dect_ref[d, tile, kr, c4:c4 + 1] * s + kv
                s_ref[d, pair] = s
        yield

    yield from _for_row_tiles(seq_len, scan_tile)

    if has_state_out:
        for d in range(2):
            for pair in range(n_pairs):
                s = s_ref[d, pair]
                sout_refs[d][2 * pair] = s[0:GLA_DK, 0:GLA_DV]
                sout_refs[d][2 * pair + 1] = s[GLA_DK:, GLA_DV:]

    def stage3(i):
        for pair in range(n_pairs):
            kl = slice(pair * pair_k, (pair + 1) * pair_k)
            vl = slice(pair * pair_v, (pair + 1) * pair_v)
            for c4 in range(tile_chunks):
                crow = _tile_rows(i, c4 * c, c)
                q2 = jnp.concatenate([qd_ref[0, crow, kl], qd_ref[1, crow, kl]], axis=1)
                o_ref[crow, vl] = o_ref[crow, vl] + _dot(q2, sst_ref[pair, i * tile_chunks + c4])
        yield
        rows = _tile_rows(i)
        heads = []
        for h in range(GLA_HEADS):
            hl = slice(h * GLA_DV, (h + 1) * GLA_DV)
            oh = o_ref[rows, hl]
            r = lax.rsqrt(jnp.mean(oh * oh, axis=-1, keepdims=True) + EPS)
            heads.append(oh * r * ggla_ref[:, hl])
        y_b = jnp.concatenate(heads, axis=1) * _silu(og_ref[rows, :])
        y = jnp.concatenate([ya_ref[rows, :], y_b.astype(BF16)], axis=1)
        x1_ref[rows, :] = x_ref[rows, :] + m[2:3] * _dot(y, wout_ref[...])
        yield

    yield from _for_row_tiles(seq_len, stage3)


def _mixer_call(x, mod3, mod_row_of_step, states, weights, *, period, has_state_out, seqs_per_step):
    n_seq, seq_len, _ = x.shape
    sps = seqs_per_step
    has_state_in = states is not None
    stages_w_in = weights[1].dtype == F32
    kernel = functools.partial(_mixer_kernel, seqs_per_step=sps, seq_len=seq_len, period=period,
                               has_state_in=has_state_in, has_state_out=has_state_out, stages_w_in=stages_w_in)
    state_spec = pl.BlockSpec((sps, None, GLA_HEADS, GLA_DK, GLA_DV), lambda b: (b, 0, 0, 0, 0))
    const2 = lambda b: (0, 0)
    in_specs = [pl.BlockSpec((sps, seq_len, D_MODEL), lambda b: (b, 0, 0)),
                pl.BlockSpec((1, N_MOD, D_MODEL), lambda b: (mod_row_of_step(b), 0, 0))]
    args = [x, mod3]
    if has_state_in:
        in_specs += [state_spec, state_spec]
        args += list(states)
    in_specs += [pl.BlockSpec(w.shape, lambda b, nd=w.ndim: (0,) * nd, pipeline_mode=pl.Buffered(1))
                 for w in weights]
    args += list(weights)
    out_specs = [pl.BlockSpec((sps, seq_len, D_MODEL), lambda b: (b, 0, 0))]
    out_shape = [jax.ShapeDtypeStruct((n_seq, seq_len, D_MODEL), F32)]
    if has_state_out:
        out_specs += [state_spec, state_spec]
        out_shape += [jax.ShapeDtypeStruct((n_seq, 1, GLA_HEADS, GLA_DK, GLA_DV), F32)] * 2
    if stages_w_in:
        out_specs.append(pl.BlockSpec((D_MODEL, P_PAD), const2))
        out_shape.append(jax.ShapeDtypeStruct((D_MODEL, P_PAD), BF16))
    n_tiles = seq_len // ROW_TILE
    n_pairs = GLA_HEADS // 2
    per_seq = lambda shape, dtype: pltpu.VMEM((sps,) + shape, dtype)
    scratch = [per_seq((seq_len, GLA_DV_TOT), F32),
               per_seq((2, seq_len, GLA_DK_TOT), BF16),
               per_seq((2, seq_len, GLA_DK_TOT), BF16),
               per_seq((2, n_tiles, GLA_DK_TOT, ROW_TILE), BF16),
               per_seq((2, n_tiles, GLA_DK_TOT, LANES), F32),
               per_seq((seq_len, GLA_DV_TOT), BF16),
               per_seq((2, n_pairs, 2 * GLA_DK, 2 * GLA_DV), F32),
               per_seq((n_pairs, seq_len // GLA_CHUNK, 4 * GLA_DK, 2 * GLA_DV), BF16),
               per_seq((seq_len, GLA_DV_TOT), F32),
               per_seq((seq_len, D_CONV), BF16)]
    return pl.pallas_call(
        kernel,
        grid=(n_seq // sps,),
        in_specs=in_specs,
        out_specs=out_specs,
        out_shape=out_shape,
        scratch_shapes=scratch,
        compiler_params=_params(1),
        name="mixer",
    )(*args)


SUBLANES = 8


ROW_TOKENS = LANES // N_EXPERTS


def _route_kernel(x1_ref, mod_ref, g2_ref, wr2_ref, xs_ref, rank_ref, gate_ref,
                  pt_ref, h2_ref, spread_ref, bar_ref, cnt_ref, win_ref, *, seq_len, cap):
    m = mod_ref[0]
    n_rows = seq_len // ROW_TOKENS
    for t in range(seq_len // ROW_TILE):
        rows = slice(t * ROW_TILE, (t + 1) * ROW_TILE)
        h2 = _modulated_norm(x1_ref[rows, :], g2_ref[...], m[4:5], m[3:4])
        hi = h2.astype(BF16)
        h2_ref[rows, :] = hi
        by_hi = _dot_nt(wr2_ref[...], hi)
        pt_ref[:, rows] = by_hi[:N_EXPERTS] + by_hi[N_EXPERTS:]

    logits = pt_ref[...]
    ex = jnp.exp(logits - jnp.max(logits, axis=0, keepdims=True))
    probs = ex / jnp.sum(ex, axis=0, keepdims=True)
    pt_ref[...] = probs
    pad = jnp.zeros((LANES - N_EXPERTS, seq_len), F32)
    p_tok = jnp.transpose(jnp.concatenate([probs, pad], axis=0))

    spread = p_tok
    sh = N_EXPERTS
    while sh < LANES:
        spread = spread + pltpu.roll(spread, sh, 1)
        sh *= 2
    spread_ref[...] = spread
    sub_j = lax.broadcasted_iota(I32, (SUBLANES, 1), 0)
    lane_g = lax.broadcasted_iota(I32, (1, LANES), 1) >> (N_EXPERTS.bit_length() - 1)
    lane_j = (ROW_TOKENS - lane_g) & (ROW_TOKENS - 1)
    own_group = jnp.where(lane_j == sub_j, spread.reshape(n_rows, SUBLANES, LANES), 0.0)
    bar = jnp.sum(own_group, axis=1, keepdims=True)
    bar_ref[...] = jnp.broadcast_to(bar, (n_rows, SUBLANES, LANES))
    cnt_ref[...] = jnp.zeros((n_rows, SUBLANES, LANES), F32)

    rows_per_block = LANES // ROW_TOKENS

    def count_block(g_s, g_t, relation):
        if isinstance(g_s, int):
            s_rows = slice(g_s * LANES, (g_s + 1) * LANES)
        else:
            s_rows = pl.ds(pl.multiple_of(g_s * LANES, LANES), LANES)
        s_blk = spread_ref[s_rows, :]
        wins = [jnp.zeros((SUBLANES, LANES), F32)] * rows_per_block
        for tl in range(rows_per_block):
            r = g_t * rows_per_block + tl
            bar_r = bar_ref[r]
            acc = cnt_ref[r]
            for sl in range(rows_per_block):
                s_vreg = s_blk[sl * SUBLANES:(sl + 1) * SUBLANES]
                order = relation if relation != "same" else ("before" if sl < tl else "after" if sl > tl else "tie")
                if order == "before":
                    won = jnp.where(s_vreg >= bar_r, 1.0, 0.0)
                    acc = acc + won
                    if relation == "before":
                        wins[sl] = wins[sl] + won
                elif order == "after":
                    acc = acc + jnp.where(s_vreg > bar_r, 1.0, 0.0)
                else:
                    acc = acc + jnp.where(s_vreg > bar_r, 1.0, 0.0) \
                        + jnp.where((s_vreg == bar_r) & (sub_j < lane_j), 1.0, 0.0)
            cnt_ref[r] = acc
        if relation == "before":
            if isinstance(g_s, int):
                w_rows = slice(g_s * rows_per_block, (g_s + 1) * rows_per_block)
            else:
                w_rows = pl.ds(g_s * rows_per_block, rows_per_block)
            win_ref[w_rows] = win_ref[w_rows] + jnp.stack(wins)

    n_grp = seq_len // LANES
    win_ref[...] = jnp.zeros((n_rows, SUBLANES, LANES), F32)
    if n_grp <= 2:
        for g_t in range(n_grp):
            for g_s in range(g_t + 1):
                count_block(g_s, g_t, "before" if g_s < g_t else "same")
    else:
        def per_target_group(g_t, carry):
            def before(g_s, c):
                count_block(g_s, g_t, "before")
                return c
            lax.fori_loop(0, g_t, before, 0)
            count_block(g_t, g_t, "same")
            return carry
        lax.fori_loop(0, n_grp, per_target_group, 0)

    counts = jnp.sum(cnt_ref[...], axis=1, keepdims=True)
    counts = jnp.broadcast_to(counts, (n_rows, SUBLANES, LANES)).reshape(seq_len, LANES)
    rank_tok = pltpu.roll(counts, 0, 1, stride=N_EXPERTS, stride_axis=0)
    wins = win_ref[...].reshape(seq_len, LANES)
    sh = N_EXPERTS
    while sh < LANES:
        wins = wins + pltpu.roll(wins, sh, 1)
        sh *= 2
    tok = lax.broadcasted_iota(I32, (seq_len, 1), 0)
    later = ((n_grp - 1 - (tok >> (LANES.bit_length() - 1))) * LANES).astype(F32)
    rank_tok = rank_tok + (later - wins)
    rank_ref[...] = jnp.transpose(rank_tok)[:N_EXPERTS].astype(I32)

    group = DISPATCH_ROWS // cap
    slot = lax.broadcasted_iota(I32, (cap, 1), 0)
    half = D_MODEL // 2
    for gi in range(N_EXPERTS // group):
        picks = []
        for e in range(gi * group, (gi + 1) * group):
            oh = rank_ref[e:e + 1, :] == slot
            gate = jnp.sum(jnp.where(oh, pt_ref[e:e + 1, :], 0.0), axis=1, keepdims=True)
            gate_ref[e * cap:(e + 1) * cap, :] = jnp.broadcast_to(gate, (cap, LANES))
            picks.append(oh)
        ohb = jnp.where(jnp.concatenate(picks, axis=0), 1.0, 0.0).astype(BF16)
        out_rows = slice(gi * DISPATCH_ROWS, (gi + 1) * DISPATCH_ROWS)
        xs_ref[out_rows, :half] = _dot(ohb, h2_ref[:, :half]).astype(BF16)
        xs_ref[out_rows, half:] = _dot(ohb, h2_ref[:, half:]).astype(BF16)


def _rank_spec(seq_len):
    return pl.BlockSpec((None, N_EXPERTS, seq_len), lambda b: (b, 0, 0))


def _route_call(x1, mod3, mod_row_of_seq, g2, wr_both):
    n_seq, seq_len, _ = x1.shape
    cap = EC_CAPACITY_FACTOR * seq_len // N_EXPERTS
    kernel = functools.partial(_route_kernel, seq_len=seq_len, cap=cap)
    const2 = lambda b: (0, 0)
    rank_rows = (seq_len // ROW_TOKENS, SUBLANES, LANES)
    return pl.pallas_call(
        kernel,
        grid=(n_seq,),
        in_specs=[pl.BlockSpec((None, seq_len, D_MODEL), lambda b: (b, 0, 0)),
                  pl.BlockSpec((1, N_MOD, D_MODEL), lambda b: (mod_row_of_seq(b), 0, 0)),
                  pl.BlockSpec((1, D_MODEL), const2),
                  pl.BlockSpec((2 * N_EXPERTS, D_MODEL), const2)],
        out_specs=[pl.BlockSpec((None, N_EXPERTS * cap, D_MODEL), lambda b: (b, 0, 0)),
                   _rank_spec(seq_len),
                   pl.BlockSpec((None, N_EXPERTS * cap, LANES), lambda b: (b, 0, 0))],
        out_shape=[jax.ShapeDtypeStruct((n_seq, N_EXPERTS * cap, D_MODEL), BF16),
                   jax.ShapeDtypeStruct((n_seq, N_EXPERTS, seq_len), I32),
                   jax.ShapeDtypeStruct((n_seq, N_EXPERTS * cap, LANES), F32)],
        scratch_shapes=[pltpu.VMEM((N_EXPERTS, seq_len), F32),
                        pltpu.VMEM((seq_len, D_MODEL), BF16),
                        pltpu.VMEM((seq_len, LANES), F32),
                        pltpu.VMEM(rank_rows, F32),
                        pltpu.VMEM(rank_rows, F32),
                        pltpu.VMEM(rank_rows, F32)],
        compiler_params=_params(1),
        name="route",
    )(x1, mod3, g2, wr_both)


def _experts_kernel(xc_ref, xl_ref, gc_ref, gl_ref, wg_ref, wu_ref, wd_ref, yc_ref, yl_ref,
                    wgb_ref, wub_ref, wdb_ref):
    wgb_ref[...] = wg_ref[...].astype(BF16)
    wub_ref[...] = wu_ref[...].astype(BF16)
    wdb_ref[...] = wd_ref[...].astype(BF16)

    def run(x_ref, g_ref, y_ref):
        n_seq, cap, _ = x_ref.shape
        seqs = DISPATCH_ROWS // cap
        for s0 in range(0, n_seq, seqs):
            x = x_ref[s0:s0 + seqs].reshape(DISPATCH_ROWS, D_MODEL)
            a = (_silu(_dot(x, wgb_ref[...])) * _dot(x, wub_ref[...])).astype(BF16)
            gate = g_ref[s0:s0 + seqs].reshape(DISPATCH_ROWS, LANES)
            y = _dot(a, wdb_ref[...]) * jnp.concatenate([gate] * (D_MODEL // LANES), axis=1)
            y_ref[s0:s0 + seqs] = y.astype(BF16).reshape(seqs, cap, D_MODEL)

    run(xc_ref, gc_ref, yc_ref)
    run(xl_ref, gl_ref, yl_ref)


def _experts_call(xs_ctx, xs_lat, gates_ctx, gates_lat, w_gate, w_up, w_down):
    def slot_spec(a):
        n_seq, _, cap, width = a.shape
        return pl.BlockSpec((n_seq, None, cap, width), lambda e: (0, e, 0, 0))

    w_spec = pl.BlockSpec((None, D_MODEL, D_EXPERT), lambda e: (e, 0, 0))
    return pl.pallas_call(
        _experts_kernel,
        grid=(N_EXPERTS,),
        in_specs=[slot_spec(xs_ctx), slot_spec(xs_lat), slot_spec(gates_ctx), slot_spec(gates_lat),
                  w_spec, w_spec, pl.BlockSpec((None, D_EXPERT, D_MODEL), lambda e: (e, 0, 0))],
        out_specs=[slot_spec(xs_ctx), slot_spec(xs_lat)],
        out_shape=[jax.ShapeDtypeStruct(xs_ctx.shape, BF16), jax.ShapeDtypeStruct(xs_lat.shape, BF16)],
        scratch_shapes=[pltpu.VMEM((D_MODEL, D_EXPERT), BF16),
                        pltpu.VMEM((D_MODEL, D_EXPERT), BF16),
                        pltpu.VMEM((D_EXPERT, D_MODEL), BF16)],
        compiler_params=_params(1),
        name="experts",
    )(xs_ctx, xs_lat, gates_ctx, gates_lat, w_gate, w_up, w_down)


def _combine_kernel(x1_ref, y_ref, rank_ref, mod_ref, gf_ref, o_ref, oh_ref, *, seqs_per_step, seq_len, cap):
    m = mod_ref[0]
    slot = lax.broadcasted_iota(I32, (cap, 1), 0)
    for j in range(seqs_per_step):
        for e in range(N_EXPERTS):
            oh_ref[j, e * cap:(e + 1) * cap, :] = jnp.where(rank_ref[j, e:e + 1, :] == slot, 1.0, 0.0).astype(BF16)
        for t in range(seq_len // ROW_TILE):
            rows = slice(t * ROW_TILE, (t + 1) * ROW_TILE)
            moe = _dot_tn(oh_ref[j, :, rows], y_ref[j])
            x2 = x1_ref[j, rows, :] + m[5:6] * moe
            r = lax.rsqrt(jnp.mean(x2 * x2, axis=-1, keepdims=True) + EPS)
            o_ref[j, rows, :] = (x2 * r) * gf_ref[...]


def _combine_call(x1, y, rank, mod3, mod_row_of_step, g_final, *, seqs_per_step):
    n_seq, seq_len, _ = x1.shape
    n_slots = y.shape[1]
    sps = seqs_per_step
    kernel = functools.partial(_combine_kernel, seqs_per_step=sps, seq_len=seq_len, cap=n_slots // N_EXPERTS)
    seq_spec = pl.BlockSpec((sps, seq_len, D_MODEL), lambda b: (b, 0, 0))
    return pl.pallas_call(
        kernel,
        grid=(n_seq // sps,),
        in_specs=[seq_spec,
                  pl.BlockSpec((sps, n_slots, D_MODEL), lambda b: (b, 0, 0)),
                  pl.BlockSpec((sps, N_EXPERTS, seq_len), lambda b: (b, 0, 0)),
                  pl.BlockSpec((1, N_MOD, D_MODEL), lambda b: (mod_row_of_step(b), 0, 0)),
                  pl.BlockSpec((1, D_MODEL), lambda b: (0, 0))],
        out_specs=seq_spec,
        out_shape=jax.ShapeDtypeStruct((n_seq, seq_len, D_MODEL), F32),
        scratch_shapes=[pltpu.VMEM((sps, n_slots, seq_len), BF16)],
        compiler_params=_params(1),
        name="combine",
    )(x1, y, rank, mod3, g_final)


def kernel(x_prompt, x_sample, state_gla_fwd, state_gla_bwd, c, c_ctx, w_mod, b_mod, g_norm1, g_norm2,
           w_in, w_conv, b_conv, w_a_up_f, b_a_f, w_a_up_b, b_a_b, g_gla_norm, w_out, w_router,
           w_gate, w_up, w_down, g_final):
    assert w_mod.shape[0] == 1, "single trunk layer"
    n_ctx, ctx_len, _ = x_prompt.shape
    n_lat, lat_len, _ = x_sample.shape
    ctx_cap = EC_CAPACITY_FACTOR * ctx_len // N_EXPERTS
    lat_cap = EC_CAPACITY_FACTOR * lat_len // N_EXPERTS

    c_rows = jnp.concatenate([c_ctx[None, :], c, jnp.zeros((8 - 1 - n_lat, D_MODEL), F32)], axis=0)
    mod3 = _mod_call(c_rows, w_mod[0], b_mod).reshape(8, N_MOD, D_MODEL)

    w_up_gate = jnp.zeros((P_PAD - OFF_ALOW, 2 * GLA_DK_TOT), F32)
    w_up_gate = w_up_gate.at[:GLA_LOW_RANK, :GLA_DK_TOT].set(w_a_up_f[0])
    w_up_gate = w_up_gate.at[GLA_LOW_RANK:2 * GLA_LOW_RANK, GLA_DK_TOT:].set(w_a_up_b[0]).astype(BF16)
    b_up_gate = jnp.concatenate([b_a_f[0], b_a_b[0]])[None, :]
    mixer_weights = [g_norm1, jnp.transpose(w_in[0]), w_conv[0], b_conv, w_up_gate, b_up_gate,
                     g_gla_norm[0].reshape(1, GLA_DV_TOT), w_out[0].astype(BF16)]
    wr_t = jnp.transpose(w_router[0])
    wr_hi = wr_t.astype(BF16)
    wr_lo = (wr_t - wr_hi.astype(F32)).astype(BF16)

    ctx_row = lambda b: 0
    lat_row = lambda b: b + 1

    x1_ctx, new_f, new_b, w_in_bf = _mixer_call(x_prompt, mod3, ctx_row, None, mixer_weights,
                                                period=ctx_len, has_state_out=True, seqs_per_step=2)
    mixer_weights[1] = w_in_bf
    (x1_lat,) = _mixer_call(x_sample, mod3, lat_row, (state_gla_fwd, state_gla_bwd), mixer_weights,
                            period=GRID_W, has_state_out=False, seqs_per_step=1)

    wr_both = jnp.concatenate([wr_hi, wr_lo], axis=0)
    xs_ctx, rank_ctx, gates_ctx = _route_call(x1_ctx, mod3, ctx_row, g_norm2, wr_both)
    xs_lat, rank_lat, gates_lat = _route_call(x1_lat, mod3, lat_row, g_norm2, wr_both)

    per_expert = lambda a, n, cap: a.reshape(n, N_EXPERTS, cap, a.shape[-1])
    y_ctx, y_lat = _experts_call(per_expert(xs_ctx, n_ctx, ctx_cap), per_expert(xs_lat, n_lat, lat_cap),
                                 per_expert(gates_ctx, n_ctx, ctx_cap), per_expert(gates_lat, n_lat, lat_cap),
                                 w_gate[0], w_up[0], w_down[0])

    g_fin = g_final[None, :]
    y_prompt = _combine_call(x1_ctx, y_ctx.reshape(xs_ctx.shape), rank_ctx, mod3, ctx_row, g_fin, seqs_per_step=4)
    y_sample = _combine_call(x1_lat, y_lat.reshape(xs_lat.shape), rank_lat, mod3, lat_row, g_fin, seqs_per_step=1)
    return y_prompt, y_sample, new_f, new_b
```

```python
import functools

import jax
import jax.numpy as jnp
from jax import lax
from jax.experimental import pallas as pl
from jax.experimental.pallas import tpu as pltpu

F32 = jnp.float32
BF16 = jnp.bfloat16
I32 = jnp.int32

D_MODEL = 1024
D_CONV = D_MODEL // 2
GRID_W = 64
GLA_HEADS = 4
GLA_DK = 64
GLA_DV = 128
GLA_DK_TOT = GLA_HEADS * GLA_DK
GLA_DV_TOT = GLA_HEADS * GLA_DV
GLA_LOW_RANK = 16
GLA_TAU = 16.0
GLA_CHUNK = 64
N_EXPERTS = 16
EC_CAPACITY_FACTOR = 2
D_EXPERT = 1024
N_MOD = 6
EPS = 1e-6
LOG2_E = 1.4426950408889634

OFF_XB = 0
OFF_XC = D_CONV
OFF_XV = 2 * D_CONV
OFF_Q = 3 * D_CONV
OFF_K = OFF_Q + GLA_DK_TOT
OFF_V = OFF_K + GLA_DK_TOT
OFF_OG = OFF_V + GLA_DV_TOT
OFF_ALOW = OFF_OG + GLA_DV_TOT
P_TOT = OFF_ALOW + 2 * GLA_LOW_RANK

LANES = 128
P_PAD = -(-P_TOT // LANES) * LANES
ROW_TILE = 256
TOKEN_TILE = 512
DISPATCH_ROWS = 512
VMEM_LIMIT = 56 * 1024 * 1024


def _dot(a, b):
    return jnp.dot(a, b, preferred_element_type=F32)


def _dot_nt(a, b):
    return lax.dot_general(a, b, (((1,), (1,)), ((), ())), preferred_element_type=F32)


def _dot_tn(a, b):
    return lax.dot_general(a, b, (((0,), (0,)), ((), ())), preferred_element_type=F32)


def _split(a):
    hi = a.astype(BF16)
    lo = (a - hi.astype(F32)).astype(BF16)
    return hi, lo


def _silu(x):
    return x * jax.nn.sigmoid(x)


def _modulated_norm(x, g, scale, shift):
    r = lax.rsqrt(jnp.mean(x * x, axis=-1, keepdims=True) + EPS)
    return (x * r) * (g * (1.0 + scale)) + shift


def _params(n_axes):
    return pltpu.CompilerParams(dimension_semantics=("arbitrary",) * n_axes,
                                vmem_limit_bytes=VMEM_LIMIT)


def _mod_kernel(c_ref, w_ref, b_ref, o_ref):
    rows = c_ref.shape[0]
    s = _silu(c_ref[...])
    s_hi, s_lo = _split(jnp.concatenate([s, s], axis=0))
    upper = lax.broadcasted_iota(I32, (2 * rows, 1), 0) < rows
    w_hi, w_lo = _split(w_ref[...])
    by_hi = _dot(jnp.where(upper, s_hi, s_lo), w_hi)
    part = by_hi[:rows] + by_hi[rows:] + _dot(s_hi[:rows], w_lo)

    @pl.when(pl.program_id(0) == 0)
    def _():
        o_ref[...] = part + b_ref[...]

    @pl.when(pl.program_id(0) != 0)
    def _():
        o_ref[...] = o_ref[...] + part


def _mod_call(c_rows, w_mod, b_mod):
    rows, d = c_rows.shape
    n = w_mod.shape[1]
    tk = D_MODEL // 4
    return pl.pallas_call(
        _mod_kernel,
        grid=(d // tk,),
        in_specs=[pl.BlockSpec((rows, tk), lambda k: (0, k)),
                  pl.BlockSpec((tk, n), lambda k: (k, 0)),
                  pl.BlockSpec((1, n), lambda k: (0, 0))],
        out_specs=pl.BlockSpec((rows, n), lambda k: (0, 0)),
        out_shape=jax.ShapeDtypeStruct((rows, n), F32),
        compiler_params=_params(1),
        name="mod",
    )(c_rows, w_mod, b_mod)


def _staggered(programs):
    programs = list(programs)
    started = 0
    while programs:
        started = min(started + 1, len(programs))
        running = [p for p in programs[:started] if next(p, "done") != "done"]
        programs = running + programs[started:]
        started = len(running)
        yield


def _for_row_tiles(seq_len, phases, independent):
    n = seq_len // ROW_TILE
    if independent:
        yield from _staggered(phases(i) for i in range(n))
    else:
        for i in range(n):
            yield from phases(i)


def _tile_rows(tile, offset=0, size=ROW_TILE):
    if isinstance(tile, int):
        return pl.ds(tile * ROW_TILE + offset, size)
    return pl.ds(pl.multiple_of(tile * ROW_TILE + offset, size), size)


def _mixer_kernel(*refs, seqs_per_step, has_state_in, has_state_out, stages_w_in, **static):
    refs = list(refs)
    n_in = 2 + (2 if has_state_in else 0)
    per_seq_in, refs = [refs[0]] + refs[2:n_in], [refs[1]] + refs[n_in:]
    mod_ref, weights, refs = refs[0], refs[1:9], refs[9:]
    n_out = 1 + (2 if has_state_out else 0)
    per_seq_out, refs = refs[:n_out], refs[n_out:]
    if stages_w_in:
        win_f32_ref, win_bf_ref, scratch = weights[1], refs[0], refs[1:]
        weights = weights[:1] + [win_bf_ref] + weights[2:]

        @pl.when(pl.program_id(0) == 0)
        def _():
            for c0 in range(0, P_PAD, LANES):
                n = min(LANES, P_TOT - c0)
                cols = win_f32_ref[c0:c0 + n, :]
                if n < LANES:
                    cols = jnp.concatenate([cols, jnp.zeros((LANES - n, D_MODEL), F32)], axis=0)
                win_bf_ref[:, c0:c0 + LANES] = jnp.transpose(cols).astype(BF16)
    else:
        scratch = refs
    programs = []
    for j in range(seqs_per_step):
        ins = [r.at[j] for r in per_seq_in]
        outs = [r.at[j] for r in per_seq_out]
        programs.append(_mixer_sequence(ins[0], mod_ref, ins[1:], weights, outs[0], outs[1:],
                                        [r.at[j] for r in scratch], **static))
    for _ in _staggered(programs):
        pass


def _mixer_sequence(x_ref, mod_ref, s0_refs, weights, x1_ref, sout_refs, scratch, *, seq_len, period):
    has_state_in = bool(s0_refs)
    has_state_out = bool(sout_refs)
    g1_ref, win_ref, wconv_ref, bconv_ref, wup_ref, bup_ref, ggla_ref, wout_ref = weights
    og_ref, qd_ref, kd_ref, kst_ref, dect_ref, v_ref, s_ref, sst_ref, o_ref, ya_ref = scratch

    c = GLA_CHUNK
    tile_chunks = ROW_TILE // c
    n_tiles = seq_len // ROW_TILE
    n_pairs = GLA_HEADS // 2
    pair_k = 2 * GLA_DK
    pair_v = 2 * GLA_DV
    m = mod_ref[0]

    def stage1(ti):
        rows = _tile_rows(ti)
        h = _modulated_norm(x_ref[rows, :], g1_ref[...], m[1:2], m[0:1]).astype(BF16)
        row_i = lax.broadcasted_iota(I32, (ROW_TILE, 1), 0)
        yield
        p_gate = _dot(h, win_ref[:, OFF_OG:P_PAD])
        og_ref[rows, :] = p_gate[:, :GLA_DV_TOT]
        yield
        z = _dot(p_gate[:, GLA_DV_TOT:].astype(BF16), wup_ref[...]) + bup_ref[...]
        la = (jnp.minimum(z, 0.0) - jnp.log(1.0 + jnp.exp(-jnp.abs(z)))) * (LOG2_E / GLA_TAU)
        col_j = lax.broadcasted_iota(I32, (1, ROW_TILE), 1)
        same_chunk = (row_i & -c) == (col_j & -c)
        lower = jnp.where(same_chunk & (col_j <= row_i), 1.0, 0.0).astype(BF16)
        la_parts = jnp.concatenate(_split(la), axis=1)
        n_gate = 2 * GLA_DK_TOT
        yield
        pre = _dot(lower, la_parts)
        pre = pre[:, :n_gate] + pre[:, n_gate:]
        tot = jnp.concatenate([jnp.broadcast_to(pre[(n + 1) * c - 1:(n + 1) * c], (c, n_gate))
                               for n in range(tile_chunks)], axis=0)
        p_qkv = _dot(h, win_ref[:, OFF_Q:OFF_OG])
        yield
        q = p_qkv[:, :GLA_DK_TOT] * (GLA_DK ** -0.5)
        k = p_qkv[:, GLA_DK_TOT:2 * GLA_DK_TOT]
        v_ref[rows, :] = p_qkv[:, 2 * GLA_DK_TOT:].astype(BF16)
        for d in range(2):
            cols = slice(d * GLA_DK_TOT, (d + 1) * GLA_DK_TOT)
            if d == 0:
                bq = pre[:, cols]
                bk = tot[:, cols] - bq
            else:
                bk = pre[:, cols] - la[:, cols]
                bq = tot[:, cols] - bk
            qd_ref[d, rows, :] = (q * jnp.exp2(bq)).astype(BF16)
            kd_ref[d, rows, :] = (k * jnp.exp2(-bq)).astype(BF16)
            kst_ref[d, ti] = jnp.transpose(k * jnp.exp2(bk)).astype(BF16)
            totals = [tot[n * c:n * c + 1, cols] for n in range(tile_chunks)]
            totals.append(jnp.zeros((LANES - tile_chunks, GLA_DK_TOT), F32))
            dect_ref[d, ti] = jnp.transpose(jnp.exp2(jnp.concatenate(totals, axis=0)))
        yield
        p_conv = _dot(h, win_ref[:, :OFF_Q])
        yield
        pos = row_i & (period - 1)
        u = p_conv[:, OFF_XC:OFF_XC + D_CONV] * p_conv[:, OFF_XV:OFF_XV + D_CONV]
        u_prev = jnp.where(pos == 0, 0.0, pltpu.roll(u, 1, 0))
        u_next = jnp.where(pos == period - 1, 0.0, pltpu.roll(u, ROW_TILE - 1, 0))
        conv = u_prev * wconv_ref[0:1, :] + u * wconv_ref[1:2, :] + u_next * wconv_ref[2:3, :] + bconv_ref[...]
        ya_ref[rows, :] = (p_conv[:, OFF_XB:OFF_XB + D_CONV] * conv).astype(BF16)
        yield

    yield from _for_row_tiles(seq_len, stage1, independent=True)

    for d in range(2):
        for pair in range(n_pairs):
            if has_state_in:
                zero = jnp.zeros((GLA_DK, GLA_DV), F32)
                top = jnp.concatenate([s0_refs[d][2 * pair], zero], axis=1)
                bot = jnp.concatenate([zero, s0_refs[d][2 * pair + 1]], axis=1)
                s_ref[d, pair] = jnp.concatenate([top, bot], axis=0)
            else:
                s_ref[d, pair] = jnp.zeros((pair_k, pair_v), F32)

    def scan_tile(i):
        upper_lane = lax.broadcasted_iota(I32, (1, LANES), 1) >= GLA_DK
        qi = lax.broadcasted_iota(I32, (LANES, 1), 0)
        kj = lax.broadcasted_iota(I32, (1, 2 * LANES), 1) & (LANES - 1)
        same_chunk = (qi & c) == (kj & c)
        causal = (same_chunk & (kj <= qi), same_chunk & (kj >= qi))
        for pair in range(n_pairs):
            kl = slice(pair * pair_k, (pair + 1) * pair_k)
            vl = slice(pair * pair_v, (pair + 1) * pair_v)
            for blk in range(ROW_TILE // LANES):
                rows = _tile_rows(i, blk * LANES, LANES)
                att = None
                for d in range(2):
                    kd = kd_ref[d, rows, kl]
                    zk = jnp.zeros_like(kd)
                    keys = jnp.concatenate([jnp.where(upper_lane, zk, kd), jnp.where(upper_lane, kd, zk)], axis=0)
                    a = jnp.where(causal[d], _dot_nt(qd_ref[d, rows, kl], keys), 0.0)
                    att = a if att is None else att + a
                v = v_ref[rows, vl]
                zv = jnp.zeros((LANES, GLA_DV), BF16)
                v_bd = jnp.concatenate([jnp.concatenate([v[:, :GLA_DV], zv], axis=1),
                                        jnp.concatenate([zv, v[:, GLA_DV:]], axis=1)], axis=0)
                o_ref[rows, vl] = _dot(att.astype(BF16), v_bd)
        yield
        key_row = lax.broadcasted_iota(I32, (pair_k, 1), 0)
        val_col = lax.broadcasted_iota(I32, (1, pair_v), 1)
        blockdiag = (key_row >= GLA_DK) == (val_col >= GLA_DV)
        for d in range(2):
            tile = i if d == 0 else n_tiles - 1 - i
            chunks = range(tile_chunks)
            for pair in range(n_pairs):
                kr = slice(pair * pair_k, (pair + 1) * pair_k)
                vl = slice(pair * pair_v, (pair + 1) * pair_v)
                s = s_ref[d, pair]
                for c4 in (chunks if d == 0 else reversed(chunks)):
                    blk, half = divmod(c4, 2)
                    kst = kst_ref[d, tile, kr, blk * LANES:(blk + 1) * LANES]
                    kst = jnp.where(upper_lane if half else ~upper_lane, kst, jnp.zeros_like(kst))
                    kv = jnp.where(blockdiag, _dot(kst, v_ref[_tile_rows(tile, blk * LANES, LANES), vl]), 0.0)
                    sst_ref[pair, tile * tile_chunks + c4, d * pair_k:(d + 1) * pair_k, :] = s.astype(BF16)
                    s = dect_ref[d, tile, kr, c4:c4 + 1] * s + kv
                s_ref[d, pair] = s
        yield

    yield from _for_row_tiles(seq_len, scan_tile, independent=False)

    if has_state_out:
        for d in range(2):
            for pair in range(n_pairs):
                s = s_ref[d, pair]
                sout_refs[d][2 * pair] = s[0:GLA_DK, 0:GLA_DV]
                sout_refs[d][2 * pair + 1] = s[GLA_DK:, GLA_DV:]

    def stage3(i):
        for pair in range(n_pairs):
            kl = slice(pair * pair_k, (pair + 1) * pair_k)
            vl = slice(pair * pair_v, (pair + 1) * pair_v)
            for c4 in range(tile_chunks):
                crow = _tile_rows(i, c4 * c, c)
                q2 = jnp.concatenate([qd_ref[0, crow, kl], qd_ref[1, crow, kl]], axis=1)
                o_ref[crow, vl] = o_ref[crow, vl] + _dot(q2, sst_ref[pair, i * tile_chunks + c4])
        yield
        rows = _tile_rows(i)
        heads = []
        for h in range(GLA_HEADS):
            hl = slice(h * GLA_DV, (h + 1) * GLA_DV)
            oh = o_ref[rows, hl]
            r = lax.rsqrt(jnp.mean(oh * oh, axis=-1, keepdims=True) + EPS)
            heads.append(oh * r * ggla_ref[:, hl])
        y_b = jnp.concatenate(heads, axis=1) * _silu(og_ref[rows, :])
        y = jnp.concatenate([ya_ref[rows, :], y_b.astype(BF16)], axis=1)
        x1_ref[rows, :] = x_ref[rows, :] + m[2:3] * _dot(y, wout_ref[...])
        yield

    yield from _for_row_tiles(seq_len, stage3, independent=True)


def _mixer_call(x, mod3, mod_row_of_step, states, weights, *, period, has_state_out, seqs_per_step):
    n_seq, seq_len, _ = x.shape
    sps = seqs_per_step
    has_state_in = states is not None
    stages_w_in = weights[1].dtype == F32
    kernel = functools.partial(_mixer_kernel, seqs_per_step=sps, seq_len=seq_len, period=period,
                               has_state_in=has_state_in, has_state_out=has_state_out, stages_w_in=stages_w_in)
    state_spec = pl.BlockSpec((sps, None, GLA_HEADS, GLA_DK, GLA_DV), lambda b: (b, 0, 0, 0, 0))
    const2 = lambda b: (0, 0)
    in_specs = [pl.BlockSpec((sps, seq_len, D_MODEL), lambda b: (b, 0, 0)),
                pl.BlockSpec((1, N_MOD, D_MODEL), lambda b: (mod_row_of_step(b), 0, 0))]
    args = [x, mod3]
    if has_state_in:
        in_specs += [state_spec, state_spec]
        args += list(states)
    in_specs += [pl.BlockSpec(w.shape, lambda b, nd=w.ndim: (0,) * nd, pipeline_mode=pl.Buffered(1))
                 for w in weights]
    args += list(weights)
    out_specs = [pl.BlockSpec((sps, seq_len, D_MODEL), lambda b: (b, 0, 0))]
    out_shape = [jax.ShapeDtypeStruct((n_seq, seq_len, D_MODEL), F32)]
    if has_state_out:
        out_specs += [state_spec, state_spec]
        out_shape += [jax.ShapeDtypeStruct((n_seq, 1, GLA_HEADS, GLA_DK, GLA_DV), F32)] * 2
    if stages_w_in:
        out_specs.append(pl.BlockSpec((D_MODEL, P_PAD), const2))
        out_shape.append(jax.ShapeDtypeStruct((D_MODEL, P_PAD), BF16))
    n_tiles = seq_len // ROW_TILE
    n_pairs = GLA_HEADS // 2
    per_seq = lambda shape, dtype: pltpu.VMEM((sps,) + shape, dtype)
    scratch = [per_seq((seq_len, GLA_DV_TOT), F32),
               per_seq((2, seq_len, GLA_DK_TOT), BF16),
               per_seq((2, seq_len, GLA_DK_TOT), BF16),
               per_seq((2, n_tiles, GLA_DK_TOT, ROW_TILE), BF16),
               per_seq((2, n_tiles, GLA_DK_TOT, LANES), F32),
               per_seq((seq_len, GLA_DV_TOT), BF16),
               per_seq((2, n_pairs, 2 * GLA_DK, 2 * GLA_DV), F32),
               per_seq((n_pairs, seq_len // GLA_CHUNK, 4 * GLA_DK, 2 * GLA_DV), BF16),
               per_seq((seq_len, GLA_DV_TOT), F32),
               per_seq((seq_len, D_CONV), BF16)]
    return pl.pallas_call(
        kernel,
        grid=(n_seq // sps,),
        in_specs=in_specs,
        out_specs=out_specs,
        out_shape=out_shape,
        scratch_shapes=scratch,
        compiler_params=_params(1),
        name="mixer",
    )(*args)


SUBLANES = 8


ROW_TOKENS = LANES // N_EXPERTS


def _route_kernel(x1_ref, mod_ref, g2_ref, wr2_ref, xs_ref, rank_ref, gate_ref,
                  pt_ref, h2_ref, spread_ref, bar_ref, cnt_ref, win_ref, *, seq_len, cap):
    m = mod_ref[0]
    n_rows = seq_len // ROW_TOKENS
    for t in range(seq_len // ROW_TILE):
        rows = slice(t * ROW_TILE, (t + 1) * ROW_TILE)
        h2 = _modulated_norm(x1_ref[rows, :], g2_ref[...], m[4:5], m[3:4])
        hi = h2.astype(BF16)
        h2_ref[rows, :] = hi
        by_hi = _dot_nt(wr2_ref[...], hi)
        pt_ref[:, rows] = by_hi[:N_EXPERTS] + by_hi[N_EXPERTS:]

    logits = pt_ref[...]
    ex = jnp.exp(logits - jnp.max(logits, axis=0, keepdims=True))
    probs = ex / jnp.sum(ex, axis=0, keepdims=True)
    pt_ref[...] = probs
    pad = jnp.zeros((LANES - N_EXPERTS, seq_len), F32)
    p_tok = jnp.transpose(jnp.concatenate([probs, pad], axis=0))

    spread = p_tok
    sh = N_EXPERTS
    while sh < LANES:
        spread = spread + pltpu.roll(spread, sh, 1)
        sh *= 2
    spread_ref[...] = spread
    sub_j = lax.broadcasted_iota(I32, (SUBLANES, 1), 0)
    lane_g = lax.broadcasted_iota(I32, (1, LANES), 1) >> (N_EXPERTS.bit_length() - 1)
    lane_j = (ROW_TOKENS - lane_g) & (ROW_TOKENS - 1)
    own_group = jnp.where(lane_j == sub_j, spread.reshape(n_rows, SUBLANES, LANES), 0.0)
    bar = jnp.sum(own_group, axis=1, keepdims=True)
    bar_ref[...] = jnp.broadcast_to(bar, (n_rows, SUBLANES, LANES))
    cnt_ref[...] = jnp.zeros((n_rows, SUBLANES, LANES), F32)

    rows_per_block = LANES // ROW_TOKENS

    def count_block(g_s, g_t, relation):
        if isinstance(g_s, int):
            s_rows = slice(g_s * LANES, (g_s + 1) * LANES)
        else:
            s_rows = pl.ds(pl.multiple_of(g_s * LANES, LANES), LANES)
        s_blk = spread_ref[s_rows, :]
        wins = [jnp.zeros((SUBLANES, LANES), F32)] * rows_per_block
        for tl in range(rows_per_block):
            r = g_t * rows_per_block + tl
            bar_r = bar_ref[r]
            acc = cnt_ref[r]
            for sl in range(rows_per_block):
                s_vreg = s_blk[sl * SUBLANES:(sl + 1) * SUBLANES]
                order = relation if relation != "same" else ("before" if sl < tl else "after" if sl > tl else "tie")
                if order == "before":
                    won = jnp.where(s_vreg >= bar_r, 1.0, 0.0)
                    acc = acc + won
                    if relation == "before":
                        wins[sl] = wins[sl] + won
                elif order == "after":
                    acc = acc + jnp.where(s_vreg > bar_r, 1.0, 0.0)
                else:
                    acc = acc + jnp.where(s_vreg > bar_r, 1.0, 0.0) \
                        + jnp.where((s_vreg == bar_r) & (sub_j < lane_j), 1.0, 0.0)
            cnt_ref[r] = acc
        if relation == "before":
            if isinstance(g_s, int):
                w_rows = slice(g_s * rows_per_block, (g_s + 1) * rows_per_block)
            else:
                w_rows = pl.ds(g_s * rows_per_block, rows_per_block)
            win_ref[w_rows] = win_ref[w_rows] + jnp.stack(wins)

    n_grp = seq_len // LANES
    win_ref[...] = jnp.zeros((n_rows, SUBLANES, LANES), F32)
    if n_grp <= 2:
        for g_t in range(n_grp):
            for g_s in range(g_t + 1):
                count_block(g_s, g_t, "before" if g_s < g_t else "same")
    else:
        def per_target_group(g_t, carry):
            def before(g_s, c):
                count_block(g_s, g_t, "before")
                return c
            lax.fori_loop(0, g_t, before, 0)
            count_block(g_t, g_t, "same")
            return carry
        lax.fori_loop(0, n_grp, per_target_group, 0)

    counts = jnp.sum(cnt_ref[...], axis=1, keepdims=True)
    counts = jnp.broadcast_to(counts, (n_rows, SUBLANES, LANES)).reshape(seq_len, LANES)
    rank_tok = pltpu.roll(counts, 0, 1, stride=N_EXPERTS, stride_axis=0)
    wins = win_ref[...].reshape(seq_len, LANES)
    sh = N_EXPERTS
    while sh < LANES:
        wins = wins + pltpu.roll(wins, sh, 1)
        sh *= 2
    tok = lax.broadcasted_iota(I32, (seq_len, 1), 0)
    later = ((n_grp - 1 - (tok >> (LANES.bit_length() - 1))) * LANES).astype(F32)
    rank_tok = rank_tok + (later - wins)
    rank_ref[...] = jnp.transpose(rank_tok)[:N_EXPERTS].astype(I32)

    group = DISPATCH_ROWS // cap
    slot = lax.broadcasted_iota(I32, (cap, 1), 0)
    half = D_MODEL // 2
    for gi in range(N_EXPERTS // group):
        picks = []
        for e in range(gi * group, (gi + 1) * group):
            oh = rank_ref[e:e + 1, :] == slot
            gate = jnp.sum(jnp.where(oh, pt_ref[e:e + 1, :], 0.0), axis=1, keepdims=True)
            gate_ref[e * cap:(e + 1) * cap, :] = jnp.broadcast_to(gate, (cap, LANES))
            picks.append(oh)
        ohb = jnp.where(jnp.concatenate(picks, axis=0), 1.0, 0.0).astype(BF16)
        out_rows = slice(gi * DISPATCH_ROWS, (gi + 1) * DISPATCH_ROWS)
        xs_ref[out_rows, :half] = _dot(ohb, h2_ref[:, :half]).astype(BF16)
        xs_ref[out_rows, half:] = _dot(ohb, h2_ref[:, half:]).astype(BF16)


def _rank_spec(seq_len):
    return pl.BlockSpec((None, N_EXPERTS, seq_len), lambda b: (b, 0, 0))


def _route_call(x1, mod3, mod_row_of_seq, g2, wr_both):
    n_seq, seq_len, _ = x1.shape
    cap = EC_CAPACITY_FACTOR * seq_len // N_EXPERTS
    kernel = functools.partial(_route_kernel, seq_len=seq_len, cap=cap)
    const2 = lambda b: (0, 0)
    rank_rows = (seq_len // ROW_TOKENS, SUBLANES, LANES)
    return pl.pallas_call(
        kernel,
        grid=(n_seq,),
        in_specs=[pl.BlockSpec((None, seq_len, D_MODEL), lambda b: (b, 0, 0)),
                  pl.BlockSpec((1, N_MOD, D_MODEL), lambda b: (mod_row_of_seq(b), 0, 0)),
                  pl.BlockSpec((1, D_MODEL), const2),
                  pl.BlockSpec((2 * N_EXPERTS, D_MODEL), const2)],
        out_specs=[pl.BlockSpec((None, N_EXPERTS * cap, D_MODEL), lambda b: (b, 0, 0)),
                   _rank_spec(seq_len),
                   pl.BlockSpec((None, N_EXPERTS * cap, LANES), lambda b: (b, 0, 0))],
        out_shape=[jax.ShapeDtypeStruct((n_seq, N_EXPERTS * cap, D_MODEL), BF16),
                   jax.ShapeDtypeStruct((n_seq, N_EXPERTS, seq_len), I32),
                   jax.ShapeDtypeStruct((n_seq, N_EXPERTS * cap, LANES), F32)],
        scratch_shapes=[pltpu.VMEM((N_EXPERTS, seq_len), F32),
                        pltpu.VMEM((seq_len, D_MODEL), BF16),
                        pltpu.VMEM((seq_len, LANES), F32),
                        pltpu.VMEM(rank_rows, F32),
                        pltpu.VMEM(rank_rows, F32),
                        pltpu.VMEM(rank_rows, F32)],
        compiler_params=_params(1),
        name="route",
    )(x1, mod3, g2, wr_both)


def _experts_kernel(xc_ref, xl_ref, gc_ref, gl_ref, wg_ref, wu_ref, wd_ref, yc_ref, yl_ref,
                    wgb_ref, wub_ref, wdb_ref):
    wgb_ref[...] = wg_ref[...].astype(BF16)
    wub_ref[...] = wu_ref[...].astype(BF16)
    wdb_ref[...] = wd_ref[...].astype(BF16)

    def run(x_ref, g_ref, y_ref):
        n_seq, cap, _ = x_ref.shape
        seqs = DISPATCH_ROWS // cap
        for s0 in range(0, n_seq, seqs):
            x = x_ref[s0:s0 + seqs].reshape(DISPATCH_ROWS, D_MODEL)
            a = (_silu(_dot(x, wgb_ref[...])) * _dot(x, wub_ref[...])).astype(BF16)
            gate = g_ref[s0:s0 + seqs].reshape(DISPATCH_ROWS, LANES)
            y = _dot(a, wdb_ref[...]) * jnp.concatenate([gate] * (D_MODEL // LANES), axis=1)
            y_ref[s0:s0 + seqs] = y.astype(BF16).reshape(seqs, cap, D_MODEL)

    run(xc_ref, gc_ref, yc_ref)
    run(xl_ref, gl_ref, yl_ref)


def _experts_call(xs_ctx, xs_lat, gates_ctx, gates_lat, w_gate, w_up, w_down):
    def slot_spec(a):
        n_seq, _, cap, width = a.shape
        return pl.BlockSpec((n_seq, None, cap, width), lambda e: (0, e, 0, 0))

    w_spec = pl.BlockSpec((None, D_MODEL, D_EXPERT), lambda e: (e, 0, 0))
    return pl.pallas_call(
        _experts_kernel,
        grid=(N_EXPERTS,),
        in_specs=[slot_spec(xs_ctx), slot_spec(xs_lat), slot_spec(gates_ctx), slot_spec(gates_lat),
                  w_spec, w_spec, pl.BlockSpec((None, D_EXPERT, D_MODEL), lambda e: (e, 0, 0))],
        out_specs=[slot_spec(xs_ctx), slot_spec(xs_lat)],
        out_shape=[jax.ShapeDtypeStruct(xs_ctx.shape, BF16), jax.ShapeDtypeStruct(xs_lat.shape, BF16)],
        scratch_shapes=[pltpu.VMEM((D_MODEL, D_EXPERT), BF16),
                        pltpu.VMEM((D_MODEL, D_EXPERT), BF16),
                        pltpu.VMEM((D_EXPERT, D_MODEL), BF16)],
        compiler_params=_params(1),
        name="experts",
    )(xs_ctx, xs_lat, gates_ctx, gates_lat, w_gate, w_up, w_down)


def _combine_kernel(x1_ref, y_ref, rank_ref, mod_ref, gf_ref, o_ref, oh_ref, *, seqs_per_step, seq_len, cap):
    m = mod_ref[0]
    slot = lax.broadcasted_iota(I32, (cap, 1), 0)
    for j in range(seqs_per_step):
        for e in range(N_EXPERTS):
            oh_ref[j, e * cap:(e + 1) * cap, :] = jnp.where(rank_ref[j, e:e + 1, :] == slot, 1.0, 0.0).astype(BF16)
        for t in range(seq_len // ROW_TILE):
            rows = slice(t * ROW_TILE, (t + 1) * ROW_TILE)
            moe = _dot_tn(oh_ref[j, :, rows], y_ref[j])
            x2 = x1_ref[j, rows, :] + m[5:6] * moe
            r = lax.rsqrt(jnp.mean(x2 * x2, axis=-1, keepdims=True) + EPS)
            o_ref[j, rows, :] = (x2 * r) * gf_ref[...]


def _combine_call(x1, y, rank, mod3, mod_row_of_step, g_final, *, seqs_per_step):
    n_seq, seq_len, _ = x1.shape
    n_slots = y.shape[1]
    sps = seqs_per_step
    kernel = functools.partial(_combine_kernel, seqs_per_step=sps, seq_len=seq_len, cap=n_slots // N_EXPERTS)
    seq_spec = pl.BlockSpec((sps, seq_len, D_MODEL), lambda b: (b, 0, 0))
    return pl.pallas_call(
        kernel,
        grid=(n_seq // sps,),
        in_specs=[seq_spec,
                  pl.BlockSpec((sps, n_slots, D_MODEL), lambda b: (b, 0, 0)),
                  pl.BlockSpec((sps, N_EXPERTS, seq_len), lambda b: (b, 0, 0)),
                  pl.BlockSpec((1, N_MOD, D_MODEL), lambda b: (mod_row_of_step(b), 0, 0)),
                  pl.BlockSpec((1, D_MODEL), lambda b: (0, 0))],
        out_specs=seq_spec,
        out_shape=jax.ShapeDtypeStruct((n_seq, seq_len, D_MODEL), F32),
        scratch_shapes=[pltpu.VMEM((sps, n_slots, seq_len), BF16)],
        compiler_params=_params(1),
        name="combine",
    )(x1, y, rank, mod3, g_final)


def kernel(x_prompt, x_sample, state_gla_fwd, state_gla_bwd, c, c_ctx, w_mod, b_mod, g_norm1, g_norm2,
           w_in, w_conv, b_conv, w_a_up_f, b_a_f, w_a_up_b, b_a_b, g_gla_norm, w_out, w_router,
           w_gate, w_up, w_down, g_final):
    assert w_mod.shape[0] == 1, "single trunk layer"
    n_ctx, ctx_len, _ = x_prompt.shape
    n_lat, lat_len, _ = x_sample.shape
    ctx_cap = EC_CAPACITY_FACTOR * ctx_len // N_EXPERTS
    lat_cap = EC_CAPACITY_FACTOR * lat_len // N_EXPERTS

    c_rows = jnp.concatenate([c_ctx[None, :], c, jnp.zeros((8 - 1 - n_lat, D_MODEL), F32)], axis=0)
    mod3 = _mod_call(c_rows, w_mod[0], b_mod).reshape(8, N_MOD, D_MODEL)

    w_up_gate = jnp.zeros((P_PAD - OFF_ALOW, 2 * GLA_DK_TOT), F32)
    w_up_gate = w_up_gate.at[:GLA_LOW_RANK, :GLA_DK_TOT].set(w_a_up_f[0])
    w_up_gate = w_up_gate.at[GLA_LOW_RANK:2 * GLA_LOW_RANK, GLA_DK_TOT:].set(w_a_up_b[0]).astype(BF16)
    b_up_gate = jnp.concatenate([b_a_f[0], b_a_b[0]])[None, :]
    mixer_weights = [g_norm1, jnp.transpose(w_in[0]), w_conv[0], b_conv, w_up_gate, b_up_gate,
                     g_gla_norm[0].reshape(1, GLA_DV_TOT), w_out[0].astype(BF16)]
    wr_t = jnp.transpose(w_router[0])
    wr_hi = wr_t.astype(BF16)
    wr_lo = (wr_t - wr_hi.astype(F32)).astype(BF16)

    ctx_row = lambda b: 0
    lat_row = lambda b: b + 1

    x1_ctx, new_f, new_b, w_in_bf = _mixer_call(x_prompt, mod3, ctx_row, None, mixer_weights,
                                                period=ctx_len, has_state_out=True, seqs_per_step=2)
    mixer_weights[1] = w_in_bf
    (x1_lat,) = _mixer_call(x_sample, mod3, lat_row, (state_gla_fwd, state_gla_bwd), mixer_weights,
                            period=GRID_W, has_state_out=False, seqs_per_step=1)

    wr_both = jnp.concatenate([wr_hi, wr_lo], axis=0)
    xs_ctx, rank_ctx, gates_ctx = _route_call(x1_ctx, mod3, ctx_row, g_norm2, wr_both)
    xs_lat, rank_lat, gates_lat = _route_call(x1_lat, mod3, lat_row, g_norm2, wr_both)

    per_expert = lambda a, n, cap: a.reshape(n, N_EXPERTS, cap, a.shape[-1])
    y_ctx, y_lat = _experts_call(per_expert(xs_ctx, n_ctx, ctx_cap), per_expert(xs_lat, n_lat, lat_cap),
                                 per_expert(gates_ctx, n_ctx, ctx_cap), per_expert(gates_lat, n_lat, lat_cap),
                                 w_gate[0], w_up[0], w_down[0])

    g_fin = g_final[None, :]
    y_prompt = _combine_call(x1_ctx, y_ctx.reshape(xs_ctx.shape), rank_ctx, mod3, ctx_row, g_fin, seqs_per_step=4)
    y_sample = _combine_call(x1_lat, y_lat.reshape(xs_lat.shape), rank_lat, mod3, lat_row, g_fin, seqs_per_step=1)
    return y_prompt, y_sample, new_f, new_b
```

```python
import functools

import jax
import jax.numpy as jnp
from jax import lax
from jax.experimental import pallas as pl
from jax.experimental.pallas import tpu as pltpu

F32 = jnp.float32
BF16 = jnp.bfloat16
I32 = jnp.int32

D_MODEL = 1024
D_CONV = D_MODEL // 2
GRID_W = 64
GLA_HEADS = 4
GLA_DK = 64
GLA_DV = 128
GLA_DK_TOT = GLA_HEADS * GLA_DK
GLA_DV_TOT = GLA_HEADS * GLA_DV
GLA_LOW_RANK = 16
GLA_TAU = 16.0
GLA_CHUNK = 64
N_EXPERTS = 16
EC_CAPACITY_FACTOR = 2
D_EXPERT = 1024
N_MOD = 6
EPS = 1e-6
LOG2_E = 1.4426950408889634

OFF_XB = 0
OFF_XC = D_CONV
OFF_XV = 2 * D_CONV
OFF_Q = 3 * D_CONV
OFF_K = OFF_Q + GLA_DK_TOT
OFF_V = OFF_K + GLA_DK_TOT
OFF_OG = OFF_V + GLA_DV_TOT
OFF_ALOW = OFF_OG + GLA_DV_TOT
P_TOT = OFF_ALOW + 2 * GLA_LOW_RANK

LANES = 128
P_PAD = -(-P_TOT // LANES) * LANES
ROW_TILE = 256
TOKEN_TILE = 512
DISPATCH_ROWS = 512
VMEM_LIMIT = 56 * 1024 * 1024


def _dot(a, b):
    return jnp.dot(a, b, preferred_element_type=F32)


def _dot_nt(a, b):
    return lax.dot_general(a, b, (((1,), (1,)), ((), ())), preferred_element_type=F32)


def _dot_tn(a, b):
    return lax.dot_general(a, b, (((0,), (0,)), ((), ())), preferred_element_type=F32)


def _split(a):
    hi = a.astype(BF16)
    lo = (a - hi.astype(F32)).astype(BF16)
    return hi, lo


def _silu(x):
    return x * jax.nn.sigmoid(x)


def _modulated_norm(x, g, scale, shift):
    r = lax.rsqrt(jnp.mean(x * x, axis=-1, keepdims=True) + EPS)
    return (x * r) * (g * (1.0 + scale)) + shift


def _params(n_axes):
    return pltpu.CompilerParams(dimension_semantics=("arbitrary",) * n_axes,
                                vmem_limit_bytes=VMEM_LIMIT)


def _mod_kernel(c_ref, w_ref, b_ref, o_ref):
    rows = c_ref.shape[0]
    s = _silu(c_ref[...])
    s_hi, s_lo = _split(jnp.concatenate([s, s], axis=0))
    upper = lax.broadcasted_iota(I32, (2 * rows, 1), 0) < rows
    w_hi, w_lo = _split(w_ref[...])
    by_hi = _dot(jnp.where(upper, s_hi, s_lo), w_hi)
    part = by_hi[:rows] + by_hi[rows:] + _dot(s_hi[:rows], w_lo)

    @pl.when(pl.program_id(0) == 0)
    def _():
        o_ref[...] = part + b_ref[...]

    @pl.when(pl.program_id(0) != 0)
    def _():
        o_ref[...] = o_ref[...] + part


def _mod_call(c_rows, w_mod, b_mod):
    rows, d = c_rows.shape
    n = w_mod.shape[1]
    tk = D_MODEL // 4
    return pl.pallas_call(
        _mod_kernel,
        grid=(d // tk,),
        in_specs=[pl.BlockSpec((rows, tk), lambda k: (0, k)),
                  pl.BlockSpec((tk, n), lambda k: (k, 0)),
                  pl.BlockSpec((1, n), lambda k: (0, 0))],
        out_specs=pl.BlockSpec((rows, n), lambda k: (0, 0)),
        out_shape=jax.ShapeDtypeStruct((rows, n), F32),
        compiler_params=_params(1),
        name="mod",
    )(c_rows, w_mod, b_mod)


def _staggered(programs):
    programs = list(programs)
    started = 0
    while programs:
        started = min(started + 1, len(programs))
        running = [p for p in programs[:started] if next(p, "done") != "done"]
        programs = running + programs[started:]
        started = len(running)
        yield


def _for_row_tiles(seq_len, phases, independent):
    n = seq_len // ROW_TILE
    if independent:
        yield from _staggered(phases(i) for i in range(n))
    else:
        for i in range(n):
            yield from phases(i)


def _tile_rows(tile, offset=0, size=ROW_TILE):
    if isinstance(tile, int):
        return pl.ds(tile * ROW_TILE + offset, size)
    return pl.ds(pl.multiple_of(tile * ROW_TILE + offset, size), size)


def _mixer_kernel(*refs, seqs_per_step, has_state_in, has_state_out, stages_w_in, **static):
    refs = list(refs)
    n_in = 2 + (2 if has_state_in else 0)
    per_seq_in, refs = [refs[0]] + refs[2:n_in], [refs[1]] + refs[n_in:]
    mod_ref, weights, refs = refs[0], refs[1:9], refs[9:]
    n_out = 1 + (2 if has_state_out else 0)
    per_seq_out, refs = refs[:n_out], refs[n_out:]
    if stages_w_in:
        win_f32_ref, win_bf_ref, scratch = weights[1], refs[0], refs[1:]
        weights = weights[:1] + [win_bf_ref] + weights[2:]

        @pl.when(pl.program_id(0) == 0)
        def _():
            for c0 in range(0, P_PAD, LANES):
                n = min(LANES, P_TOT - c0)
                cols = win_f32_ref[c0:c0 + n, :]
                if n < LANES:
                    cols = jnp.concatenate([cols, jnp.zeros((LANES - n, D_MODEL), F32)], axis=0)
                win_bf_ref[:, c0:c0 + LANES] = jnp.transpose(cols).astype(BF16)
    else:
        scratch = refs
    programs = []
    for j in range(seqs_per_step):
        ins = [r.at[j] for r in per_seq_in]
        outs = [r.at[j] for r in per_seq_out]
        programs.append(_mixer_sequence(ins[0], mod_ref, ins[1:], weights, outs[0], outs[1:],
                                        [r.at[j] for r in scratch], **static))
    for _ in _staggered(programs):
        pass


def _mixer_sequence(x_ref, mod_ref, s0_refs, weights, x1_ref, sout_refs, scratch, *, seq_len, period):
    has_state_in = bool(s0_refs)
    has_state_out = bool(sout_refs)
    g1_ref, win_ref, wconv_ref, bconv_ref, wup_ref, bup_ref, ggla_ref, wout_ref = weights
    og_ref, qd_ref, kd_ref, kst_ref, dect_ref, v_ref, s_ref, sst_ref, o_ref, ya_ref = scratch

    c = GLA_CHUNK
    tile_chunks = ROW_TILE // c
    n_tiles = seq_len // ROW_TILE
    n_pairs = GLA_HEADS // 2
    pair_k = 2 * GLA_DK
    pair_v = 2 * GLA_DV
    m = mod_ref[0]

    def stage1(ti):
        rows = _tile_rows(ti)
        h = _modulated_norm(x_ref[rows, :], g1_ref[...], m[1:2], m[0:1]).astype(BF16)
        row_i = lax.broadcasted_iota(I32, (ROW_TILE, 1), 0)
        yield
        p_gate = _dot(h, win_ref[:, OFF_OG:P_PAD])
        og_ref[rows, :] = p_gate[:, :GLA_DV_TOT]
        yield
        z = _dot(p_gate[:, GLA_DV_TOT:].astype(BF16), wup_ref[...]) + bup_ref[...]
        la = (jnp.minimum(z, 0.0) - jnp.log(1.0 + jnp.exp(-jnp.abs(z)))) * (LOG2_E / GLA_TAU)
        col_j = lax.broadcasted_iota(I32, (1, ROW_TILE), 1)
        same_chunk = (row_i & -c) == (col_j & -c)
        lower = jnp.where(same_chunk & (col_j <= row_i), 1.0, 0.0).astype(BF16)
        la_parts = jnp.concatenate(_split(la), axis=1)
        n_gate = 2 * GLA_DK_TOT
        yield
        pre = _dot(lower, la_parts)
        pre = pre[:, :n_gate] + pre[:, n_gate:]
        tot = jnp.concatenate([jnp.broadcast_to(pre[(n + 1) * c - 1:(n + 1) * c], (c, n_gate))
                               for n in range(tile_chunks)], axis=0)
        p_qkv = _dot(h, win_ref[:, OFF_Q:OFF_OG])
        yield
        q = p_qkv[:, :GLA_DK_TOT] * (GLA_DK ** -0.5)
        k = p_qkv[:, GLA_DK_TOT:2 * GLA_DK_TOT]
        v_ref[rows, :] = p_qkv[:, 2 * GLA_DK_TOT:].astype(BF16)
        for d in range(2):
            cols = slice(d * GLA_DK_TOT, (d + 1) * GLA_DK_TOT)
            if d == 0:
                bq = pre[:, cols]
                bk = tot[:, cols] - bq
            else:
                bk = pre[:, cols] - la[:, cols]
                bq = tot[:, cols] - bk
            qd_ref[d, rows, :] = (q * jnp.exp2(bq)).astype(BF16)
            kd_ref[d, rows, :] = (k * jnp.exp2(-bq)).astype(BF16)
            kst_ref[d, ti] = jnp.transpose(k * jnp.exp2(bk)).astype(BF16)
            totals = [tot[n * c:n * c + 1, cols] for n in range(tile_chunks)]
            totals.append(jnp.zeros((LANES - tile_chunks, GLA_DK_TOT), F32))
            dect_ref[d, ti] = jnp.transpose(jnp.exp2(jnp.concatenate(totals, axis=0)))
        yield
        p_conv = _dot(h, win_ref[:, :OFF_Q])
        yield
        pos = row_i & (period - 1)
        u = p_conv[:, OFF_XC:OFF_XC + D_CONV] * p_conv[:, OFF_XV:OFF_XV + D_CONV]
        u_prev = jnp.where(pos == 0, 0.0, pltpu.roll(u, 1, 0))
        u_next = jnp.where(pos == period - 1, 0.0, pltpu.roll(u, ROW_TILE - 1, 0))
        conv = u_prev * wconv_ref[0:1, :] + u * wconv_ref[1:2, :] + u_next * wconv_ref[2:3, :] + bconv_ref[...]
        ya_ref[rows, :] = (p_conv[:, OFF_XB:OFF_XB + D_CONV] * conv).astype(BF16)
        yield

    yield from _for_row_tiles(seq_len, stage1, independent=True)

    for d in range(2):
        for pair in range(n_pairs):
            if has_state_in:
                zero = jnp.zeros((GLA_DK, GLA_DV), F32)
                top = jnp.concatenate([s0_refs[d][2 * pair], zero], axis=1)
                bot = jnp.concatenate([zero, s0_refs[d][2 * pair + 1]], axis=1)
                s_ref[d, pair] = jnp.concatenate([top, bot], axis=0)
            else:
                s_ref[d, pair] = jnp.zeros((pair_k, pair_v), F32)

    def scan_tile(i):
        upper_lane = lax.broadcasted_iota(I32, (1, LANES), 1) >= GLA_DK
        qi = lax.broadcasted_iota(I32, (LANES, 1), 0)
        kj = lax.broadcasted_iota(I32, (1, 2 * LANES), 1) & (LANES - 1)
        same_chunk = (qi & c) == (kj & c)
        causal = (same_chunk & (kj <= qi), same_chunk & (kj >= qi))
        for pair in range(n_pairs):
            kl = slice(pair * pair_k, (pair + 1) * pair_k)
            vl = slice(pair * pair_v, (pair + 1) * pair_v)
            for blk in range(ROW_TILE // LANES):
                rows = _tile_rows(i, blk * LANES, LANES)
                att = None
                for d in range(2):
                    kd = kd_ref[d, rows, kl]
                    zk = jnp.zeros_like(kd)
                    keys = jnp.concatenate([jnp.where(upper_lane, zk, kd), jnp.where(upper_lane, kd, zk)], axis=0)
                    a = jnp.where(causal[d], _dot_nt(qd_ref[d, rows, kl], keys), 0.0)
                    att = a if att is None else att + a
                v = v_ref[rows, vl]
                zv = jnp.zeros((LANES, GLA_DV), BF16)
                v_bd = jnp.concatenate([jnp.concatenate([v[:, :GLA_DV], zv], axis=1),
                                        jnp.concatenate([zv, v[:, GLA_DV:]], axis=1)], axis=0)
                o_ref[rows, vl] = _dot(att.astype(BF16), v_bd)
        yield
        key_row = lax.broadcasted_iota(I32, (pair_k, 1), 0)
        val_col = lax.broadcasted_iota(I32, (1, pair_v), 1)
        blockdiag = (key_row >= GLA_DK) == (val_col >= GLA_DV)
        for d in range(2):
            tile = i if d == 0 else n_tiles - 1 - i
            chunks = range(tile_chunks)
            for pair in range(n_pairs):
                kr = slice(pair * pair_k, (pair + 1) * pair_k)
                vl = slice(pair * pair_v, (pair + 1) * pair_v)
                s = s_ref[d, pair]
                for c4 in (chunks if d == 0 else reversed(chunks)):
                    blk, half = divmod(c4, 2)
                    kst = kst_ref[d, tile, kr, blk * LANES:(blk + 1) * LANES]
                    kst = jnp.where(upper_lane if half else ~upper_lane, kst, jnp.zeros_like(kst))
                    kv = jnp.where(blockdiag, _dot(kst, v_ref[_tile_rows(tile, blk * LANES, LANES), vl]), 0.0)
                    sst_ref[pair, tile * tile_chunks + c4, d * pair_k:(d + 1) * pair_k, :] = s.astype(BF16)
                    s = dect_ref[d, tile, kr, c4:c4 + 1] * s + kv
                s_ref[d, pair] = s
        yield

    yield from _for_row_tiles(seq_len, scan_tile, independent=False)

    if has_state_out:
        for d in range(2):
            for pair in range(n_pairs):
                s = s_ref[d, pair]
                sout_refs[d][2 * pair] = s[0:GLA_DK, 0:GLA_DV]
                sout_refs[d][2 * pair + 1] = s[GLA_DK:, GLA_DV:]

    def stage3(i):
        for pair in range(n_pairs):
            kl = slice(pair * pair_k, (pair + 1) * pair_k)
            vl = slice(pair * pair_v, (pair + 1) * pair_v)
            for c4 in range(tile_chunks):
                crow = _tile_rows(i, c4 * c, c)
                q2 = jnp.concatenate([qd_ref[0, crow, kl], qd_ref[1, crow, kl]], axis=1)
                o_ref[crow, vl] = o_ref[crow, vl] + _dot(q2, sst_ref[pair, i * tile_chunks + c4])
        yield
        rows = _tile_rows(i)
        heads = []
        for h in range(GLA_HEADS):
            hl = slice(h * GLA_DV, (h + 1) * GLA_DV)
            oh = o_ref[rows, hl]
            r = lax.rsqrt(jnp.mean(oh * oh, axis=-1, keepdims=True) + EPS)
            heads.append(oh * r * ggla_ref[:, hl])
        y_b = jnp.concatenate(heads, axis=1) * _silu(og_ref[rows, :])
        y = jnp.concatenate([ya_ref[rows, :], y_b.astype(BF16)], axis=1)
        x1_ref[rows, :] = x_ref[rows, :] + m[2:3] * _dot(y, wout_ref[...])
        yield

    yield from _for_row_tiles(seq_len, stage3, independent=True)


def _mixer_call(x, mod3, mod_row_of_step, states, weights, *, period, has_state_out, seqs_per_step):
    n_seq, seq_len, _ = x.shape
    sps = seqs_per_step
    has_state_in = states is not None
    stages_w_in = weights[1].dtype == F32
    kernel = functools.partial(_mixer_kernel, seqs_per_step=sps, seq_len=seq_len, period=period,
                               has_state_in=has_state_in, has_state_out=has_state_out, stages_w_in=stages_w_in)
    state_spec = pl.BlockSpec((sps, None, GLA_HEADS, GLA_DK, GLA_DV), lambda b: (b, 0, 0, 0, 0))
    const2 = lambda b: (0, 0)
    in_specs = [pl.BlockSpec((sps, seq_len, D_MODEL), lambda b: (b, 0, 0)),
                pl.BlockSpec((1, N_MOD, D_MODEL), lambda b: (mod_row_of_step(b), 0, 0))]
    args = [x, mod3]
    if has_state_in:
        in_specs += [state_spec, state_spec]
        args += list(states)
    in_specs += [pl.BlockSpec(w.shape, lambda b, nd=w.ndim: (0,) * nd, pipeline_mode=pl.Buffered(1))
                 for w in weights]
    args += list(weights)
    out_specs = [pl.BlockSpec((sps, seq_len, D_MODEL), lambda b: (b, 0, 0))]
    out_shape = [jax.ShapeDtypeStruct((n_seq, seq_len, D_MODEL), F32)]
    if has_state_out:
        out_specs += [state_spec, state_spec]
        out_shape += [jax.ShapeDtypeStruct((n_seq, 1, GLA_HEADS, GLA_DK, GLA_DV), F32)] * 2
    if stages_w_in:
        out_specs.append(pl.BlockSpec((D_MODEL, P_PAD), const2))
        out_shape.append(jax.ShapeDtypeStruct((D_MODEL, P_PAD), BF16))
    n_tiles = seq_len // ROW_TILE
    n_pairs = GLA_HEADS // 2
    per_seq = lambda shape, dtype: pltpu.VMEM((sps,) + shape, dtype)
    scratch = [per_seq((seq_len, GLA_DV_TOT), F32),
               per_seq((2, seq_len, GLA_DK_TOT), BF16),
               per_seq((2, seq_len, GLA_DK_TOT), BF16),
               per_seq((2, n_tiles, GLA_DK_TOT, ROW_TILE), BF16),
               per_seq((2, n_tiles, GLA_DK_TOT, LANES), F32),
               per_seq((seq_len, GLA_DV_TOT), BF16),
               per_seq((2, n_pairs, 2 * GLA_DK, 2 * GLA_DV), F32),
               per_seq((n_pairs, seq_len // GLA_CHUNK, 4 * GLA_DK, 2 * GLA_DV), BF16),
               per_seq((seq_len, GLA_DV_TOT), F32),
               per_seq((seq_len, D_CONV), BF16)]
    return pl.pallas_call(
        kernel,
        grid=(n_seq // sps,),
        in_specs=in_specs,
        out_specs=out_specs,
        out_shape=out_shape,
        scratch_shapes=scratch,
        compiler_params=_params(1),
        name="mixer",
    )(*args)


SUBLANES = 8


ROW_TOKENS = LANES // N_EXPERTS


def _route_kernel(x1_ref, mod_ref, g2_ref, wr2_ref, *refs, seqs_per_step, **static):
    programs = [_route_sequence(x1_ref.at[j], mod_ref, g2_ref, wr2_ref, *[r.at[j] for r in refs], **static)
                for j in range(seqs_per_step)]
    for _ in _staggered(programs):
        pass


def _route_sequence(x1_ref, mod_ref, g2_ref, wr2_ref, xs_ref, rank_ref, gate_ref,
                    pt_ref, h2_ref, spread_ref, bar_ref, cnt_ref, win_ref, *, seq_len, cap):
    m = mod_ref[0]
    n_rows = seq_len // ROW_TOKENS
    for t in range(seq_len // ROW_TILE):
        rows = slice(t * ROW_TILE, (t + 1) * ROW_TILE)
        h2 = _modulated_norm(x1_ref[rows, :], g2_ref[...], m[4:5], m[3:4])
        hi = h2.astype(BF16)
        h2_ref[rows, :] = hi
        yield
        by_hi = _dot_nt(wr2_ref[...], hi)
        pt_ref[:, rows] = by_hi[:N_EXPERTS] + by_hi[N_EXPERTS:]
    yield

    logits = pt_ref[...]
    ex = jnp.exp(logits - jnp.max(logits, axis=0, keepdims=True))
    probs = ex / jnp.sum(ex, axis=0, keepdims=True)
    pt_ref[...] = probs
    pad = jnp.zeros((LANES - N_EXPERTS, seq_len), F32)
    p_tok = jnp.transpose(jnp.concatenate([probs, pad], axis=0))

    spread = p_tok
    sh = N_EXPERTS
    while sh < LANES:
        spread = spread + pltpu.roll(spread, sh, 1)
        sh *= 2
    spread_ref[...] = spread
    sub_j = lax.broadcasted_iota(I32, (SUBLANES, 1), 0)
    lane_g = lax.broadcasted_iota(I32, (1, LANES), 1) >> (N_EXPERTS.bit_length() - 1)
    lane_j = (ROW_TOKENS - lane_g) & (ROW_TOKENS - 1)
    own_group = jnp.where(lane_j == sub_j, spread.reshape(n_rows, SUBLANES, LANES), 0.0)
    bar = jnp.sum(own_group, axis=1, keepdims=True)
    bar_ref[...] = jnp.broadcast_to(bar, (n_rows, SUBLANES, LANES))
    cnt_ref[...] = jnp.zeros((n_rows, SUBLANES, LANES), F32)

    rows_per_block = LANES // ROW_TOKENS

    def count_block(g_s, g_t, relation):
        if isinstance(g_s, int):
            s_rows = slice(g_s * LANES, (g_s + 1) * LANES)
        else:
            s_rows = pl.ds(pl.multiple_of(g_s * LANES, LANES), LANES)
        s_blk = spread_ref[s_rows, :]
        wins = [jnp.zeros((SUBLANES, LANES), F32)] * rows_per_block
        for tl in range(rows_per_block):
            r = g_t * rows_per_block + tl
            bar_r = bar_ref[r]
            acc = cnt_ref[r]
            for sl in range(rows_per_block):
                s_vreg = s_blk[sl * SUBLANES:(sl + 1) * SUBLANES]
                order = relation if relation != "same" else ("before" if sl < tl else "after" if sl > tl else "tie")
                if order == "before":
                    won = jnp.where(s_vreg >= bar_r, 1.0, 0.0)
                    acc = acc + won
                    if relation == "before":
                        wins[sl] = wins[sl] + won
                elif order == "after":
                    acc = acc + jnp.where(s_vreg > bar_r, 1.0, 0.0)
                else:
                    acc = acc + jnp.where(s_vreg > bar_r, 1.0, 0.0) \
                        + jnp.where((s_vreg == bar_r) & (sub_j < lane_j), 1.0, 0.0)
            cnt_ref[r] = acc
        if relation == "before":
            if isinstance(g_s, int):
                w_rows = slice(g_s * rows_per_block, (g_s + 1) * rows_per_block)
            else:
                w_rows = pl.ds(g_s * rows_per_block, rows_per_block)
            win_ref[w_rows] = win_ref[w_rows] + jnp.stack(wins)

    n_grp = seq_len // LANES
    win_ref[...] = jnp.zeros((n_rows, SUBLANES, LANES), F32)
    yield
    if n_grp <= 2:
        for g_t in range(n_grp):
            for g_s in range(g_t + 1):
                count_block(g_s, g_t, "before" if g_s < g_t else "same")
                yield
    else:
        def per_target_group(g_t, carry):
            def before(g_s, c):
                count_block(g_s, g_t, "before")
                return c
            lax.fori_loop(0, g_t, before, 0)
            count_block(g_t, g_t, "same")
            return carry
        lax.fori_loop(0, n_grp, per_target_group, 0)

    counts = jnp.sum(cnt_ref[...], axis=1, keepdims=True)
    counts = jnp.broadcast_to(counts, (n_rows, SUBLANES, LANES)).reshape(seq_len, LANES)
    rank_tok = pltpu.roll(counts, 0, 1, stride=N_EXPERTS, stride_axis=0)
    wins = win_ref[...].reshape(seq_len, LANES)
    sh = N_EXPERTS
    while sh < LANES:
        wins = wins + pltpu.roll(wins, sh, 1)
        sh *= 2
    tok = lax.broadcasted_iota(I32, (seq_len, 1), 0)
    later = ((n_grp - 1 - (tok >> (LANES.bit_length() - 1))) * LANES).astype(F32)
    rank_tok = rank_tok + (later - wins)
    rank_ref[...] = jnp.transpose(rank_tok)[:N_EXPERTS].astype(I32)
    yield

    group = DISPATCH_ROWS // cap
    slot = lax.broadcasted_iota(I32, (cap, 1), 0)
    half = D_MODEL // 2
    for gi in range(N_EXPERTS // group):
        picks = []
        for e in range(gi * group, (gi + 1) * group):
            oh = rank_ref[e:e + 1, :] == slot
            gate = jnp.sum(jnp.where(oh, pt_ref[e:e + 1, :], 0.0), axis=1, keepdims=True)
            gate_ref[e * cap:(e + 1) * cap, :] = jnp.broadcast_to(gate, (cap, LANES))
            picks.append(oh)
        ohb = jnp.where(jnp.concatenate(picks, axis=0), 1.0, 0.0).astype(BF16)
        yield
        out_rows = slice(gi * DISPATCH_ROWS, (gi + 1) * DISPATCH_ROWS)
        xs_ref[out_rows, :half] = _dot(ohb, h2_ref[:, :half]).astype(BF16)
        xs_ref[out_rows, half:] = _dot(ohb, h2_ref[:, half:]).astype(BF16)
        yield


def _route_call(x1, mod3, mod_row_of_step, g2, wr_both, *, seqs_per_step):
    n_seq, seq_len, _ = x1.shape
    sps = seqs_per_step
    cap = EC_CAPACITY_FACTOR * seq_len // N_EXPERTS
    kernel = functools.partial(_route_kernel, seqs_per_step=sps, seq_len=seq_len, cap=cap)
    const2 = lambda b: (0, 0)
    per_seq = lambda shape, dtype: pltpu.VMEM((sps,) + shape, dtype)
    rank_rows = (seq_len // ROW_TOKENS, SUBLANES, LANES)
    return pl.pallas_call(
        kernel,
        grid=(n_seq // sps,),
        in_specs=[pl.BlockSpec((sps, seq_len, D_MODEL), lambda b: (b, 0, 0)),
                  pl.BlockSpec((1, N_MOD, D_MODEL), lambda b: (mod_row_of_step(b), 0, 0)),
                  pl.BlockSpec((1, D_MODEL), const2),
                  pl.BlockSpec((2 * N_EXPERTS, D_MODEL), const2)],
        out_specs=[pl.BlockSpec((sps, N_EXPERTS * cap, D_MODEL), lambda b: (b, 0, 0)),
                   pl.BlockSpec((sps, N_EXPERTS, seq_len), lambda b: (b, 0, 0)),
                   pl.BlockSpec((sps, N_EXPERTS * cap, LANES), lambda b: (b, 0, 0))],
        out_shape=[jax.ShapeDtypeStruct((n_seq, N_EXPERTS * cap, D_MODEL), BF16),
                   jax.ShapeDtypeStruct((n_seq, N_EXPERTS, seq_len), I32),
                   jax.ShapeDtypeStruct((n_seq, N_EXPERTS * cap, LANES), F32)],
        scratch_shapes=[per_seq((N_EXPERTS, seq_len), F32),
                        per_seq((seq_len, D_MODEL), BF16),
                        per_seq((seq_len, LANES), F32),
                        per_seq(rank_rows, F32),
                        per_seq(rank_rows, F32),
                        per_seq(rank_rows, F32)],
        compiler_params=_params(1),
        name="route",
    )(x1, mod3, g2, wr_both)


def _experts_kernel(xc_ref, xl_ref, gc_ref, gl_ref, wg_ref, wu_ref, wd_ref, yc_ref, yl_ref,
                    wgb_ref, wub_ref, wdb_ref):
    wgb_ref[...] = wg_ref[...].astype(BF16)
    wub_ref[...] = wu_ref[...].astype(BF16)
    wdb_ref[...] = wd_ref[...].astype(BF16)

    def run(x_ref, g_ref, y_ref):
        n_seq, cap, _ = x_ref.shape
        seqs = DISPATCH_ROWS // cap
        for s0 in range(0, n_seq, seqs):
            x = x_ref[s0:s0 + seqs].reshape(DISPATCH_ROWS, D_MODEL)
            a = (_silu(_dot(x, wgb_ref[...])) * _dot(x, wub_ref[...])).astype(BF16)
            gate = g_ref[s0:s0 + seqs].reshape(DISPATCH_ROWS, LANES)
            y = _dot(a, wdb_ref[...]) * jnp.concatenate([gate] * (D_MODEL // LANES), axis=1)
            y_ref[s0:s0 + seqs] = y.astype(BF16).reshape(seqs, cap, D_MODEL)

    run(xc_ref, gc_ref, yc_ref)
    run(xl_ref, gl_ref, yl_ref)


def _experts_call(xs_ctx, xs_lat, gates_ctx, gates_lat, w_gate, w_up, w_down):
    def slot_spec(a):
        n_seq, _, cap, width = a.shape
        return pl.BlockSpec((n_seq, None, cap, width), lambda e: (0, e, 0, 0))

    w_spec = pl.BlockSpec((None, D_MODEL, D_EXPERT), lambda e: (e, 0, 0))
    return pl.pallas_call(
        _experts_kernel,
        grid=(N_EXPERTS,),
        in_specs=[slot_spec(xs_ctx), slot_spec(xs_lat), slot_spec(gates_ctx), slot_spec(gates_lat),
                  w_spec, w_spec, pl.BlockSpec((None, D_EXPERT, D_MODEL), lambda e: (e, 0, 0))],
        out_specs=[slot_spec(xs_ctx), slot_spec(xs_lat)],
        out_shape=[jax.ShapeDtypeStruct(xs_ctx.shape, BF16), jax.ShapeDtypeStruct(xs_lat.shape, BF16)],
        scratch_shapes=[pltpu.VMEM((D_MODEL, D_EXPERT), BF16),
                        pltpu.VMEM((D_MODEL, D_EXPERT), BF16),
                        pltpu.VMEM((D_EXPERT, D_MODEL), BF16)],
        compiler_params=_params(1),
        name="experts",
    )(xs_ctx, xs_lat, gates_ctx, gates_lat, w_gate, w_up, w_down)


def _combine_kernel(x1_ref, y_ref, rank_ref, mod_ref, gf_ref, o_ref, oh_ref, *, seqs_per_step, seq_len, cap):
    m = mod_ref[0]
    slot = lax.broadcasted_iota(I32, (cap, 1), 0)
    for j in range(seqs_per_step):
        for e in range(N_EXPERTS):
            oh_ref[j, e * cap:(e + 1) * cap, :] = jnp.where(rank_ref[j, e:e + 1, :] == slot, 1.0, 0.0).astype(BF16)
        for t in range(seq_len // ROW_TILE):
            rows = slice(t * ROW_TILE, (t + 1) * ROW_TILE)
            moe = _dot_tn(oh_ref[j, :, rows], y_ref[j])
            x2 = x1_ref[j, rows, :] + m[5:6] * moe
            r = lax.rsqrt(jnp.mean(x2 * x2, axis=-1, keepdims=True) + EPS)
            o_ref[j, rows, :] = (x2 * r) * gf_ref[...]


def _combine_call(x1, y, rank, mod3, mod_row_of_step, g_final, *, seqs_per_step):
    n_seq, seq_len, _ = x1.shape
    n_slots = y.shape[1]
    sps = seqs_per_step
    kernel = functools.partial(_combine_kernel, seqs_per_step=sps, seq_len=seq_len, cap=n_slots // N_EXPERTS)
    seq_spec = pl.BlockSpec((sps, seq_len, D_MODEL), lambda b: (b, 0, 0))
    return pl.pallas_call(
        kernel,
        grid=(n_seq // sps,),
        in_specs=[seq_spec,
                  pl.BlockSpec((sps, n_slots, D_MODEL), lambda b: (b, 0, 0)),
                  pl.BlockSpec((sps, N_EXPERTS, seq_len), lambda b: (b, 0, 0)),
                  pl.BlockSpec((1, N_MOD, D_MODEL), lambda b: (mod_row_of_step(b), 0, 0)),
                  pl.BlockSpec((1, D_MODEL), lambda b: (0, 0))],
        out_specs=seq_spec,
        out_shape=jax.ShapeDtypeStruct((n_seq, seq_len, D_MODEL), F32),
        scratch_shapes=[pltpu.VMEM((sps, n_slots, seq_len), BF16)],
        compiler_params=_params(1),
        name="combine",
    )(x1, y, rank, mod3, g_final)


def kernel(x_prompt, x_sample, state_gla_fwd, state_gla_bwd, c, c_ctx, w_mod, b_mod, g_norm1, g_norm2,
           w_in, w_conv, b_conv, w_a_up_f, b_a_f, w_a_up_b, b_a_b, g_gla_norm, w_out, w_router,
           w_gate, w_up, w_down, g_final):
    assert w_mod.shape[0] == 1, "single trunk layer"
    n_ctx, ctx_len, _ = x_prompt.shape
    n_lat, lat_len, _ = x_sample.shape
    ctx_cap = EC_CAPACITY_FACTOR * ctx_len // N_EXPERTS
    lat_cap = EC_CAPACITY_FACTOR * lat_len // N_EXPERTS

    c_rows = jnp.concatenate([c_ctx[None, :], c, jnp.zeros((8 - 1 - n_lat, D_MODEL), F32)], axis=0)
    mod3 = _mod_call(c_rows, w_mod[0], b_mod).reshape(8, N_MOD, D_MODEL)

    w_up_gate = jnp.zeros((P_PAD - OFF_ALOW, 2 * GLA_DK_TOT), F32)
    w_up_gate = w_up_gate.at[:GLA_LOW_RANK, :GLA_DK_TOT].set(w_a_up_f[0])
    w_up_gate = w_up_gate.at[GLA_LOW_RANK:2 * GLA_LOW_RANK, GLA_DK_TOT:].set(w_a_up_b[0]).astype(BF16)
    b_up_gate = jnp.concatenate([b_a_f[0], b_a_b[0]])[None, :]
    mixer_weights = [g_norm1, jnp.transpose(w_in[0]), w_conv[0], b_conv, w_up_gate, b_up_gate,
                     g_gla_norm[0].reshape(1, GLA_DV_TOT), w_out[0].astype(BF16)]
    wr_t = jnp.transpose(w_router[0])
    wr_hi = wr_t.astype(BF16)
    wr_lo = (wr_t - wr_hi.astype(F32)).astype(BF16)

    ctx_row = lambda b: 0
    lat_row = lambda b: b + 1

    x1_ctx, new_f, new_b, w_in_bf = _mixer_call(x_prompt, mod3, ctx_row, None, mixer_weights,
                                                period=ctx_len, has_state_out=True, seqs_per_step=4)
    mixer_weights[1] = w_in_bf
    (x1_lat,) = _mixer_call(x_sample, mod3, lat_row, (state_gla_fwd, state_gla_bwd), mixer_weights,
                            period=GRID_W, has_state_out=False, seqs_per_step=1)

    wr_both = jnp.concatenate([wr_hi, wr_lo], axis=0)
    xs_ctx, rank_ctx, gates_ctx = _route_call(x1_ctx, mod3, ctx_row, g_norm2, wr_both, seqs_per_step=4)
    xs_lat, rank_lat, gates_lat = _route_call(x1_lat, mod3, lat_row, g_norm2, wr_both, seqs_per_step=1)

    per_expert = lambda a, n, cap: a.reshape(n, N_EXPERTS, cap, a.shape[-1])
    y_ctx, y_lat = _experts_call(per_expert(xs_ctx, n_ctx, ctx_cap), per_expert(xs_lat, n_lat, lat_cap),
                                 per_expert(gates_ctx, n_ctx, ctx_cap), per_expert(gates_lat, n_lat, lat_cap),
                                 w_gate[0], w_up[0], w_down[0])

    g_fin = g_final[None, :]
    y_prompt = _combine_call(x1_ctx, y_ctx.reshape(xs_ctx.shape), rank_ctx, mod3, ctx_row, g_fin, seqs_per_step=4)
    y_sample = _combine_call(x1_lat, y_lat.reshape(xs_lat.shape), rank_lat, mod3, lat_row, g_fin, seqs_per_step=1)
    return y_prompt, y_sample, new_f, new_b
```

```python
import functools

import jax
import jax.numpy as jnp
from jax import lax
from jax.experimental import pallas as pl
from jax.experimental.pallas import tpu as pltpu

F32 = jnp.float32
BF16 = jnp.bfloat16
I32 = jnp.int32

D_MODEL = 1024
D_CONV = D_MODEL // 2
GRID_W = 64
GLA_HEADS = 4
GLA_DK = 64
GLA_DV = 128
GLA_DK_TOT = GLA_HEADS * GLA_DK
GLA_DV_TOT = GLA_HEADS * GLA_DV
GLA_LOW_RANK = 16
GLA_TAU = 16.0
GLA_CHUNK = 64
N_EXPERTS = 16
EC_CAPACITY_FACTOR = 2
D_EXPERT = 1024
N_MOD = 6
EPS = 1e-6
LOG2_E = 1.4426950408889634

OFF_XB = 0
OFF_XC = D_CONV
OFF_XV = 2 * D_CONV
OFF_Q = 3 * D_CONV
OFF_K = OFF_Q + GLA_DK_TOT
OFF_V = OFF_K + GLA_DK_TOT
OFF_OG = OFF_V + GLA_DV_TOT
OFF_ALOW = OFF_OG + GLA_DV_TOT
P_TOT = OFF_ALOW + 2 * GLA_LOW_RANK

LANES = 128
P_PAD = -(-P_TOT // LANES) * LANES
ROW_TILE = 256
N_MIXER_WEIGHTS = 10
DISPATCH_ROWS = 512
VMEM_LIMIT = 56 * 1024 * 1024


def _dot(a, b):
    return jnp.dot(a, b, preferred_element_type=F32)


def _dot_nt(a, b):
    return lax.dot_general(a, b, (((1,), (1,)), ((), ())), preferred_element_type=F32)


def _dot_tn(a, b):
    return lax.dot_general(a, b, (((0,), (0,)), ((), ())), preferred_element_type=F32)


def _split(a):
    hi = a.astype(BF16)
    lo = (a - hi.astype(F32)).astype(BF16)
    return hi, lo


def _silu(x):
    return x * jax.nn.sigmoid(x)


def _modulated_norm(x, g, scale, shift):
    r = lax.rsqrt(jnp.mean(x * x, axis=-1, keepdims=True) + EPS)
    return (x * r) * (g * (1.0 + scale)) + shift


def _params(n_axes):
    return pltpu.CompilerParams(dimension_semantics=("arbitrary",) * n_axes,
                                vmem_limit_bytes=VMEM_LIMIT)


def _mod_kernel(c_ref, w_ref, b_ref, ctx_ref, lat_ref, acc_ref):
    rows = c_ref.shape[0]
    s = _silu(c_ref[...])
    s_hi, s_lo = _split(jnp.concatenate([s, s], axis=0))
    upper = lax.broadcasted_iota(I32, (2 * rows, 1), 0) < rows
    w_hi, w_lo = _split(w_ref[...])
    by_hi = _dot(jnp.where(upper, s_hi, s_lo), w_hi)
    part = by_hi[:rows] + by_hi[rows:] + _dot(s_hi[:rows], w_lo)

    @pl.when(pl.program_id(0) == 0)
    def _():
        acc_ref[...] = part + b_ref[...]

    @pl.when(pl.program_id(0) != 0)
    def _():
        acc_ref[...] = acc_ref[...] + part

    @pl.when(pl.program_id(0) == pl.num_programs(0) - 1)
    def _():
        ctx_ref[0] = acc_ref[0:1, :]
        for i in range(lat_ref.shape[0]):
            lat_ref[i] = acc_ref[1 + i:2 + i, :]


def _mod_call(c_rows, n_lat, w_mod, b_mod):
    rows, d = c_rows.shape
    n = w_mod.shape[1]
    tk = D_MODEL // 4
    return pl.pallas_call(
        _mod_kernel,
        grid=(d // tk,),
        in_specs=[pl.BlockSpec((rows, tk), lambda k: (0, k)),
                  pl.BlockSpec((tk, n), lambda k: (k, 0)),
                  pl.BlockSpec((1, n), lambda k: (0, 0))],
        out_specs=[pl.BlockSpec((1, 1, n), lambda k: (0, 0, 0)),
                   pl.BlockSpec((n_lat, 1, n), lambda k: (0, 0, 0))],
        out_shape=[jax.ShapeDtypeStruct((1, 1, n), F32), jax.ShapeDtypeStruct((n_lat, 1, n), F32)],
        scratch_shapes=[pltpu.VMEM((rows, n), F32)],
        compiler_params=_params(1),
        name="mod",
    )(c_rows, w_mod, b_mod)


def _mod_rows(mod_ref, j):
    row = mod_ref[j % mod_ref.shape[0]]
    return [row[:, i * D_MODEL:(i + 1) * D_MODEL] for i in range(N_MOD)]


def _mod_spec(mod, seqs_per_step):
    if mod.shape[0] == 1:
        return pl.BlockSpec(mod.shape, lambda b: (0, 0, 0))
    return pl.BlockSpec((seqs_per_step,) + mod.shape[1:], lambda b: (b, 0, 0))


def _staggered(programs):
    programs = list(programs)
    started = 0
    while programs:
        started = min(started + 1, len(programs))
        running = [p for p in programs[:started] if next(p, "done") != "done"]
        programs = running + programs[started:]
        started = len(running)
        yield


def _for_row_tiles(seq_len, phases, independent):
    n = seq_len // ROW_TILE
    if independent:
        yield from _staggered(phases(i) for i in range(n))
    else:
        for i in range(n):
            yield from phases(i)


def _tile_rows(tile, offset=0, size=ROW_TILE):
    if isinstance(tile, int):
        return pl.ds(tile * ROW_TILE + offset, size)
    return pl.ds(pl.multiple_of(tile * ROW_TILE + offset, size), size)


def _mixer_kernel(*refs, seqs_per_step, has_state_in, has_state_out, stages_w_in, **static):
    refs = list(refs)
    n_in = 2 + (2 if has_state_in else 0)
    per_seq_in, refs = [refs[0]] + refs[2:n_in], [refs[1]] + refs[n_in:]
    mod_ref, weights, refs = refs[0], refs[1:1 + N_MIXER_WEIGHTS], refs[1 + N_MIXER_WEIGHTS:]
    n_out = 1 + (2 if has_state_out else 0)
    per_seq_out, refs = refs[:n_out], refs[n_out:]
    if stages_w_in:
        win_f32_ref, win_bf_ref, scratch = weights[1], refs[0], refs[1:]
        weights = weights[:1] + [win_bf_ref] + weights[2:]

        @pl.when(pl.program_id(0) == 0)
        def _():
            for c0 in range(0, P_PAD, LANES):
                n = min(LANES, P_TOT - c0)
                cols = win_f32_ref[c0:c0 + n, :]
                if n < LANES:
                    cols = jnp.concatenate([cols, jnp.zeros((LANES - n, D_MODEL), F32)], axis=0)
                win_bf_ref[:, c0:c0 + LANES] = jnp.transpose(cols).astype(BF16)
    else:
        scratch = refs
    programs = []
    for j in range(seqs_per_step):
        ins = [r.at[j] for r in per_seq_in]
        outs = [r.at[j] for r in per_seq_out]
        programs.append(_mixer_sequence(ins[0], _mod_rows(mod_ref, j), ins[1:], weights, outs[0], outs[1:],
                                        [r.at[j] for r in scratch], **static))
    for _ in _staggered(programs):
        pass


def _mixer_sequence(x_ref, m, s0_refs, weights, x1_ref, sout_refs, scratch, *, seq_len, period):
    has_state_in = bool(s0_refs)
    has_state_out = bool(sout_refs)
    g1_ref, win_ref, wconv_ref, bconv_ref, wupf_ref, bupf_ref, wupb_ref, bupb_ref, ggla_ref, wout_ref = weights
    og_ref, qd_ref, kd_ref, kst_ref, dect_ref, v_ref, s_ref, sst_ref, o_ref, ya_ref = scratch

    c = GLA_CHUNK
    tile_chunks = ROW_TILE // c
    n_tiles = seq_len // ROW_TILE
    n_pairs = GLA_HEADS // 2
    pair_k = 2 * GLA_DK
    pair_v = 2 * GLA_DV

    def stage1(ti):
        rows = _tile_rows(ti)
        h = _modulated_norm(x_ref[rows, :], g1_ref[...], m[1], m[0]).astype(BF16)
        row_i = lax.broadcasted_iota(I32, (ROW_TILE, 1), 0)
        yield
        p_gate = _dot(h, win_ref[:, OFF_OG:P_PAD])
        og_ref[rows, :] = p_gate[:, :GLA_DV_TOT]
        yield
        zero_up = jnp.zeros((GLA_LOW_RANK, GLA_DK_TOT), F32)
        w_up = jnp.concatenate([jnp.concatenate([wupf_ref[0], zero_up], axis=1),
                                jnp.concatenate([zero_up, wupb_ref[0]], axis=1),
                                jnp.zeros((P_PAD - P_TOT, 2 * GLA_DK_TOT), F32)], axis=0).astype(BF16)
        b_up = jnp.concatenate([bupf_ref[...], bupb_ref[...]], axis=1)
        z = _dot(p_gate[:, GLA_DV_TOT:].astype(BF16), w_up) + b_up
        la = (jnp.minimum(z, 0.0) - jnp.log(1.0 + jnp.exp(-jnp.abs(z)))) * (LOG2_E / GLA_TAU)
        col_j = lax.broadcasted_iota(I32, (1, ROW_TILE), 1)
        same_chunk = (row_i & -c) == (col_j & -c)
        lower = jnp.where(same_chunk & (col_j <= row_i), 1.0, 0.0).astype(BF16)
        la_parts = jnp.concatenate(_split(la), axis=1)
        n_gate = 2 * GLA_DK_TOT
        yield
        pre = _dot(lower, la_parts)
        pre = pre[:, :n_gate] + pre[:, n_gate:]
        tot = jnp.concatenate([jnp.broadcast_to(pre[(n + 1) * c - 1:(n + 1) * c], (c, n_gate))
                               for n in range(tile_chunks)], axis=0)
        p_qkv = _dot(h, win_ref[:, OFF_Q:OFF_OG])
        yield
        q = p_qkv[:, :GLA_DK_TOT] * (GLA_DK ** -0.5)
        k = p_qkv[:, GLA_DK_TOT:2 * GLA_DK_TOT]
        v_ref[rows, :] = p_qkv[:, 2 * GLA_DK_TOT:].astype(BF16)
        for d in range(2):
            cols = slice(d * GLA_DK_TOT, (d + 1) * GLA_DK_TOT)
            if d == 0:
                bq = pre[:, cols]
                bk = tot[:, cols] - bq
            else:
                bk = pre[:, cols] - la[:, cols]
                bq = tot[:, cols] - bk
            qd_ref[d, rows, :] = (q * jnp.exp2(bq)).astype(BF16)
            kd_ref[d, rows, :] = (k * jnp.exp2(-bq)).astype(BF16)
            kst_ref[d, ti] = jnp.transpose(k * jnp.exp2(bk)).astype(BF16)
            totals = [tot[n * c:n * c + 1, cols] for n in range(tile_chunks)]
            totals.append(jnp.zeros((LANES - tile_chunks, GLA_DK_TOT), F32))
            dect_ref[d, ti] = jnp.transpose(jnp.exp2(jnp.concatenate(totals, axis=0)))
        yield
        p_conv = _dot(h, win_ref[:, :OFF_Q])
        yield
        pos = row_i & (period - 1)
        u = p_conv[:, OFF_XC:OFF_XC + D_CONV] * p_conv[:, OFF_XV:OFF_XV + D_CONV]
        u_prev = jnp.where(pos == 0, 0.0, pltpu.roll(u, 1, 0))
        u_next = jnp.where(pos == period - 1, 0.0, pltpu.roll(u, ROW_TILE - 1, 0))
        conv = u_prev * wconv_ref[0, 0:1, :] + u * wconv_ref[0, 1:2, :] + u_next * wconv_ref[0, 2:3, :] + bconv_ref[...]
        ya_ref[rows, :] = (p_conv[:, OFF_XB:OFF_XB + D_CONV] * conv).astype(BF16)
        yield

    yield from _for_row_tiles(seq_len, stage1, independent=True)

    for d in range(2):
        for pair in range(n_pairs):
            if has_state_in:
                zero = jnp.zeros((GLA_DK, GLA_DV), F32)
                top = jnp.concatenate([s0_refs[d][2 * pair], zero], axis=1)
                bot = jnp.concatenate([zero, s0_refs[d][2 * pair + 1]], axis=1)
                s_ref[d, pair] = jnp.concatenate([top, bot], axis=0)
            else:
                s_ref[d, pair] = jnp.zeros((pair_k, pair_v), F32)

    def scan_tile(i):
        upper_lane = lax.broadcasted_iota(I32, (1, LANES), 1) >= GLA_DK
        qi = lax.broadcasted_iota(I32, (LANES, 1), 0)
        kj = lax.broadcasted_iota(I32, (1, 2 * LANES), 1) & (LANES - 1)
        same_chunk = (qi & c) == (kj & c)
        causal = (same_chunk & (kj <= qi), same_chunk & (kj >= qi))
        for pair in range(n_pairs):
            kl = slice(pair * pair_k, (pair + 1) * pair_k)
            vl = slice(pair * pair_v, (pair + 1) * pair_v)
            for blk in range(ROW_TILE // LANES):
                rows = _tile_rows(i, blk * LANES, LANES)
                att = None
                for d in range(2):
                    kd = kd_ref[d, rows, kl]
                    zk = jnp.zeros_like(kd)
                    keys = jnp.concatenate([jnp.where(upper_lane, zk, kd), jnp.where(upper_lane, kd, zk)], axis=0)
                    a = jnp.where(causal[d], _dot_nt(qd_ref[d, rows, kl], keys), 0.0)
                    att = a if att is None else att + a
                v = v_ref[rows, vl]
                zv = jnp.zeros((LANES, GLA_DV), BF16)
                v_bd = jnp.concatenate([jnp.concatenate([v[:, :GLA_DV], zv], axis=1),
                                        jnp.concatenate([zv, v[:, GLA_DV:]], axis=1)], axis=0)
                o_ref[rows, vl] = _dot(att.astype(BF16), v_bd)
        yield
        key_row = lax.broadcasted_iota(I32, (pair_k, 1), 0)
        val_col = lax.broadcasted_iota(I32, (1, pair_v), 1)
        blockdiag = (key_row >= GLA_DK) == (val_col >= GLA_DV)
        for d in range(2):
            tile = i if d == 0 else n_tiles - 1 - i
            chunks = range(tile_chunks)
            for pair in range(n_pairs):
                kr = slice(pair * pair_k, (pair + 1) * pair_k)
                vl = slice(pair * pair_v, (pair + 1) * pair_v)
                s = s_ref[d, pair]
                for c4 in (chunks if d == 0 else reversed(chunks)):
                    blk, half = divmod(c4, 2)
                    kst = kst_ref[d, tile, kr, blk * LANES:(blk + 1) * LANES]
                    kst = jnp.where(upper_lane if half else ~upper_lane, kst, jnp.zeros_like(kst))
                    kv = jnp.where(blockdiag, _dot(kst, v_ref[_tile_rows(tile, blk * LANES, LANES), vl]), 0.0)
                    sst_ref[pair, tile * tile_chunks + c4, d * pair_k:(d + 1) * pair_k, :] = s.astype(BF16)
                    s = dect_ref[d, tile, kr, c4:c4 + 1] * s + kv
                s_ref[d, pair] = s
        yield

    yield from _for_row_tiles(seq_len, scan_tile, independent=False)

    if has_state_out:
        for d in range(2):
            for pair in range(n_pairs):
                s = s_ref[d, pair]
                sout_refs[d][2 * pair] = s[0:GLA_DK, 0:GLA_DV]
                sout_refs[d][2 * pair + 1] = s[GLA_DK:, GLA_DV:]

    def stage3(i):
        for pair in range(n_pairs):
            kl = slice(pair * pair_k, (pair + 1) * pair_k)
            vl = slice(pair * pair_v, (pair + 1) * pair_v)
            for c4 in range(tile_chunks):
                crow = _tile_rows(i, c4 * c, c)
                q2 = jnp.concatenate([qd_ref[0, crow, kl], qd_ref[1, crow, kl]], axis=1)
                o_ref[crow, vl] = o_ref[crow, vl] + _dot(q2, sst_ref[pair, i * tile_chunks + c4])
        yield
        rows = _tile_rows(i)
        heads = []
        for h in range(GLA_HEADS):
            hl = slice(h * GLA_DV, (h + 1) * GLA_DV)
            oh = o_ref[rows, hl]
            r = lax.rsqrt(jnp.mean(oh * oh, axis=-1, keepdims=True) + EPS)
            heads.append(oh * r * ggla_ref[:, hl])
        y_b = jnp.concatenate(heads, axis=1) * _silu(og_ref[rows, :])
        y = jnp.concatenate([ya_ref[rows, :], y_b.astype(BF16)], axis=1)
        x1_ref[rows, :] = x_ref[rows, :] + m[2] * _dot(y, wout_ref[...])
        yield

    yield from _for_row_tiles(seq_len, stage3, independent=True)


def _mixer_call(x, mod, states, weights, *, period, has_state_out, seqs_per_step):
    n_seq, seq_len, _ = x.shape
    sps = seqs_per_step
    has_state_in = states is not None
    stages_w_in = weights[1].dtype == F32
    kernel = functools.partial(_mixer_kernel, seqs_per_step=sps, seq_len=seq_len, period=period,
                               has_state_in=has_state_in, has_state_out=has_state_out, stages_w_in=stages_w_in)
    state_spec = pl.BlockSpec((sps, None, GLA_HEADS, GLA_DK, GLA_DV), lambda b: (b, 0, 0, 0, 0))
    const2 = lambda b: (0, 0)
    in_specs = [pl.BlockSpec((sps, seq_len, D_MODEL), lambda b: (b, 0, 0)),
                _mod_spec(mod, sps)]
    args = [x, mod]
    if has_state_in:
        in_specs += [state_spec, state_spec]
        args += list(states)
    in_specs += [pl.BlockSpec(w.shape, lambda b, nd=w.ndim: (0,) * nd, pipeline_mode=pl.Buffered(1))
                 for w in weights]
    args += list(weights)
    out_specs = [pl.BlockSpec((sps, seq_len, D_MODEL), lambda b: (b, 0, 0))]
    out_shape = [jax.ShapeDtypeStruct((n_seq, seq_len, D_MODEL), F32)]
    if has_state_out:
        out_specs += [state_spec, state_spec]
        out_shape += [jax.ShapeDtypeStruct((n_seq, 1, GLA_HEADS, GLA_DK, GLA_DV), F32)] * 2
    if stages_w_in:
        out_specs.append(pl.BlockSpec((D_MODEL, P_PAD), const2))
        out_shape.append(jax.ShapeDtypeStruct((D_MODEL, P_PAD), BF16))
    n_tiles = seq_len // ROW_TILE
    n_pairs = GLA_HEADS // 2
    per_seq = lambda shape, dtype: pltpu.VMEM((sps,) + shape, dtype)
    scratch = [per_seq((seq_len, GLA_DV_TOT), F32),
               per_seq((2, seq_len, GLA_DK_TOT), BF16),
               per_seq((2, seq_len, GLA_DK_TOT), BF16),
               per_seq((2, n_tiles, GLA_DK_TOT, ROW_TILE), BF16),
               per_seq((2, n_tiles, GLA_DK_TOT, LANES), F32),
               per_seq((seq_len, GLA_DV_TOT), BF16),
               per_seq((2, n_pairs, 2 * GLA_DK, 2 * GLA_DV), F32),
               per_seq((n_pairs, seq_len // GLA_CHUNK, 4 * GLA_DK, 2 * GLA_DV), BF16),
               per_seq((seq_len, GLA_DV_TOT), F32),
               per_seq((seq_len, D_CONV), BF16)]
    return pl.pallas_call(
        kernel,
        grid=(n_seq // sps,),
        in_specs=in_specs,
        out_specs=out_specs,
        out_shape=out_shape,
        scratch_shapes=scratch,
        compiler_params=_params(1),
        name="mixer",
    )(*args)


SUBLANES = 8


ROW_TOKENS = LANES // N_EXPERTS


def _route_kernel(x1_ref, mod_ref, g2_ref, wr2_ref, *refs, seqs_per_step, **static):
    programs = [_route_sequence(x1_ref.at[j], _mod_rows(mod_ref, j), g2_ref, wr2_ref, *[r.at[j] for r in refs],
                                **static)
                for j in range(seqs_per_step)]
    for _ in _staggered(programs):
        pass


def _route_sequence(x1_ref, m, g2_ref, wr2_ref, xs_ref, rank_ref, gate_ref,
                    pt_ref, h2_ref, spread_ref, bar_ref, cnt_ref, win_ref, *, seq_len, cap):
    n_rows = seq_len // ROW_TOKENS
    for t in range(seq_len // ROW_TILE):
        rows = slice(t * ROW_TILE, (t + 1) * ROW_TILE)
        h2 = _modulated_norm(x1_ref[rows, :], g2_ref[...], m[4], m[3])
        hi = h2.astype(BF16)
        h2_ref[rows, :] = hi
        yield
        by_hi = _dot_nt(wr2_ref[...], hi)
        pt_ref[:, rows] = by_hi[:N_EXPERTS] + by_hi[N_EXPERTS:]
    yield

    logits = pt_ref[...]
    ex = jnp.exp(logits - jnp.max(logits, axis=0, keepdims=True))
    probs = ex / jnp.sum(ex, axis=0, keepdims=True)
    pt_ref[...] = probs
    pad = jnp.zeros((LANES - N_EXPERTS, seq_len), F32)
    p_tok = jnp.transpose(jnp.concatenate([probs, pad], axis=0))

    spread = p_tok
    sh = N_EXPERTS
    while sh < LANES:
        spread = spread + pltpu.roll(spread, sh, 1)
        sh *= 2
    spread_ref[...] = spread
    sub_j = lax.broadcasted_iota(I32, (SUBLANES, 1), 0)
    lane_g = lax.broadcasted_iota(I32, (1, LANES), 1) >> (N_EXPERTS.bit_length() - 1)
    lane_j = (ROW_TOKENS - lane_g) & (ROW_TOKENS - 1)
    own_group = jnp.where(lane_j == sub_j, spread.reshape(n_rows, SUBLANES, LANES), 0.0)
    bar = jnp.sum(own_group, axis=1, keepdims=True)
    bar_ref[...] = jnp.broadcast_to(bar, (n_rows, SUBLANES, LANES))
    cnt_ref[...] = jnp.zeros((n_rows, SUBLANES, LANES), F32)

    rows_per_block = LANES // ROW_TOKENS

    def count_block(g_s, g_t, relation):
        s_blk = spread_ref[g_s * LANES:(g_s + 1) * LANES, :]
        wins = [jnp.zeros((SUBLANES, LANES), F32)] * rows_per_block
        for tl in range(rows_per_block):
            r = g_t * rows_per_block + tl
            bar_r = bar_ref[r]
            acc = cnt_ref[r]
            for sl in range(rows_per_block):
                s_vreg = s_blk[sl * SUBLANES:(sl + 1) * SUBLANES]
                order = relation if relation != "same" else ("before" if sl < tl else "after" if sl > tl else "tie")
                if order == "before":
                    won = jnp.where(s_vreg >= bar_r, 1.0, 0.0)
                    acc = acc + won
                    if relation == "before":
                        wins[sl] = wins[sl] + won
                elif order == "after":
                    acc = acc + jnp.where(s_vreg > bar_r, 1.0, 0.0)
                else:
                    acc = acc + jnp.where(s_vreg > bar_r, 1.0, 0.0) \
                        + jnp.where((s_vreg == bar_r) & (sub_j < lane_j), 1.0, 0.0)
            cnt_ref[r] = acc
        if relation == "before":
            w_rows = slice(g_s * rows_per_block, (g_s + 1) * rows_per_block)
            win_ref[w_rows] = win_ref[w_rows] + jnp.stack(wins)

    n_grp = seq_len // LANES
    win_ref[...] = jnp.zeros((n_rows, SUBLANES, LANES), F32)
    yield
    for g_t in range(n_grp):
        for g_s in range(g_t + 1):
            count_block(g_s, g_t, "before" if g_s < g_t else "same")
            yield

    counts = jnp.sum(cnt_ref[...], axis=1, keepdims=True)
    counts = jnp.broadcast_to(counts, (n_rows, SUBLANES, LANES)).reshape(seq_len, LANES)
    rank_tok = pltpu.roll(counts, 0, 1, stride=N_EXPERTS, stride_axis=0)
    wins = win_ref[...].reshape(seq_len, LANES)
    sh = N_EXPERTS
    while sh < LANES:
        wins = wins + pltpu.roll(wins, sh, 1)
        sh *= 2
    tok = lax.broadcasted_iota(I32, (seq_len, 1), 0)
    later = ((n_grp - 1 - (tok >> (LANES.bit_length() - 1))) * LANES).astype(F32)
    rank_tok = rank_tok + (later - wins)
    rank_ref[...] = jnp.transpose(rank_tok)[:N_EXPERTS].astype(I32)
    yield

    group = DISPATCH_ROWS // cap
    slot = lax.broadcasted_iota(I32, (cap, 1), 0)
    half = D_MODEL // 2
    for gi in range(N_EXPERTS // group):
        picks = []
        for e in range(gi * group, (gi + 1) * group):
            oh = rank_ref[e:e + 1, :] == slot
            gate = jnp.sum(jnp.where(oh, pt_ref[e:e + 1, :], 0.0), axis=1, keepdims=True)
            gate_ref[e * cap:(e + 1) * cap, :] = jnp.broadcast_to(gate, (cap, LANES))
            picks.append(oh)
        ohb = jnp.where(jnp.concatenate(picks, axis=0), 1.0, 0.0).astype(BF16)
        yield
        out_rows = slice(gi * DISPATCH_ROWS, (gi + 1) * DISPATCH_ROWS)
        xs_ref[out_rows, :half] = _dot(ohb, h2_ref[:, :half]).astype(BF16)
        xs_ref[out_rows, half:] = _dot(ohb, h2_ref[:, half:]).astype(BF16)
        yield


def _route_call(x1, mod, g2, wr_both, *, seqs_per_step):
    n_seq, seq_len, _ = x1.shape
    sps = seqs_per_step
    cap = EC_CAPACITY_FACTOR * seq_len // N_EXPERTS
    kernel = functools.partial(_route_kernel, seqs_per_step=sps, seq_len=seq_len, cap=cap)
    const2 = lambda b: (0, 0)
    per_seq = lambda shape, dtype: pltpu.VMEM((sps,) + shape, dtype)
    rank_rows = (seq_len // ROW_TOKENS, SUBLANES, LANES)
    return pl.pallas_call(
        kernel,
        grid=(n_seq // sps,),
        in_specs=[pl.BlockSpec((sps, seq_len, D_MODEL), lambda b: (b, 0, 0)),
                  _mod_spec(mod, sps),
                  pl.BlockSpec((1, D_MODEL), const2),
                  pl.BlockSpec((2 * N_EXPERTS, D_MODEL), const2)],
        out_specs=[pl.BlockSpec((sps, N_EXPERTS * cap, D_MODEL), lambda b: (b, 0, 0)),
                   pl.BlockSpec((sps, N_EXPERTS, seq_len), lambda b: (b, 0, 0)),
                   pl.BlockSpec((sps, N_EXPERTS * cap, LANES), lambda b: (b, 0, 0))],
        out_shape=[jax.ShapeDtypeStruct((n_seq, N_EXPERTS * cap, D_MODEL), BF16),
                   jax.ShapeDtypeStruct((n_seq, N_EXPERTS, seq_len), I32),
                   jax.ShapeDtypeStruct((n_seq, N_EXPERTS * cap, LANES), F32)],
        scratch_shapes=[per_seq((N_EXPERTS, seq_len), F32),
                        per_seq((seq_len, D_MODEL), BF16),
                        per_seq((seq_len, LANES), F32),
                        per_seq(rank_rows, F32),
                        per_seq(rank_rows, F32),
                        per_seq(rank_rows, F32)],
        compiler_params=_params(1),
        name="route",
    )(x1, mod, g2, wr_both)


def _experts_kernel(xc_ref, xl_ref, gc_ref, gl_ref, wg_ref, wu_ref, wd_ref, yc_ref, yl_ref,
                    wgb_ref, wub_ref, wdb_ref):
    wgb_ref[...] = wg_ref[...].astype(BF16)
    wub_ref[...] = wu_ref[...].astype(BF16)
    wdb_ref[...] = wd_ref[...].astype(BF16)

    def run(x_ref, g_ref, y_ref):
        n_seq, cap, _ = x_ref.shape
        seqs = DISPATCH_ROWS // cap
        for s0 in range(0, n_seq, seqs):
            x = x_ref[s0:s0 + seqs].reshape(DISPATCH_ROWS, D_MODEL)
            a = (_silu(_dot(x, wgb_ref[...])) * _dot(x, wub_ref[...])).astype(BF16)
            gate = g_ref[s0:s0 + seqs].reshape(DISPATCH_ROWS, LANES)
            y = _dot(a, wdb_ref[...]) * jnp.concatenate([gate] * (D_MODEL // LANES), axis=1)
            y_ref[s0:s0 + seqs] = y.astype(BF16).reshape(seqs, cap, D_MODEL)

    run(xc_ref, gc_ref, yc_ref)
    run(xl_ref, gl_ref, yl_ref)


def _experts_call(xs_ctx, xs_lat, gates_ctx, gates_lat, w_gate, w_up, w_down):
    def slot_spec(a):
        n_seq, _, cap, width = a.shape
        return pl.BlockSpec((n_seq, None, cap, width), lambda e: (0, e, 0, 0))

    w_spec = pl.BlockSpec((None, D_MODEL, D_EXPERT), lambda e: (e, 0, 0))
    return pl.pallas_call(
        _experts_kernel,
        grid=(N_EXPERTS,),
        in_specs=[slot_spec(xs_ctx), slot_spec(xs_lat), slot_spec(gates_ctx), slot_spec(gates_lat),
                  w_spec, w_spec, pl.BlockSpec((None, D_EXPERT, D_MODEL), lambda e: (e, 0, 0))],
        out_specs=[slot_spec(xs_ctx), slot_spec(xs_lat)],
        out_shape=[jax.ShapeDtypeStruct(xs_ctx.shape, BF16), jax.ShapeDtypeStruct(xs_lat.shape, BF16)],
        scratch_shapes=[pltpu.VMEM((D_MODEL, D_EXPERT), BF16),
                        pltpu.VMEM((D_MODEL, D_EXPERT), BF16),
                        pltpu.VMEM((D_EXPERT, D_MODEL), BF16)],
        compiler_params=_params(1),
        name="experts",
    )(xs_ctx, xs_lat, gates_ctx, gates_lat, w_gate, w_up, w_down)


def _combine_kernel(x1_ref, y_ref, rank_ref, mod_ref, gf_ref, o_ref, oh_ref, *, seqs_per_step, seq_len, cap):
    slot = lax.broadcasted_iota(I32, (cap, 1), 0)
    for j in range(seqs_per_step):
        m = _mod_rows(mod_ref, j)
        for e in range(N_EXPERTS):
            oh_ref[j, e * cap:(e + 1) * cap, :] = jnp.where(rank_ref[j, e:e + 1, :] == slot, 1.0, 0.0).astype(BF16)
        for t in range(seq_len // ROW_TILE):
            rows = slice(t * ROW_TILE, (t + 1) * ROW_TILE)
            moe = _dot_tn(oh_ref[j, :, rows], y_ref[j])
            x2 = x1_ref[j, rows, :] + m[5] * moe
            r = lax.rsqrt(jnp.mean(x2 * x2, axis=-1, keepdims=True) + EPS)
            o_ref[j, rows, :] = (x2 * r) * gf_ref[...]


def _combine_call(x1, y, rank, mod, g_final, *, seqs_per_step):
    n_seq, seq_len, _ = x1.shape
    n_slots = y.shape[1]
    sps = seqs_per_step
    kernel = functools.partial(_combine_kernel, seqs_per_step=sps, seq_len=seq_len, cap=n_slots // N_EXPERTS)
    seq_spec = pl.BlockSpec((sps, seq_len, D_MODEL), lambda b: (b, 0, 0))
    return pl.pallas_call(
        kernel,
        grid=(n_seq // sps,),
        in_specs=[seq_spec,
                  pl.BlockSpec((sps, n_slots, D_MODEL), lambda b: (b, 0, 0)),
                  pl.BlockSpec((sps, N_EXPERTS, seq_len), lambda b: (b, 0, 0)),
                  _mod_spec(mod, sps),
                  pl.BlockSpec((1, D_MODEL), lambda b: (0, 0))],
        out_specs=seq_spec,
        out_shape=jax.ShapeDtypeStruct((n_seq, seq_len, D_MODEL), F32),
        scratch_shapes=[pltpu.VMEM((sps, n_slots, seq_len), BF16)],
        compiler_params=_params(1),
        name="combine",
    )(x1, y, rank, mod, g_final)


def kernel(x_prompt, x_sample, state_gla_fwd, state_gla_bwd, c, c_ctx, w_mod, b_mod, g_norm1, g_norm2,
           w_in, w_conv, b_conv, w_a_up_f, b_a_f, w_a_up_b, b_a_b, g_gla_norm, w_out, w_router,
           w_gate, w_up, w_down, g_final):
    assert w_mod.shape[0] == 1, "single trunk layer"
    n_ctx, ctx_len, _ = x_prompt.shape
    n_lat, lat_len, _ = x_sample.shape
    ctx_cap = EC_CAPACITY_FACTOR * ctx_len // N_EXPERTS
    lat_cap = EC_CAPACITY_FACTOR * lat_len // N_EXPERTS

    c_rows = jnp.concatenate([c_ctx[None, :], c, jnp.zeros((8 - 1 - n_lat, D_MODEL), F32)], axis=0)
    mod_ctx, mod_lat = _mod_call(c_rows, n_lat, w_mod[0], b_mod)

    mixer_weights = [g_norm1, jnp.transpose(w_in[0]), w_conv, b_conv, w_a_up_f, b_a_f, w_a_up_b, b_a_b,
                     g_gla_norm[0].reshape(1, GLA_DV_TOT), w_out[0].astype(BF16)]
    assert len(mixer_weights) == N_MIXER_WEIGHTS
    wr_t = jnp.transpose(w_router[0])
    wr_hi = wr_t.astype(BF16)
    wr_lo = (wr_t - wr_hi.astype(F32)).astype(BF16)

    x1_ctx, new_f, new_b, w_in_bf = _mixer_call(x_prompt, mod_ctx, None, mixer_weights,
                                                period=ctx_len, has_state_out=True, seqs_per_step=4)
    mixer_weights[1] = w_in_bf
    (x1_lat,) = _mixer_call(x_sample, mod_lat, (state_gla_fwd, state_gla_bwd), mixer_weights,
                            period=GRID_W, has_state_out=False, seqs_per_step=1)

    wr_both = jnp.concatenate([wr_hi, wr_lo], axis=0)
    xs_ctx, rank_ctx, gates_ctx = _route_call(x1_ctx, mod_ctx, g_norm2, wr_both, seqs_per_step=4)
    xs_lat, rank_lat, gates_lat = _route_call(x1_lat, mod_lat, g_norm2, wr_both, seqs_per_step=1)

    per_expert = lambda a, n, cap: a.reshape(n, N_EXPERTS, cap, a.shape[-1])
    y_ctx, y_lat = _experts_call(per_expert(xs_ctx, n_ctx, ctx_cap), per_expert(xs_lat, n_lat, lat_cap),
                                 per_expert(gates_ctx, n_ctx, ctx_cap), per_expert(gates_lat, n_lat, lat_cap),
                                 w_gate[0], w_up[0], w_down[0])

    g_fin = g_final[None, :]
    y_prompt = _combine_call(x1_ctx, y_ctx.reshape(xs_ctx.shape), rank_ctx, mod_ctx, g_fin, seqs_per_step=4)
    y_sample = _combine_call(x1_lat, y_lat.reshape(xs_lat.shape), rank_lat, mod_lat, g_fin, seqs_per_step=1)
    return y_prompt, y_sample, new_f, new_b
```

```python
import functools

import jax
import jax.numpy as jnp
from jax import lax
from jax.experimental import pallas as pl
from jax.experimental.pallas import tpu as pltpu

F32 = jnp.float32
BF16 = jnp.bfloat16
I32 = jnp.int32

D_MODEL = 1024
D_CONV = D_MODEL // 2
GRID_W = 64
GLA_HEADS = 4
GLA_DK = 64
GLA_DV = 128
GLA_DK_TOT = GLA_HEADS * GLA_DK
GLA_DV_TOT = GLA_HEADS * GLA_DV
GLA_LOW_RANK = 16
GLA_TAU = 16.0
GLA_CHUNK = 64
N_EXPERTS = 16
EC_CAPACITY_FACTOR = 2
D_EXPERT = 1024
N_MOD = 6
EPS = 1e-6
LOG2_E = 1.4426950408889634

OFF_XB = 0
OFF_XC = D_CONV
OFF_XV = 2 * D_CONV
OFF_Q = 3 * D_CONV
OFF_K = OFF_Q + GLA_DK_TOT
OFF_V = OFF_K + GLA_DK_TOT
OFF_OG = OFF_V + GLA_DV_TOT
OFF_ALOW = OFF_OG + GLA_DV_TOT
P_TOT = OFF_ALOW + 2 * GLA_LOW_RANK

LANES = 128
P_PAD = -(-P_TOT // LANES) * LANES
ROW_TILE = 256
N_MIXER_WEIGHTS = 10
DISPATCH_ROWS = 512
VMEM_LIMIT = 56 * 1024 * 1024


def _dot(a, b):
    return jnp.dot(a, b, preferred_element_type=F32)


def _dot_nt(a, b):
    return lax.dot_general(a, b, (((1,), (1,)), ((), ())), preferred_element_type=F32)


def _dot_tn(a, b):
    return lax.dot_general(a, b, (((0,), (0,)), ((), ())), preferred_element_type=F32)


def _split(a):
    hi = a.astype(BF16)
    lo = (a - hi.astype(F32)).astype(BF16)
    return hi, lo


def _silu(x):
    return x * jax.nn.sigmoid(x)


def _modulated_norm(x, g, scale, shift):
    r = lax.rsqrt(jnp.mean(x * x, axis=-1, keepdims=True) + EPS)
    return (x * r) * (g * (1.0 + scale)) + shift


def _params(n_axes):
    return pltpu.CompilerParams(dimension_semantics=("arbitrary",) * n_axes,
                                vmem_limit_bytes=VMEM_LIMIT)


def _mod_kernel(c_ref, w_ref, b_ref, ctx_ref, lat_ref, acc_ref):
    rows = c_ref.shape[0]
    s = _silu(c_ref[...])
    s_hi, s_lo = _split(jnp.concatenate([s, s], axis=0))
    upper = lax.broadcasted_iota(I32, (2 * rows, 1), 0) < rows
    w_hi, w_lo = _split(w_ref[...])
    by_hi = _dot(jnp.where(upper, s_hi, s_lo), w_hi)
    part = by_hi[:rows] + by_hi[rows:] + _dot(s_hi[:rows], w_lo)

    @pl.when(pl.program_id(0) == 0)
    def _():
        acc_ref[...] = part + b_ref[...]

    @pl.when(pl.program_id(0) != 0)
    def _():
        acc_ref[...] = acc_ref[...] + part

    @pl.when(pl.program_id(0) == pl.num_programs(0) - 1)
    def _():
        ctx_ref[0] = acc_ref[0:1, :]
        for i in range(lat_ref.shape[0]):
            lat_ref[i] = acc_ref[1 + i:2 + i, :]


def _mod_call(c_rows, n_lat, w_mod, b_mod):
    rows, d = c_rows.shape
    n = w_mod.shape[1]
    tk = D_MODEL // 4
    return pl.pallas_call(
        _mod_kernel,
        grid=(d // tk,),
        in_specs=[pl.BlockSpec((rows, tk), lambda k: (0, k)),
                  pl.BlockSpec((tk, n), lambda k: (k, 0)),
                  pl.BlockSpec((1, n), lambda k: (0, 0))],
        out_specs=[pl.BlockSpec((1, 1, n), lambda k: (0, 0, 0)),
                   pl.BlockSpec((n_lat, 1, n), lambda k: (0, 0, 0))],
        out_shape=[jax.ShapeDtypeStruct((1, 1, n), F32), jax.ShapeDtypeStruct((n_lat, 1, n), F32)],
        scratch_shapes=[pltpu.VMEM((rows, n), F32)],
        compiler_params=_params(1),
        name="mod",
    )(c_rows, w_mod, b_mod)


def _mod_rows(mod_ref, j):
    row = mod_ref[j % mod_ref.shape[0]]
    return [row[:, i * D_MODEL:(i + 1) * D_MODEL] for i in range(N_MOD)]


def _mod_spec(mod, seqs_per_step):
    if mod.shape[0] == 1:
        return pl.BlockSpec(mod.shape, lambda b: (0, 0, 0))
    return pl.BlockSpec((seqs_per_step,) + mod.shape[1:], lambda b: (b, 0, 0))


def _staggered(programs):
    programs = list(programs)
    started = 0
    while programs:
        started = min(started + 1, len(programs))
        running = [p for p in programs[:started] if next(p, "done") != "done"]
        programs = running + programs[started:]
        started = len(running)
        yield


def _for_row_tiles(seq_len, phases, independent):
    n = seq_len // ROW_TILE
    if independent:
        yield from _staggered(phases(i) for i in range(n))
    else:
        for i in range(n):
            yield from phases(i)


def _tile_rows(tile, offset=0, size=ROW_TILE):
    if isinstance(tile, int):
        return pl.ds(tile * ROW_TILE + offset, size)
    return pl.ds(pl.multiple_of(tile * ROW_TILE + offset, size), size)


def _mixer_kernel(*refs, seqs_per_step, has_state_in, has_state_out, stages_w_in, **static):
    refs = list(refs)
    n_in = 2 + (2 if has_state_in else 0)
    per_seq_in, refs = [refs[0]] + refs[2:n_in], [refs[1]] + refs[n_in:]
    mod_ref, weights, refs = refs[0], refs[1:1 + N_MIXER_WEIGHTS], refs[1 + N_MIXER_WEIGHTS:]
    n_out = 1 + (2 if has_state_out else 0)
    per_seq_out, refs = refs[:n_out], refs[n_out:]
    if stages_w_in:
        win_f32_ref, win_bf_ref, scratch = weights[1], refs[0], refs[1:]
        weights = weights[:1] + [win_bf_ref] + weights[2:]

        @pl.when(pl.program_id(0) == 0)
        def _():
            for c0 in range(0, P_PAD, LANES):
                n = min(LANES, P_TOT - c0)
                cols = win_f32_ref[c0:c0 + n, :]
                if n < LANES:
                    cols = jnp.concatenate([cols, jnp.zeros((LANES - n, D_MODEL), F32)], axis=0)
                win_bf_ref[:, c0:c0 + LANES] = jnp.transpose(cols).astype(BF16)
    else:
        scratch = refs
    programs = []
    for j in range(seqs_per_step):
        ins = [r.at[j] for r in per_seq_in]
        outs = [r.at[j] for r in per_seq_out]
        programs.append(_mixer_sequence(ins[0], _mod_rows(mod_ref, j), ins[1:], weights, outs[0], outs[1:],
                                        [r.at[j] for r in scratch], **static))
    for _ in _staggered(programs):
        pass


def _mixer_sequence(x_ref, m, s0_refs, weights, x1_ref, sout_refs, scratch, *, seq_len, period):
    has_state_in = bool(s0_refs)
    has_state_out = bool(sout_refs)
    g1_ref, win_ref, wconv_ref, bconv_ref, wupf_ref, bupf_ref, wupb_ref, bupb_ref, ggla_ref, wout_ref = weights
    og_ref, qd_ref, kd_ref, kst_ref, dect_ref, v_ref, s_ref, sst_ref, o_ref, ya_ref = scratch

    c = GLA_CHUNK
    tile_chunks = ROW_TILE // c
    n_tiles = seq_len // ROW_TILE
    n_pairs = GLA_HEADS // 2
    pair_k = 2 * GLA_DK
    pair_v = 2 * GLA_DV

    def stage1(ti):
        rows = _tile_rows(ti)
        h = _modulated_norm(x_ref[rows, :], g1_ref[...], m[1], m[0]).astype(BF16)
        row_i = lax.broadcasted_iota(I32, (ROW_TILE, 1), 0)
        yield
        p_gate = _dot(h, win_ref[:, OFF_OG:P_PAD])
        og_ref[rows, :] = p_gate[:, :GLA_DV_TOT]
        yield
        zero_up = jnp.zeros((GLA_LOW_RANK, GLA_DK_TOT), F32)
        w_up = jnp.concatenate([jnp.concatenate([wupf_ref[0], zero_up], axis=1),
                                jnp.concatenate([zero_up, wupb_ref[0]], axis=1),
                                jnp.zeros((P_PAD - P_TOT, 2 * GLA_DK_TOT), F32)], axis=0).astype(BF16)
        b_up = jnp.concatenate([bupf_ref[...], bupb_ref[...]], axis=1)
        z = _dot(p_gate[:, GLA_DV_TOT:].astype(BF16), w_up) + b_up
        la = (jnp.minimum(z, 0.0) - jnp.log(1.0 + jnp.exp(-jnp.abs(z)))) * (LOG2_E / GLA_TAU)
        col_j = lax.broadcasted_iota(I32, (1, ROW_TILE), 1)
        same_chunk = (row_i & -c) == (col_j & -c)
        lower = jnp.where(same_chunk & (col_j <= row_i), 1.0, 0.0).astype(BF16)
        la_parts = jnp.concatenate(_split(la), axis=1)
        n_gate = 2 * GLA_DK_TOT
        yield
        pre = _dot(lower, la_parts)
        pre = pre[:, :n_gate] + pre[:, n_gate:]
        tot = jnp.concatenate([jnp.broadcast_to(pre[(n + 1) * c - 1:(n + 1) * c], (c, n_gate))
                               for n in range(tile_chunks)], axis=0)
        p_qkv = _dot(h, win_ref[:, OFF_Q:OFF_OG])
        yield
        q = p_qkv[:, :GLA_DK_TOT] * (GLA_DK ** -0.5)
        k = p_qkv[:, GLA_DK_TOT:2 * GLA_DK_TOT]
        v_ref[rows, :] = p_qkv[:, 2 * GLA_DK_TOT:].astype(BF16)
        for d in range(2):
            cols = slice(d * GLA_DK_TOT, (d + 1) * GLA_DK_TOT)
            if d == 0:
                bq = pre[:, cols]
                bk = tot[:, cols] - bq
            else:
                bk = pre[:, cols] - la[:, cols]
                bq = tot[:, cols] - bk
            qd_ref[d, rows, :] = (q * jnp.exp2(bq)).astype(BF16)
            kd_ref[d, rows, :] = (k * jnp.exp2(-bq)).astype(BF16)
            kst_ref[d, ti] = jnp.transpose(k * jnp.exp2(bk)).astype(BF16)
            totals = [tot[n * c:n * c + 1, cols] for n in range(tile_chunks)]
            totals.append(jnp.zeros((LANES - tile_chunks, GLA_DK_TOT), F32))
            dect_ref[d, ti] = jnp.transpose(jnp.exp2(jnp.concatenate(totals, axis=0)))
        yield
        p_conv = _dot(h, win_ref[:, :OFF_Q])
        yield
        pos = row_i & (period - 1)
        u = p_conv[:, OFF_XC:OFF_XC + D_CONV] * p_conv[:, OFF_XV:OFF_XV + D_CONV]
        u_prev = jnp.where(pos == 0, 0.0, pltpu.roll(u, 1, 0))
        u_next = jnp.where(pos == period - 1, 0.0, pltpu.roll(u, ROW_TILE - 1, 0))
        conv = u_prev * wconv_ref[0, 0:1, :] + u * wconv_ref[0, 1:2, :] + u_next * wconv_ref[0, 2:3, :] + bconv_ref[...]
        ya_ref[rows, :] = (p_conv[:, OFF_XB:OFF_XB + D_CONV] * conv).astype(BF16)
        yield

    yield from _for_row_tiles(seq_len, stage1, independent=True)

    for d in range(2):
        for pair in range(n_pairs):
            if has_state_in:
                zero = jnp.zeros((GLA_DK, GLA_DV), F32)
                top = jnp.concatenate([s0_refs[d][2 * pair], zero], axis=1)
                bot = jnp.concatenate([zero, s0_refs[d][2 * pair + 1]], axis=1)
                s_ref[d, pair] = jnp.concatenate([top, bot], axis=0)
            else:
                s_ref[d, pair] = jnp.zeros((pair_k, pair_v), F32)

    def scan_tile(i):
        upper_lane = lax.broadcasted_iota(I32, (1, LANES), 1) >= GLA_DK
        qi = lax.broadcasted_iota(I32, (LANES, 1), 0)
        kj = lax.broadcasted_iota(I32, (1, 2 * LANES), 1) & (LANES - 1)
        same_chunk = (qi & c) == (kj & c)
        causal = (same_chunk & (kj <= qi), same_chunk & (kj >= qi))
        for pair in range(n_pairs):
            kl = slice(pair * pair_k, (pair + 1) * pair_k)
            vl = slice(pair * pair_v, (pair + 1) * pair_v)
            for blk in range(ROW_TILE // LANES):
                rows = _tile_rows(i, blk * LANES, LANES)
                att = None
                for d in range(2):
                    kd = kd_ref[d, rows, kl]
                    zk = jnp.zeros_like(kd)
                    keys = jnp.concatenate([jnp.where(upper_lane, zk, kd), jnp.where(upper_lane, kd, zk)], axis=0)
                    a = jnp.where(causal[d], _dot_nt(qd_ref[d, rows, kl], keys), 0.0)
                    att = a if att is None else att + a
                v = v_ref[rows, vl]
                zv = jnp.zeros((LANES, GLA_DV), BF16)
                v_bd = jnp.concatenate([jnp.concatenate([v[:, :GLA_DV], zv], axis=1),
                                        jnp.concatenate([zv, v[:, GLA_DV:]], axis=1)], axis=0)
                o_ref[rows, vl] = _dot(att.astype(BF16), v_bd)
        yield
        key_row = lax.broadcasted_iota(I32, (pair_k, 1), 0)
        val_col = lax.broadcasted_iota(I32, (1, pair_v), 1)
        blockdiag = (key_row >= GLA_DK) == (val_col >= GLA_DV)
        for d in range(2):
            tile = i if d == 0 else n_tiles - 1 - i
            chunks = range(tile_chunks)
            for pair in range(n_pairs):
                kr = slice(pair * pair_k, (pair + 1) * pair_k)
                vl = slice(pair * pair_v, (pair + 1) * pair_v)
                s = s_ref[d, pair]
                for c4 in (chunks if d == 0 else reversed(chunks)):
                    blk, half = divmod(c4, 2)
                    kst = kst_ref[d, tile, kr, blk * LANES:(blk + 1) * LANES]
                    kst = jnp.where(upper_lane if half else ~upper_lane, kst, jnp.zeros_like(kst))
                    kv = jnp.where(blockdiag, _dot(kst, v_ref[_tile_rows(tile, blk * LANES, LANES), vl]), 0.0)
                    sst_ref[pair, tile * tile_chunks + c4, d * pair_k:(d + 1) * pair_k, :] = s.astype(BF16)
                    s = dect_ref[d, tile, kr, c4:c4 + 1] * s + kv
                s_ref[d, pair] = s
        yield

    yield from _for_row_tiles(seq_len, scan_tile, independent=False)

    if has_state_out:
        for d in range(2):
            for pair in range(n_pairs):
                s = s_ref[d, pair]
                sout_refs[d][2 * pair] = s[0:GLA_DK, 0:GLA_DV]
                sout_refs[d][2 * pair + 1] = s[GLA_DK:, GLA_DV:]

    def stage3(i):
        for pair in range(n_pairs):
            kl = slice(pair * pair_k, (pair + 1) * pair_k)
            vl = slice(pair * pair_v, (pair + 1) * pair_v)
            for c4 in range(tile_chunks):
                crow = _tile_rows(i, c4 * c, c)
                q2 = jnp.concatenate([qd_ref[0, crow, kl], qd_ref[1, crow, kl]], axis=1)
                o_ref[crow, vl] = o_ref[crow, vl] + _dot(q2, sst_ref[pair, i * tile_chunks + c4])
        yield
        rows = _tile_rows(i)
        heads = []
        for h in range(GLA_HEADS):
            hl = slice(h * GLA_DV, (h + 1) * GLA_DV)
            oh = o_ref[rows, hl]
            r = lax.rsqrt(jnp.mean(oh * oh, axis=-1, keepdims=True) + EPS)
            heads.append(oh * r * ggla_ref[:, hl])
        y_b = jnp.concatenate(heads, axis=1) * _silu(og_ref[rows, :])
        y = jnp.concatenate([ya_ref[rows, :], y_b.astype(BF16)], axis=1)
        x1_ref[rows, :] = x_ref[rows, :] + m[2] * _dot(y, wout_ref[...])
        yield

    yield from _for_row_tiles(seq_len, stage3, independent=True)


def _mixer_call(x, mod, states, weights, *, period, has_state_out, seqs_per_step):
    n_seq, seq_len, _ = x.shape
    sps = seqs_per_step
    has_state_in = states is not None
    stages_w_in = weights[1].dtype == F32
    kernel = functools.partial(_mixer_kernel, seqs_per_step=sps, seq_len=seq_len, period=period,
                               has_state_in=has_state_in, has_state_out=has_state_out, stages_w_in=stages_w_in)
    state_spec = pl.BlockSpec((sps, None, GLA_HEADS, GLA_DK, GLA_DV), lambda b: (b, 0, 0, 0, 0))
    const2 = lambda b: (0, 0)
    in_specs = [pl.BlockSpec((sps, seq_len, D_MODEL), lambda b: (b, 0, 0)),
                _mod_spec(mod, sps)]
    args = [x, mod]
    if has_state_in:
        in_specs += [state_spec, state_spec]
        args += list(states)
    in_specs += [pl.BlockSpec(w.shape, lambda b, nd=w.ndim: (0,) * nd, pipeline_mode=pl.Buffered(1))
                 for w in weights]
    args += list(weights)
    out_specs = [pl.BlockSpec((sps, seq_len, D_MODEL), lambda b: (b, 0, 0))]
    out_shape = [jax.ShapeDtypeStruct((n_seq, seq_len, D_MODEL), F32)]
    if has_state_out:
        out_specs += [state_spec, state_spec]
        out_shape += [jax.ShapeDtypeStruct((n_seq, 1, GLA_HEADS, GLA_DK, GLA_DV), F32)] * 2
    if stages_w_in:
        out_specs.append(pl.BlockSpec((D_MODEL, P_PAD), const2))
        out_shape.append(jax.ShapeDtypeStruct((D_MODEL, P_PAD), BF16))
    n_tiles = seq_len // ROW_TILE
    n_pairs = GLA_HEADS // 2
    per_seq = lambda shape, dtype: pltpu.VMEM((sps,) + shape, dtype)
    scratch = [per_seq((seq_len, GLA_DV_TOT), F32),
               per_seq((2, seq_len, GLA_DK_TOT), BF16),
               per_seq((2, seq_len, GLA_DK_TOT), BF16),
               per_seq((2, n_tiles, GLA_DK_TOT, ROW_TILE), BF16),
               per_seq((2, n_tiles, GLA_DK_TOT, LANES), F32),
               per_seq((seq_len, GLA_DV_TOT), BF16),
               per_seq((2, n_pairs, 2 * GLA_DK, 2 * GLA_DV), F32),
               per_seq((n_pairs, seq_len // GLA_CHUNK, 4 * GLA_DK, 2 * GLA_DV), BF16),
               per_seq((seq_len, GLA_DV_TOT), F32),
               per_seq((seq_len, D_CONV), BF16)]
    return pl.pallas_call(
        kernel,
        grid=(n_seq // sps,),
        in_specs=in_specs,
        out_specs=out_specs,
        out_shape=out_shape,
        scratch_shapes=scratch,
        compiler_params=_params(1),
        name="mixer",
    )(*args)


SUBLANES = 8


ROW_TOKENS = LANES // N_EXPERTS


def _route_kernel(x1_ref, mod_ref, g2_ref, wr2_ref, *refs, seqs_per_step, **static):
    programs = [_route_sequence(x1_ref.at[j], _mod_rows(mod_ref, j), g2_ref, wr2_ref, *[r.at[j] for r in refs],
                                **static)
                for j in range(seqs_per_step)]
    for _ in _staggered(programs):
        pass


def _route_sequence(x1_ref, m, g2_ref, wr2_ref, xs_ref, rank_ref, gate_ref,
                    pt_ref, h2_ref, spread_ref, bar_ref, cnt_ref, win_ref, *, seq_len, cap):
    n_rows = seq_len // ROW_TOKENS
    for t in range(seq_len // ROW_TILE):
        rows = slice(t * ROW_TILE, (t + 1) * ROW_TILE)
        h2 = _modulated_norm(x1_ref[rows, :], g2_ref[...], m[4], m[3])
        hi = h2.astype(BF16)
        h2_ref[rows, :] = hi
        yield
        by_hi = _dot_nt(wr2_ref[...], hi)
        pt_ref[:, rows] = by_hi[:N_EXPERTS] + by_hi[N_EXPERTS:]
    yield

    logits = pt_ref[...]
    ex = jnp.exp(logits - jnp.max(logits, axis=0, keepdims=True))
    probs = ex / jnp.sum(ex, axis=0, keepdims=True)
    pt_ref[...] = probs
    pad = jnp.zeros((LANES - N_EXPERTS, seq_len), F32)
    p_tok = jnp.transpose(jnp.concatenate([probs, pad], axis=0))

    spread = p_tok
    sh = N_EXPERTS
    while sh < LANES:
        spread = spread + pltpu.roll(spread, sh, 1)
        sh *= 2
    spread_ref[...] = spread
    sub_j = lax.broadcasted_iota(I32, (SUBLANES, 1), 0)
    lane_g = lax.broadcasted_iota(I32, (1, LANES), 1) >> (N_EXPERTS.bit_length() - 1)
    lane_j = (ROW_TOKENS - lane_g) & (ROW_TOKENS - 1)
    own_group = jnp.where(lane_j == sub_j, spread.reshape(n_rows, SUBLANES, LANES), 0.0)
    bar = jnp.sum(own_group, axis=1, keepdims=True)
    bar_ref[...] = jnp.broadcast_to(bar, (n_rows, SUBLANES, LANES))
    cnt_ref[...] = jnp.zeros((n_rows, SUBLANES, LANES), F32)

    rows_per_block = LANES // ROW_TOKENS

    def count_block(g_s, g_t, relation):
        s_blk = spread_ref[g_s * LANES:(g_s + 1) * LANES, :]
        wins = [jnp.zeros((SUBLANES, LANES), F32)] * rows_per_block
        for tl in range(rows_per_block):
            r = g_t * rows_per_block + tl
            bar_r = bar_ref[r]
            acc = cnt_ref[r]
            for sl in range(rows_per_block):
                s_vreg = s_blk[sl * SUBLANES:(sl + 1) * SUBLANES]
                order = relation if relation != "same" else ("before" if sl < tl else "after" if sl > tl else "tie")
                if order == "before":
                    won = jnp.where(s_vreg >= bar_r, 1.0, 0.0)
                    acc = acc + won
                    if relation == "before":
                        wins[sl] = wins[sl] + won
                elif order == "after":
                    acc = acc + jnp.where(s_vreg > bar_r, 1.0, 0.0)
                else:
                    acc = acc + jnp.where(s_vreg > bar_r, 1.0, 0.0) \
                        + jnp.where((s_vreg == bar_r) & (sub_j < lane_j), 1.0, 0.0)
            cnt_ref[r] = acc
        if relation == "before":
            w_rows = slice(g_s * rows_per_block, (g_s + 1) * rows_per_block)
            win_ref[w_rows] = win_ref[w_rows] + jnp.stack(wins)

    n_grp = seq_len // LANES
    win_ref[...] = jnp.zeros((n_rows, SUBLANES, LANES), F32)
    yield
    for g_t in range(n_grp):
        for g_s in range(g_t + 1):
            count_block(g_s, g_t, "before" if g_s < g_t else "same")
            yield

    counts = jnp.sum(cnt_ref[...], axis=1, keepdims=True)
    counts = jnp.broadcast_to(counts, (n_rows, SUBLANES, LANES)).reshape(seq_len, LANES)
    rank_tok = pltpu.roll(counts, 0, 1, stride=N_EXPERTS, stride_axis=0)
    wins = win_ref[...].reshape(seq_len, LANES)
    sh = N_EXPERTS
    while sh < LANES:
        wins = wins + pltpu.roll(wins, sh, 1)
        sh *= 2
    tok = lax.broadcasted_iota(I32, (seq_len, 1), 0)
    later = ((n_grp - 1 - (tok >> (LANES.bit_length() - 1))) * LANES).astype(F32)
    rank_tok = rank_tok + (later - wins)
    rank_ref[...] = jnp.transpose(rank_tok)[:N_EXPERTS].astype(I32)
    yield

    group = DISPATCH_ROWS // cap
    slot = lax.broadcasted_iota(I32, (cap, 1), 0)
    half = D_MODEL // 2
    for gi in range(N_EXPERTS // group):
        picks = []
        for e in range(gi * group, (gi + 1) * group):
            oh = rank_ref[e:e + 1, :] == slot
            gate = jnp.sum(jnp.where(oh, pt_ref[e:e + 1, :], 0.0), axis=1, keepdims=True)
            gate_ref[e * cap:(e + 1) * cap, :] = jnp.broadcast_to(gate, (cap, LANES))
            picks.append(oh)
        ohb = jnp.where(jnp.concatenate(picks, axis=0), 1.0, 0.0).astype(BF16)
        yield
        out_rows = slice(gi * DISPATCH_ROWS, (gi + 1) * DISPATCH_ROWS)
        xs_ref[out_rows, :half] = _dot(ohb, h2_ref[:, :half]).astype(BF16)
        xs_ref[out_rows, half:] = _dot(ohb, h2_ref[:, half:]).astype(BF16)
        yield


def _route_call(x1, mod, g2, wr_both, *, seqs_per_step):
    n_seq, seq_len, _ = x1.shape
    sps = seqs_per_step
    cap = EC_CAPACITY_FACTOR * seq_len // N_EXPERTS
    kernel = functools.partial(_route_kernel, seqs_per_step=sps, seq_len=seq_len, cap=cap)
    const2 = lambda b: (0, 0)
    per_seq = lambda shape, dtype: pltpu.VMEM((sps,) + shape, dtype)
    rank_rows = (seq_len // ROW_TOKENS, SUBLANES, LANES)
    return pl.pallas_call(
        kernel,
        grid=(n_seq // sps,),
        in_specs=[pl.BlockSpec((sps, seq_len, D_MODEL), lambda b: (b, 0, 0)),
                  _mod_spec(mod, sps),
                  pl.BlockSpec((1, D_MODEL), const2),
                  pl.BlockSpec((2 * N_EXPERTS, D_MODEL), const2)],
        out_specs=[pl.BlockSpec((sps, N_EXPERTS * cap, D_MODEL), lambda b: (b, 0, 0)),
                   pl.BlockSpec((sps, N_EXPERTS, seq_len), lambda b: (b, 0, 0)),
                   pl.BlockSpec((sps, N_EXPERTS * cap, LANES), lambda b: (b, 0, 0))],
        out_shape=[jax.ShapeDtypeStruct((n_seq, N_EXPERTS * cap, D_MODEL), BF16),
                   jax.ShapeDtypeStruct((n_seq, N_EXPERTS, seq_len), I32),
                   jax.ShapeDtypeStruct((n_seq, N_EXPERTS * cap, LANES), F32)],
        scratch_shapes=[per_seq((N_EXPERTS, seq_len), F32),
                        per_seq((seq_len, D_MODEL), BF16),
                        per_seq((seq_len, LANES), F32),
                        per_seq(rank_rows, F32),
                        per_seq(rank_rows, F32),
                        per_seq(rank_rows, F32)],
        compiler_params=_params(1),
        name="route",
    )(x1, mod, g2, wr_both)


def _experts_kernel(xc_ref, xl_ref, gc_ref, gl_ref, wg_ref, wu_ref, wd_ref, yc_ref, yl_ref,
                    wgb_ref, wub_ref, wdb_ref, a_ref):
    wgb_ref[...] = wg_ref[...].astype(BF16)
    wub_ref[...] = wu_ref[...].astype(BF16)
    wdb_ref[...] = wd_ref[...].astype(BF16)
    f_tile = 2 * LANES

    def run(x_ref, g_ref, y_ref):
        n_seq, cap, _ = x_ref.shape
        seqs = DISPATCH_ROWS // cap
        for s0 in range(0, n_seq, seqs):
            x = x_ref[s0:s0 + seqs].reshape(DISPATCH_ROWS, D_MODEL)
            for f0 in range(0, D_EXPERT, f_tile):
                cols = slice(f0, f0 + f_tile)
                a_ref[:, cols] = (_silu(_dot(x, wgb_ref[:, cols])) * _dot(x, wub_ref[:, cols])).astype(BF16)
            gate = g_ref[s0:s0 + seqs].reshape(DISPATCH_ROWS, LANES)
            y = _dot(a_ref[...], wdb_ref[...]) * jnp.concatenate([gate] * (D_MODEL // LANES), axis=1)
            y_ref[s0:s0 + seqs] = y.astype(BF16).reshape(seqs, cap, D_MODEL)

    run(xc_ref, gc_ref, yc_ref)
    run(xl_ref, gl_ref, yl_ref)


def _experts_call(xs_ctx, xs_lat, gates_ctx, gates_lat, w_gate, w_up, w_down):
    def slot_spec(a):
        n_seq, _, cap, width = a.shape
        return pl.BlockSpec((n_seq, None, cap, width), lambda e: (0, e, 0, 0))

    w_spec = pl.BlockSpec((None, D_MODEL, D_EXPERT), lambda e: (e, 0, 0))
    return pl.pallas_call(
        _experts_kernel,
        grid=(N_EXPERTS,),
        in_specs=[slot_spec(xs_ctx), slot_spec(xs_lat), slot_spec(gates_ctx), slot_spec(gates_lat),
                  w_spec, w_spec, pl.BlockSpec((None, D_EXPERT, D_MODEL), lambda e: (e, 0, 0))],
        out_specs=[slot_spec(xs_ctx), slot_spec(xs_lat)],
        out_shape=[jax.ShapeDtypeStruct(xs_ctx.shape, BF16), jax.ShapeDtypeStruct(xs_lat.shape, BF16)],
        scratch_shapes=[pltpu.VMEM((D_MODEL, D_EXPERT), BF16),
                        pltpu.VMEM((D_MODEL, D_EXPERT), BF16),
                        pltpu.VMEM((D_EXPERT, D_MODEL), BF16),
                        pltpu.VMEM((DISPATCH_ROWS, D_EXPERT), BF16)],
        compiler_params=_params(1),
        name="experts",
    )(xs_ctx, xs_lat, gates_ctx, gates_lat, w_gate, w_up, w_down)


def _combine_kernel(x1_ref, y_ref, rank_ref, mod_ref, gf_ref, o_ref, oh_ref, *, seqs_per_step, seq_len, cap):
    slot = lax.broadcasted_iota(I32, (cap, 1), 0)
    for j in range(seqs_per_step):
        m = _mod_rows(mod_ref, j)
        for e in range(N_EXPERTS):
            oh_ref[j, e * cap:(e + 1) * cap, :] = jnp.where(rank_ref[j, e:e + 1, :] == slot, 1.0, 0.0).astype(BF16)
        for t in range(seq_len // ROW_TILE):
            rows = slice(t * ROW_TILE, (t + 1) * ROW_TILE)
            moe = _dot_tn(oh_ref[j, :, rows], y_ref[j])
            x2 = x1_ref[j, rows, :] + m[5] * moe
            r = lax.rsqrt(jnp.mean(x2 * x2, axis=-1, keepdims=True) + EPS)
            o_ref[j, rows, :] = (x2 * r) * gf_ref[...]


def _combine_call(x1, y, rank, mod, g_final, *, seqs_per_step):
    n_seq, seq_len, _ = x1.shape
    n_slots = y.shape[1]
    sps = seqs_per_step
    kernel = functools.partial(_combine_kernel, seqs_per_step=sps, seq_len=seq_len, cap=n_slots // N_EXPERTS)
    seq_spec = pl.BlockSpec((sps, seq_len, D_MODEL), lambda b: (b, 0, 0))
    return pl.pallas_call(
        kernel,
        grid=(n_seq // sps,),
        in_specs=[seq_spec,
                  pl.BlockSpec((sps, n_slots, D_MODEL), lambda b: (b, 0, 0)),
                  pl.BlockSpec((sps, N_EXPERTS, seq_len), lambda b: (b, 0, 0)),
                  _mod_spec(mod, sps),
                  pl.BlockSpec((1, D_MODEL), lambda b: (0, 0))],
        out_specs=seq_spec,
        out_shape=jax.ShapeDtypeStruct((n_seq, seq_len, D_MODEL), F32),
        scratch_shapes=[pltpu.VMEM((sps, n_slots, seq_len), BF16)],
        compiler_params=_params(1),
        name="combine",
    )(x1, y, rank, mod, g_final)


def kernel(x_prompt, x_sample, state_gla_fwd, state_gla_bwd, c, c_ctx, w_mod, b_mod, g_norm1, g_norm2,
           w_in, w_conv, b_conv, w_a_up_f, b_a_f, w_a_up_b, b_a_b, g_gla_norm, w_out, w_router,
           w_gate, w_up, w_down, g_final):
    assert w_mod.shape[0] == 1, "single trunk layer"
    n_ctx, ctx_len, _ = x_prompt.shape
    n_lat, lat_len, _ = x_sample.shape
    ctx_cap = EC_CAPACITY_FACTOR * ctx_len // N_EXPERTS
    lat_cap = EC_CAPACITY_FACTOR * lat_len // N_EXPERTS

    c_rows = jnp.concatenate([c_ctx[None, :], c, jnp.zeros((8 - 1 - n_lat, D_MODEL), F32)], axis=0)
    mod_ctx, mod_lat = _mod_call(c_rows, n_lat, w_mod[0], b_mod)

    mixer_weights = [g_norm1, jnp.transpose(w_in[0]), w_conv, b_conv, w_a_up_f, b_a_f, w_a_up_b, b_a_b,
                     g_gla_norm[0].reshape(1, GLA_DV_TOT), w_out[0].astype(BF16)]
    assert len(mixer_weights) == N_MIXER_WEIGHTS
    wr_t = jnp.transpose(w_router[0])
    wr_hi = wr_t.astype(BF16)
    wr_lo = (wr_t - wr_hi.astype(F32)).astype(BF16)

    x1_ctx, new_f, new_b, w_in_bf = _mixer_call(x_prompt, mod_ctx, None, mixer_weights,
                                                period=ctx_len, has_state_out=True, seqs_per_step=4)
    mixer_weights[1] = w_in_bf
    (x1_lat,) = _mixer_call(x_sample, mod_lat, (state_gla_fwd, state_gla_bwd), mixer_weights,
                            period=GRID_W, has_state_out=False, seqs_per_step=1)

    wr_both = jnp.concatenate([wr_hi, wr_lo], axis=0)
    xs_ctx, rank_ctx, gates_ctx = _route_call(x1_ctx, mod_ctx, g_norm2, wr_both, seqs_per_step=4)
    xs_lat, rank_lat, gates_lat = _route_call(x1_lat, mod_lat, g_norm2, wr_both, seqs_per_step=1)

    per_expert = lambda a, n, cap: a.reshape(n, N_EXPERTS, cap, a.shape[-1])
    y_ctx, y_lat = _experts_call(per_expert(xs_ctx, n_ctx, ctx_cap), per_expert(xs_lat, n_lat, lat_cap),
                                 per_expert(gates_ctx, n_ctx, ctx_cap), per_expert(gates_lat, n_lat, lat_cap),
                                 w_gate[0], w_up[0], w_down[0])

    g_fin = g_final[None, :]
    y_prompt = _combine_call(x1_ctx, y_ctx.reshape(xs_ctx.shape), rank_ctx, mod_ctx, g_fin, seqs_per_step=4)
    y_sample = _combine_call(x1_lat, y_lat.reshape(xs_lat.shape), rank_lat, mod_lat, g_fin, seqs_per_step=1)
    return y_prompt, y_sample, new_f, new_b
```

```python
import functools

import jax
import jax.numpy as jnp
from jax import lax
from jax.experimental import pallas as pl
from jax.experimental.pallas import tpu as pltpu

F32 = jnp.float32
BF16 = jnp.bfloat16
I32 = jnp.int32

D_MODEL = 1024
D_CONV = D_MODEL // 2
GRID_W = 64
GLA_HEADS = 4
GLA_DK = 64
GLA_DV = 128
GLA_DK_TOT = GLA_HEADS * GLA_DK
GLA_DV_TOT = GLA_HEADS * GLA_DV
GLA_LOW_RANK = 16
GLA_TAU = 16.0
GLA_CHUNK = 64
N_EXPERTS = 16
EC_CAPACITY_FACTOR = 2
D_EXPERT = 1024
N_MOD = 6
EPS = 1e-6
LOG2_E = 1.4426950408889634

OFF_XB = 0
OFF_XC = D_CONV
OFF_XV = 2 * D_CONV
OFF_Q = 3 * D_CONV
OFF_K = OFF_Q + GLA_DK_TOT
OFF_V = OFF_K + GLA_DK_TOT
OFF_OG = OFF_V + GLA_DV_TOT
OFF_ALOW = OFF_OG + GLA_DV_TOT
P_TOT = OFF_ALOW + 2 * GLA_LOW_RANK

LANES = 128
P_PAD = -(-P_TOT // LANES) * LANES
ROW_TILE = 256
N_MIXER_WEIGHTS = 10
DISPATCH_ROWS = 512
VMEM_LIMIT = 56 * 1024 * 1024


def _dot(a, b):
    return jnp.dot(a, b, preferred_element_type=F32)


def _dot_nt(a, b):
    return lax.dot_general(a, b, (((1,), (1,)), ((), ())), preferred_element_type=F32)


def _dot_tn(a, b):
    return lax.dot_general(a, b, (((0,), (0,)), ((), ())), preferred_element_type=F32)


def _split(a):
    hi = a.astype(BF16)
    lo = (a - hi.astype(F32)).astype(BF16)
    return hi, lo


def _silu(x):
    return x * jax.nn.sigmoid(x)


def _modulated_norm(x, g, scale, shift):
    r = lax.rsqrt(jnp.mean(x * x, axis=-1, keepdims=True) + EPS)
    return (x * r) * (g * (1.0 + scale)) + shift


def _params(n_axes):
    return pltpu.CompilerParams(dimension_semantics=("arbitrary",) * n_axes,
                                vmem_limit_bytes=VMEM_LIMIT)


def _mod_kernel(c_ref, w_ref, b_ref, ctx_ref, lat_ref, acc_ref):
    rows = c_ref.shape[0]
    s = _silu(c_ref[...])
    s_hi, s_lo = _split(jnp.concatenate([s, s], axis=0))
    upper = lax.broadcasted_iota(I32, (2 * rows, 1), 0) < rows
    w_hi, w_lo = _split(w_ref[...])
    by_hi = _dot(jnp.where(upper, s_hi, s_lo), w_hi)
    part = by_hi[:rows] + by_hi[rows:] + _dot(s_hi[:rows], w_lo)

    @pl.when(pl.program_id(0) == 0)
    def _():
        acc_ref[...] = part + b_ref[...]

    @pl.when(pl.program_id(0) != 0)
    def _():
        acc_ref[...] = acc_ref[...] + part

    @pl.when(pl.program_id(0) == pl.num_programs(0) - 1)
    def _():
        ctx_ref[0] = acc_ref[0:1, :]
        for i in range(lat_ref.shape[0]):
            lat_ref[i] = acc_ref[1 + i:2 + i, :]


def _mod_call(c_rows, n_lat, w_mod, b_mod):
    rows, d = c_rows.shape
    n = w_mod.shape[1]
    tk = D_MODEL // 8
    return pl.pallas_call(
        _mod_kernel,
        grid=(d // tk,),
        in_specs=[pl.BlockSpec((rows, tk), lambda k: (0, k)),
                  pl.BlockSpec((tk, n), lambda k: (k, 0)),
                  pl.BlockSpec((1, n), lambda k: (0, 0))],
        out_specs=[pl.BlockSpec((1, 1, n), lambda k: (0, 0, 0)),
                   pl.BlockSpec((n_lat, 1, n), lambda k: (0, 0, 0))],
        out_shape=[jax.ShapeDtypeStruct((1, 1, n), F32), jax.ShapeDtypeStruct((n_lat, 1, n), F32)],
        scratch_shapes=[pltpu.VMEM((rows, n), F32)],
        compiler_params=_params(1),
        name="mod",
    )(c_rows, w_mod, b_mod)


def _mod_rows(mod_ref, j):
    row = mod_ref[j % mod_ref.shape[0]]
    return [row[:, i * D_MODEL:(i + 1) * D_MODEL] for i in range(N_MOD)]


def _mod_spec(mod, seqs_per_step):
    if mod.shape[0] == 1:
        return pl.BlockSpec(mod.shape, lambda b: (0, 0, 0))
    return pl.BlockSpec((seqs_per_step,) + mod.shape[1:], lambda b: (b, 0, 0))


def _staggered(programs):
    programs = list(programs)
    started = 0
    while programs:
        started = min(started + 1, len(programs))
        running = [p for p in programs[:started] if next(p, "done") != "done"]
        programs = running + programs[started:]
        started = len(running)
        yield


def _for_row_tiles(seq_len, phases, independent):
    n = seq_len // ROW_TILE
    if independent:
        yield from _staggered(phases(i) for i in range(n))
    else:
        for i in range(n):
            yield from phases(i)


def _tile_rows(tile, offset=0, size=ROW_TILE):
    if isinstance(tile, int):
        return pl.ds(tile * ROW_TILE + offset, size)
    return pl.ds(pl.multiple_of(tile * ROW_TILE + offset, size), size)


def _mixer_kernel(*refs, seqs_per_step, has_state_in, has_state_out, stages_w_in, **static):
    refs = list(refs)
    n_in = 2 + (2 if has_state_in else 0)
    per_seq_in, refs = [refs[0]] + refs[2:n_in], [refs[1]] + refs[n_in:]
    mod_ref, weights, refs = refs[0], refs[1:1 + N_MIXER_WEIGHTS], refs[1 + N_MIXER_WEIGHTS:]
    n_out = 1 + (2 if has_state_out else 0)
    per_seq_out, refs = refs[:n_out], refs[n_out:]
    if stages_w_in:
        win_f32_ref, win_bf_ref, scratch = weights[1], refs[0], refs[1:]
        weights = weights[:1] + [win_bf_ref] + weights[2:]

        @pl.when(pl.program_id(0) == 0)
        def _():
            for c0 in range(0, P_PAD, LANES):
                n = min(LANES, P_TOT - c0)
                cols = win_f32_ref[c0:c0 + n, :]
                if n < LANES:
                    cols = jnp.concatenate([cols, jnp.zeros((LANES - n, D_MODEL), F32)], axis=0)
                win_bf_ref[:, c0:c0 + LANES] = jnp.transpose(cols).astype(BF16)
    else:
        scratch = refs
    programs = []
    for j in range(seqs_per_step):
        ins = [r.at[j] for r in per_seq_in]
        outs = [r.at[j] for r in per_seq_out]
        programs.append(_mixer_sequence(ins[0], _mod_rows(mod_ref, j), ins[1:], weights, outs[0], outs[1:],
                                        [r.at[j] for r in scratch], **static))
    for _ in _staggered(programs):
        pass


def _mixer_sequence(x_ref, m, s0_refs, weights, x1_ref, sout_refs, scratch, *, seq_len, period):
    has_state_in = bool(s0_refs)
    has_state_out = bool(sout_refs)
    g1_ref, win_ref, wconv_ref, bconv_ref, wupf_ref, bupf_ref, wupb_ref, bupb_ref, ggla_ref, wout_ref = weights
    og_ref, qd_ref, kd_ref, kst_ref, dect_ref, v_ref, s_ref, sst_ref, o_ref, ya_ref = scratch

    c = GLA_CHUNK
    tile_chunks = ROW_TILE // c
    n_tiles = seq_len // ROW_TILE
    n_pairs = GLA_HEADS // 2
    pair_k = 2 * GLA_DK
    pair_v = 2 * GLA_DV

    def stage1(ti):
        rows = _tile_rows(ti)
        h = _modulated_norm(x_ref[rows, :], g1_ref[...], m[1], m[0]).astype(BF16)
        row_i = lax.broadcasted_iota(I32, (ROW_TILE, 1), 0)
        yield
        p_gate = _dot(h, win_ref[:, OFF_OG:P_PAD])
        og_ref[rows, :] = p_gate[:, :GLA_DV_TOT]
        yield
        zero_up = jnp.zeros((GLA_LOW_RANK, GLA_DK_TOT), F32)
        w_up = jnp.concatenate([jnp.concatenate([wupf_ref[0], zero_up], axis=1),
                                jnp.concatenate([zero_up, wupb_ref[0]], axis=1),
                                jnp.zeros((P_PAD - P_TOT, 2 * GLA_DK_TOT), F32)], axis=0).astype(BF16)
        b_up = jnp.concatenate([bupf_ref[...], bupb_ref[...]], axis=1)
        z = _dot(p_gate[:, GLA_DV_TOT:].astype(BF16), w_up) + b_up
        la = (jnp.minimum(z, 0.0) - jnp.log(1.0 + jnp.exp(-jnp.abs(z)))) * (LOG2_E / GLA_TAU)
        col_j = lax.broadcasted_iota(I32, (1, ROW_TILE), 1)
        same_chunk = (row_i & -c) == (col_j & -c)
        lower = jnp.where(same_chunk & (col_j <= row_i), 1.0, 0.0).astype(BF16)
        la_parts = jnp.concatenate(_split(la), axis=1)
        n_gate = 2 * GLA_DK_TOT
        yield
        pre = _dot(lower, la_parts)
        pre = pre[:, :n_gate] + pre[:, n_gate:]
        tot = jnp.concatenate([jnp.broadcast_to(pre[(n + 1) * c - 1:(n + 1) * c], (c, n_gate))
                               for n in range(tile_chunks)], axis=0)
        p_qkv = _dot(h, win_ref[:, OFF_Q:OFF_OG])
        yield
        q = p_qkv[:, :GLA_DK_TOT] * (GLA_DK ** -0.5)
        k = p_qkv[:, GLA_DK_TOT:2 * GLA_DK_TOT]
        v_ref[rows, :] = p_qkv[:, 2 * GLA_DK_TOT:].astype(BF16)
        for d in range(2):
            cols = slice(d * GLA_DK_TOT, (d + 1) * GLA_DK_TOT)
            if d == 0:
                bq = pre[:, cols]
                bk = tot[:, cols] - bq
            else:
                bk = pre[:, cols] - la[:, cols]
                bq = tot[:, cols] - bk
            qd_ref[d, rows, :] = (q * jnp.exp2(bq)).astype(BF16)
            kd_ref[d, rows, :] = (k * jnp.exp2(-bq)).astype(BF16)
            kst_ref[d, ti] = jnp.transpose(k * jnp.exp2(bk)).astype(BF16)
            totals = [tot[n * c:n * c + 1, cols] for n in range(tile_chunks)]
            totals.append(jnp.zeros((LANES - tile_chunks, GLA_DK_TOT), F32))
            dect_ref[d, ti] = jnp.transpose(jnp.exp2(jnp.concatenate(totals, axis=0)))
        yield
        p_conv = _dot(h, win_ref[:, :OFF_Q])
        yield
        pos = row_i & (period - 1)
        u = p_conv[:, OFF_XC:OFF_XC + D_CONV] * p_conv[:, OFF_XV:OFF_XV + D_CONV]
        u_prev = jnp.where(pos == 0, 0.0, pltpu.roll(u, 1, 0))
        u_next = jnp.where(pos == period - 1, 0.0, pltpu.roll(u, ROW_TILE - 1, 0))
        conv = u_prev * wconv_ref[0, 0:1, :] + u * wconv_ref[0, 1:2, :] + u_next * wconv_ref[0, 2:3, :] + bconv_ref[...]
        ya_ref[rows, :] = (p_conv[:, OFF_XB:OFF_XB + D_CONV] * conv).astype(BF16)
        yield

    yield from _for_row_tiles(seq_len, stage1, independent=True)

    for d in range(2):
        for pair in range(n_pairs):
            if has_state_in:
                zero = jnp.zeros((GLA_DK, GLA_DV), F32)
                top = jnp.concatenate([s0_refs[d][2 * pair], zero], axis=1)
                bot = jnp.concatenate([zero, s0_refs[d][2 * pair + 1]], axis=1)
                s_ref[d, pair] = jnp.concatenate([top, bot], axis=0)
            else:
                s_ref[d, pair] = jnp.zeros((pair_k, pair_v), F32)

    def scan_tile(i):
        upper_lane = lax.broadcasted_iota(I32, (1, LANES), 1) >= GLA_DK
        qi = lax.broadcasted_iota(I32, (LANES, 1), 0)
        kj = lax.broadcasted_iota(I32, (1, 2 * LANES), 1) & (LANES - 1)
        same_chunk = (qi & c) == (kj & c)
        causal = (same_chunk & (kj <= qi), same_chunk & (kj >= qi))
        for pair in range(n_pairs):
            kl = slice(pair * pair_k, (pair + 1) * pair_k)
            vl = slice(pair * pair_v, (pair + 1) * pair_v)
            for blk in range(ROW_TILE // LANES):
                rows = _tile_rows(i, blk * LANES, LANES)
                att = None
                for d in range(2):
                    kd = kd_ref[d, rows, kl]
                    zk = jnp.zeros_like(kd)
                    keys = jnp.concatenate([jnp.where(upper_lane, zk, kd), jnp.where(upper_lane, kd, zk)], axis=0)
                    a = jnp.where(causal[d], _dot_nt(qd_ref[d, rows, kl], keys), 0.0)
                    att = a if att is None else att + a
                v = v_ref[rows, vl]
                zv = jnp.zeros((LANES, GLA_DV), BF16)
                v_bd = jnp.concatenate([jnp.concatenate([v[:, :GLA_DV], zv], axis=1),
                                        jnp.concatenate([zv, v[:, GLA_DV:]], axis=1)], axis=0)
                o_ref[rows, vl] = _dot(att.astype(BF16), v_bd)
        yield
        key_row = lax.broadcasted_iota(I32, (pair_k, 1), 0)
        val_col = lax.broadcasted_iota(I32, (1, pair_v), 1)
        blockdiag = (key_row >= GLA_DK) == (val_col >= GLA_DV)
        for d in range(2):
            tile = i if d == 0 else n_tiles - 1 - i
            chunks = range(tile_chunks)
            for pair in range(n_pairs):
                kr = slice(pair * pair_k, (pair + 1) * pair_k)
                vl = slice(pair * pair_v, (pair + 1) * pair_v)
                s = s_ref[d, pair]
                for c4 in (chunks if d == 0 else reversed(chunks)):
                    blk, half = divmod(c4, 2)
                    kst = kst_ref[d, tile, kr, blk * LANES:(blk + 1) * LANES]
                    kst = jnp.where(upper_lane if half else ~upper_lane, kst, jnp.zeros_like(kst))
                    kv = jnp.where(blockdiag, _dot(kst, v_ref[_tile_rows(tile, blk * LANES, LANES), vl]), 0.0)
                    sst_ref[pair, tile * tile_chunks + c4, d * pair_k:(d + 1) * pair_k, :] = s.astype(BF16)
                    s = dect_ref[d, tile, kr, c4:c4 + 1] * s + kv
                s_ref[d, pair] = s
        yield

    yield from _for_row_tiles(seq_len, scan_tile, independent=False)

    if has_state_out:
        for d in range(2):
            for pair in range(n_pairs):
                s = s_ref[d, pair]
                sout_refs[d][2 * pair] = s[0:GLA_DK, 0:GLA_DV]
                sout_refs[d][2 * pair + 1] = s[GLA_DK:, GLA_DV:]

    def stage3(i):
        for pair in range(n_pairs):
            kl = slice(pair * pair_k, (pair + 1) * pair_k)
            vl = slice(pair * pair_v, (pair + 1) * pair_v)
            for c4 in range(tile_chunks):
                crow = _tile_rows(i, c4 * c, c)
                q2 = jnp.concatenate([qd_ref[0, crow, kl], qd_ref[1, crow, kl]], axis=1)
                o_ref[crow, vl] = o_ref[crow, vl] + _dot(q2, sst_ref[pair, i * tile_chunks + c4])
        yield
        rows = _tile_rows(i)
        heads = []
        for h in range(GLA_HEADS):
            hl = slice(h * GLA_DV, (h + 1) * GLA_DV)
            oh = o_ref[rows, hl]
            r = lax.rsqrt(jnp.mean(oh * oh, axis=-1, keepdims=True) + EPS)
            heads.append(oh * r * ggla_ref[:, hl])
        y_b = jnp.concatenate(heads, axis=1) * _silu(og_ref[rows, :])
        y = jnp.concatenate([ya_ref[rows, :], y_b.astype(BF16)], axis=1)
        x1_ref[rows, :] = x_ref[rows, :] + m[2] * _dot(y, wout_ref[...])
        yield

    yield from _for_row_tiles(seq_len, stage3, independent=True)


def _mixer_call(x, mod, states, weights, *, period, has_state_out, seqs_per_step):
    n_seq, seq_len, _ = x.shape
    sps = seqs_per_step
    has_state_in = states is not None
    stages_w_in = weights[1].dtype == F32
    kernel = functools.partial(_mixer_kernel, seqs_per_step=sps, seq_len=seq_len, period=period,
                               has_state_in=has_state_in, has_state_out=has_state_out, stages_w_in=stages_w_in)
    state_spec = pl.BlockSpec((sps, None, GLA_HEADS, GLA_DK, GLA_DV), lambda b: (b, 0, 0, 0, 0))
    const2 = lambda b: (0, 0)
    in_specs = [pl.BlockSpec((sps, seq_len, D_MODEL), lambda b: (b, 0, 0)),
                _mod_spec(mod, sps)]
    args = [x, mod]
    if has_state_in:
        in_specs += [state_spec, state_spec]
        args += list(states)
    in_specs += [pl.BlockSpec(w.shape, lambda b, nd=w.ndim: (0,) * nd, pipeline_mode=pl.Buffered(1))
                 for w in weights]
    args += list(weights)
    out_specs = [pl.BlockSpec((sps, seq_len, D_MODEL), lambda b: (b, 0, 0))]
    out_shape = [jax.ShapeDtypeStruct((n_seq, seq_len, D_MODEL), F32)]
    if has_state_out:
        out_specs += [state_spec, state_spec]
        out_shape += [jax.ShapeDtypeStruct((n_seq, 1, GLA_HEADS, GLA_DK, GLA_DV), F32)] * 2
    if stages_w_in:
        out_specs.append(pl.BlockSpec((D_MODEL, P_PAD), const2))
        out_shape.append(jax.ShapeDtypeStruct((D_MODEL, P_PAD), BF16))
    n_tiles = seq_len // ROW_TILE
    n_pairs = GLA_HEADS // 2
    per_seq = lambda shape, dtype: pltpu.VMEM((sps,) + shape, dtype)
    scratch = [per_seq((seq_len, GLA_DV_TOT), F32),
               per_seq((2, seq_len, GLA_DK_TOT), BF16),
               per_seq((2, seq_len, GLA_DK_TOT), BF16),
               per_seq((2, n_tiles, GLA_DK_TOT, ROW_TILE), BF16),
               per_seq((2, n_tiles, GLA_DK_TOT, LANES), F32),
               per_seq((seq_len, GLA_DV_TOT), BF16),
               per_seq((2, n_pairs, 2 * GLA_DK, 2 * GLA_DV), F32),
               per_seq((n_pairs, seq_len // GLA_CHUNK, 4 * GLA_DK, 2 * GLA_DV), BF16),
               per_seq((seq_len, GLA_DV_TOT), F32),
               per_seq((seq_len, D_CONV), BF16)]
    return pl.pallas_call(
        kernel,
        grid=(n_seq // sps,),
        in_specs=in_specs,
        out_specs=out_specs,
        out_shape=out_shape,
        scratch_shapes=scratch,
        compiler_params=_params(1),
        name="mixer",
    )(*args)


SUBLANES = 8


ROW_TOKENS = LANES // N_EXPERTS


def _route_kernel(x1_ref, mod_ref, g2_ref, wr2_ref, *refs, seqs_per_step, **static):
    programs = [_route_sequence(x1_ref.at[j], _mod_rows(mod_ref, j), g2_ref, wr2_ref, *[r.at[j] for r in refs],
                                **static)
                for j in range(seqs_per_step)]
    for _ in _staggered(programs):
        pass


def _route_sequence(x1_ref, m, g2_ref, wr2_ref, xs_ref, rank_ref, gate_ref,
                    pt_ref, h2_ref, spread_ref, bar_ref, cnt_ref, win_ref, *, seq_len, cap):
    n_rows = seq_len // ROW_TOKENS
    for t in range(seq_len // ROW_TILE):
        rows = slice(t * ROW_TILE, (t + 1) * ROW_TILE)
        h2 = _modulated_norm(x1_ref[rows, :], g2_ref[...], m[4], m[3])
        hi = h2.astype(BF16)
        h2_ref[rows, :] = hi
        yield
        by_hi = _dot_nt(wr2_ref[...], hi)
        pt_ref[:, rows] = by_hi[:N_EXPERTS] + by_hi[N_EXPERTS:]
    yield

    logits = pt_ref[...]
    ex = jnp.exp(logits - jnp.max(logits, axis=0, keepdims=True))
    probs = ex / jnp.sum(ex, axis=0, keepdims=True)
    pt_ref[...] = probs
    pad = jnp.zeros((LANES - N_EXPERTS, seq_len), F32)
    p_tok = jnp.transpose(jnp.concatenate([probs, pad], axis=0))

    spread = p_tok
    sh = N_EXPERTS
    while sh < LANES:
        spread = spread + pltpu.roll(spread, sh, 1)
        sh *= 2
    spread_ref[...] = spread
    sub_j = lax.broadcasted_iota(I32, (SUBLANES, 1), 0)
    lane_g = lax.broadcasted_iota(I32, (1, LANES), 1) >> (N_EXPERTS.bit_length() - 1)
    lane_j = (ROW_TOKENS - lane_g) & (ROW_TOKENS - 1)
    own_group = jnp.where(lane_j == sub_j, spread.reshape(n_rows, SUBLANES, LANES), 0.0)
    bar = jnp.sum(own_group, axis=1, keepdims=True)
    bar_ref[...] = jnp.broadcast_to(bar, (n_rows, SUBLANES, LANES))
    cnt_ref[...] = jnp.zeros((n_rows, SUBLANES, LANES), F32)

    rows_per_block = LANES // ROW_TOKENS

    def count_block(g_s, g_t, relation):
        s_blk = spread_ref[g_s * LANES:(g_s + 1) * LANES, :]
        wins = [jnp.zeros((SUBLANES, LANES), F32)] * rows_per_block
        for tl in range(rows_per_block):
            r = g_t * rows_per_block + tl
            bar_r = bar_ref[r]
            acc = cnt_ref[r]
            for sl in range(rows_per_block):
                s_vreg = s_blk[sl * SUBLANES:(sl + 1) * SUBLANES]
                order = relation if relation != "same" else ("before" if sl < tl else "after" if sl > tl else "tie")
                if order == "before":
                    won = jnp.where(s_vreg >= bar_r, 1.0, 0.0)
                    acc = acc + won
                    if relation == "before":
                        wins[sl] = wins[sl] + won
                elif order == "after":
                    acc = acc + jnp.where(s_vreg > bar_r, 1.0, 0.0)
                else:
                    acc = acc + jnp.where(s_vreg > bar_r, 1.0, 0.0) \
                        + jnp.where((s_vreg == bar_r) & (sub_j < lane_j), 1.0, 0.0)
            cnt_ref[r] = acc
        if relation == "before":
            w_rows = slice(g_s * rows_per_block, (g_s + 1) * rows_per_block)
            win_ref[w_rows] = win_ref[w_rows] + jnp.stack(wins)

    n_grp = seq_len // LANES
    win_ref[...] = jnp.zeros((n_rows, SUBLANES, LANES), F32)
    yield
    for g_t in range(n_grp):
        for g_s in range(g_t + 1):
            count_block(g_s, g_t, "before" if g_s < g_t else "same")
            yield

    counts = jnp.sum(cnt_ref[...], axis=1, keepdims=True)
    counts = jnp.broadcast_to(counts, (n_rows, SUBLANES, LANES)).reshape(seq_len, LANES)
    rank_tok = pltpu.roll(counts, 0, 1, stride=N_EXPERTS, stride_axis=0)
    wins = win_ref[...].reshape(seq_len, LANES)
    sh = N_EXPERTS
    while sh < LANES:
        wins = wins + pltpu.roll(wins, sh, 1)
        sh *= 2
    tok = lax.broadcasted_iota(I32, (seq_len, 1), 0)
    later = ((n_grp - 1 - (tok >> (LANES.bit_length() - 1))) * LANES).astype(F32)
    rank_tok = rank_tok + (later - wins)
    rank_ref[...] = jnp.transpose(rank_tok)[:N_EXPERTS].astype(I32)
    yield

    group = DISPATCH_ROWS // cap
    slot = lax.broadcasted_iota(I32, (cap, 1), 0)
    half = D_MODEL // 2
    for gi in range(N_EXPERTS // group):
        picks = []
        for e in range(gi * group, (gi + 1) * group):
            oh = rank_ref[e:e + 1, :] == slot
            gate = jnp.sum(jnp.where(oh, pt_ref[e:e + 1, :], 0.0), axis=1, keepdims=True)
            gate_ref[e * cap:(e + 1) * cap, :] = jnp.broadcast_to(gate, (cap, LANES))
            picks.append(oh)
        ohb = jnp.where(jnp.concatenate(picks, axis=0), 1.0, 0.0).astype(BF16)
        yield
        out_rows = slice(gi * DISPATCH_ROWS, (gi + 1) * DISPATCH_ROWS)
        xs_ref[out_rows, :half] = _dot(ohb, h2_ref[:, :half]).astype(BF16)
        xs_ref[out_rows, half:] = _dot(ohb, h2_ref[:, half:]).astype(BF16)
        yield


def _route_call(x1, mod, g2, wr_both, *, seqs_per_step):
    n_seq, seq_len, _ = x1.shape
    sps = seqs_per_step
    cap = EC_CAPACITY_FACTOR * seq_len // N_EXPERTS
    kernel = functools.partial(_route_kernel, seqs_per_step=sps, seq_len=seq_len, cap=cap)
    const2 = lambda b: (0, 0)
    per_seq = lambda shape, dtype: pltpu.VMEM((sps,) + shape, dtype)
    rank_rows = (seq_len // ROW_TOKENS, SUBLANES, LANES)
    return pl.pallas_call(
        kernel,
        grid=(n_seq // sps,),
        in_specs=[pl.BlockSpec((sps, seq_len, D_MODEL), lambda b: (b, 0, 0)),
                  _mod_spec(mod, sps),
                  pl.BlockSpec((1, D_MODEL), const2),
                  pl.BlockSpec((2 * N_EXPERTS, D_MODEL), const2)],
        out_specs=[pl.BlockSpec((sps, N_EXPERTS * cap, D_MODEL), lambda b: (b, 0, 0)),
                   pl.BlockSpec((sps, N_EXPERTS, seq_len), lambda b: (b, 0, 0)),
                   pl.BlockSpec((sps, N_EXPERTS * cap, LANES), lambda b: (b, 0, 0))],
        out_shape=[jax.ShapeDtypeStruct((n_seq, N_EXPERTS * cap, D_MODEL), BF16),
                   jax.ShapeDtypeStruct((n_seq, N_EXPERTS, seq_len), I32),
                   jax.ShapeDtypeStruct((n_seq, N_EXPERTS * cap, LANES), F32)],
        scratch_shapes=[per_seq((N_EXPERTS, seq_len), F32),
                        per_seq((seq_len, D_MODEL), BF16),
                        per_seq((seq_len, LANES), F32),
                        per_seq(rank_rows, F32),
                        per_seq(rank_rows, F32),
                        per_seq(rank_rows, F32)],
        compiler_params=_params(1),
        name="route",
    )(x1, mod, g2, wr_both)


def _experts_kernel(xc_ref, xl_ref, gc_ref, gl_ref, wg_ref, wu_ref, wd_ref, yc_ref, yl_ref,
                    wgb_ref, wub_ref, wdb_ref, a_ref):
    wgb_ref[...] = wg_ref[...].astype(BF16)
    wub_ref[...] = wu_ref[...].astype(BF16)
    wdb_ref[...] = wd_ref[...].astype(BF16)
    f_tile = 2 * LANES

    def run(x_ref, g_ref, y_ref):
        n_seq, cap, _ = x_ref.shape
        seqs = DISPATCH_ROWS // cap
        for s0 in range(0, n_seq, seqs):
            x = x_ref[s0:s0 + seqs].reshape(DISPATCH_ROWS, D_MODEL)
            for f0 in range(0, D_EXPERT, f_tile):
                cols = slice(f0, f0 + f_tile)
                a_ref[:, cols] = (_silu(_dot(x, wgb_ref[:, cols])) * _dot(x, wub_ref[:, cols])).astype(BF16)
            gate = g_ref[s0:s0 + seqs].reshape(DISPATCH_ROWS, LANES)
            y = _dot(a_ref[...], wdb_ref[...]) * jnp.concatenate([gate] * (D_MODEL // LANES), axis=1)
            y_ref[s0:s0 + seqs] = y.astype(BF16).reshape(seqs, cap, D_MODEL)

    run(xc_ref, gc_ref, yc_ref)
    run(xl_ref, gl_ref, yl_ref)


def _experts_call(xs_ctx, xs_lat, gates_ctx, gates_lat, w_gate, w_up, w_down):
    def slot_spec(a):
        n_seq, _, cap, width = a.shape
        return pl.BlockSpec((n_seq, None, cap, width), lambda e: (0, e, 0, 0))

    w_spec = pl.BlockSpec((None, D_MODEL, D_EXPERT), lambda e: (e, 0, 0))
    return pl.pallas_call(
        _experts_kernel,
        grid=(N_EXPERTS,),
        in_specs=[slot_spec(xs_ctx), slot_spec(xs_lat), slot_spec(gates_ctx), slot_spec(gates_lat),
                  w_spec, w_spec, pl.BlockSpec((None, D_EXPERT, D_MODEL), lambda e: (e, 0, 0))],
        out_specs=[slot_spec(xs_ctx), slot_spec(xs_lat)],
        out_shape=[jax.ShapeDtypeStruct(xs_ctx.shape, BF16), jax.ShapeDtypeStruct(xs_lat.shape, BF16)],
        scratch_shapes=[pltpu.VMEM((D_MODEL, D_EXPERT), BF16),
                        pltpu.VMEM((D_MODEL, D_EXPERT), BF16),
                        pltpu.VMEM((D_EXPERT, D_MODEL), BF16),
                        pltpu.VMEM((DISPATCH_ROWS, D_EXPERT), BF16)],
        compiler_params=_params(1),
        name="experts",
    )(xs_ctx, xs_lat, gates_ctx, gates_lat, w_gate, w_up, w_down)


def _combine_kernel(x1_ref, y_ref, rank_ref, mod_ref, gf_ref, o_ref, oh_ref, *, seqs_per_step, seq_len, cap):
    slot = lax.broadcasted_iota(I32, (cap, 1), 0)
    for j in range(seqs_per_step):
        m = _mod_rows(mod_ref, j)
        for e in range(N_EXPERTS):
            oh_ref[j, e * cap:(e + 1) * cap, :] = jnp.where(rank_ref[j, e:e + 1, :] == slot, 1.0, 0.0).astype(BF16)
        for t in range(seq_len // ROW_TILE):
            rows = slice(t * ROW_TILE, (t + 1) * ROW_TILE)
            moe = _dot_tn(oh_ref[j, :, rows], y_ref[j])
            x2 = x1_ref[j, rows, :] + m[5] * moe
            r = lax.rsqrt(jnp.mean(x2 * x2, axis=-1, keepdims=True) + EPS)
            o_ref[j, rows, :] = (x2 * r) * gf_ref[...]


def _combine_call(x1, y, rank, mod, g_final, *, seqs_per_step):
    n_seq, seq_len, _ = x1.shape
    n_slots = y.shape[1]
    sps = seqs_per_step
    kernel = functools.partial(_combine_kernel, seqs_per_step=sps, seq_len=seq_len, cap=n_slots // N_EXPERTS)
    seq_spec = pl.BlockSpec((sps, seq_len, D_MODEL), lambda b: (b, 0, 0))
    return pl.pallas_call(
        kernel,
        grid=(n_seq // sps,),
        in_specs=[seq_spec,
                  pl.BlockSpec((sps, n_slots, D_MODEL), lambda b: (b, 0, 0)),
                  pl.BlockSpec((sps, N_EXPERTS, seq_len), lambda b: (b, 0, 0)),
                  _mod_spec(mod, sps),
                  pl.BlockSpec((1, D_MODEL), lambda b: (0, 0))],
        out_specs=seq_spec,
        out_shape=jax.ShapeDtypeStruct((n_seq, seq_len, D_MODEL), F32),
        scratch_shapes=[pltpu.VMEM((sps, n_slots, seq_len), BF16)],
        compiler_params=_params(1),
        name="combine",
    )(x1, y, rank, mod, g_final)


def kernel(x_prompt, x_sample, state_gla_fwd, state_gla_bwd, c, c_ctx, w_mod, b_mod, g_norm1, g_norm2,
           w_in, w_conv, b_conv, w_a_up_f, b_a_f, w_a_up_b, b_a_b, g_gla_norm, w_out, w_router,
           w_gate, w_up, w_down, g_final):
    assert w_mod.shape[0] == 1, "single trunk layer"
    n_ctx, ctx_len, _ = x_prompt.shape
    n_lat, lat_len, _ = x_sample.shape
    ctx_cap = EC_CAPACITY_FACTOR * ctx_len // N_EXPERTS
    lat_cap = EC_CAPACITY_FACTOR * lat_len // N_EXPERTS

    c_rows = jnp.concatenate([c_ctx[None, :], c, jnp.zeros((8 - 1 - n_lat, D_MODEL), F32)], axis=0)
    mod_ctx, mod_lat = _mod_call(c_rows, n_lat, w_mod[0], b_mod)

    mixer_weights = [g_norm1, jnp.transpose(w_in[0]), w_conv, b_conv, w_a_up_f, b_a_f, w_a_up_b, b_a_b,
                     g_gla_norm[0].reshape(1, GLA_DV_TOT), w_out[0].astype(BF16)]
    assert len(mixer_weights) == N_MIXER_WEIGHTS
    wr_t = jnp.transpose(w_router[0])
    wr_hi = wr_t.astype(BF16)
    wr_lo = (wr_t - wr_hi.astype(F32)).astype(BF16)

    x1_ctx, new_f, new_b, w_in_bf = _mixer_call(x_prompt, mod_ctx, None, mixer_weights,
                                                period=ctx_len, has_state_out=True, seqs_per_step=4)
    mixer_weights[1] = w_in_bf
    (x1_lat,) = _mixer_call(x_sample, mod_lat, (state_gla_fwd, state_gla_bwd), mixer_weights,
                            period=GRID_W, has_state_out=False, seqs_per_step=1)

    wr_both = jnp.concatenate([wr_hi, wr_lo], axis=0)
    xs_ctx, rank_ctx, gates_ctx = _route_call(x1_ctx, mod_ctx, g_norm2, wr_both, seqs_per_step=8)
    xs_lat, rank_lat, gates_lat = _route_call(x1_lat, mod_lat, g_norm2, wr_both, seqs_per_step=1)

    per_expert = lambda a, n, cap: a.reshape(n, N_EXPERTS, cap, a.shape[-1])
    y_ctx, y_lat = _experts_call(per_expert(xs_ctx, n_ctx, ctx_cap), per_expert(xs_lat, n_lat, lat_cap),
                                 per_expert(gates_ctx, n_ctx, ctx_cap), per_expert(gates_lat, n_lat, lat_cap),
                                 w_gate[0], w_up[0], w_down[0])

    g_fin = g_final[None, :]
    y_prompt = _combine_call(x1_ctx, y_ctx.reshape(xs_ctx.shape), rank_ctx, mod_ctx, g_fin, seqs_per_step=8)
    y_sample = _combine_call(x1_lat, y_lat.reshape(xs_lat.shape), rank_lat, mod_lat, g_fin, seqs_per_step=1)
    return y_prompt, y_sample, new_f, new_b
```

```python
import functools

import jax
import jax.numpy as jnp
from jax import lax
from jax.experimental import pallas as pl
from jax.experimental.pallas import tpu as pltpu

F32 = jnp.float32
BF16 = jnp.bfloat16
I32 = jnp.int32

D_MODEL = 1024
D_CONV = D_MODEL // 2
GRID_W = 64
GLA_HEADS = 4
GLA_DK = 64
GLA_DV = 128
GLA_DK_TOT = GLA_HEADS * GLA_DK
GLA_DV_TOT = GLA_HEADS * GLA_DV
GLA_LOW_RANK = 16
GLA_TAU = 16.0
GLA_CHUNK = 64
N_EXPERTS = 16
EC_CAPACITY_FACTOR = 2
D_EXPERT = 1024
N_MOD = 6
EPS = 1e-6
LOG2_E = 1.4426950408889634

OFF_XB = 0
OFF_XC = D_CONV
OFF_XV = 2 * D_CONV
OFF_Q = 3 * D_CONV
OFF_K = OFF_Q + GLA_DK_TOT
OFF_V = OFF_K + GLA_DK_TOT
OFF_OG = OFF_V + GLA_DV_TOT
OFF_ALOW = OFF_OG + GLA_DV_TOT
P_TOT = OFF_ALOW + 2 * GLA_LOW_RANK

LANES = 128
SUBLANES = 8
ROW_TOKENS = LANES // N_EXPERTS
P_PAD = -(-P_TOT // LANES) * LANES
ROW_TILE = 256
N_MIXER_WEIGHTS = 10
DISPATCH_ROWS = 512
VMEM_LIMIT = 56 * 1024 * 1024


def _dot(a, b):
    return jnp.dot(a, b, preferred_element_type=F32)


def _dot_nt(a, b):
    return lax.dot_general(a, b, (((1,), (1,)), ((), ())), preferred_element_type=F32)


def _dot_tn(a, b):
    return lax.dot_general(a, b, (((0,), (0,)), ((), ())), preferred_element_type=F32)


def _split(a):
    hi = a.astype(BF16)
    lo = (a - hi.astype(F32)).astype(BF16)
    return hi, lo


def _silu(x):
    return x * jax.nn.sigmoid(x)


def _modulated_norm(x, g, scale, shift):
    r = lax.rsqrt(jnp.mean(x * x, axis=-1, keepdims=True) + EPS)
    return (x * r) * (g * (1.0 + scale)) + shift


def _params(n_axes):
    return pltpu.CompilerParams(dimension_semantics=("arbitrary",) * n_axes,
                                vmem_limit_bytes=VMEM_LIMIT)


def _mod_kernel(c_ref, w_ref, b_ref, ctx_ref, lat_ref, acc_ref):
    rows = c_ref.shape[0]
    s = _silu(c_ref[...])
    s_hi, s_lo = _split(jnp.concatenate([s, s], axis=0))
    upper = lax.broadcasted_iota(I32, (2 * rows, 1), 0) < rows
    w_hi, w_lo = _split(w_ref[...])
    by_hi = _dot(jnp.where(upper, s_hi, s_lo), w_hi)
    part = by_hi[:rows] + by_hi[rows:] + _dot(s_hi[:rows], w_lo)

    @pl.when(pl.program_id(0) == 0)
    def _():
        acc_ref[...] = part + b_ref[...]

    @pl.when(pl.program_id(0) != 0)
    def _():
        acc_ref[...] = acc_ref[...] + part

    @pl.when(pl.program_id(0) == pl.num_programs(0) - 1)
    def _():
        ctx_ref[0] = acc_ref[0:1, :]
        for i in range(lat_ref.shape[0]):
            lat_ref[i] = acc_ref[1 + i:2 + i, :]


def _mod_call(c_rows, n_lat, w_mod, b_mod):
    rows, d = c_rows.shape
    n = w_mod.shape[1]
    tk = D_MODEL // 4
    return pl.pallas_call(
        _mod_kernel,
        grid=(d // tk,),
        in_specs=[pl.BlockSpec((rows, tk), lambda k: (0, k)),
                  pl.BlockSpec((tk, n), lambda k: (k, 0)),
                  pl.BlockSpec((1, n), lambda k: (0, 0))],
        out_specs=[pl.BlockSpec((1, 1, n), lambda k: (0, 0, 0)),
                   pl.BlockSpec((n_lat, 1, n), lambda k: (0, 0, 0))],
        out_shape=[jax.ShapeDtypeStruct((1, 1, n), F32), jax.ShapeDtypeStruct((n_lat, 1, n), F32)],
        scratch_shapes=[pltpu.VMEM((rows, n), F32)],
        compiler_params=_params(1),
        name="mod",
    )(c_rows, w_mod, b_mod)


def _mod_rows(mod_ref, j):
    row = mod_ref[j % mod_ref.shape[0]]
    return [row[:, i * D_MODEL:(i + 1) * D_MODEL] for i in range(N_MOD)]


def _mod_spec(mod, seqs_per_step):
    if mod.shape[0] == 1:
        return pl.BlockSpec(mod.shape, lambda b: (0, 0, 0))
    return pl.BlockSpec((seqs_per_step,) + mod.shape[1:], lambda b: (b, 0, 0))


def _staggered(programs):
    programs = list(programs)
    started = 0
    while programs:
        started = min(started + 1, len(programs))
        running = [p for p in programs[:started] if next(p, "done") != "done"]
        programs = running + programs[started:]
        started = len(running)
        yield


def _for_row_tiles(seq_len, phases, independent):
    n = seq_len // ROW_TILE
    if independent:
        yield from _staggered(phases(i) for i in range(n))
    else:
        for i in range(n):
            yield from phases(i)


def _tile_rows(tile, offset=0, size=ROW_TILE):
    return slice(tile * ROW_TILE + offset, tile * ROW_TILE + offset + size)


def _mixer_kernel(*refs, seqs_per_step, has_state_in, has_state_out, stages_w_in, **static):
    refs = list(refs)
    n_in = 2 + (2 if has_state_in else 0)
    per_seq_in, refs = [refs[0]] + refs[2:n_in], [refs[1]] + refs[n_in:]
    mod_ref, weights, refs = refs[0], refs[1:1 + N_MIXER_WEIGHTS], refs[1 + N_MIXER_WEIGHTS:]
    n_out = 1 + (2 if has_state_out else 0)
    per_seq_out, refs = refs[:n_out], refs[n_out:]
    if stages_w_in:
        win_f32_ref, win_bf_ref, scratch = weights[1], refs[0], refs[1:]
        weights = weights[:1] + [win_bf_ref] + weights[2:]

        @pl.when(pl.program_id(0) == 0)
        def _():
            for c0 in range(0, P_PAD, LANES):
                n = min(LANES, P_TOT - c0)
                cols = win_f32_ref[c0:c0 + n, :]
                if n < LANES:
                    cols = jnp.concatenate([cols, jnp.zeros((LANES - n, D_MODEL), F32)], axis=0)
                win_bf_ref[:, c0:c0 + LANES] = jnp.transpose(cols).astype(BF16)
    else:
        scratch = refs
    programs = []
    for j in range(seqs_per_step):
        ins = [r.at[j] for r in per_seq_in]
        outs = [r.at[j] for r in per_seq_out]
        programs.append(_mixer_sequence(ins[0], _mod_rows(mod_ref, j), ins[1:], weights, outs[0], outs[1:],
                                        [r.at[j] for r in scratch], **static))
    for _ in _staggered(programs):
        pass


def _mixer_sequence(x_ref, m, s0_refs, weights, x1_ref, sout_refs, scratch, *, seq_len, period):
    has_state_in = bool(s0_refs)
    has_state_out = bool(sout_refs)
    g1_ref, win_ref, wconv_ref, bconv_ref, wupf_ref, bupf_ref, wupb_ref, bupb_ref, ggla_ref, wout_ref = weights
    og_ref, qd_ref, kd_ref, kst_ref, dect_ref, v_ref, s_ref, sst_ref, o_ref, ya_ref = scratch

    c = GLA_CHUNK
    tile_chunks = ROW_TILE // c
    n_tiles = seq_len // ROW_TILE
    n_pairs = GLA_HEADS // 2
    pair_k = 2 * GLA_DK
    pair_v = 2 * GLA_DV

    def stage1(ti):
        rows = _tile_rows(ti)
        h = _modulated_norm(x_ref[rows, :], g1_ref[...], m[1], m[0]).astype(BF16)
        row_i = lax.broadcasted_iota(I32, (ROW_TILE, 1), 0)
        yield
        p_gate = _dot(h, win_ref[:, OFF_OG:P_PAD])
        og_ref[rows, :] = p_gate[:, :GLA_DV_TOT]
        yield
        zero_up = jnp.zeros((GLA_LOW_RANK, GLA_DK_TOT), F32)
        w_up = jnp.concatenate([jnp.concatenate([wupf_ref[0], zero_up], axis=1),
                                jnp.concatenate([zero_up, wupb_ref[0]], axis=1),
                                jnp.zeros((P_PAD - P_TOT, 2 * GLA_DK_TOT), F32)], axis=0).astype(BF16)
        b_up = jnp.concatenate([bupf_ref[...], bupb_ref[...]], axis=1)
        z = _dot(p_gate[:, GLA_DV_TOT:].astype(BF16), w_up) + b_up
        la = (jnp.minimum(z, 0.0) - jnp.log(1.0 + jnp.exp(-jnp.abs(z)))) * (LOG2_E / GLA_TAU)
        col_j = lax.broadcasted_iota(I32, (1, ROW_TILE), 1)
        same_chunk = (row_i & -c) == (col_j & -c)
        lower = jnp.where(same_chunk & (col_j <= row_i), 1.0, 0.0).astype(BF16)
        la_parts = jnp.concatenate(_split(la), axis=1)
        n_gate = 2 * GLA_DK_TOT
        yield
        pre = _dot(lower, la_parts)
        pre = pre[:, :n_gate] + pre[:, n_gate:]
        tot = jnp.concatenate([jnp.broadcast_to(pre[(n + 1) * c - 1:(n + 1) * c], (c, n_gate))
                               for n in range(tile_chunks)], axis=0)
        p_qkv = _dot(h, win_ref[:, OFF_Q:OFF_OG])
        yield
        q = p_qkv[:, :GLA_DK_TOT] * (GLA_DK ** -0.5)
        k = p_qkv[:, GLA_DK_TOT:2 * GLA_DK_TOT]
        v_ref[rows, :] = p_qkv[:, 2 * GLA_DK_TOT:].astype(BF16)
        for d in range(2):
            cols = slice(d * GLA_DK_TOT, (d + 1) * GLA_DK_TOT)
            if d == 0:
                bq = pre[:, cols]
                bk = tot[:, cols] - bq
            else:
                bk = pre[:, cols] - la[:, cols]
                bq = tot[:, cols] - bk
            qd_ref[d, rows, :] = (q * jnp.exp2(bq)).astype(BF16)
            kd_ref[d, rows, :] = (k * jnp.exp2(-bq)).astype(BF16)
            kst_ref[d, ti] = jnp.transpose(k * jnp.exp2(bk)).astype(BF16)
            totals = [tot[n * c:n * c + 1, cols] for n in range(tile_chunks)]
            totals.append(jnp.zeros((LANES - tile_chunks, GLA_DK_TOT), F32))
            dect_ref[d, ti] = jnp.transpose(jnp.exp2(jnp.concatenate(totals, axis=0)))
        yield
        p_conv = _dot(h, win_ref[:, :OFF_Q])
        yield
        pos = row_i & (period - 1)
        u = p_conv[:, OFF_XC:OFF_XC + D_CONV] * p_conv[:, OFF_XV:OFF_XV + D_CONV]
        u_prev = jnp.where(pos == 0, 0.0, pltpu.roll(u, 1, 0))
        u_next = jnp.where(pos == period - 1, 0.0, pltpu.roll(u, ROW_TILE - 1, 0))
        conv = u_prev * wconv_ref[0, 0:1, :] + u * wconv_ref[0, 1:2, :] + u_next * wconv_ref[0, 2:3, :] + bconv_ref[...]
        ya_ref[rows, :] = (p_conv[:, OFF_XB:OFF_XB + D_CONV] * conv).astype(BF16)
        yield

    yield from _for_row_tiles(seq_len, stage1, independent=True)

    for d in range(2):
        for pair in range(n_pairs):
            if has_state_in:
                zero = jnp.zeros((GLA_DK, GLA_DV), F32)
                top = jnp.concatenate([s0_refs[d][2 * pair], zero], axis=1)
                bot = jnp.concatenate([zero, s0_refs[d][2 * pair + 1]], axis=1)
                s_ref[d, pair] = jnp.concatenate([top, bot], axis=0)
            else:
                s_ref[d, pair] = jnp.zeros((pair_k, pair_v), F32)

    def scan_tile(i):
        upper_lane = lax.broadcasted_iota(I32, (1, LANES), 1) >= GLA_DK
        qi = lax.broadcasted_iota(I32, (LANES, 1), 0)
        kj = lax.broadcasted_iota(I32, (1, 2 * LANES), 1) & (LANES - 1)
        same_chunk = (qi & c) == (kj & c)
        causal = (same_chunk & (kj <= qi), same_chunk & (kj >= qi))
        for pair in range(n_pairs):
            kl = slice(pair * pair_k, (pair + 1) * pair_k)
            vl = slice(pair * pair_v, (pair + 1) * pair_v)
            for blk in range(ROW_TILE // LANES):
                rows = _tile_rows(i, blk * LANES, LANES)
                att = None
                for d in range(2):
                    kd = kd_ref[d, rows, kl]
                    zk = jnp.zeros_like(kd)
                    keys = jnp.concatenate([jnp.where(upper_lane, zk, kd), jnp.where(upper_lane, kd, zk)], axis=0)
                    a = jnp.where(causal[d], _dot_nt(qd_ref[d, rows, kl], keys), 0.0)
                    att = a if att is None else att + a
                v = v_ref[rows, vl]
                zv = jnp.zeros((LANES, GLA_DV), BF16)
                v_bd = jnp.concatenate([jnp.concatenate([v[:, :GLA_DV], zv], axis=1),
                                        jnp.concatenate([zv, v[:, GLA_DV:]], axis=1)], axis=0)
                o_ref[rows, vl] = _dot(att.astype(BF16), v_bd)
        yield
        key_row = lax.broadcasted_iota(I32, (pair_k, 1), 0)
        val_col = lax.broadcasted_iota(I32, (1, pair_v), 1)
        blockdiag = (key_row >= GLA_DK) == (val_col >= GLA_DV)
        for d in range(2):
            tile = i if d == 0 else n_tiles - 1 - i
            chunks = range(tile_chunks)
            for pair in range(n_pairs):
                kr = slice(pair * pair_k, (pair + 1) * pair_k)
                vl = slice(pair * pair_v, (pair + 1) * pair_v)
                s = s_ref[d, pair]
                for c4 in (chunks if d == 0 else reversed(chunks)):
                    blk, half = divmod(c4, 2)
                    kst = kst_ref[d, tile, kr, blk * LANES:(blk + 1) * LANES]
                    kst = jnp.where(upper_lane if half else ~upper_lane, kst, jnp.zeros_like(kst))
                    kv = jnp.where(blockdiag, _dot(kst, v_ref[_tile_rows(tile, blk * LANES, LANES), vl]), 0.0)
                    sst_ref[pair, tile * tile_chunks + c4, d * pair_k:(d + 1) * pair_k, :] = s.astype(BF16)
                    s = dect_ref[d, tile, kr, c4:c4 + 1] * s + kv
                s_ref[d, pair] = s
        yield

    yield from _for_row_tiles(seq_len, scan_tile, independent=False)

    if has_state_out:
        for d in range(2):
            for pair in range(n_pairs):
                s = s_ref[d, pair]
                sout_refs[d][2 * pair] = s[0:GLA_DK, 0:GLA_DV]
                sout_refs[d][2 * pair + 1] = s[GLA_DK:, GLA_DV:]

    def stage3(i):
        for pair in range(n_pairs):
            kl = slice(pair * pair_k, (pair + 1) * pair_k)
            vl = slice(pair * pair_v, (pair + 1) * pair_v)
            for c4 in range(tile_chunks):
                crow = _tile_rows(i, c4 * c, c)
                q2 = jnp.concatenate([qd_ref[0, crow, kl], qd_ref[1, crow, kl]], axis=1)
                o_ref[crow, vl] = o_ref[crow, vl] + _dot(q2, sst_ref[pair, i * tile_chunks + c4])
        yield
        rows = _tile_rows(i)
        heads = []
        for h in range(GLA_HEADS):
            hl = slice(h * GLA_DV, (h + 1) * GLA_DV)
            oh = o_ref[rows, hl]
            r = lax.rsqrt(jnp.mean(oh * oh, axis=-1, keepdims=True) + EPS)
            heads.append(oh * r * ggla_ref[:, hl])
        y_b = jnp.concatenate(heads, axis=1) * _silu(og_ref[rows, :])
        y = jnp.concatenate([ya_ref[rows, :], y_b.astype(BF16)], axis=1)
        x1_ref[rows, :] = x_ref[rows, :] + m[2] * _dot(y, wout_ref[...])
        yield

    yield from _for_row_tiles(seq_len, stage3, independent=True)


def _mixer_call(x, mod, states, weights, *, period, has_state_out, seqs_per_step):
    n_seq, seq_len, _ = x.shape
    sps = seqs_per_step
    has_state_in = states is not None
    stages_w_in = weights[1].dtype == F32
    kernel = functools.partial(_mixer_kernel, seqs_per_step=sps, seq_len=seq_len, period=period,
                               has_state_in=has_state_in, has_state_out=has_state_out, stages_w_in=stages_w_in)
    state_spec = pl.BlockSpec((sps, None, GLA_HEADS, GLA_DK, GLA_DV), lambda b: (b, 0, 0, 0, 0))
    const2 = lambda b: (0, 0)
    in_specs = [pl.BlockSpec((sps, seq_len, D_MODEL), lambda b: (b, 0, 0)),
                _mod_spec(mod, sps)]
    args = [x, mod]
    if has_state_in:
        in_specs += [state_spec, state_spec]
        args += list(states)
    in_specs += [pl.BlockSpec(w.shape, lambda b, nd=w.ndim: (0,) * nd, pipeline_mode=pl.Buffered(1))
                 for w in weights]
    args += list(weights)
    out_specs = [pl.BlockSpec((sps, seq_len, D_MODEL), lambda b: (b, 0, 0))]
    out_shape = [jax.ShapeDtypeStruct((n_seq, seq_len, D_MODEL), F32)]
    if has_state_out:
        out_specs += [state_spec, state_spec]
        out_shape += [jax.ShapeDtypeStruct((n_seq, 1, GLA_HEADS, GLA_DK, GLA_DV), F32)] * 2
    if stages_w_in:
        out_specs.append(pl.BlockSpec((D_MODEL, P_PAD), const2))
        out_shape.append(jax.ShapeDtypeStruct((D_MODEL, P_PAD), BF16))
    n_tiles = seq_len // ROW_TILE
    n_pairs = GLA_HEADS // 2
    per_seq = lambda shape, dtype: pltpu.VMEM((sps,) + shape, dtype)
    scratch = [per_seq((seq_len, GLA_DV_TOT), F32),
               per_seq((2, seq_len, GLA_DK_TOT), BF16),
               per_seq((2, seq_len, GLA_DK_TOT), BF16),
               per_seq((2, n_tiles, GLA_DK_TOT, ROW_TILE), BF16),
               per_seq((2, n_tiles, GLA_DK_TOT, LANES), F32),
               per_seq((seq_len, GLA_DV_TOT), BF16),
               per_seq((2, n_pairs, 2 * GLA_DK, 2 * GLA_DV), F32),
               per_seq((n_pairs, seq_len // GLA_CHUNK, 4 * GLA_DK, 2 * GLA_DV), BF16),
               per_seq((seq_len, GLA_DV_TOT), F32),
               per_seq((seq_len, D_CONV), BF16)]
    return pl.pallas_call(
        kernel,
        grid=(n_seq // sps,),
        in_specs=in_specs,
        out_specs=out_specs,
        out_shape=out_shape,
        scratch_shapes=scratch,
        compiler_params=_params(1),
        name="mixer",
    )(*args)


def _route_kernel(x1_ref, mod_ref, g2_ref, wr2_ref, *refs, seqs_per_step, **static):
    programs = [_route_sequence(x1_ref.at[j], _mod_rows(mod_ref, j), g2_ref, wr2_ref, *[r.at[j] for r in refs],
                                **static)
                for j in range(seqs_per_step)]
    for _ in _staggered(programs):
        pass


def _route_sequence(x1_ref, m, g2_ref, wr2_ref, xs_ref, rank_ref, gate_ref,
                    pt_ref, h2_ref, spread_ref, bar_ref, cnt_ref, win_ref, *, seq_len, cap):
    n_rows = seq_len // ROW_TOKENS
    for t in range(seq_len // ROW_TILE):
        rows = slice(t * ROW_TILE, (t + 1) * ROW_TILE)
        h2 = _modulated_norm(x1_ref[rows, :], g2_ref[...], m[4], m[3])
        hi = h2.astype(BF16)
        h2_ref[rows, :] = hi
        yield
        by_hi = _dot_nt(wr2_ref[...], hi)
        pt_ref[:, rows] = by_hi[:N_EXPERTS] + by_hi[N_EXPERTS:]
    yield

    logits = pt_ref[...]
    ex = jnp.exp(logits - jnp.max(logits, axis=0, keepdims=True))
    probs = ex / jnp.sum(ex, axis=0, keepdims=True)
    pt_ref[...] = probs
    pad = jnp.zeros((LANES - N_EXPERTS, seq_len), F32)
    p_tok = jnp.transpose(jnp.concatenate([probs, pad], axis=0))

    spread = p_tok
    sh = N_EXPERTS
    while sh < LANES:
        spread = spread + pltpu.roll(spread, sh, 1)
        sh *= 2
    spread_ref[...] = spread
    sub_j = lax.broadcasted_iota(I32, (SUBLANES, 1), 0)
    lane_g = lax.broadcasted_iota(I32, (1, LANES), 1) >> (N_EXPERTS.bit_length() - 1)
    lane_j = (ROW_TOKENS - lane_g) & (ROW_TOKENS - 1)
    own_group = jnp.where(lane_j == sub_j, spread.reshape(n_rows, SUBLANES, LANES), 0.0)
    bar = jnp.sum(own_group, axis=1, keepdims=True)
    bar_ref[...] = jnp.broadcast_to(bar, (n_rows, SUBLANES, LANES))
    cnt_ref[...] = jnp.zeros((n_rows, SUBLANES, LANES), F32)

    rows_per_block = LANES // ROW_TOKENS

    def count_block(g_s, g_t, relation):
        s_blk = spread_ref[g_s * LANES:(g_s + 1) * LANES, :]
        wins = [jnp.zeros((SUBLANES, LANES), F32)] * rows_per_block
        for tl in range(rows_per_block):
            r = g_t * rows_per_block + tl
            bar_r = bar_ref[r]
            acc = cnt_ref[r]
            for sl in range(rows_per_block):
                s_vreg = s_blk[sl * SUBLANES:(sl + 1) * SUBLANES]
                order = relation if relation != "same" else ("before" if sl < tl else "after" if sl > tl else "tie")
                if order == "before":
                    won = jnp.where(s_vreg >= bar_r, 1.0, 0.0)
                    acc = acc + won
                    if relation == "before":
                        wins[sl] = wins[sl] + won
                elif order == "after":
                    acc = acc + jnp.where(s_vreg > bar_r, 1.0, 0.0)
                else:
                    acc = acc + jnp.where(s_vreg > bar_r, 1.0, 0.0) \
                        + jnp.where((s_vreg == bar_r) & (sub_j < lane_j), 1.0, 0.0)
            cnt_ref[r] = acc
        if relation == "before":
            w_rows = slice(g_s * rows_per_block, (g_s + 1) * rows_per_block)
            win_ref[w_rows] = win_ref[w_rows] + jnp.stack(wins)

    n_grp = seq_len // LANES
    win_ref[...] = jnp.zeros((n_rows, SUBLANES, LANES), F32)
    yield
    for g_t in range(n_grp):
        for g_s in range(g_t + 1):
            count_block(g_s, g_t, "before" if g_s < g_t else "same")
            yield

    counts = jnp.sum(cnt_ref[...], axis=1, keepdims=True)
    counts = jnp.broadcast_to(counts, (n_rows, SUBLANES, LANES)).reshape(seq_len, LANES)
    rank_tok = pltpu.roll(counts, 0, 1, stride=N_EXPERTS, stride_axis=0)
    wins = win_ref[...].reshape(seq_len, LANES)
    sh = N_EXPERTS
    while sh < LANES:
        wins = wins + pltpu.roll(wins, sh, 1)
        sh *= 2
    tok = lax.broadcasted_iota(I32, (seq_len, 1), 0)
    later = ((n_grp - 1 - (tok >> (LANES.bit_length() - 1))) * LANES).astype(F32)
    rank_tok = rank_tok + (later - wins)
    rank_ref[...] = jnp.transpose(rank_tok)[:N_EXPERTS].astype(I32)
    yield

    group = DISPATCH_ROWS // cap
    slot = lax.broadcasted_iota(I32, (cap, 1), 0)
    half = D_MODEL // 2
    for gi in range(N_EXPERTS // group):
        picks = []
        for e in range(gi * group, (gi + 1) * group):
            oh = rank_ref[e:e + 1, :] == slot
            gate = jnp.sum(jnp.where(oh, pt_ref[e:e + 1, :], 0.0), axis=1, keepdims=True)
            gate_ref[e * cap:(e + 1) * cap, :] = jnp.broadcast_to(gate, (cap, LANES))
            picks.append(oh)
        ohb = jnp.where(jnp.concatenate(picks, axis=0), 1.0, 0.0).astype(BF16)
        yield
        out_rows = slice(gi * DISPATCH_ROWS, (gi + 1) * DISPATCH_ROWS)
        xs_ref[out_rows, :half] = _dot(ohb, h2_ref[:, :half]).astype(BF16)
        xs_ref[out_rows, half:] = _dot(ohb, h2_ref[:, half:]).astype(BF16)
        yield


def _route_call(x1, mod, g2, wr_both, *, seqs_per_step):
    n_seq, seq_len, _ = x1.shape
    sps = seqs_per_step
    cap = EC_CAPACITY_FACTOR * seq_len // N_EXPERTS
    kernel = functools.partial(_route_kernel, seqs_per_step=sps, seq_len=seq_len, cap=cap)
    const2 = lambda b: (0, 0)
    per_seq = lambda shape, dtype: pltpu.VMEM((sps,) + shape, dtype)
    rank_rows = (seq_len // ROW_TOKENS, SUBLANES, LANES)
    return pl.pallas_call(
        kernel,
        grid=(n_seq // sps,),
        in_specs=[pl.BlockSpec((sps, seq_len, D_MODEL), lambda b: (b, 0, 0)),
                  _mod_spec(mod, sps),
                  pl.BlockSpec((1, D_MODEL), const2),
                  pl.BlockSpec((2 * N_EXPERTS, D_MODEL), const2)],
        out_specs=[pl.BlockSpec((sps, N_EXPERTS * cap, D_MODEL), lambda b: (b, 0, 0)),
                   pl.BlockSpec((sps, N_EXPERTS, seq_len), lambda b: (b, 0, 0)),
                   pl.BlockSpec((sps, N_EXPERTS * cap, LANES), lambda b: (b, 0, 0))],
        out_shape=[jax.ShapeDtypeStruct((n_seq, N_EXPERTS * cap, D_MODEL), BF16),
                   jax.ShapeDtypeStruct((n_seq, N_EXPERTS, seq_len), I32),
                   jax.ShapeDtypeStruct((n_seq, N_EXPERTS * cap, LANES), F32)],
        scratch_shapes=[per_seq((N_EXPERTS, seq_len), F32),
                        per_seq((seq_len, D_MODEL), BF16),
                        per_seq((seq_len, LANES), F32),
                        per_seq(rank_rows, F32),
                        per_seq(rank_rows, F32),
                        per_seq(rank_rows, F32)],
        compiler_params=_params(1),
        name="route",
    )(x1, mod, g2, wr_both)


def _experts_kernel(xc_ref, xl_ref, gc_ref, gl_ref, wg_ref, wu_ref, wd_ref, yc_ref, yl_ref,
                    wgb_ref, wub_ref, wdb_ref, a_ref):
    wgb_ref[...] = wg_ref[...].astype(BF16)
    wub_ref[...] = wu_ref[...].astype(BF16)
    wdb_ref[...] = wd_ref[...].astype(BF16)
    f_tile = 2 * LANES

    def run(x_ref, g_ref, y_ref):
        n_seq, cap, _ = x_ref.shape
        seqs = DISPATCH_ROWS // cap
        for s0 in range(0, n_seq, seqs):
            x = x_ref[s0:s0 + seqs].reshape(DISPATCH_ROWS, D_MODEL)
            for f0 in range(0, D_EXPERT, f_tile):
                cols = slice(f0, f0 + f_tile)
                a_ref[:, cols] = (_silu(_dot(x, wgb_ref[:, cols])) * _dot(x, wub_ref[:, cols])).astype(BF16)
            gate = g_ref[s0:s0 + seqs].reshape(DISPATCH_ROWS, LANES)
            y = _dot(a_ref[...], wdb_ref[...]) * jnp.concatenate([gate] * (D_MODEL // LANES), axis=1)
            y_ref[s0:s0 + seqs] = y.astype(BF16).reshape(seqs, cap, D_MODEL)

    run(xc_ref, gc_ref, yc_ref)
    run(xl_ref, gl_ref, yl_ref)


def _experts_call(xs_ctx, xs_lat, gates_ctx, gates_lat, w_gate, w_up, w_down):
    def slot_spec(a):
        n_seq, _, cap, width = a.shape
        return pl.BlockSpec((n_seq, None, cap, width), lambda e: (0, e, 0, 0))

    w_spec = pl.BlockSpec((None, D_MODEL, D_EXPERT), lambda e: (e, 0, 0))
    return pl.pallas_call(
        _experts_kernel,
        grid=(N_EXPERTS,),
        in_specs=[slot_spec(xs_ctx), slot_spec(xs_lat), slot_spec(gates_ctx), slot_spec(gates_lat),
                  w_spec, w_spec, pl.BlockSpec((None, D_EXPERT, D_MODEL), lambda e: (e, 0, 0))],
        out_specs=[slot_spec(xs_ctx), slot_spec(xs_lat)],
        out_shape=[jax.ShapeDtypeStruct(xs_ctx.shape, BF16), jax.ShapeDtypeStruct(xs_lat.shape, BF16)],
        scratch_shapes=[pltpu.VMEM((D_MODEL, D_EXPERT), BF16),
                        pltpu.VMEM((D_MODEL, D_EXPERT), BF16),
                        pltpu.VMEM((D_EXPERT, D_MODEL), BF16),
                        pltpu.VMEM((DISPATCH_ROWS, D_EXPERT), BF16)],
        compiler_params=_params(1),
        name="experts",
    )(xs_ctx, xs_lat, gates_ctx, gates_lat, w_gate, w_up, w_down)


def _combine_kernel(x1_ref, y_ref, rank_ref, mod_ref, gf_ref, o_ref, oh_ref, *, seqs_per_step, seq_len, cap):
    slot = lax.broadcasted_iota(I32, (cap, 1), 0)
    for j in range(seqs_per_step):
        m = _mod_rows(mod_ref, j)
        for e in range(N_EXPERTS):
            oh_ref[j, e * cap:(e + 1) * cap, :] = jnp.where(rank_ref[j, e:e + 1, :] == slot, 1.0, 0.0).astype(BF16)
        for t in range(seq_len // ROW_TILE):
            rows = slice(t * ROW_TILE, (t + 1) * ROW_TILE)
            moe = _dot_tn(oh_ref[j, :, rows], y_ref[j])
            x2 = x1_ref[j, rows, :] + m[5] * moe
            r = lax.rsqrt(jnp.mean(x2 * x2, axis=-1, keepdims=True) + EPS)
            o_ref[j, rows, :] = (x2 * r) * gf_ref[...]


def _combine_call(x1, y, rank, mod, g_final, *, seqs_per_step):
    n_seq, seq_len, _ = x1.shape
    n_slots = y.shape[1]
    sps = seqs_per_step
    kernel = functools.partial(_combine_kernel, seqs_per_step=sps, seq_len=seq_len, cap=n_slots // N_EXPERTS)
    seq_spec = pl.BlockSpec((sps, seq_len, D_MODEL), lambda b: (b, 0, 0))
    return pl.pallas_call(
        kernel,
        grid=(n_seq // sps,),
        in_specs=[seq_spec,
                  pl.BlockSpec((sps, n_slots, D_MODEL), lambda b: (b, 0, 0)),
                  pl.BlockSpec((sps, N_EXPERTS, seq_len), lambda b: (b, 0, 0)),
                  _mod_spec(mod, sps),
                  pl.BlockSpec((1, D_MODEL), lambda b: (0, 0))],
        out_specs=seq_spec,
        out_shape=jax.ShapeDtypeStruct((n_seq, seq_len, D_MODEL), F32),
        scratch_shapes=[pltpu.VMEM((sps, n_slots, seq_len), BF16)],
        compiler_params=_params(1),
        name="combine",
    )(x1, y, rank, mod, g_final)


def kernel(x_prompt, x_sample, state_gla_fwd, state_gla_bwd, c, c_ctx, w_mod, b_mod, g_norm1, g_norm2,
           w_in, w_conv, b_conv, w_a_up_f, b_a_f, w_a_up_b, b_a_b, g_gla_norm, w_out, w_router,
           w_gate, w_up, w_down, g_final):
    assert w_mod.shape[0] == 1, "single trunk layer"
    n_ctx, ctx_len, _ = x_prompt.shape
    n_lat, lat_len, _ = x_sample.shape
    ctx_cap = EC_CAPACITY_FACTOR * ctx_len // N_EXPERTS
    lat_cap = EC_CAPACITY_FACTOR * lat_len // N_EXPERTS

    c_rows = jnp.concatenate([c_ctx[None, :], c, jnp.zeros((8 - 1 - n_lat, D_MODEL), F32)], axis=0)
    mod_ctx, mod_lat = _mod_call(c_rows, n_lat, w_mod[0], b_mod)

    mixer_weights = [g_norm1, jnp.transpose(w_in[0]), w_conv, b_conv, w_a_up_f, b_a_f, w_a_up_b, b_a_b,
                     g_gla_norm[0].reshape(1, GLA_DV_TOT), w_out[0].astype(BF16)]
    assert len(mixer_weights) == N_MIXER_WEIGHTS
    wr_t = jnp.transpose(w_router[0])
    wr_hi = wr_t.astype(BF16)
    wr_lo = (wr_t - wr_hi.astype(F32)).astype(BF16)

    x1_ctx, new_f, new_b, w_in_bf = _mixer_call(x_prompt, mod_ctx, None, mixer_weights,
                                                period=ctx_len, has_state_out=True, seqs_per_step=4)
    mixer_weights[1] = w_in_bf
    (x1_lat,) = _mixer_call(x_sample, mod_lat, (state_gla_fwd, state_gla_bwd), mixer_weights,
                            period=GRID_W, has_state_out=False, seqs_per_step=1)

    wr_both = jnp.concatenate([wr_hi, wr_lo], axis=0)
    xs_ctx, rank_ctx, gates_ctx = _route_call(x1_ctx, mod_ctx, g_norm2, wr_both, seqs_per_step=4)
    xs_lat, rank_lat, gates_lat = _route_call(x1_lat, mod_lat, g_norm2, wr_both, seqs_per_step=1)

    per_expert = lambda a, n, cap: a.reshape(n, N_EXPERTS, cap, a.shape[-1])
    y_ctx, y_lat = _experts_call(per_expert(xs_ctx, n_ctx, ctx_cap), per_expert(xs_lat, n_lat, lat_cap),
                                 per_expert(gates_ctx, n_ctx, ctx_cap), per_expert(gates_lat, n_lat, lat_cap),
                                 w_gate[0], w_up[0], w_down[0])

    g_fin = g_final[None, :]
    y_prompt = _combine_call(x1_ctx, y_ctx.reshape(xs_ctx.shape), rank_ctx, mod_ctx, g_fin, seqs_per_step=4)
    y_sample = _combine_call(x1_lat, y_lat.reshape(xs_lat.shape), rank_lat, mod_lat, g_fin, seqs_per_step=1)
    return y_prompt, y_sample, new_f, new_b
```

```python
import functools

import jax
import jax.numpy as jnp
from jax import lax
from jax.experimental import pallas as pl
from jax.experimental.pallas import tpu as pltpu

F32 = jnp.float32
BF16 = jnp.bfloat16
I32 = jnp.int32

D_MODEL = 1024
D_CONV = D_MODEL // 2
GRID_W = 64
GLA_HEADS = 4
GLA_DK = 64
GLA_DV = 128
GLA_DK_TOT = GLA_HEADS * GLA_DK
GLA_DV_TOT = GLA_HEADS * GLA_DV
GLA_LOW_RANK = 16
GLA_TAU = 16.0
GLA_CHUNK = 64
N_EXPERTS = 16
EC_CAPACITY_FACTOR = 2
D_EXPERT = 1024
N_MOD = 6
EPS = 1e-6
LOG2_E = 1.4426950408889634

OFF_XB = 0
OFF_XC = D_CONV
OFF_XV = 2 * D_CONV
OFF_Q = 3 * D_CONV
OFF_K = OFF_Q + GLA_DK_TOT
OFF_V = OFF_K + GLA_DK_TOT
OFF_OG = OFF_V + GLA_DV_TOT
OFF_ALOW = OFF_OG + GLA_DV_TOT
P_TOT = OFF_ALOW + 2 * GLA_LOW_RANK

LANES = 128
P_PAD = -(-P_TOT // LANES) * LANES
ROW_TILE = 256
N_MIXER_WEIGHTS = 10
DISPATCH_ROWS = 512
VMEM_LIMIT = 56 * 1024 * 1024


def _dot(a, b):
    return jnp.dot(a, b, preferred_element_type=F32)


def _dot_nt(a, b):
    return lax.dot_general(a, b, (((1,), (1,)), ((), ())), preferred_element_type=F32)


def _dot_tn(a, b):
    return lax.dot_general(a, b, (((0,), (0,)), ((), ())), preferred_element_type=F32)


def _split(a):
    hi = a.astype(BF16)
    lo = (a - hi.astype(F32)).astype(BF16)
    return hi, lo


def _silu(x):
    return x * jax.nn.sigmoid(x)


def _modulated_norm(x, g, scale, shift):
    r = lax.rsqrt(jnp.mean(x * x, axis=-1, keepdims=True) + EPS)
    return (x * r) * (g * (1.0 + scale)) + shift


def _params(n_axes):
    return pltpu.CompilerParams(dimension_semantics=("arbitrary",) * n_axes,
                                vmem_limit_bytes=VMEM_LIMIT)


def _mod_kernel(c_ref, w_ref, b_ref, ctx_ref, lat_ref, acc_ref):
    rows = c_ref.shape[0]
    s = _silu(c_ref[...])
    s_hi, s_lo = _split(jnp.concatenate([s, s], axis=0))
    upper = lax.broadcasted_iota(I32, (2 * rows, 1), 0) < rows
    w_hi, w_lo = _split(w_ref[...])
    by_hi = _dot(jnp.where(upper, s_hi, s_lo), w_hi)
    part = by_hi[:rows] + by_hi[rows:] + _dot(s_hi[:rows], w_lo)

    @pl.when(pl.program_id(0) == 0)
    def _():
        acc_ref[...] = part + b_ref[...]

    @pl.when(pl.program_id(0) != 0)
    def _():
        acc_ref[...] = acc_ref[...] + part

    @pl.when(pl.program_id(0) == pl.num_programs(0) - 1)
    def _():
        ctx_ref[0] = acc_ref[0:1, :]
        for i in range(lat_ref.shape[0]):
            lat_ref[i] = acc_ref[1 + i:2 + i, :]


def _mod_call(c_rows, n_lat, w_mod, b_mod):
    rows, d = c_rows.shape
    n = w_mod.shape[1]
    tk = D_MODEL // 4
    return pl.pallas_call(
        _mod_kernel,
        grid=(d // tk,),
        in_specs=[pl.BlockSpec((rows, tk), lambda k: (0, k)),
                  pl.BlockSpec((tk, n), lambda k: (k, 0)),
                  pl.BlockSpec((1, n), lambda k: (0, 0))],
        out_specs=[pl.BlockSpec((1, 1, n), lambda k: (0, 0, 0)),
                   pl.BlockSpec((n_lat, 1, n), lambda k: (0, 0, 0))],
        out_shape=[jax.ShapeDtypeStruct((1, 1, n), F32), jax.ShapeDtypeStruct((n_lat, 1, n), F32)],
        scratch_shapes=[pltpu.VMEM((rows, n), F32)],
        compiler_params=_params(1),
        name="mod",
    )(c_rows, w_mod, b_mod)


def _mod_rows(mod_ref, j):
    row = mod_ref[j % mod_ref.shape[0]]
    return [row[:, i * D_MODEL:(i + 1) * D_MODEL] for i in range(N_MOD)]


def _mod_spec(mod, seqs_per_step):
    if mod.shape[0] == 1:
        return pl.BlockSpec(mod.shape, lambda b: (0, 0, 0))
    return pl.BlockSpec((seqs_per_step,) + mod.shape[1:], lambda b: (b, 0, 0))


def _staggered(programs):
    programs = list(programs)
    started = 0
    while programs:
        started = min(started + 1, len(programs))
        running = [p for p in programs[:started] if next(p, "done") != "done"]
        programs = running + programs[started:]
        started = len(running)
        yield


def _for_row_tiles(seq_len, phases, independent):
    n = seq_len // ROW_TILE
    if independent:
        yield from _staggered(phases(i) for i in range(n))
    else:
        for i in range(n):
            yield from phases(i)


def _tile_rows(tile, offset=0, size=ROW_TILE):
    if isinstance(tile, int):
        return pl.ds(tile * ROW_TILE + offset, size)
    return pl.ds(pl.multiple_of(tile * ROW_TILE + offset, size), size)


def _mixer_kernel(*refs, seqs_per_step, has_state_in, has_state_out, stages_w_in, **static):
    refs = list(refs)
    n_in = 2 + (2 if has_state_in else 0)
    per_seq_in, refs = [refs[0]] + refs[2:n_in], [refs[1]] + refs[n_in:]
    mod_ref, weights, refs = refs[0], refs[1:1 + N_MIXER_WEIGHTS], refs[1 + N_MIXER_WEIGHTS:]
    n_out = 1 + (2 if has_state_out else 0)
    per_seq_out, refs = refs[:n_out], refs[n_out:]
    if stages_w_in:
        win_f32_ref, win_bf_ref, scratch = weights[1], refs[0], refs[1:]
        weights = weights[:1] + [win_bf_ref] + weights[2:]

        @pl.when(pl.program_id(0) == 0)
        def _():
            for c0 in range(0, P_PAD, LANES):
                n = min(LANES, P_TOT - c0)
                cols = win_f32_ref[c0:c0 + n, :]
                if n < LANES:
                    cols = jnp.concatenate([cols, jnp.zeros((LANES - n, D_MODEL), F32)], axis=0)
                win_bf_ref[:, c0:c0 + LANES] = jnp.transpose(cols).astype(BF16)
    else:
        scratch = refs
    programs = []
    for j in range(seqs_per_step):
        ins = [r.at[j] for r in per_seq_in]
        outs = [r.at[j] for r in per_seq_out]
        programs.append(_mixer_sequence(ins[0], _mod_rows(mod_ref, j), ins[1:], weights, outs[0], outs[1:],
                                        [r.at[j] for r in scratch], **static))
    for _ in _staggered(programs):
        pass


def _mixer_sequence(x_ref, m, s0_refs, weights, x1_ref, sout_refs, scratch, *, seq_len, period):
    has_state_in = bool(s0_refs)
    has_state_out = bool(sout_refs)
    g1_ref, win_ref, wconv_ref, bconv_ref, wupf_ref, bupf_ref, wupb_ref, bupb_ref, ggla_ref, wout_ref = weights
    og_ref, qd_ref, kd_ref, kst_ref, dect_ref, v_ref, s_ref, sst_ref, o_ref, ya_ref = scratch

    c = GLA_CHUNK
    tile_chunks = ROW_TILE // c
    n_tiles = seq_len // ROW_TILE
    n_pairs = GLA_HEADS // 2
    pair_k = 2 * GLA_DK
    pair_v = 2 * GLA_DV

    def stage1(ti):
        rows = _tile_rows(ti)
        h = _modulated_norm(x_ref[rows, :], g1_ref[...], m[1], m[0]).astype(BF16)
        row_i = lax.broadcasted_iota(I32, (ROW_TILE, 1), 0)
        yield
        p_gate = _dot(h, win_ref[:, OFF_OG:P_PAD])
        og_ref[rows, :] = p_gate[:, :GLA_DV_TOT]
        yield
        zero_up = jnp.zeros((GLA_LOW_RANK, GLA_DK_TOT), F32)
        w_up = jnp.concatenate([jnp.concatenate([wupf_ref[0], zero_up], axis=1),
                                jnp.concatenate([zero_up, wupb_ref[0]], axis=1),
                                jnp.zeros((P_PAD - P_TOT, 2 * GLA_DK_TOT), F32)], axis=0).astype(BF16)
        b_up = jnp.concatenate([bupf_ref[...], bupb_ref[...]], axis=1)
        z = _dot(p_gate[:, GLA_DV_TOT:].astype(BF16), w_up) + b_up
        la = (jnp.minimum(z, 0.0) - jnp.log(1.0 + jnp.exp(-jnp.abs(z)))) * (LOG2_E / GLA_TAU)
        col_j = lax.broadcasted_iota(I32, (1, ROW_TILE), 1)
        same_chunk = (row_i & -c) == (col_j & -c)
        lower = jnp.where(same_chunk & (col_j <= row_i), 1.0, 0.0).astype(BF16)
        la_parts = jnp.concatenate(_split(la), axis=1)
        n_gate = 2 * GLA_DK_TOT
        yield
        pre = _dot(lower, la_parts)
        pre = pre[:, :n_gate] + pre[:, n_gate:]
        tot = jnp.concatenate([jnp.broadcast_to(pre[(n + 1) * c - 1:(n + 1) * c], (c, n_gate))
                               for n in range(tile_chunks)], axis=0)
        p_qkv = _dot(h, win_ref[:, OFF_Q:OFF_OG])
        yield
        q = p_qkv[:, :GLA_DK_TOT] * (GLA_DK ** -0.5)
        k = p_qkv[:, GLA_DK_TOT:2 * GLA_DK_TOT]
        v_ref[rows, :] = p_qkv[:, 2 * GLA_DK_TOT:].astype(BF16)
        for d in range(2):
            cols = slice(d * GLA_DK_TOT, (d + 1) * GLA_DK_TOT)
            if d == 0:
                bq = pre[:, cols]
                bk = tot[:, cols] - bq
            else:
                bk = pre[:, cols] - la[:, cols]
                bq = tot[:, cols] - bk
            qd_ref[d, rows, :] = (q * jnp.exp2(bq)).astype(BF16)
            kd_ref[d, rows, :] = (k * jnp.exp2(-bq)).astype(BF16)
            kst_ref[d, ti] = jnp.transpose(k * jnp.exp2(bk)).astype(BF16)
            totals = [tot[n * c:n * c + 1, cols] for n in range(tile_chunks)]
            totals.append(jnp.zeros((LANES - tile_chunks, GLA_DK_TOT), F32))
            dect_ref[d, ti] = jnp.transpose(jnp.exp2(jnp.concatenate(totals, axis=0)))
        yield
        p_conv = _dot(h, win_ref[:, :OFF_Q])
        yield
        pos = row_i & (period - 1)
        u = p_conv[:, OFF_XC:OFF_XC + D_CONV] * p_conv[:, OFF_XV:OFF_XV + D_CONV]
        u_prev = jnp.where(pos == 0, 0.0, pltpu.roll(u, 1, 0))
        u_next = jnp.where(pos == period - 1, 0.0, pltpu.roll(u, ROW_TILE - 1, 0))
        conv = u_prev * wconv_ref[0, 0:1, :] + u * wconv_ref[0, 1:2, :] + u_next * wconv_ref[0, 2:3, :] + bconv_ref[...]
        ya_ref[rows, :] = (p_conv[:, OFF_XB:OFF_XB + D_CONV] * conv).astype(BF16)
        yield

    yield from _for_row_tiles(seq_len, stage1, independent=True)

    for d in range(2):
        for pair in range(n_pairs):
            if has_state_in:
                zero = jnp.zeros((GLA_DK, GLA_DV), F32)
                top = jnp.concatenate([s0_refs[d][2 * pair], zero], axis=1)
                bot = jnp.concatenate([zero, s0_refs[d][2 * pair + 1]], axis=1)
                s_ref[d, pair] = jnp.concatenate([top, bot], axis=0)
            else:
                s_ref[d, pair] = jnp.zeros((pair_k, pair_v), F32)

    def scan_tile(i):
        upper_lane = lax.broadcasted_iota(I32, (1, LANES), 1) >= GLA_DK
        qi = lax.broadcasted_iota(I32, (LANES, 1), 0)
        kj = lax.broadcasted_iota(I32, (1, 2 * LANES), 1) & (LANES - 1)
        same_chunk = (qi & c) == (kj & c)
        causal = (same_chunk & (kj <= qi), same_chunk & (kj >= qi))
        for pair in range(n_pairs):
            kl = slice(pair * pair_k, (pair + 1) * pair_k)
            vl = slice(pair * pair_v, (pair + 1) * pair_v)
            for blk in range(ROW_TILE // LANES):
                rows = _tile_rows(i, blk * LANES, LANES)
                att = None
                for d in range(2):
                    kd = kd_ref[d, rows, kl]
                    zk = jnp.zeros_like(kd)
                    keys = jnp.concatenate([jnp.where(upper_lane, zk, kd), jnp.where(upper_lane, kd, zk)], axis=0)
                    a = jnp.where(causal[d], _dot_nt(qd_ref[d, rows, kl], keys), 0.0)
                    att = a if att is None else att + a
                v = v_ref[rows, vl]
                zv = jnp.zeros((LANES, GLA_DV), BF16)
                v_bd = jnp.concatenate([jnp.concatenate([v[:, :GLA_DV], zv], axis=1),
                                        jnp.concatenate([zv, v[:, GLA_DV:]], axis=1)], axis=0)
                o_ref[rows, vl] = _dot(att.astype(BF16), v_bd)
        yield
        key_row = lax.broadcasted_iota(I32, (pair_k, 1), 0)
        val_col = lax.broadcasted_iota(I32, (1, pair_v), 1)
        blockdiag = (key_row >= GLA_DK) == (val_col >= GLA_DV)
        for d in range(2):
            tile = i if d == 0 else n_tiles - 1 - i
            chunks = range(tile_chunks)
            for pair in range(n_pairs):
                kr = slice(pair * pair_k, (pair + 1) * pair_k)
                vl = slice(pair * pair_v, (pair + 1) * pair_v)
                s = s_ref[d, pair]
                for c4 in (chunks if d == 0 else reversed(chunks)):
                    blk, half = divmod(c4, 2)
                    kst = kst_ref[d, tile, kr, blk * LANES:(blk + 1) * LANES]
                    kst = jnp.where(upper_lane if half else ~upper_lane, kst, jnp.zeros_like(kst))
                    kv = jnp.where(blockdiag, _dot(kst, v_ref[_tile_rows(tile, blk * LANES, LANES), vl]), 0.0)
                    sst_ref[pair, tile * tile_chunks + c4, d * pair_k:(d + 1) * pair_k, :] = s.astype(BF16)
                    s = dect_ref[d, tile, kr, c4:c4 + 1] * s + kv
                s_ref[d, pair] = s
        yield

    yield from _for_row_tiles(seq_len, scan_tile, independent=False)

    if has_state_out:
        for d in range(2):
            for pair in range(n_pairs):
                s = s_ref[d, pair]
                sout_refs[d][2 * pair] = s[0:GLA_DK, 0:GLA_DV]
                sout_refs[d][2 * pair + 1] = s[GLA_DK:, GLA_DV:]

    def stage3(i):
        for pair in range(n_pairs):
            kl = slice(pair * pair_k, (pair + 1) * pair_k)
            vl = slice(pair * pair_v, (pair + 1) * pair_v)
            for c4 in range(tile_chunks):
                crow = _tile_rows(i, c4 * c, c)
                q2 = jnp.concatenate([qd_ref[0, crow, kl], qd_ref[1, crow, kl]], axis=1)
                o_ref[crow, vl] = o_ref[crow, vl] + _dot(q2, sst_ref[pair, i * tile_chunks + c4])
        yield
        rows = _tile_rows(i)
        heads = []
        for h in range(GLA_HEADS):
            hl = slice(h * GLA_DV, (h + 1) * GLA_DV)
            oh = o_ref[rows, hl]
            r = lax.rsqrt(jnp.mean(oh * oh, axis=-1, keepdims=True) + EPS)
            heads.append(oh * r * ggla_ref[:, hl])
        y_b = jnp.concatenate(heads, axis=1) * _silu(og_ref[rows, :])
        y = jnp.concatenate([ya_ref[rows, :], y_b.astype(BF16)], axis=1)
        x1_ref[rows, :] = x_ref[rows, :] + m[2] * _dot(y, wout_ref[...])
        yield

    yield from _for_row_tiles(seq_len, stage3, independent=True)


def _mixer_call(x, mod, states, weights, *, period, has_state_out, seqs_per_step):
    n_seq, seq_len, _ = x.shape
    sps = seqs_per_step
    has_state_in = states is not None
    stages_w_in = weights[1].dtype == F32
    kernel = functools.partial(_mixer_kernel, seqs_per_step=sps, seq_len=seq_len, period=period,
                               has_state_in=has_state_in, has_state_out=has_state_out, stages_w_in=stages_w_in)
    state_spec = pl.BlockSpec((sps, None, GLA_HEADS, GLA_DK, GLA_DV), lambda b: (b, 0, 0, 0, 0))
    const2 = lambda b: (0, 0)
    in_specs = [pl.BlockSpec((sps, seq_len, D_MODEL), lambda b: (b, 0, 0)),
                _mod_spec(mod, sps)]
    args = [x, mod]
    if has_state_in:
        in_specs += [state_spec, state_spec]
        args += list(states)
    in_specs += [pl.BlockSpec(w.shape, lambda b, nd=w.ndim: (0,) * nd, pipeline_mode=pl.Buffered(1))
                 for w in weights]
    args += list(weights)
    out_specs = [pl.BlockSpec((sps, seq_len, D_MODEL), lambda b: (b, 0, 0))]
    out_shape = [jax.ShapeDtypeStruct((n_seq, seq_len, D_MODEL), F32)]
    if has_state_out:
        out_specs += [state_spec, state_spec]
        out_shape += [jax.ShapeDtypeStruct((n_seq, 1, GLA_HEADS, GLA_DK, GLA_DV), F32)] * 2
    if stages_w_in:
        out_specs.append(pl.BlockSpec((D_MODEL, P_PAD), const2))
        out_shape.append(jax.ShapeDtypeStruct((D_MODEL, P_PAD), BF16))
    n_tiles = seq_len // ROW_TILE
    n_pairs = GLA_HEADS // 2
    per_seq = lambda shape, dtype: pltpu.VMEM((sps,) + shape, dtype)
    scratch = [per_seq((seq_len, GLA_DV_TOT), F32),
               per_seq((2, seq_len, GLA_DK_TOT), BF16),
               per_seq((2, seq_len, GLA_DK_TOT), BF16),
               per_seq((2, n_tiles, GLA_DK_TOT, ROW_TILE), BF16),
               per_seq((2, n_tiles, GLA_DK_TOT, LANES), F32),
               per_seq((seq_len, GLA_DV_TOT), BF16),
               per_seq((2, n_pairs, 2 * GLA_DK, 2 * GLA_DV), F32),
               per_seq((n_pairs, seq_len // GLA_CHUNK, 4 * GLA_DK, 2 * GLA_DV), BF16),
               per_seq((seq_len, GLA_DV_TOT), F32),
               per_seq((seq_len, D_CONV), BF16)]
    return pl.pallas_call(
        kernel,
        grid=(n_seq // sps,),
        in_specs=in_specs,
        out_specs=out_specs,
        out_shape=out_shape,
        scratch_shapes=scratch,
        compiler_params=_params(1),
        name="mixer",
    )(*args)


SUBLANES = 8


ROW_TOKENS = LANES // N_EXPERTS


def _route_kernel(x1_ref, mod_ref, g2_ref, wr2_ref, *refs, seqs_per_step, **static):
    programs = [_route_sequence(x1_ref.at[j], _mod_rows(mod_ref, j), g2_ref, wr2_ref, *[r.at[j] for r in refs],
                                **static)
                for j in range(seqs_per_step)]
    for _ in _staggered(programs):
        pass


def _route_sequence(x1_ref, m, g2_ref, wr2_ref, xs_ref, rank_ref, gate_ref,
                    pt_ref, h2_ref, spread_ref, bar_ref, cnt_ref, win_ref, *, seq_len, cap):
    n_rows = seq_len // ROW_TOKENS
    for t in range(seq_len // ROW_TILE):
        rows = slice(t * ROW_TILE, (t + 1) * ROW_TILE)
        h2 = _modulated_norm(x1_ref[rows, :], g2_ref[...], m[4], m[3])
        hi = h2.astype(BF16)
        h2_ref[rows, :] = hi
        yield
        by_hi = _dot_nt(wr2_ref[...], hi)
        pt_ref[:, rows] = by_hi[:N_EXPERTS] + by_hi[N_EXPERTS:]
    yield

    logits = pt_ref[...]
    ex = jnp.exp(logits - jnp.max(logits, axis=0, keepdims=True))
    probs = ex / jnp.sum(ex, axis=0, keepdims=True)
    pt_ref[...] = probs
    pad = jnp.zeros((LANES - N_EXPERTS, seq_len), F32)
    p_tok = jnp.transpose(jnp.concatenate([probs, pad], axis=0))

    spread = p_tok
    sh = N_EXPERTS
    while sh < LANES:
        spread = spread + pltpu.roll(spread, sh, 1)
        sh *= 2
    spread_ref[...] = spread
    sub_j = lax.broadcasted_iota(I32, (SUBLANES, 1), 0)
    lane_g = lax.broadcasted_iota(I32, (1, LANES), 1) >> (N_EXPERTS.bit_length() - 1)
    lane_j = (ROW_TOKENS - lane_g) & (ROW_TOKENS - 1)
    own_group = jnp.where(lane_j == sub_j, spread.reshape(n_rows, SUBLANES, LANES), 0.0)
    bar = jnp.sum(own_group, axis=1, keepdims=True)
    bar_ref[...] = jnp.broadcast_to(bar, (n_rows, SUBLANES, LANES))
    cnt_ref[...] = jnp.zeros((n_rows, SUBLANES, LANES), F32)

    rows_per_block = LANES // ROW_TOKENS

    def count_block(g_s, g_t, relation):
        s_blk = spread_ref[g_s * LANES:(g_s + 1) * LANES, :]
        wins = [jnp.zeros((SUBLANES, LANES), F32)] * rows_per_block
        for tl in range(rows_per_block):
            r = g_t * rows_per_block + tl
            bar_r = bar_ref[r]
            acc = cnt_ref[r]
            for sl in range(rows_per_block):
                s_vreg = s_blk[sl * SUBLANES:(sl + 1) * SUBLANES]
                if relation == "before" or sl < tl:
                    won = jnp.where(s_vreg >= bar_r, 1.0, 0.0)
                    acc = acc + won
                    wins[sl] = wins[sl] + won
                elif sl == tl:
                    acc = acc + jnp.where(s_vreg > bar_r, 1.0, 0.0) \
                        + jnp.where((s_vreg == bar_r) & (sub_j < lane_j), 1.0, 0.0)
            cnt_ref[r] = acc
        w_rows = slice(g_s * rows_per_block, (g_s + 1) * rows_per_block)
        win_ref[w_rows] = win_ref[w_rows] + jnp.stack(wins)

    n_grp = seq_len // LANES
    win_ref[...] = jnp.zeros((n_rows, SUBLANES, LANES), F32)
    yield
    for g_t in range(n_grp):
        for g_s in range(g_t + 1):
            count_block(g_s, g_t, "before" if g_s < g_t else "same")
            yield

    counts = jnp.sum(cnt_ref[...], axis=1, keepdims=True)
    counts = jnp.broadcast_to(counts, (n_rows, SUBLANES, LANES)).reshape(seq_len, LANES)
    rank_tok = pltpu.roll(counts, 0, 1, stride=N_EXPERTS, stride_axis=0)
    wins = win_ref[...].reshape(seq_len, LANES)
    sh = N_EXPERTS
    while sh < LANES:
        wins = wins + pltpu.roll(wins, sh, 1)
        sh *= 2
    tok = lax.broadcasted_iota(I32, (seq_len, 1), 0)
    later = ((n_rows - 1 - (tok >> (ROW_TOKENS.bit_length() - 1))) * ROW_TOKENS).astype(F32)
    rank_tok = rank_tok + (later - wins)
    rank_ref[...] = jnp.transpose(rank_tok)[:N_EXPERTS].astype(I32)
    yield

    group = DISPATCH_ROWS // cap
    slot = lax.broadcasted_iota(I32, (cap, 1), 0)
    half = D_MODEL // 2
    for gi in range(N_EXPERTS // group):
        picks = []
        for e in range(gi * group, (gi + 1) * group):
            oh = rank_ref[e:e + 1, :] == slot
            gate = jnp.sum(jnp.where(oh, pt_ref[e:e + 1, :], 0.0), axis=1, keepdims=True)
            gate_ref[e * cap:(e + 1) * cap, :] = jnp.broadcast_to(gate, (cap, LANES))
            picks.append(oh)
        ohb = jnp.where(jnp.concatenate(picks, axis=0), 1.0, 0.0).astype(BF16)
        yield
        out_rows = slice(gi * DISPATCH_ROWS, (gi + 1) * DISPATCH_ROWS)
        xs_ref[out_rows, :half] = _dot(ohb, h2_ref[:, :half]).astype(BF16)
        xs_ref[out_rows, half:] = _dot(ohb, h2_ref[:, half:]).astype(BF16)
        yield


def _route_call(x1, mod, g2, wr_both, *, seqs_per_step):
    n_seq, seq_len, _ = x1.shape
    sps = seqs_per_step
    cap = EC_CAPACITY_FACTOR * seq_len // N_EXPERTS
    kernel = functools.partial(_route_kernel, seqs_per_step=sps, seq_len=seq_len, cap=cap)
    const2 = lambda b: (0, 0)
    per_seq = lambda shape, dtype: pltpu.VMEM((sps,) + shape, dtype)
    rank_rows = (seq_len // ROW_TOKENS, SUBLANES, LANES)
    return pl.pallas_call(
        kernel,
        grid=(n_seq // sps,),
        in_specs=[pl.BlockSpec((sps, seq_len, D_MODEL), lambda b: (b, 0, 0)),
                  _mod_spec(mod, sps),
                  pl.BlockSpec((1, D_MODEL), const2),
                  pl.BlockSpec((2 * N_EXPERTS, D_MODEL), const2)],
        out_specs=[pl.BlockSpec((sps, N_EXPERTS * cap, D_MODEL), lambda b: (b, 0, 0)),
                   pl.BlockSpec((sps, N_EXPERTS, seq_len), lambda b: (b, 0, 0)),
                   pl.BlockSpec((sps, N_EXPERTS * cap, LANES), lambda b: (b, 0, 0))],
        out_shape=[jax.ShapeDtypeStruct((n_seq, N_EXPERTS * cap, D_MODEL), BF16),
                   jax.ShapeDtypeStruct((n_seq, N_EXPERTS, seq_len), I32),
                   jax.ShapeDtypeStruct((n_seq, N_EXPERTS * cap, LANES), F32)],
        scratch_shapes=[per_seq((N_EXPERTS, seq_len), F32),
                        per_seq((seq_len, D_MODEL), BF16),
                        per_seq((seq_len, LANES), F32),
                        per_seq(rank_rows, F32),
                        per_seq(rank_rows, F32),
                        per_seq(rank_rows, F32)],
        compiler_params=_params(1),
        name="route",
    )(x1, mod, g2, wr_both)


def _experts_kernel(xc_ref, xl_ref, gc_ref, gl_ref, wg_ref, wu_ref, wd_ref, yc_ref, yl_ref,
                    wgb_ref, wub_ref, wdb_ref, a_ref):
    wgb_ref[...] = wg_ref[...].astype(BF16)
    wub_ref[...] = wu_ref[...].astype(BF16)
    wdb_ref[...] = wd_ref[...].astype(BF16)
    f_tile = 2 * LANES

    def run(x_ref, g_ref, y_ref):
        n_seq, cap, _ = x_ref.shape
        seqs = DISPATCH_ROWS // cap
        for s0 in range(0, n_seq, seqs):
            x = x_ref[s0:s0 + seqs].reshape(DISPATCH_ROWS, D_MODEL)
            for f0 in range(0, D_EXPERT, f_tile):
                cols = slice(f0, f0 + f_tile)
                a_ref[:, cols] = (_silu(_dot(x, wgb_ref[:, cols])) * _dot(x, wub_ref[:, cols])).astype(BF16)
            gate = g_ref[s0:s0 + seqs].reshape(DISPATCH_ROWS, LANES)
            y = _dot(a_ref[...], wdb_ref[...]) * jnp.concatenate([gate] * (D_MODEL // LANES), axis=1)
            y_ref[s0:s0 + seqs] = y.astype(BF16).reshape(seqs, cap, D_MODEL)

    run(xc_ref, gc_ref, yc_ref)
    run(xl_ref, gl_ref, yl_ref)


def _experts_call(xs_ctx, xs_lat, gates_ctx, gates_lat, w_gate, w_up, w_down):
    def slot_spec(a):
        n_seq, _, cap, width = a.shape
        return pl.BlockSpec((n_seq, None, cap, width), lambda e: (0, e, 0, 0))

    w_spec = pl.BlockSpec((None, D_MODEL, D_EXPERT), lambda e: (e, 0, 0))
    return pl.pallas_call(
        _experts_kernel,
        grid=(N_EXPERTS,),
        in_specs=[slot_spec(xs_ctx), slot_spec(xs_lat), slot_spec(gates_ctx), slot_spec(gates_lat),
                  w_spec, w_spec, pl.BlockSpec((None, D_EXPERT, D_MODEL), lambda e: (e, 0, 0))],
        out_specs=[slot_spec(xs_ctx), slot_spec(xs_lat)],
        out_shape=[jax.ShapeDtypeStruct(xs_ctx.shape, BF16), jax.ShapeDtypeStruct(xs_lat.shape, BF16)],
        scratch_shapes=[pltpu.VMEM((D_MODEL, D_EXPERT), BF16),
                        pltpu.VMEM((D_MODEL, D_EXPERT), BF16),
                        pltpu.VMEM((D_EXPERT, D_MODEL), BF16),
                        pltpu.VMEM((DISPATCH_ROWS, D_EXPERT), BF16)],
        compiler_params=_params(1),
        name="experts",
    )(xs_ctx, xs_lat, gates_ctx, gates_lat, w_gate, w_up, w_down)


def _combine_kernel(x1_ref, y_ref, rank_ref, mod_ref, gf_ref, o_ref, oh_ref, *, seqs_per_step, seq_len, cap):
    slot = lax.broadcasted_iota(I32, (cap, 1), 0)
    for j in range(seqs_per_step):
        m = _mod_rows(mod_ref, j)
        for e in range(N_EXPERTS):
            oh_ref[j, e * cap:(e + 1) * cap, :] = jnp.where(rank_ref[j, e:e + 1, :] == slot, 1.0, 0.0).astype(BF16)
        for t in range(seq_len // ROW_TILE):
            rows = slice(t * ROW_TILE, (t + 1) * ROW_TILE)
            moe = _dot_tn(oh_ref[j, :, rows], y_ref[j])
            x2 = x1_ref[j, rows, :] + m[5] * moe
            r = lax.rsqrt(jnp.mean(x2 * x2, axis=-1, keepdims=True) + EPS)
            o_ref[j, rows, :] = (x2 * r) * gf_ref[...]


def _combine_call(x1, y, rank, mod, g_final, *, seqs_per_step):
    n_seq, seq_len, _ = x1.shape
    n_slots = y.shape[1]
    sps = seqs_per_step
    kernel = functools.partial(_combine_kernel, seqs_per_step=sps, seq_len=seq_len, cap=n_slots // N_EXPERTS)
    seq_spec = pl.BlockSpec((sps, seq_len, D_MODEL), lambda b: (b, 0, 0))
    return pl.pallas_call(
        kernel,
        grid=(n_seq // sps,),
        in_specs=[seq_spec,
                  pl.BlockSpec((sps, n_slots, D_MODEL), lambda b: (b, 0, 0)),
                  pl.BlockSpec((sps, N_EXPERTS, seq_len), lambda b: (b, 0, 0)),
                  _mod_spec(mod, sps),
                  pl.BlockSpec((1, D_MODEL), lambda b: (0, 0))],
        out_specs=seq_spec,
        out_shape=jax.ShapeDtypeStruct((n_seq, seq_len, D_MODEL), F32),
        scratch_shapes=[pltpu.VMEM((sps, n_slots, seq_len), BF16)],
        compiler_params=_params(1),
        name="combine",
    )(x1, y, rank, mod, g_final)


def kernel(x_prompt, x_sample, state_gla_fwd, state_gla_bwd, c, c_ctx, w_mod, b_mod, g_norm1, g_norm2,
           w_in, w_conv, b_conv, w_a_up_f, b_a_f, w_a_up_b, b_a_b, g_gla_norm, w_out, w_router,
           w_gate, w_up, w_down, g_final):
    assert w_mod.shape[0] == 1, "single trunk layer"
    n_ctx, ctx_len, _ = x_prompt.shape
    n_lat, lat_len, _ = x_sample.shape
    ctx_cap = EC_CAPACITY_FACTOR * ctx_len // N_EXPERTS
    lat_cap = EC_CAPACITY_FACTOR * lat_len // N_EXPERTS

    c_rows = jnp.concatenate([c_ctx[None, :], c, jnp.zeros((8 - 1 - n_lat, D_MODEL), F32)], axis=0)
    mod_ctx, mod_lat = _mod_call(c_rows, n_lat, w_mod[0], b_mod)

    mixer_weights = [g_norm1, jnp.transpose(w_in[0]), w_conv, b_conv, w_a_up_f, b_a_f, w_a_up_b, b_a_b,
                     g_gla_norm[0].reshape(1, GLA_DV_TOT), w_out[0].astype(BF16)]
    assert len(mixer_weights) == N_MIXER_WEIGHTS
    wr_t = jnp.transpose(w_router[0])
    wr_hi = wr_t.astype(BF16)
    wr_lo = (wr_t - wr_hi.astype(F32)).astype(BF16)

    x1_ctx, new_f, new_b, w_in_bf = _mixer_call(x_prompt, mod_ctx, None, mixer_weights,
                                                period=ctx_len, has_state_out=True, seqs_per_step=4)
    mixer_weights[1] = w_in_bf
    (x1_lat,) = _mixer_call(x_sample, mod_lat, (state_gla_fwd, state_gla_bwd), mixer_weights,
                            period=GRID_W, has_state_out=False, seqs_per_step=1)

    wr_both = jnp.concatenate([wr_hi, wr_lo], axis=0)
    xs_ctx, rank_ctx, gates_ctx = _route_call(x1_ctx, mod_ctx, g_norm2, wr_both, seqs_per_step=4)
    xs_lat, rank_lat, gates_lat = _route_call(x1_lat, mod_lat, g_norm2, wr_both, seqs_per_step=1)

    per_expert = lambda a, n, cap: a.reshape(n, N_EXPERTS, cap, a.shape[-1])
    y_ctx, y_lat = _experts_call(per_expert(xs_ctx, n_ctx, ctx_cap), per_expert(xs_lat, n_lat, lat_cap),
                                 per_expert(gates_ctx, n_ctx, ctx_cap), per_expert(gates_lat, n_lat, lat_cap),
                                 w_gate[0], w_up[0], w_down[0])

    g_fin = g_final[None, :]
    y_prompt = _combine_call(x1_ctx, y_ctx.reshape(xs_ctx.shape), rank_ctx, mod_ctx, g_fin, seqs_per_step=4)
    y_sample = _combine_call(x1_lat, y_lat.reshape(xs_lat.shape), rank_lat, mod_lat, g_fin, seqs_per_step=1)
    return y_prompt, y_sample, new_f, new_b
```

```python
import functools

import jax
import jax.numpy as jnp
from jax import lax
from jax.experimental import pallas as pl
from jax.experimental.pallas import tpu as pltpu

F32 = jnp.float32
BF16 = jnp.bfloat16
I32 = jnp.int32

D_MODEL = 1024
D_CONV = D_MODEL // 2
GRID_W = 64
GLA_HEADS = 4
GLA_DK = 64
GLA_DV = 128
GLA_DK_TOT = GLA_HEADS * GLA_DK
GLA_DV_TOT = GLA_HEADS * GLA_DV
GLA_LOW_RANK = 16
GLA_TAU = 16.0
GLA_CHUNK = 64
N_EXPERTS = 16
EC_CAPACITY_FACTOR = 2
D_EXPERT = 1024
N_MOD = 6
EPS = 1e-6
LOG2_E = 1.4426950408889634

OFF_XB = 0
OFF_XC = D_CONV
OFF_XV = 2 * D_CONV
OFF_Q = 3 * D_CONV
OFF_K = OFF_Q + GLA_DK_TOT
OFF_V = OFF_K + GLA_DK_TOT
OFF_OG = OFF_V + GLA_DV_TOT
OFF_ALOW = OFF_OG + GLA_DV_TOT
P_TOT = OFF_ALOW + 2 * GLA_LOW_RANK

LANES = 128
P_PAD = -(-P_TOT // LANES) * LANES
ROW_TILE = 256
N_MIXER_WEIGHTS = 10
DISPATCH_ROWS = 512
VMEM_LIMIT = 56 * 1024 * 1024


def _dot(a, b):
    return jnp.dot(a, b, preferred_element_type=F32)


def _dot_nt(a, b):
    return lax.dot_general(a, b, (((1,), (1,)), ((), ())), preferred_element_type=F32)


def _dot_tn(a, b):
    return lax.dot_general(a, b, (((0,), (0,)), ((), ())), preferred_element_type=F32)


def _split(a):
    hi = a.astype(BF16)
    lo = (a - hi.astype(F32)).astype(BF16)
    return hi, lo


def _silu(x):
    return x * jax.nn.sigmoid(x)


def _modulated_norm(x, g, scale, shift):
    r = lax.rsqrt(jnp.mean(x * x, axis=-1, keepdims=True) + EPS)
    return (x * r) * (g * (1.0 + scale)) + shift


def _params(n_axes):
    return pltpu.CompilerParams(dimension_semantics=("arbitrary",) * n_axes,
                                vmem_limit_bytes=VMEM_LIMIT)


def _mod_kernel(c_ref, w_top_ref, w_bot_ref, b_ref, ctx_ref, lat_ref, acc_ref):
    rows = c_ref.shape[0]
    s = _silu(c_ref[...])
    s_hi, s_lo = _split(jnp.concatenate([s, s], axis=0))
    upper = lax.broadcasted_iota(I32, (2 * rows, 1), 0) < rows
    w_hi, w_lo = _split(jnp.concatenate([w_top_ref[...], w_bot_ref[...]], axis=0))
    by_hi = _dot(jnp.where(upper, s_hi, s_lo), w_hi)
    part = by_hi[:rows] + by_hi[rows:] + _dot(s_hi[:rows], w_lo)

    @pl.when(pl.program_id(0) == 0)
    def _():
        acc_ref[...] = part + b_ref[...]

    @pl.when(pl.program_id(0) != 0)
    def _():
        acc_ref[...] = acc_ref[...] + part

    @pl.when(pl.program_id(0) == pl.num_programs(0) - 1)
    def _():
        ctx_ref[0] = acc_ref[0:1, :]
        for i in range(lat_ref.shape[0]):
            lat_ref[i] = acc_ref[1 + i:2 + i, :]


def _mod_call(c_rows, n_lat, w_mod, b_mod):
    rows, d = c_rows.shape
    n = w_mod.shape[1]
    tk = D_MODEL // 4
    return pl.pallas_call(
        _mod_kernel,
        grid=(d // tk,),
        in_specs=[pl.BlockSpec((rows, tk), lambda k: (0, k)),
                  pl.BlockSpec((tk // 2, n), lambda k: (2 * k, 0)),
                  pl.BlockSpec((tk // 2, n), lambda k: (2 * k + 1, 0)),
                  pl.BlockSpec((1, n), lambda k: (0, 0))],
        out_specs=[pl.BlockSpec((1, 1, n), lambda k: (0, 0, 0)),
                   pl.BlockSpec((n_lat, 1, n), lambda k: (0, 0, 0))],
        out_shape=[jax.ShapeDtypeStruct((1, 1, n), F32), jax.ShapeDtypeStruct((n_lat, 1, n), F32)],
        scratch_shapes=[pltpu.VMEM((rows, n), F32)],
        compiler_params=_params(1),
        name="mod",
    )(c_rows, w_mod, w_mod, b_mod)


def _mod_rows(mod_ref, j):
    row = mod_ref[j % mod_ref.shape[0]]
    return [row[:, i * D_MODEL:(i + 1) * D_MODEL] for i in range(N_MOD)]


def _mod_spec(mod, seqs_per_step):
    if mod.shape[0] == 1:
        return pl.BlockSpec(mod.shape, lambda b: (0, 0, 0))
    return pl.BlockSpec((seqs_per_step,) + mod.shape[1:], lambda b: (b, 0, 0))


def _staggered(programs):
    programs = list(programs)
    started = 0
    while programs:
        started = min(started + 1, len(programs))
        running = [p for p in programs[:started] if next(p, "done") != "done"]
        programs = running + programs[started:]
        started = len(running)
        yield


def _for_row_tiles(seq_len, phases, independent):
    n = seq_len // ROW_TILE
    if independent:
        yield from _staggered(phases(i) for i in range(n))
    else:
        for i in range(n):
            yield from phases(i)


def _tile_rows(tile, offset=0, size=ROW_TILE):
    if isinstance(tile, int):
        return pl.ds(tile * ROW_TILE + offset, size)
    return pl.ds(pl.multiple_of(tile * ROW_TILE + offset, size), size)


def _mixer_kernel(*refs, seqs_per_step, has_state_in, has_state_out, stages_w_in, **static):
    refs = list(refs)
    n_in = 2 + (2 if has_state_in else 0)
    per_seq_in, refs = [refs[0]] + refs[2:n_in], [refs[1]] + refs[n_in:]
    mod_ref, weights, refs = refs[0], refs[1:1 + N_MIXER_WEIGHTS], refs[1 + N_MIXER_WEIGHTS:]
    n_out = 1 + (2 if has_state_out else 0)
    per_seq_out, refs = refs[:n_out], refs[n_out:]
    if stages_w_in:
        win_f32_ref, win_bf_ref, scratch = weights[1], refs[0], refs[1:]
        weights = weights[:1] + [win_bf_ref] + weights[2:]

        @pl.when(pl.program_id(0) == 0)
        def _():
            for c0 in range(0, P_PAD, LANES):
                n = min(LANES, P_TOT - c0)
                cols = win_f32_ref[c0:c0 + n, :]
                if n < LANES:
                    cols = jnp.concatenate([cols, jnp.zeros((LANES - n, D_MODEL), F32)], axis=0)
                win_bf_ref[:, c0:c0 + LANES] = jnp.transpose(cols).astype(BF16)
    else:
        scratch = refs
    programs = []
    for j in range(seqs_per_step):
        ins = [r.at[j] for r in per_seq_in]
        outs = [r.at[j] for r in per_seq_out]
        programs.append(_mixer_sequence(ins[0], _mod_rows(mod_ref, j), ins[1:], weights, outs[0], outs[1:],
                                        [r.at[j] for r in scratch], **static))
    for _ in _staggered(programs):
        pass


def _mixer_sequence(x_ref, m, s0_refs, weights, x1_ref, sout_refs, scratch, *, seq_len, period):
    has_state_in = bool(s0_refs)
    has_state_out = bool(sout_refs)
    g1_ref, win_ref, wconv_ref, bconv_ref, wupf_ref, bupf_ref, wupb_ref, bupb_ref, ggla_ref, wout_ref = weights
    og_ref, qd_ref, kd_ref, kst_ref, dect_ref, v_ref, s_ref, sst_ref, o_ref, ya_ref = scratch

    c = GLA_CHUNK
    tile_chunks = ROW_TILE // c
    n_tiles = seq_len // ROW_TILE
    n_pairs = GLA_HEADS // 2
    pair_k = 2 * GLA_DK
    pair_v = 2 * GLA_DV

    def stage1(ti):
        rows = _tile_rows(ti)
        h = _modulated_norm(x_ref[rows, :], g1_ref[...], m[1], m[0]).astype(BF16)
        row_i = lax.broadcasted_iota(I32, (ROW_TILE, 1), 0)
        yield
        p_gate = _dot(h, win_ref[:, OFF_OG:P_PAD])
        og_ref[rows, :] = p_gate[:, :GLA_DV_TOT]
        yield
        zero_up = jnp.zeros((GLA_LOW_RANK, GLA_DK_TOT), F32)
        w_up = jnp.concatenate([jnp.concatenate([wupf_ref[0], zero_up], axis=1),
                                jnp.concatenate([zero_up, wupb_ref[0]], axis=1),
                                jnp.zeros((P_PAD - P_TOT, 2 * GLA_DK_TOT), F32)], axis=0).astype(BF16)
        b_up = jnp.concatenate([bupf_ref[...], bupb_ref[...]], axis=1)
        z = _dot(p_gate[:, GLA_DV_TOT:].astype(BF16), w_up) + b_up
        la = (jnp.minimum(z, 0.0) - jnp.log(1.0 + jnp.exp(-jnp.abs(z)))) * (LOG2_E / GLA_TAU)
        col_j = lax.broadcasted_iota(I32, (1, ROW_TILE), 1)
        same_chunk = (row_i & -c) == (col_j & -c)
        lower = jnp.where(same_chunk & (col_j <= row_i), 1.0, 0.0).astype(BF16)
        la_parts = jnp.concatenate(_split(la), axis=1)
        n_gate = 2 * GLA_DK_TOT
        yield
        pre = _dot(lower, la_parts)
        pre = pre[:, :n_gate] + pre[:, n_gate:]
        tot = jnp.concatenate([jnp.broadcast_to(pre[(n + 1) * c - 1:(n + 1) * c], (c, n_gate))
                               for n in range(tile_chunks)], axis=0)
        p_qkv = _dot(h, win_ref[:, OFF_Q:OFF_OG])
        yield
        q = p_qkv[:, :GLA_DK_TOT] * (GLA_DK ** -0.5)
        k = p_qkv[:, GLA_DK_TOT:2 * GLA_DK_TOT]
        v_ref[rows, :] = p_qkv[:, 2 * GLA_DK_TOT:].astype(BF16)
        for d in range(2):
            cols = slice(d * GLA_DK_TOT, (d + 1) * GLA_DK_TOT)
            if d == 0:
                bq = pre[:, cols]
                bk = tot[:, cols] - bq
            else:
                bk = pre[:, cols] - la[:, cols]
                bq = tot[:, cols] - bk
            qd_ref[d, rows, :] = (q * jnp.exp2(bq)).astype(BF16)
            kd_ref[d, rows, :] = (k * jnp.exp2(-bq)).astype(BF16)
            kst_ref[d, ti] = jnp.transpose(k * jnp.exp2(bk)).astype(BF16)
            totals = [tot[n * c:n * c + 1, cols] for n in range(tile_chunks)]
            totals.append(jnp.zeros((LANES - tile_chunks, GLA_DK_TOT), F32))
            dect_ref[d, ti] = jnp.transpose(jnp.exp2(jnp.concatenate(totals, axis=0)))
        yield
        p_conv = _dot(h, win_ref[:, :OFF_Q])
        yield
        pos = row_i & (period - 1)
        u = p_conv[:, OFF_XC:OFF_XC + D_CONV] * p_conv[:, OFF_XV:OFF_XV + D_CONV]
        u_prev = jnp.where(pos == 0, 0.0, pltpu.roll(u, 1, 0))
        u_next = jnp.where(pos == period - 1, 0.0, pltpu.roll(u, ROW_TILE - 1, 0))
        conv = u_prev * wconv_ref[0, 0:1, :] + u * wconv_ref[0, 1:2, :] + u_next * wconv_ref[0, 2:3, :] + bconv_ref[...]
        ya_ref[rows, :] = (p_conv[:, OFF_XB:OFF_XB + D_CONV] * conv).astype(BF16)
        yield

    yield from _for_row_tiles(seq_len, stage1, independent=True)

    for d in range(2):
        for pair in range(n_pairs):
            if has_state_in:
                zero = jnp.zeros((GLA_DK, GLA_DV), F32)
                top = jnp.concatenate([s0_refs[d][2 * pair], zero], axis=1)
                bot = jnp.concatenate([zero, s0_refs[d][2 * pair + 1]], axis=1)
                s_ref[d, pair] = jnp.concatenate([top, bot], axis=0)
            else:
                s_ref[d, pair] = jnp.zeros((pair_k, pair_v), F32)

    def scan_tile(i):
        upper_lane = lax.broadcasted_iota(I32, (1, LANES), 1) >= GLA_DK
        qi = lax.broadcasted_iota(I32, (LANES, 1), 0)
        kj = lax.broadcasted_iota(I32, (1, 2 * LANES), 1) & (LANES - 1)
        same_chunk = (qi & c) == (kj & c)
        causal = (same_chunk & (kj <= qi), same_chunk & (kj >= qi))
        for pair in range(n_pairs):
            kl = slice(pair * pair_k, (pair + 1) * pair_k)
            vl = slice(pair * pair_v, (pair + 1) * pair_v)
            for blk in range(ROW_TILE // LANES):
                rows = _tile_rows(i, blk * LANES, LANES)
                att = None
                for d in range(2):
                    kd = kd_ref[d, rows, kl]
                    zk = jnp.zeros_like(kd)
                    keys = jnp.concatenate([jnp.where(upper_lane, zk, kd), jnp.where(upper_lane, kd, zk)], axis=0)
                    a = jnp.where(causal[d], _dot_nt(qd_ref[d, rows, kl], keys), 0.0)
                    att = a if att is None else att + a
                v = v_ref[rows, vl]
                zv = jnp.zeros((LANES, GLA_DV), BF16)
                v_bd = jnp.concatenate([jnp.concatenate([v[:, :GLA_DV], zv], axis=1),
                                        jnp.concatenate([zv, v[:, GLA_DV:]], axis=1)], axis=0)
                o_ref[rows, vl] = _dot(att.astype(BF16), v_bd)
        yield
        key_row = lax.broadcasted_iota(I32, (pair_k, 1), 0)
        val_col = lax.broadcasted_iota(I32, (1, pair_v), 1)
        blockdiag = (key_row >= GLA_DK) == (val_col >= GLA_DV)
        for d in range(2):
            tile = i if d == 0 else n_tiles - 1 - i
            chunks = range(tile_chunks)
            for pair in range(n_pairs):
                kr = slice(pair * pair_k, (pair + 1) * pair_k)
                vl = slice(pair * pair_v, (pair + 1) * pair_v)
                s = s_ref[d, pair]
                for c4 in (chunks if d == 0 else reversed(chunks)):
                    blk, half = divmod(c4, 2)
                    kst = kst_ref[d, tile, kr, blk * LANES:(blk + 1) * LANES]
                    kst = jnp.where(upper_lane if half else ~upper_lane, kst, jnp.zeros_like(kst))
                    kv = jnp.where(blockdiag, _dot(kst, v_ref[_tile_rows(tile, blk * LANES, LANES), vl]), 0.0)
                    sst_ref[pair, tile * tile_chunks + c4, d * pair_k:(d + 1) * pair_k, :] = s.astype(BF16)
                    s = dect_ref[d, tile, kr, c4:c4 + 1] * s + kv
                s_ref[d, pair] = s
        yield

    yield from _for_row_tiles(seq_len, scan_tile, independent=False)

    if has_state_out:
        for d in range(2):
            for pair in range(n_pairs):
                s = s_ref[d, pair]
                sout_refs[d][2 * pair] = s[0:GLA_DK, 0:GLA_DV]
                sout_refs[d][2 * pair + 1] = s[GLA_DK:, GLA_DV:]

    def stage3(i):
        for pair in range(n_pairs):
            kl = slice(pair * pair_k, (pair + 1) * pair_k)
            vl = slice(pair * pair_v, (pair + 1) * pair_v)
            for c4 in range(tile_chunks):
                crow = _tile_rows(i, c4 * c, c)
                q2 = jnp.concatenate([qd_ref[0, crow, kl], qd_ref[1, crow, kl]], axis=1)
                o_ref[crow, vl] = o_ref[crow, vl] + _dot(q2, sst_ref[pair, i * tile_chunks + c4])
        yield
        rows = _tile_rows(i)
        heads = []
        for h in range(GLA_HEADS):
            hl = slice(h * GLA_DV, (h + 1) * GLA_DV)
            oh = o_ref[rows, hl]
            r = lax.rsqrt(jnp.mean(oh * oh, axis=-1, keepdims=True) + EPS)
            heads.append(oh * r * ggla_ref[:, hl])
        y_b = jnp.concatenate(heads, axis=1) * _silu(og_ref[rows, :])
        y = jnp.concatenate([ya_ref[rows, :], y_b.astype(BF16)], axis=1)
        x1_ref[rows, :] = x_ref[rows, :] + m[2] * _dot(y, wout_ref[...])
        yield

    yield from _for_row_tiles(seq_len, stage3, independent=True)


def _mixer_call(x, mod, states, weights, *, period, has_state_out, seqs_per_step):
    n_seq, seq_len, _ = x.shape
    sps = seqs_per_step
    has_state_in = states is not None
    stages_w_in = weights[1].dtype == F32
    kernel = functools.partial(_mixer_kernel, seqs_per_step=sps, seq_len=seq_len, period=period,
                               has_state_in=has_state_in, has_state_out=has_state_out, stages_w_in=stages_w_in)
    state_spec = pl.BlockSpec((sps, None, GLA_HEADS, GLA_DK, GLA_DV), lambda b: (b, 0, 0, 0, 0))
    const2 = lambda b: (0, 0)
    in_specs = [pl.BlockSpec((sps, seq_len, D_MODEL), lambda b: (b, 0, 0)),
                _mod_spec(mod, sps)]
    args = [x, mod]
    if has_state_in:
        in_specs += [state_spec, state_spec]
        args += list(states)
    in_specs += [pl.BlockSpec(w.shape, lambda b, nd=w.ndim: (0,) * nd, pipeline_mode=pl.Buffered(1))
                 for w in weights]
    args += list(weights)
    out_specs = [pl.BlockSpec((sps, seq_len, D_MODEL), lambda b: (b, 0, 0))]
    out_shape = [jax.ShapeDtypeStruct((n_seq, seq_len, D_MODEL), F32)]
    if has_state_out:
        out_specs += [state_spec, state_spec]
        out_shape += [jax.ShapeDtypeStruct((n_seq, 1, GLA_HEADS, GLA_DK, GLA_DV), F32)] * 2
    if stages_w_in:
        out_specs.append(pl.BlockSpec((D_MODEL, P_PAD), const2))
        out_shape.append(jax.ShapeDtypeStruct((D_MODEL, P_PAD), BF16))
    n_tiles = seq_len // ROW_TILE
    n_pairs = GLA_HEADS // 2
    per_seq = lambda shape, dtype: pltpu.VMEM((sps,) + shape, dtype)
    scratch = [per_seq((seq_len, GLA_DV_TOT), F32),
               per_seq((2, seq_len, GLA_DK_TOT), BF16),
               per_seq((2, seq_len, GLA_DK_TOT), BF16),
               per_seq((2, n_tiles, GLA_DK_TOT, ROW_TILE), BF16),
               per_seq((2, n_tiles, GLA_DK_TOT, LANES), F32),
               per_seq((seq_len, GLA_DV_TOT), BF16),
               per_seq((2, n_pairs, 2 * GLA_DK, 2 * GLA_DV), F32),
               per_seq((n_pairs, seq_len // GLA_CHUNK, 4 * GLA_DK, 2 * GLA_DV), BF16),
               per_seq((seq_len, GLA_DV_TOT), F32),
               per_seq((seq_len, D_CONV), BF16)]
    return pl.pallas_call(
        kernel,
        grid=(n_seq // sps,),
        in_specs=in_specs,
        out_specs=out_specs,
        out_shape=out_shape,
        scratch_shapes=scratch,
        compiler_params=_params(1),
        name="mixer",
    )(*args)


SUBLANES = 8


ROW_TOKENS = LANES // N_EXPERTS


def _route_kernel(x1_ref, mod_ref, g2_ref, wr2_ref, *refs, seqs_per_step, **static):
    programs = [_route_sequence(x1_ref.at[j], _mod_rows(mod_ref, j), g2_ref, wr2_ref, *[r.at[j] for r in refs],
                                **static)
                for j in range(seqs_per_step)]
    for _ in _staggered(programs):
        pass


def _route_sequence(x1_ref, m, g2_ref, wr2_ref, xs_ref, rank_ref, gate_ref,
                    pt_ref, h2_ref, spread_ref, bar_ref, cnt_ref, win_ref, *, seq_len, cap):
    n_rows = seq_len // ROW_TOKENS
    for t in range(seq_len // ROW_TILE):
        rows = slice(t * ROW_TILE, (t + 1) * ROW_TILE)
        h2 = _modulated_norm(x1_ref[rows, :], g2_ref[...], m[4], m[3])
        hi = h2.astype(BF16)
        h2_ref[rows, :] = hi
        yield
        by_hi = _dot_nt(wr2_ref[...], hi)
        pt_ref[:, rows] = by_hi[:N_EXPERTS] + by_hi[N_EXPERTS:]
    yield

    logits = pt_ref[...]
    ex = jnp.exp(logits - jnp.max(logits, axis=0, keepdims=True))
    probs = ex / jnp.sum(ex, axis=0, keepdims=True)
    pt_ref[...] = probs
    pad = jnp.zeros((LANES - N_EXPERTS, seq_len), F32)
    p_tok = jnp.transpose(jnp.concatenate([probs, pad], axis=0))

    spread = p_tok
    sh = N_EXPERTS
    while sh < LANES:
        spread = spread + pltpu.roll(spread, sh, 1)
        sh *= 2
    spread_ref[...] = spread
    sub_j = lax.broadcasted_iota(I32, (SUBLANES, 1), 0)
    lane_g = lax.broadcasted_iota(I32, (1, LANES), 1) >> (N_EXPERTS.bit_length() - 1)
    lane_j = (ROW_TOKENS - lane_g) & (ROW_TOKENS - 1)
    own_group = jnp.where(lane_j == sub_j, spread.reshape(n_rows, SUBLANES, LANES), 0.0)
    bar = jnp.sum(own_group, axis=1, keepdims=True)
    bar_ref[...] = jnp.broadcast_to(bar, (n_rows, SUBLANES, LANES))
    cnt_ref[...] = jnp.zeros((n_rows, SUBLANES, LANES), F32)

    rows_per_block = LANES // ROW_TOKENS

    def count_block(g_s, g_t, relation):
        s_blk = spread_ref[g_s * LANES:(g_s + 1) * LANES, :]
        wins = [jnp.zeros((SUBLANES, LANES), F32)] * rows_per_block
        for tl in range(rows_per_block):
            r = g_t * rows_per_block + tl
            bar_r = bar_ref[r]
            acc = cnt_ref[r]
            for sl in range(rows_per_block):
                s_vreg = s_blk[sl * SUBLANES:(sl + 1) * SUBLANES]
                if relation == "before" or sl < tl:
                    won = jnp.where(s_vreg >= bar_r, 1.0, 0.0)
                    acc = acc + won
                    wins[sl] = wins[sl] + won
                elif sl == tl:
                    acc = acc + jnp.where(s_vreg > bar_r, 1.0, 0.0) \
                        + jnp.where((s_vreg == bar_r) & (sub_j < lane_j), 1.0, 0.0)
            cnt_ref[r] = acc
        w_rows = slice(g_s * rows_per_block, (g_s + 1) * rows_per_block)
        win_ref[w_rows] = win_ref[w_rows] + jnp.stack(wins)

    n_grp = seq_len // LANES
    win_ref[...] = jnp.zeros((n_rows, SUBLANES, LANES), F32)
    yield
    for g_t in range(n_grp):
        for g_s in range(g_t + 1):
            count_block(g_s, g_t, "before" if g_s < g_t else "same")
            yield

    counts = jnp.sum(cnt_ref[...], axis=1, keepdims=True)
    counts = jnp.broadcast_to(counts, (n_rows, SUBLANES, LANES)).reshape(seq_len, LANES)
    rank_tok = pltpu.roll(counts, 0, 1, stride=N_EXPERTS, stride_axis=0)
    wins = win_ref[...].reshape(seq_len, LANES)
    sh = N_EXPERTS
    while sh < LANES:
        wins = wins + pltpu.roll(wins, sh, 1)
        sh *= 2
    tok = lax.broadcasted_iota(I32, (seq_len, 1), 0)
    later = ((n_rows - 1 - (tok >> (ROW_TOKENS.bit_length() - 1))) * ROW_TOKENS).astype(F32)
    rank_tok = rank_tok + (later - wins)
    rank_ref[...] = jnp.transpose(rank_tok)[:N_EXPERTS].astype(I32)
    yield

    group = DISPATCH_ROWS // cap
    slot = lax.broadcasted_iota(I32, (cap, 1), 0)
    half = D_MODEL // 2
    for gi in range(N_EXPERTS // group):
        picks = []
        for e in range(gi * group, (gi + 1) * group):
            oh = rank_ref[e:e + 1, :] == slot
            gate = jnp.sum(jnp.where(oh, pt_ref[e:e + 1, :], 0.0), axis=1, keepdims=True)
            gate_ref[e * cap:(e + 1) * cap, :] = jnp.broadcast_to(gate, (cap, LANES))
            picks.append(oh)
        ohb = jnp.where(jnp.concatenate(picks, axis=0), 1.0, 0.0).astype(BF16)
        yield
        out_rows = slice(gi * DISPATCH_ROWS, (gi + 1) * DISPATCH_ROWS)
        xs_ref[out_rows, :half] = _dot(ohb, h2_ref[:, :half]).astype(BF16)
        xs_ref[out_rows, half:] = _dot(ohb, h2_ref[:, half:]).astype(BF16)
        yield


def _route_call(x1, mod, g2, wr_both, *, seqs_per_step):
    n_seq, seq_len, _ = x1.shape
    sps = seqs_per_step
    cap = EC_CAPACITY_FACTOR * seq_len // N_EXPERTS
    kernel = functools.partial(_route_kernel, seqs_per_step=sps, seq_len=seq_len, cap=cap)
    const2 = lambda b: (0, 0)
    per_seq = lambda shape, dtype: pltpu.VMEM((sps,) + shape, dtype)
    rank_rows = (seq_len // ROW_TOKENS, SUBLANES, LANES)
    return pl.pallas_call(
        kernel,
        grid=(n_seq // sps,),
        in_specs=[pl.BlockSpec((sps, seq_len, D_MODEL), lambda b: (b, 0, 0)),
                  _mod_spec(mod, sps),
                  pl.BlockSpec((1, D_MODEL), const2),
                  pl.BlockSpec((2 * N_EXPERTS, D_MODEL), const2)],
        out_specs=[pl.BlockSpec((sps, N_EXPERTS * cap, D_MODEL), lambda b: (b, 0, 0)),
                   pl.BlockSpec((sps, N_EXPERTS, seq_len), lambda b: (b, 0, 0)),
                   pl.BlockSpec((sps, N_EXPERTS * cap, LANES), lambda b: (b, 0, 0))],
        out_shape=[jax.ShapeDtypeStruct((n_seq, N_EXPERTS * cap, D_MODEL), BF16),
                   jax.ShapeDtypeStruct((n_seq, N_EXPERTS, seq_len), I32),
                   jax.ShapeDtypeStruct((n_seq, N_EXPERTS * cap, LANES), F32)],
        scratch_shapes=[per_seq((N_EXPERTS, seq_len), F32),
                        per_seq((seq_len, D_MODEL), BF16),
                        per_seq((seq_len, LANES), F32),
                        per_seq(rank_rows, F32),
                        per_seq(rank_rows, F32),
                        per_seq(rank_rows, F32)],
        compiler_params=_params(1),
        name="route",
    )(x1, mod, g2, wr_both)


def _experts_kernel(xc_ref, xl_ref, gc_ref, gl_ref, wg_ref, wu_ref, wd_ref, yc_ref, yl_ref,
                    wgb_ref, wub_ref, wdb_ref, a_ref):
    wgb_ref[...] = wg_ref[...].astype(BF16)
    wub_ref[...] = wu_ref[...].astype(BF16)
    wdb_ref[...] = wd_ref[...].astype(BF16)
    f_tile = 2 * LANES

    def run(x_ref, g_ref, y_ref):
        n_seq, cap, _ = x_ref.shape
        seqs = DISPATCH_ROWS // cap
        for s0 in range(0, n_seq, seqs):
            x = x_ref[s0:s0 + seqs].reshape(DISPATCH_ROWS, D_MODEL)
            for f0 in range(0, D_EXPERT, f_tile):
                cols = slice(f0, f0 + f_tile)
                a_ref[:, cols] = (_silu(_dot(x, wgb_ref[:, cols])) * _dot(x, wub_ref[:, cols])).astype(BF16)
            gate = g_ref[s0:s0 + seqs].reshape(DISPATCH_ROWS, LANES)
            y = _dot(a_ref[...], wdb_ref[...]) * jnp.concatenate([gate] * (D_MODEL // LANES), axis=1)
            y_ref[s0:s0 + seqs] = y.astype(BF16).reshape(seqs, cap, D_MODEL)

    run(xc_ref, gc_ref, yc_ref)
    run(xl_ref, gl_ref, yl_ref)


def _experts_call(xs_ctx, xs_lat, gates_ctx, gates_lat, w_gate, w_up, w_down):
    def slot_spec(a):
        n_seq, _, cap, width = a.shape
        return pl.BlockSpec((n_seq, None, cap, width), lambda e: (0, e, 0, 0))

    w_spec = pl.BlockSpec((None, D_MODEL, D_EXPERT), lambda e: (e, 0, 0))
    return pl.pallas_call(
        _experts_kernel,
        grid=(N_EXPERTS,),
        in_specs=[slot_spec(xs_ctx), slot_spec(xs_lat), slot_spec(gates_ctx), slot_spec(gates_lat),
                  w_spec, w_spec, pl.BlockSpec((None, D_EXPERT, D_MODEL), lambda e: (e, 0, 0))],
        out_specs=[slot_spec(xs_ctx), slot_spec(xs_lat)],
        out_shape=[jax.ShapeDtypeStruct(xs_ctx.shape, BF16), jax.ShapeDtypeStruct(xs_lat.shape, BF16)],
        scratch_shapes=[pltpu.VMEM((D_MODEL, D_EXPERT), BF16),
                        pltpu.VMEM((D_MODEL, D_EXPERT), BF16),
                        pltpu.VMEM((D_EXPERT, D_MODEL), BF16),
                        pltpu.VMEM((DISPATCH_ROWS, D_EXPERT), BF16)],
        compiler_params=_params(1),
        name="experts",
    )(xs_ctx, xs_lat, gates_ctx, gates_lat, w_gate, w_up, w_down)


def _combine_kernel(x1_ref, y_ref, rank_ref, mod_ref, gf_ref, o_ref, oh_ref, *, seqs_per_step, seq_len, cap):
    slot = lax.broadcasted_iota(I32, (cap, 1), 0)
    for j in range(seqs_per_step):
        m = _mod_rows(mod_ref, j)
        for e in range(N_EXPERTS):
            oh_ref[j, e * cap:(e + 1) * cap, :] = jnp.where(rank_ref[j, e:e + 1, :] == slot, 1.0, 0.0).astype(BF16)
        for t in range(seq_len // ROW_TILE):
            rows = slice(t * ROW_TILE, (t + 1) * ROW_TILE)
            moe = _dot_tn(oh_ref[j, :, rows], y_ref[j])
            x2 = x1_ref[j, rows, :] + m[5] * moe
            r = lax.rsqrt(jnp.mean(x2 * x2, axis=-1, keepdims=True) + EPS)
            o_ref[j, rows, :] = (x2 * r) * gf_ref[...]


def _combine_call(x1, y, rank, mod, g_final, *, seqs_per_step):
    n_seq, seq_len, _ = x1.shape
    n_slots = y.shape[1]
    sps = seqs_per_step
    kernel = functools.partial(_combine_kernel, seqs_per_step=sps, seq_len=seq_len, cap=n_slots // N_EXPERTS)
    seq_spec = pl.BlockSpec((sps, seq_len, D_MODEL), lambda b: (b, 0, 0))
    return pl.pallas_call(
        kernel,
        grid=(n_seq // sps,),
        in_specs=[seq_spec,
                  pl.BlockSpec((sps, n_slots, D_MODEL), lambda b: (b, 0, 0)),
                  pl.BlockSpec((sps, N_EXPERTS, seq_len), lambda b: (b, 0, 0)),
                  _mod_spec(mod, sps),
                  pl.BlockSpec((1, D_MODEL), lambda b: (0, 0))],
        out_specs=seq_spec,
        out_shape=jax.ShapeDtypeStruct((n_seq, seq_len, D_MODEL), F32),
        scratch_shapes=[pltpu.VMEM((sps, n_slots, seq_len), BF16)],
        compiler_params=_params(1),
        name="combine",
    )(x1, y, rank, mod, g_final)


def kernel(x_prompt, x_sample, state_gla_fwd, state_gla_bwd, c, c_ctx, w_mod, b_mod, g_norm1, g_norm2,
           w_in, w_conv, b_conv, w_a_up_f, b_a_f, w_a_up_b, b_a_b, g_gla_norm, w_out, w_router,
           w_gate, w_up, w_down, g_final):
    assert w_mod.shape[0] == 1, "single trunk layer"
    n_ctx, ctx_len, _ = x_prompt.shape
    n_lat, lat_len, _ = x_sample.shape
    ctx_cap = EC_CAPACITY_FACTOR * ctx_len // N_EXPERTS
    lat_cap = EC_CAPACITY_FACTOR * lat_len // N_EXPERTS

    c_rows = jnp.concatenate([c_ctx[None, :], c, jnp.zeros((8 - 1 - n_lat, D_MODEL), F32)], axis=0)
    mod_ctx, mod_lat = _mod_call(c_rows, n_lat, w_mod[0], b_mod)

    mixer_weights = [g_norm1, jnp.transpose(w_in[0]), w_conv, b_conv, w_a_up_f, b_a_f, w_a_up_b, b_a_b,
                     g_gla_norm[0].reshape(1, GLA_DV_TOT), w_out[0].astype(BF16)]
    assert len(mixer_weights) == N_MIXER_WEIGHTS
    wr_t = jnp.transpose(w_router[0])
    wr_hi = wr_t.astype(BF16)
    wr_lo = (wr_t - wr_hi.astype(F32)).astype(BF16)

    x1_ctx, new_f, new_b, w_in_bf = _mixer_call(x_prompt, mod_ctx, None, mixer_weights,
                                                period=ctx_len, has_state_out=True, seqs_per_step=4)
    mixer_weights[1] = w_in_bf
    (x1_lat,) = _mixer_call(x_sample, mod_lat, (state_gla_fwd, state_gla_bwd), mixer_weights,
                            period=GRID_W, has_state_out=False, seqs_per_step=1)

    wr_both = jnp.concatenate([wr_hi, wr_lo], axis=0)
    xs_ctx, rank_ctx, gates_ctx = _route_call(x1_ctx, mod_ctx, g_norm2, wr_both, seqs_per_step=4)
    xs_lat, rank_lat, gates_lat = _route_call(x1_lat, mod_lat, g_norm2, wr_both, seqs_per_step=1)

    per_expert = lambda a, n, cap: a.reshape(n, N_EXPERTS, cap, a.shape[-1])
    y_ctx, y_lat = _experts_call(per_expert(xs_ctx, n_ctx, ctx_cap), per_expert(xs_lat, n_lat, lat_cap),
                                 per_expert(gates_ctx, n_ctx, ctx_cap), per_expert(gates_lat, n_lat, lat_cap),
                                 w_gate[0], w_up[0], w_down[0])

    g_fin = g_final[None, :]
    y_prompt = _combine_call(x1_ctx, y_ctx.reshape(xs_ctx.shape), rank_ctx, mod_ctx, g_fin, seqs_per_step=4)
    y_sample = _combine_call(x1_lat, y_lat.reshape(xs_lat.shape), rank_lat, mod_lat, g_fin, seqs_per_step=1)
    return y_prompt, y_sample, new_f, new_b
```

```python
import functools

import jax
import jax.numpy as jnp
from jax import lax
from jax.experimental import pallas as pl
from jax.experimental.pallas import tpu as pltpu

F32 = jnp.float32
BF16 = jnp.bfloat16
I32 = jnp.int32

D_MODEL = 1024
D_CONV = D_MODEL // 2
GRID_W = 64
GLA_HEADS = 4
GLA_DK = 64
GLA_DV = 128
GLA_DK_TOT = GLA_HEADS * GLA_DK
GLA_DV_TOT = GLA_HEADS * GLA_DV
GLA_LOW_RANK = 16
GLA_TAU = 16.0
GLA_CHUNK = 64
N_EXPERTS = 16
EC_CAPACITY_FACTOR = 2
D_EXPERT = 1024
N_MOD = 6
EPS = 1e-6
LOG2_E = 1.4426950408889634

OFF_XB = 0
OFF_XC = D_CONV
OFF_XV = 2 * D_CONV
OFF_Q = 3 * D_CONV
OFF_K = OFF_Q + GLA_DK_TOT
OFF_V = OFF_K + GLA_DK_TOT
OFF_OG = OFF_V + GLA_DV_TOT
OFF_ALOW = OFF_OG + GLA_DV_TOT
P_TOT = OFF_ALOW + 2 * GLA_LOW_RANK

LANES = 128
P_PAD = -(-P_TOT // LANES) * LANES
ROW_TILE = 256
N_MIXER_WEIGHTS = 10
DISPATCH_ROWS = 512
VMEM_LIMIT = 56 * 1024 * 1024


def _dot(a, b):
    return jnp.dot(a, b, preferred_element_type=F32)


def _dot_nt(a, b):
    return lax.dot_general(a, b, (((1,), (1,)), ((), ())), preferred_element_type=F32)


def _dot_tn(a, b):
    return lax.dot_general(a, b, (((0,), (0,)), ((), ())), preferred_element_type=F32)


def _split(a):
    hi = a.astype(BF16)
    lo = (a - hi.astype(F32)).astype(BF16)
    return hi, lo


def _silu(x):
    return x * jax.nn.sigmoid(x)


def _modulated_norm(x, g, scale, shift):
    r = lax.rsqrt(jnp.mean(x * x, axis=-1, keepdims=True) + EPS)
    return (x * r) * (g * (1.0 + scale)) + shift


def _params(n_axes):
    return pltpu.CompilerParams(dimension_semantics=("arbitrary",) * n_axes,
                                vmem_limit_bytes=VMEM_LIMIT)


def _mod_kernel(c_ref, w_ref, b_ref, ctx_ref, lat_ref, acc_ref):
    rows = c_ref.shape[0]
    s = _silu(c_ref[...])
    s_hi, s_lo = _split(jnp.concatenate([s, s], axis=0))
    upper = lax.broadcasted_iota(I32, (2 * rows, 1), 0) < rows
    w_hi, w_lo = _split(w_ref[...])
    by_hi = _dot(jnp.where(upper, s_hi, s_lo), w_hi)
    part = by_hi[:rows] + by_hi[rows:] + _dot(s_hi[:rows], w_lo)

    @pl.when(pl.program_id(0) == 0)
    def _():
        acc_ref[...] = part + b_ref[...]

    @pl.when(pl.program_id(0) != 0)
    def _():
        acc_ref[...] = acc_ref[...] + part

    @pl.when(pl.program_id(0) == pl.num_programs(0) - 1)
    def _():
        ctx_ref[0] = acc_ref[0:1, :]
        for i in range(lat_ref.shape[0]):
            lat_ref[i] = acc_ref[1 + i:2 + i, :]


def _mod_call(c_rows, n_lat, w_mod, b_mod):
    rows, d = c_rows.shape
    n = w_mod.shape[1]
    tk = D_MODEL // 4
    return pl.pallas_call(
        _mod_kernel,
        grid=(d // tk,),
        in_specs=[pl.BlockSpec((rows, tk), lambda k: (0, k)),
                  pl.BlockSpec((tk, n), lambda k: (k, 0)),
                  pl.BlockSpec((1, n), lambda k: (0, 0))],
        out_specs=[pl.BlockSpec((1, 1, n), lambda k: (0, 0, 0)),
                   pl.BlockSpec((n_lat, 1, n), lambda k: (0, 0, 0))],
        out_shape=[jax.ShapeDtypeStruct((1, 1, n), F32), jax.ShapeDtypeStruct((n_lat, 1, n), F32)],
        scratch_shapes=[pltpu.VMEM((rows, n), F32)],
        compiler_params=_params(1),
        name="mod",
    )(c_rows, w_mod, b_mod)


def _mod_rows(mod_ref, j):
    row = mod_ref[j % mod_ref.shape[0]]
    return [row[:, i * D_MODEL:(i + 1) * D_MODEL] for i in range(N_MOD)]


def _mod_spec(mod, seqs_per_step):
    if mod.shape[0] == 1:
        return pl.BlockSpec(mod.shape, lambda b: (0, 0, 0))
    return pl.BlockSpec((seqs_per_step,) + mod.shape[1:], lambda b: (b, 0, 0))


def _staggered(programs):
    programs = list(programs)
    started = 0
    while programs:
        started = min(started + 1, len(programs))
        running = [p for p in programs[:started] if next(p, "done") != "done"]
        programs = running + programs[started:]
        started = len(running)
        yield


def _for_row_tiles(seq_len, phases, independent):
    n = seq_len // ROW_TILE
    if independent:
        yield from _staggered(phases(i) for i in range(n))
    else:
        for i in range(n):
            yield from phases(i)


def _tile_rows(tile, offset=0, size=ROW_TILE):
    if isinstance(tile, int):
        return pl.ds(tile * ROW_TILE + offset, size)
    return pl.ds(pl.multiple_of(tile * ROW_TILE + offset, size), size)


def _mixer_kernel(*refs, seqs_per_step, has_state_in, has_state_out, stages_w_in, **static):
    refs = list(refs)
    n_in = 2 + (2 if has_state_in else 0)
    per_seq_in, refs = [refs[0]] + refs[2:n_in], [refs[1]] + refs[n_in:]
    mod_ref, weights, refs = refs[0], refs[1:1 + N_MIXER_WEIGHTS], refs[1 + N_MIXER_WEIGHTS:]
    n_out = 1 + (2 if has_state_out else 0)
    per_seq_out, refs = refs[:n_out], refs[n_out:]
    if stages_w_in:
        win_f32_ref, win_bf_ref, scratch = weights[1], refs[0], refs[1:]
        weights = weights[:1] + [win_bf_ref] + weights[2:]

        @pl.when(pl.program_id(0) == 0)
        def _():
            for c0 in range(0, P_PAD, LANES):
                n = min(LANES, P_TOT - c0)
                cols = win_f32_ref[c0:c0 + n, :]
                if n < LANES:
                    cols = jnp.concatenate([cols, jnp.zeros((LANES - n, D_MODEL), F32)], axis=0)
                win_bf_ref[:, c0:c0 + LANES] = jnp.transpose(cols).astype(BF16)
    else:
        scratch = refs
    programs = []
    for j in range(seqs_per_step):
        ins = [r.at[j] for r in per_seq_in]
        outs = [r.at[j] for r in per_seq_out]
        programs.append(_mixer_sequence(ins[0], _mod_rows(mod_ref, j), ins[1:], weights, outs[0], outs[1:],
                                        [r.at[j] for r in scratch], **static))
    for _ in _staggered(programs):
        pass


def _mixer_sequence(x_ref, m, s0_refs, weights, x1_ref, sout_refs, scratch, *, seq_len, period):
    has_state_in = bool(s0_refs)
    has_state_out = bool(sout_refs)
    g1_ref, win_ref, wconv_ref, bconv_ref, wupf_ref, bupf_ref, wupb_ref, bupb_ref, ggla_ref, wout_ref = weights
    og_ref, qd_ref, kd_ref, kst_ref, dect_ref, v_ref, s_ref, sst_ref, o_ref, ya_ref = scratch

    c = GLA_CHUNK
    tile_chunks = ROW_TILE // c
    n_tiles = seq_len // ROW_TILE
    n_pairs = GLA_HEADS // 2
    pair_k = 2 * GLA_DK
    pair_v = 2 * GLA_DV

    def stage1(ti):
        rows = _tile_rows(ti)
        h = _modulated_norm(x_ref[rows, :], g1_ref[...], m[1], m[0]).astype(BF16)
        row_i = lax.broadcasted_iota(I32, (ROW_TILE, 1), 0)
        yield
        p_gate = _dot(h, win_ref[:, OFF_OG:P_PAD])
        og_ref[rows, :] = p_gate[:, :GLA_DV_TOT]
        yield
        zero_up = jnp.zeros((GLA_LOW_RANK, GLA_DK_TOT), F32)
        w_up = jnp.concatenate([jnp.concatenate([wupf_ref[0], zero_up], axis=1),
                                jnp.concatenate([zero_up, wupb_ref[0]], axis=1),
                                jnp.zeros((P_PAD - P_TOT, 2 * GLA_DK_TOT), F32)], axis=0).astype(BF16)
        b_up = jnp.concatenate([bupf_ref[...], bupb_ref[...]], axis=1)
        z = _dot(p_gate[:, GLA_DV_TOT:].astype(BF16), w_up) + b_up
        la = (jnp.minimum(z, 0.0) - jnp.log(1.0 + jnp.exp(-jnp.abs(z)))) * (LOG2_E / GLA_TAU)
        col_j = lax.broadcasted_iota(I32, (1, ROW_TILE), 1)
        same_chunk = (row_i & -c) == (col_j & -c)
        lower = jnp.where(same_chunk & (col_j <= row_i), 1.0, 0.0).astype(BF16)
        la_parts = jnp.concatenate(_split(la), axis=1)
        n_gate = 2 * GLA_DK_TOT
        yield
        pre = _dot(lower, la_parts)
        pre = pre[:, :n_gate] + pre[:, n_gate:]
        tot = jnp.concatenate([jnp.broadcast_to(pre[(n + 1) * c - 1:(n + 1) * c], (c, n_gate))
                               for n in range(tile_chunks)], axis=0)
        p_qkv = _dot(h, win_ref[:, OFF_Q:OFF_OG])
        yield
        q = p_qkv[:, :GLA_DK_TOT] * (GLA_DK ** -0.5)
        k = p_qkv[:, GLA_DK_TOT:2 * GLA_DK_TOT]
        v_ref[rows, :] = p_qkv[:, 2 * GLA_DK_TOT:].astype(BF16)
        for d in range(2):
            cols = slice(d * GLA_DK_TOT, (d + 1) * GLA_DK_TOT)
            if d == 0:
                bq = pre[:, cols]
                bk = tot[:, cols] - bq
            else:
                bk = pre[:, cols] - la[:, cols]
                bq = tot[:, cols] - bk
            qd_ref[d, rows, :] = (q * jnp.exp2(bq)).astype(BF16)
            kd_ref[d, rows, :] = (k * jnp.exp2(-bq)).astype(BF16)
            kst_ref[d, ti] = jnp.transpose(k * jnp.exp2(bk)).astype(BF16)
            totals = [tot[n * c:n * c + 1, cols] for n in range(tile_chunks)]
            totals.append(jnp.zeros((LANES - tile_chunks, GLA_DK_TOT), F32))
            dect_ref[d, ti] = jnp.transpose(jnp.exp2(jnp.concatenate(totals, axis=0)))
        yield
        p_conv = _dot(h, win_ref[:, :OFF_Q])
        yield
        pos = row_i & (period - 1)
        u = p_conv[:, OFF_XC:OFF_XC + D_CONV] * p_conv[:, OFF_XV:OFF_XV + D_CONV]
        u_prev = jnp.where(pos == 0, 0.0, pltpu.roll(u, 1, 0))
        u_next = jnp.where(pos == period - 1, 0.0, pltpu.roll(u, ROW_TILE - 1, 0))
        conv = u_prev * wconv_ref[0, 0:1, :] + u * wconv_ref[0, 1:2, :] + u_next * wconv_ref[0, 2:3, :] + bconv_ref[...]
        ya_ref[rows, :] = (p_conv[:, OFF_XB:OFF_XB + D_CONV] * conv).astype(BF16)
        yield

    yield from _for_row_tiles(seq_len, stage1, independent=True)

    for d in range(2):
        for pair in range(n_pairs):
            if has_state_in:
                zero = jnp.zeros((GLA_DK, GLA_DV), F32)
                top = jnp.concatenate([s0_refs[d][2 * pair], zero], axis=1)
                bot = jnp.concatenate([zero, s0_refs[d][2 * pair + 1]], axis=1)
                s_ref[d, pair] = jnp.concatenate([top, bot], axis=0)
            else:
                s_ref[d, pair] = jnp.zeros((pair_k, pair_v), F32)

    def scan_tile(i):
        upper_lane = lax.broadcasted_iota(I32, (1, LANES), 1) >= GLA_DK
        qi = lax.broadcasted_iota(I32, (LANES, 1), 0)
        kj = lax.broadcasted_iota(I32, (1, 2 * LANES), 1) & (LANES - 1)
        same_chunk = (qi & c) == (kj & c)
        causal = (same_chunk & (kj <= qi), same_chunk & (kj >= qi))
        for pair in range(n_pairs):
            kl = slice(pair * pair_k, (pair + 1) * pair_k)
            vl = slice(pair * pair_v, (pair + 1) * pair_v)
            for blk in range(ROW_TILE // LANES):
                rows = _tile_rows(i, blk * LANES, LANES)
                att = None
                for d in range(2):
                    kd = kd_ref[d, rows, kl]
                    zk = jnp.zeros_like(kd)
                    keys = jnp.concatenate([jnp.where(upper_lane, zk, kd), jnp.where(upper_lane, kd, zk)], axis=0)
                    a = jnp.where(causal[d], _dot_nt(qd_ref[d, rows, kl], keys), 0.0)
                    att = a if att is None else att + a
                v = v_ref[rows, vl]
                zv = jnp.zeros((LANES, GLA_DV), BF16)
                v_bd = jnp.concatenate([jnp.concatenate([v[:, :GLA_DV], zv], axis=1),
                                        jnp.concatenate([zv, v[:, GLA_DV:]], axis=1)], axis=0)
                o_ref[rows, vl] = _dot(att.astype(BF16), v_bd)
        yield
        key_row = lax.broadcasted_iota(I32, (pair_k, 1), 0)
        val_col = lax.broadcasted_iota(I32, (1, pair_v), 1)
        blockdiag = (key_row >= GLA_DK) == (val_col >= GLA_DV)
        for d in range(2):
            tile = i if d == 0 else n_tiles - 1 - i
            chunks = range(tile_chunks)
            for pair in range(n_pairs):
                kr = slice(pair * pair_k, (pair + 1) * pair_k)
                vl = slice(pair * pair_v, (pair + 1) * pair_v)
                s = s_ref[d, pair]
                for c4 in (chunks if d == 0 else reversed(chunks)):
                    blk, half = divmod(c4, 2)
                    kst = kst_ref[d, tile, kr, blk * LANES:(blk + 1) * LANES]
                    kst = jnp.where(upper_lane if half else ~upper_lane, kst, jnp.zeros_like(kst))
                    kv = jnp.where(blockdiag, _dot(kst, v_ref[_tile_rows(tile, blk * LANES, LANES), vl]), 0.0)
                    sst_ref[pair, tile * tile_chunks + c4, d * pair_k:(d + 1) * pair_k, :] = s.astype(BF16)
                    s = dect_ref[d, tile, kr, c4:c4 + 1] * s + kv
                s_ref[d, pair] = s
        yield

    yield from _for_row_tiles(seq_len, scan_tile, independent=False)

    if has_state_out:
        for d in range(2):
            for pair in range(n_pairs):
                s = s_ref[d, pair]
                sout_refs[d][2 * pair] = s[0:GLA_DK, 0:GLA_DV]
                sout_refs[d][2 * pair + 1] = s[GLA_DK:, GLA_DV:]

    def stage3(i):
        for pair in range(n_pairs):
            kl = slice(pair * pair_k, (pair + 1) * pair_k)
            vl = slice(pair * pair_v, (pair + 1) * pair_v)
            for c4 in range(tile_chunks):
                crow = _tile_rows(i, c4 * c, c)
                q2 = jnp.concatenate([qd_ref[0, crow, kl], qd_ref[1, crow, kl]], axis=1)
                o_ref[crow, vl] = o_ref[crow, vl] + _dot(q2, sst_ref[pair, i * tile_chunks + c4])
        yield
        rows = _tile_rows(i)
        heads = []
        for h in range(GLA_HEADS):
            hl = slice(h * GLA_DV, (h + 1) * GLA_DV)
            oh = o_ref[rows, hl]
            r = lax.rsqrt(jnp.mean(oh * oh, axis=-1, keepdims=True) + EPS)
            heads.append(oh * r * ggla_ref[:, hl])
        y_b = jnp.concatenate(heads, axis=1) * _silu(og_ref[rows, :])
        y = jnp.concatenate([ya_ref[rows, :], y_b.astype(BF16)], axis=1)
        x1_ref[rows, :] = x_ref[rows, :] + m[2] * _dot(y, wout_ref[...])
        yield

    yield from _for_row_tiles(seq_len, stage3, independent=True)


def _mixer_call(x, mod, states, weights, *, period, has_state_out, seqs_per_step):
    n_seq, seq_len, _ = x.shape
    sps = seqs_per_step
    has_state_in = states is not None
    stages_w_in = weights[1].dtype == F32
    kernel = functools.partial(_mixer_kernel, seqs_per_step=sps, seq_len=seq_len, period=period,
                               has_state_in=has_state_in, has_state_out=has_state_out, stages_w_in=stages_w_in)
    state_spec = pl.BlockSpec((sps, None, GLA_HEADS, GLA_DK, GLA_DV), lambda b: (b, 0, 0, 0, 0))
    const2 = lambda b: (0, 0)
    in_specs = [pl.BlockSpec((sps, seq_len, D_MODEL), lambda b: (b, 0, 0)),
                _mod_spec(mod, sps)]
    args = [x, mod]
    if has_state_in:
        in_specs += [state_spec, state_spec]
        args += list(states)
    in_specs += [pl.BlockSpec(w.shape, lambda b, nd=w.ndim: (0,) * nd, pipeline_mode=pl.Buffered(1))
                 for w in weights]
    args += list(weights)
    out_specs = [pl.BlockSpec((sps, seq_len, D_MODEL), lambda b: (b, 0, 0))]
    out_shape = [jax.ShapeDtypeStruct((n_seq, seq_len, D_MODEL), F32)]
    if has_state_out:
        out_specs += [state_spec, state_spec]
        out_shape += [jax.ShapeDtypeStruct((n_seq, 1, GLA_HEADS, GLA_DK, GLA_DV), F32)] * 2
    if stages_w_in:
        out_specs.append(pl.BlockSpec((D_MODEL, P_PAD), const2))
        out_shape.append(jax.ShapeDtypeStruct((D_MODEL, P_PAD), BF16))
    n_tiles = seq_len // ROW_TILE
    n_pairs = GLA_HEADS // 2
    per_seq = lambda shape, dtype: pltpu.VMEM((sps,) + shape, dtype)
    scratch = [per_seq((seq_len, GLA_DV_TOT), F32),
               per_seq((2, seq_len, GLA_DK_TOT), BF16),
               per_seq((2, seq_len, GLA_DK_TOT), BF16),
               per_seq((2, n_tiles, GLA_DK_TOT, ROW_TILE), BF16),
               per_seq((2, n_tiles, GLA_DK_TOT, LANES), F32),
               per_seq((seq_len, GLA_DV_TOT), BF16),
               per_seq((2, n_pairs, 2 * GLA_DK, 2 * GLA_DV), F32),
               per_seq((n_pairs, seq_len // GLA_CHUNK, 4 * GLA_DK, 2 * GLA_DV), BF16),
               per_seq((seq_len, GLA_DV_TOT), F32),
               per_seq((seq_len, D_CONV), BF16)]
    return pl.pallas_call(
        kernel,
        grid=(n_seq // sps,),
        in_specs=in_specs,
        out_specs=out_specs,
        out_shape=out_shape,
        scratch_shapes=scratch,
        compiler_params=_params(1),
        name="mixer",
    )(*args)


SUBLANES = 8


ROW_TOKENS = LANES // N_EXPERTS


def _route_kernel(x1_ref, mod_ref, g2_ref, wr2_ref, *refs, seqs_per_step, **static):
    programs = [_route_sequence(x1_ref.at[j], _mod_rows(mod_ref, j), g2_ref, wr2_ref, *[r.at[j] for r in refs],
                                **static)
                for j in range(seqs_per_step)]
    for _ in _staggered(programs):
        pass


def _route_sequence(x1_ref, m, g2_ref, wr2_ref, xs_ref, rank_ref, gate_ref,
                    pt_ref, h2_ref, spread_ref, bar_ref, cnt_ref, win_ref, *, seq_len, cap):
    n_rows = seq_len // ROW_TOKENS
    for t in range(seq_len // ROW_TILE):
        rows = slice(t * ROW_TILE, (t + 1) * ROW_TILE)
        h2 = _modulated_norm(x1_ref[rows, :], g2_ref[...], m[4], m[3])
        hi = h2.astype(BF16)
        h2_ref[rows, :] = hi
        yield
        by_hi = _dot_nt(wr2_ref[...], hi)
        pt_ref[:, rows] = by_hi[:N_EXPERTS] + by_hi[N_EXPERTS:]
    yield

    logits = pt_ref[...]
    ex = jnp.exp(logits - jnp.max(logits, axis=0, keepdims=True))
    probs = ex / jnp.sum(ex, axis=0, keepdims=True)
    pt_ref[...] = probs
    pad = jnp.zeros((LANES - N_EXPERTS, seq_len), F32)
    p_tok = jnp.transpose(jnp.concatenate([probs, pad], axis=0))

    spread = p_tok
    sh = N_EXPERTS
    while sh < LANES:
        spread = spread + pltpu.roll(spread, sh, 1)
        sh *= 2
    spread_ref[...] = spread
    sub_j = lax.broadcasted_iota(I32, (SUBLANES, 1), 0)
    lane_g = lax.broadcasted_iota(I32, (1, LANES), 1) >> (N_EXPERTS.bit_length() - 1)
    lane_j = (ROW_TOKENS - lane_g) & (ROW_TOKENS - 1)
    own_group = jnp.where(lane_j == sub_j, spread.reshape(n_rows, SUBLANES, LANES), 0.0)
    bar = jnp.sum(own_group, axis=1, keepdims=True)
    bar_ref[...] = jnp.broadcast_to(bar, (n_rows, SUBLANES, LANES))
    cnt_ref[...] = jnp.zeros((n_rows, SUBLANES, LANES), F32)

    rows_per_block = LANES // ROW_TOKENS

    def count_block(g_s, g_t, relation):
        s_blk = spread_ref[g_s * LANES:(g_s + 1) * LANES, :]
        wins = [jnp.zeros((SUBLANES, LANES), F32)] * rows_per_block
        for tl in range(rows_per_block):
            r = g_t * rows_per_block + tl
            bar_r = bar_ref[r]
            acc = cnt_ref[r]
            for sl in range(rows_per_block):
                s_vreg = s_blk[sl * SUBLANES:(sl + 1) * SUBLANES]
                if relation == "before" or sl < tl:
                    won = jnp.where(s_vreg >= bar_r, 1.0, 0.0)
                    acc = acc + won
                    wins[sl] = wins[sl] + won
                elif sl == tl:
                    acc = acc + jnp.where(s_vreg > bar_r, 1.0, 0.0) \
                        + jnp.where((s_vreg == bar_r) & (sub_j < lane_j), 1.0, 0.0)
            cnt_ref[r] = acc
        w_rows = slice(g_s * rows_per_block, (g_s + 1) * rows_per_block)
        win_ref[w_rows] = win_ref[w_rows] + jnp.stack(wins)

    n_grp = seq_len // LANES
    win_ref[...] = jnp.zeros((n_rows, SUBLANES, LANES), F32)
    yield
    for g_t in range(n_grp):
        for g_s in range(g_t + 1):
            count_block(g_s, g_t, "before" if g_s < g_t else "same")
            yield

    counts = jnp.sum(cnt_ref[...], axis=1, keepdims=True)
    counts = jnp.broadcast_to(counts, (n_rows, SUBLANES, LANES)).reshape(seq_len, LANES)
    rank_tok = pltpu.roll(counts, 0, 1, stride=N_EXPERTS, stride_axis=0)
    wins = win_ref[...].reshape(seq_len, LANES)
    sh = N_EXPERTS
    while sh < LANES:
        wins = wins + pltpu.roll(wins, sh, 1)
        sh *= 2
    tok = lax.broadcasted_iota(I32, (seq_len, 1), 0)
    later = ((n_rows - 1 - (tok >> (ROW_TOKENS.bit_length() - 1))) * ROW_TOKENS).astype(F32)
    rank_tok = rank_tok + (later - wins)
    rank_ref[...] = jnp.transpose(rank_tok)[:N_EXPERTS].astype(I32)
    yield

    group = DISPATCH_ROWS // cap
    slot = lax.broadcasted_iota(I32, (cap, 1), 0)
    half = D_MODEL // 2
    for gi in range(N_EXPERTS // group):
        picks = []
        for e in range(gi * group, (gi + 1) * group):
            oh = rank_ref[e:e + 1, :] == slot
            gate = jnp.sum(jnp.where(oh, pt_ref[e:e + 1, :], 0.0), axis=1, keepdims=True)
            gate_ref[e * cap:(e + 1) * cap, :] = jnp.broadcast_to(gate, (cap, LANES))
            picks.append(oh)
        ohb = jnp.where(jnp.concatenate(picks, axis=0), 1.0, 0.0).astype(BF16)
        yield
        out_rows = slice(gi * DISPATCH_ROWS, (gi + 1) * DISPATCH_ROWS)
        xs_ref[out_rows, :half] = _dot(ohb, h2_ref[:, :half]).astype(BF16)
        xs_ref[out_rows, half:] = _dot(ohb, h2_ref[:, half:]).astype(BF16)
        yield


def _route_call(x1, mod, g2, wr_both, *, seqs_per_step):
    n_seq, seq_len, _ = x1.shape
    sps = seqs_per_step
    cap = EC_CAPACITY_FACTOR * seq_len // N_EXPERTS
    kernel = functools.partial(_route_kernel, seqs_per_step=sps, seq_len=seq_len, cap=cap)
    const2 = lambda b: (0, 0)
    per_seq = lambda shape, dtype: pltpu.VMEM((sps,) + shape, dtype)
    rank_rows = (seq_len // ROW_TOKENS, SUBLANES, LANES)
    return pl.pallas_call(
        kernel,
        grid=(n_seq // sps,),
        in_specs=[pl.BlockSpec((sps, seq_len, D_MODEL), lambda b: (b, 0, 0)),
                  _mod_spec(mod, sps),
                  pl.BlockSpec((1, D_MODEL), const2),
                  pl.BlockSpec((2 * N_EXPERTS, D_MODEL), const2)],
        out_specs=[pl.BlockSpec((sps, N_EXPERTS * cap, D_MODEL), lambda b: (b, 0, 0)),
                   pl.BlockSpec((sps, N_EXPERTS, seq_len), lambda b: (b, 0, 0)),
                   pl.BlockSpec((sps, N_EXPERTS * cap, LANES), lambda b: (b, 0, 0))],
        out_shape=[jax.ShapeDtypeStruct((n_seq, N_EXPERTS * cap, D_MODEL), BF16),
                   jax.ShapeDtypeStruct((n_seq, N_EXPERTS, seq_len), I32),
                   jax.ShapeDtypeStruct((n_seq, N_EXPERTS * cap, LANES), F32)],
        scratch_shapes=[per_seq((N_EXPERTS, seq_len), F32),
                        per_seq((seq_len, D_MODEL), BF16),
                        per_seq((seq_len, LANES), F32),
                        per_seq(rank_rows, F32),
                        per_seq(rank_rows, F32),
                        per_seq(rank_rows, F32)],
        compiler_params=_params(1),
        name="route",
    )(x1, mod, g2, wr_both)


def _experts_kernel(xc_ref, xl_ref, gc_ref, gl_ref, wg_ref, wu_ref, wd_ref, yc_ref, yl_ref,
                    wgb_ref, wub_ref, wdb_ref, a_ref):
    wgb_ref[...] = wg_ref[...].astype(BF16)
    wub_ref[...] = wu_ref[...].astype(BF16)
    wdb_ref[...] = wd_ref[...].astype(BF16)
    f_tile = 2 * LANES

    def run(x_ref, g_ref, y_ref):
        n_seq, cap, _ = x_ref.shape
        seqs = DISPATCH_ROWS // cap
        for s0 in range(0, n_seq, seqs):
            x = x_ref[s0:s0 + seqs].reshape(DISPATCH_ROWS, D_MODEL)
            for f0 in range(0, D_EXPERT, f_tile):
                cols = slice(f0, f0 + f_tile)
                a_ref[:, cols] = (_silu(_dot(x, wgb_ref[:, cols])) * _dot(x, wub_ref[:, cols])).astype(BF16)
            gate = g_ref[s0:s0 + seqs].reshape(DISPATCH_ROWS, LANES)
            y = _dot(a_ref[...], wdb_ref[...]) * jnp.concatenate([gate] * (D_MODEL // LANES), axis=1)
            y_ref[s0:s0 + seqs] = y.astype(BF16).reshape(seqs, cap, D_MODEL)

    run(xc_ref, gc_ref, yc_ref)
    run(xl_ref, gl_ref, yl_ref)


def _experts_call(xs_ctx, xs_lat, gates_ctx, gates_lat, w_gate, w_up, w_down):
    def slot_spec(a):
        n_seq, _, cap, width = a.shape
        return pl.BlockSpec((n_seq, None, cap, width), lambda e: (0, e, 0, 0))

    w_spec = pl.BlockSpec((None, D_MODEL, D_EXPERT), lambda e: (e, 0, 0))
    return pl.pallas_call(
        _experts_kernel,
        grid=(N_EXPERTS,),
        in_specs=[slot_spec(xs_ctx), slot_spec(xs_lat), slot_spec(gates_ctx), slot_spec(gates_lat),
                  w_spec, w_spec, pl.BlockSpec((None, D_EXPERT, D_MODEL), lambda e: (e, 0, 0))],
        out_specs=[slot_spec(xs_ctx), slot_spec(xs_lat)],
        out_shape=[jax.ShapeDtypeStruct(xs_ctx.shape, BF16), jax.ShapeDtypeStruct(xs_lat.shape, BF16)],
        scratch_shapes=[pltpu.VMEM((D_MODEL, D_EXPERT), BF16),
                        pltpu.VMEM((D_MODEL, D_EXPERT), BF16),
                        pltpu.VMEM((D_EXPERT, D_MODEL), BF16),
                        pltpu.VMEM((DISPATCH_ROWS, D_EXPERT), BF16)],
        compiler_params=_params(1),
        name="experts",
    )(xs_ctx, xs_lat, gates_ctx, gates_lat, w_gate, w_up, w_down)


def _combine_tokens(x1_ref, y_ref, rank_ref, gate2, gf_ref, o_ref, oh_ref):
    n_tok = x1_ref.shape[0]
    cap = y_ref.shape[0] // N_EXPERTS
    slot = lax.broadcasted_iota(I32, (cap, 1), 0)
    for e in range(N_EXPERTS):
        oh_ref[e * cap:(e + 1) * cap, :] = jnp.where(rank_ref[e:e + 1, :] == slot, 1.0, 0.0).astype(BF16)
    for t in range(n_tok // ROW_TILE):
        rows = slice(t * ROW_TILE, (t + 1) * ROW_TILE)
        moe = _dot_tn(oh_ref[:, rows], y_ref[...])
        x2 = x1_ref[rows, :] + gate2 * moe
        r = lax.rsqrt(jnp.mean(x2 * x2, axis=-1, keepdims=True) + EPS)
        o_ref[rows, :] = (x2 * r) * gf_ref[...]


def _combine_kernel(x1c_ref, yc_ref, rkc_ref, modc_ref, x1l_ref, yl_ref, rkl_ref, modl_ref, gf_ref,
                    oc_ref, ol_ref, ohc_ref, ohl_ref):
    for j in range(x1c_ref.shape[0]):
        _combine_tokens(x1c_ref.at[j], yc_ref.at[j], rkc_ref.at[j], _mod_rows(modc_ref, j)[5], gf_ref,
                        oc_ref.at[j], ohc_ref.at[j])
    _combine_tokens(x1l_ref.at[0], yl_ref.at[0], rkl_ref.at[0], _mod_rows(modl_ref, 0)[5], gf_ref,
                    ol_ref.at[0], ohl_ref)


def _combine_call(ctx, lat, g_final, *, ctx_seqs_per_step):
    x1c, yc, rkc, modc = ctx
    x1l, yl, rkl, modl = lat
    n_ctx, ctx_len, _ = x1c.shape
    n_lat, lat_len, _ = x1l.shape
    sps = ctx_seqs_per_step
    n_steps = n_ctx // sps
    slabs = n_steps // n_lat
    slab = lat_len // slabs
    assert n_steps == n_lat * slabs and slab % ROW_TILE == 0 and modl.shape[0] == n_lat
    ctx_blk = lambda shape: pl.BlockSpec((sps,) + shape, lambda b: (b, 0, 0))
    lat_seq = lambda b: (b // slabs, 0, 0)
    return pl.pallas_call(
        _combine_kernel,
        grid=(n_steps,),
        in_specs=[ctx_blk((ctx_len, D_MODEL)), ctx_blk(yc.shape[1:]), ctx_blk((N_EXPERTS, ctx_len)),
                  _mod_spec(modc, sps),
                  pl.BlockSpec((1, slab, D_MODEL), lambda b: (b // slabs, b % slabs, 0)),
                  pl.BlockSpec((1,) + yl.shape[1:], lat_seq),
                  pl.BlockSpec((1, N_EXPERTS, slab), lambda b: (b // slabs, 0, b % slabs)),
                  pl.BlockSpec((1,) + modl.shape[1:], lat_seq),
                  pl.BlockSpec((1, D_MODEL), lambda b: (0, 0))],
        out_specs=[ctx_blk((ctx_len, D_MODEL)),
                   pl.BlockSpec((1, slab, D_MODEL), lambda b: (b // slabs, b % slabs, 0))],
        out_shape=[jax.ShapeDtypeStruct(x1c.shape, F32), jax.ShapeDtypeStruct(x1l.shape, F32)],
        scratch_shapes=[pltpu.VMEM((sps, yc.shape[1], ctx_len), BF16),
                        pltpu.VMEM((yl.shape[1], slab), BF16)],
        compiler_params=_params(1),
        name="combine",
    )(x1c, yc, rkc, modc, x1l, yl, rkl, modl, g_final)


def kernel(x_prompt, x_sample, state_gla_fwd, state_gla_bwd, c, c_ctx, w_mod, b_mod, g_norm1, g_norm2,
           w_in, w_conv, b_conv, w_a_up_f, b_a_f, w_a_up_b, b_a_b, g_gla_norm, w_out, w_router,
           w_gate, w_up, w_down, g_final):
    assert w_mod.shape[0] == 1, "single trunk layer"
    n_ctx, ctx_len, _ = x_prompt.shape
    n_lat, lat_len, _ = x_sample.shape
    ctx_cap = EC_CAPACITY_FACTOR * ctx_len // N_EXPERTS
    lat_cap = EC_CAPACITY_FACTOR * lat_len // N_EXPERTS

    c_rows = jnp.concatenate([c_ctx[None, :], c, jnp.zeros((8 - 1 - n_lat, D_MODEL), F32)], axis=0)
    mod_ctx, mod_lat = _mod_call(c_rows, n_lat, w_mod[0], b_mod)

    mixer_weights = [g_norm1, jnp.transpose(w_in[0]), w_conv, b_conv, w_a_up_f, b_a_f, w_a_up_b, b_a_b,
                     g_gla_norm[0].reshape(1, GLA_DV_TOT), w_out[0].astype(BF16)]
    assert len(mixer_weights) == N_MIXER_WEIGHTS
    wr_t = jnp.transpose(w_router[0])
    wr_hi = wr_t.astype(BF16)
    wr_lo = (wr_t - wr_hi.astype(F32)).astype(BF16)

    x1_ctx, new_f, new_b, w_in_bf = _mixer_call(x_prompt, mod_ctx, None, mixer_weights,
                                                period=ctx_len, has_state_out=True, seqs_per_step=4)
    mixer_weights[1] = w_in_bf
    (x1_lat,) = _mixer_call(x_sample, mod_lat, (state_gla_fwd, state_gla_bwd), mixer_weights,
                            period=GRID_W, has_state_out=False, seqs_per_step=1)

    wr_both = jnp.concatenate([wr_hi, wr_lo], axis=0)
    xs_ctx, rank_ctx, gates_ctx = _route_call(x1_ctx, mod_ctx, g_norm2, wr_both, seqs_per_step=4)
    xs_lat, rank_lat, gates_lat = _route_call(x1_lat, mod_lat, g_norm2, wr_both, seqs_per_step=1)

    per_expert = lambda a, n, cap: a.reshape(n, N_EXPERTS, cap, a.shape[-1])
    y_ctx, y_lat = _experts_call(per_expert(xs_ctx, n_ctx, ctx_cap), per_expert(xs_lat, n_lat, lat_cap),
                                 per_expert(gates_ctx, n_ctx, ctx_cap), per_expert(gates_lat, n_lat, lat_cap),
                                 w_gate[0], w_up[0], w_down[0])

    g_fin = g_final[None, :]
    y_prompt, y_sample = _combine_call((x1_ctx, y_ctx.reshape(xs_ctx.shape), rank_ctx, mod_ctx),
                                       (x1_lat, y_lat.reshape(xs_lat.shape), rank_lat, mod_lat),
                                       g_fin, ctx_seqs_per_step=4)
    return y_prompt, y_sample, new_f, new_b
```

```python
import functools

import jax
import jax.numpy as jnp
from jax import lax
from jax.experimental import pallas as pl
from jax.experimental.pallas import tpu as pltpu

F32 = jnp.float32
BF16 = jnp.bfloat16
I32 = jnp.int32

D_MODEL = 1024
D_CONV = D_MODEL // 2
GRID_W = 64
GLA_HEADS = 4
GLA_DK = 64
GLA_DV = 128
GLA_DK_TOT = GLA_HEADS * GLA_DK
GLA_DV_TOT = GLA_HEADS * GLA_DV
GLA_LOW_RANK = 16
GLA_TAU = 16.0
GLA_CHUNK = 64
N_EXPERTS = 16
EC_CAPACITY_FACTOR = 2
D_EXPERT = 1024
N_MOD = 6
EPS = 1e-6
LOG2_E = 1.4426950408889634

OFF_XB = 0
OFF_XC = D_CONV
OFF_XV = 2 * D_CONV
OFF_Q = 3 * D_CONV
OFF_K = OFF_Q + GLA_DK_TOT
OFF_V = OFF_K + GLA_DK_TOT
OFF_OG = OFF_V + GLA_DV_TOT
OFF_ALOW = OFF_OG + GLA_DV_TOT
P_TOT = OFF_ALOW + 2 * GLA_LOW_RANK

LANES = 128
P_PAD = -(-P_TOT // LANES) * LANES
ROW_TILE = 256
N_MIXER_WEIGHTS = 10
DISPATCH_ROWS = 512
VMEM_LIMIT = 56 * 1024 * 1024


def _dot(a, b):
    return jnp.dot(a, b, preferred_element_type=F32)


def _dot_nt(a, b):
    return lax.dot_general(a, b, (((1,), (1,)), ((), ())), preferred_element_type=F32)


def _dot_tn(a, b):
    return lax.dot_general(a, b, (((0,), (0,)), ((), ())), preferred_element_type=F32)


def _split(a):
    hi = a.astype(BF16)
    lo = (a - hi.astype(F32)).astype(BF16)
    return hi, lo


def _silu(x):
    return x * jax.nn.sigmoid(x)


def _modulated_norm(x, g, scale, shift):
    r = lax.rsqrt(jnp.mean(x * x, axis=-1, keepdims=True) + EPS)
    return (x * r) * (g * (1.0 + scale)) + shift


def _params(n_axes):
    return pltpu.CompilerParams(dimension_semantics=("arbitrary",) * n_axes,
                                vmem_limit_bytes=VMEM_LIMIT)


MOD_CHUNK = 128
MOD_RING = 4


def _mod_kernel(c_ref, w_hbm, b_ref, ctx_ref, lat_ref, ring_ref, sem_ref):
    rows = c_ref.shape[0]
    n_chunks = w_hbm.shape[0] // MOD_CHUNK

    def chunk_copy(i):
        slot = i % MOD_RING
        return pltpu.make_async_copy(w_hbm.at[pl.ds(i * MOD_CHUNK, MOD_CHUNK), :], ring_ref.at[slot],
                                     sem_ref.at[slot])

    for i in range(min(MOD_RING, n_chunks)):
        chunk_copy(i).start()
    s = _silu(c_ref[...])
    s_hi, s_lo = _split(jnp.concatenate([s, s], axis=0))
    upper = lax.broadcasted_iota(I32, (2 * rows, 1), 0) < rows
    s_both = jnp.where(upper, s_hi, s_lo)
    acc = jnp.broadcast_to(b_ref[...], (rows, b_ref.shape[1]))
    for i in range(n_chunks):
        cols = slice(i * MOD_CHUNK, (i + 1) * MOD_CHUNK)
        chunk_copy(i).wait()
        w_hi, w_lo = _split(ring_ref[i % MOD_RING])
        by_hi = _dot(s_both[:, cols], w_hi)
        acc = acc + (by_hi[:rows] + by_hi[rows:] + _dot(s_hi[:rows, cols], w_lo))
        if i + MOD_RING < n_chunks:
            chunk_copy(i + MOD_RING).start()
    ctx_ref[0] = acc[0:1, :]
    for i in range(lat_ref.shape[0]):
        lat_ref[i] = acc[1 + i:2 + i, :]


def _mod_call(c_rows, n_lat, w_mod, b_mod):
    rows, d = c_rows.shape
    n = w_mod.shape[1]
    assert d % MOD_CHUNK == 0
    whole = lambda shape: pl.BlockSpec(shape, lambda k: (0,) * len(shape))
    return pl.pallas_call(
        _mod_kernel,
        grid=(1,),
        in_specs=[whole((rows, d)), pl.BlockSpec(memory_space=pl.ANY), whole((1, n))],
        out_specs=[whole((1, 1, n)), whole((n_lat, 1, n))],
        out_shape=[jax.ShapeDtypeStruct((1, 1, n), F32), jax.ShapeDtypeStruct((n_lat, 1, n), F32)],
        scratch_shapes=[pltpu.VMEM((MOD_RING, MOD_CHUNK, n), F32), pltpu.SemaphoreType.DMA((MOD_RING,))],
        compiler_params=_params(1),
        name="mod",
    )(c_rows, w_mod, b_mod)


def _mod_rows(mod_ref, j):
    row = mod_ref[j % mod_ref.shape[0]]
    return [row[:, i * D_MODEL:(i + 1) * D_MODEL] for i in range(N_MOD)]


def _mod_spec(mod, seqs_per_step):
    if mod.shape[0] == 1:
        return pl.BlockSpec(mod.shape, lambda b: (0, 0, 0))
    return pl.BlockSpec((seqs_per_step,) + mod.shape[1:], lambda b: (b, 0, 0))


def _staggered(programs):
    programs = list(programs)
    started = 0
    while programs:
        started = min(started + 1, len(programs))
        running = [p for p in programs[:started] if next(p, "done") != "done"]
        programs = running + programs[started:]
        started = len(running)
        yield


def _for_row_tiles(seq_len, phases, independent):
    n = seq_len // ROW_TILE
    if independent:
        yield from _staggered(phases(i) for i in range(n))
    else:
        for i in range(n):
            yield from phases(i)


def _tile_rows(tile, offset=0, size=ROW_TILE):
    if isinstance(tile, int):
        return pl.ds(tile * ROW_TILE + offset, size)
    return pl.ds(pl.multiple_of(tile * ROW_TILE + offset, size), size)


def _mixer_kernel(*refs, seqs_per_step, has_state_in, has_state_out, stages_w_in, **static):
    refs = list(refs)
    n_in = 2 + (2 if has_state_in else 0)
    per_seq_in, refs = [refs[0]] + refs[2:n_in], [refs[1]] + refs[n_in:]
    mod_ref, weights, refs = refs[0], refs[1:1 + N_MIXER_WEIGHTS], refs[1 + N_MIXER_WEIGHTS:]
    n_out = 1 + (2 if has_state_out else 0)
    per_seq_out, refs = refs[:n_out], refs[n_out:]
    if stages_w_in:
        win_f32_ref, win_bf_ref, scratch = weights[1], refs[0], refs[1:]
        weights = weights[:1] + [win_bf_ref] + weights[2:]

        @pl.when(pl.program_id(0) == 0)
        def _():
            for c0 in range(0, P_PAD, LANES):
                n = min(LANES, P_TOT - c0)
                cols = win_f32_ref[c0:c0 + n, :]
                if n < LANES:
                    cols = jnp.concatenate([cols, jnp.zeros((LANES - n, D_MODEL), F32)], axis=0)
                win_bf_ref[:, c0:c0 + LANES] = jnp.transpose(cols).astype(BF16)
    else:
        scratch = refs
    programs = []
    for j in range(seqs_per_step):
        ins = [r.at[j] for r in per_seq_in]
        outs = [r.at[j] for r in per_seq_out]
        programs.append(_mixer_sequence(ins[0], _mod_rows(mod_ref, j), ins[1:], weights, outs[0], outs[1:],
                                        [r.at[j] for r in scratch], **static))
    for _ in _staggered(programs):
        pass


def _mixer_sequence(x_ref, m, s0_refs, weights, x1_ref, sout_refs, scratch, *, seq_len, period):
    has_state_in = bool(s0_refs)
    has_state_out = bool(sout_refs)
    g1_ref, win_ref, wconv_ref, bconv_ref, wupf_ref, bupf_ref, wupb_ref, bupb_ref, ggla_ref, wout_ref = weights
    og_ref, qd_ref, kd_ref, kst_ref, dect_ref, v_ref, s_ref, sst_ref, o_ref, ya_ref = scratch

    c = GLA_CHUNK
    tile_chunks = ROW_TILE // c
    n_tiles = seq_len // ROW_TILE
    n_pairs = GLA_HEADS // 2
    pair_k = 2 * GLA_DK
    pair_v = 2 * GLA_DV

    def stage1(ti):
        rows = _tile_rows(ti)
        h = _modulated_norm(x_ref[rows, :], g1_ref[...], m[1], m[0]).astype(BF16)
        row_i = lax.broadcasted_iota(I32, (ROW_TILE, 1), 0)
        yield
        p_gate = _dot(h, win_ref[:, OFF_OG:P_PAD])
        og_ref[rows, :] = p_gate[:, :GLA_DV_TOT]
        yield
        zero_up = jnp.zeros((GLA_LOW_RANK, GLA_DK_TOT), F32)
        w_up = jnp.concatenate([jnp.concatenate([wupf_ref[0], zero_up], axis=1),
                                jnp.concatenate([zero_up, wupb_ref[0]], axis=1),
                                jnp.zeros((P_PAD - P_TOT, 2 * GLA_DK_TOT), F32)], axis=0).astype(BF16)
        b_up = jnp.concatenate([bupf_ref[...], bupb_ref[...]], axis=1)
        z = _dot(p_gate[:, GLA_DV_TOT:].astype(BF16), w_up) + b_up
        la = (jnp.minimum(z, 0.0) - jnp.log(1.0 + jnp.exp(-jnp.abs(z)))) * (LOG2_E / GLA_TAU)
        col_j = lax.broadcasted_iota(I32, (1, ROW_TILE), 1)
        same_chunk = (row_i & -c) == (col_j & -c)
        lower = jnp.where(same_chunk & (col_j <= row_i), 1.0, 0.0).astype(BF16)
        la_parts = jnp.concatenate(_split(la), axis=1)
        n_gate = 2 * GLA_DK_TOT
        yield
        pre = _dot(lower, la_parts)
        pre = pre[:, :n_gate] + pre[:, n_gate:]
        tot = jnp.concatenate([jnp.broadcast_to(pre[(n + 1) * c - 1:(n + 1) * c], (c, n_gate))
                               for n in range(tile_chunks)], axis=0)
        p_qkv = _dot(h, win_ref[:, OFF_Q:OFF_OG])
        yield
        q = p_qkv[:, :GLA_DK_TOT] * (GLA_DK ** -0.5)
        k = p_qkv[:, GLA_DK_TOT:2 * GLA_DK_TOT]
        v_ref[rows, :] = p_qkv[:, 2 * GLA_DK_TOT:].astype(BF16)
        for d in range(2):
            cols = slice(d * GLA_DK_TOT, (d + 1) * GLA_DK_TOT)
            if d == 0:
                bq = pre[:, cols]
                bk = tot[:, cols] - bq
            else:
                bk = pre[:, cols] - la[:, cols]
                bq = tot[:, cols] - bk
            qd_ref[d, rows, :] = (q * jnp.exp2(bq)).astype(BF16)
            kd_ref[d, rows, :] = (k * jnp.exp2(-bq)).astype(BF16)
            kst_ref[d, ti] = jnp.transpose(k * jnp.exp2(bk)).astype(BF16)
            totals = [tot[n * c:n * c + 1, cols] for n in range(tile_chunks)]
            totals.append(jnp.zeros((LANES - tile_chunks, GLA_DK_TOT), F32))
            dect_ref[d, ti] = jnp.transpose(jnp.exp2(jnp.concatenate(totals, axis=0)))
        yield
        p_conv = _dot(h, win_ref[:, :OFF_Q])
        yield
        pos = row_i & (period - 1)
        u = p_conv[:, OFF_XC:OFF_XC + D_CONV] * p_conv[:, OFF_XV:OFF_XV + D_CONV]
        u_prev = jnp.where(pos == 0, 0.0, pltpu.roll(u, 1, 0))
        u_next = jnp.where(pos == period - 1, 0.0, pltpu.roll(u, ROW_TILE - 1, 0))
        conv = u_prev * wconv_ref[0, 0:1, :] + u * wconv_ref[0, 1:2, :] + u_next * wconv_ref[0, 2:3, :] + bconv_ref[...]
        ya_ref[rows, :] = (p_conv[:, OFF_XB:OFF_XB + D_CONV] * conv).astype(BF16)
        yield

    yield from _for_row_tiles(seq_len, stage1, independent=True)

    for d in range(2):
        for pair in range(n_pairs):
            if has_state_in:
                zero = jnp.zeros((GLA_DK, GLA_DV), F32)
                top = jnp.concatenate([s0_refs[d][2 * pair], zero], axis=1)
                bot = jnp.concatenate([zero, s0_refs[d][2 * pair + 1]], axis=1)
                s_ref[d, pair] = jnp.concatenate([top, bot], axis=0)
            else:
                s_ref[d, pair] = jnp.zeros((pair_k, pair_v), F32)

    def scan_tile(i):
        upper_lane = lax.broadcasted_iota(I32, (1, LANES), 1) >= GLA_DK
        qi = lax.broadcasted_iota(I32, (LANES, 1), 0)
        kj = lax.broadcasted_iota(I32, (1, 2 * LANES), 1) & (LANES - 1)
        same_chunk = (qi & c) == (kj & c)
        causal = (same_chunk & (kj <= qi), same_chunk & (kj >= qi))
        for pair in range(n_pairs):
            kl = slice(pair * pair_k, (pair + 1) * pair_k)
            vl = slice(pair * pair_v, (pair + 1) * pair_v)
            for blk in range(ROW_TILE // LANES):
                rows = _tile_rows(i, blk * LANES, LANES)
                att = None
                for d in range(2):
                    kd = kd_ref[d, rows, kl]
                    zk = jnp.zeros_like(kd)
                    keys = jnp.concatenate([jnp.where(upper_lane, zk, kd), jnp.where(upper_lane, kd, zk)], axis=0)
                    a = jnp.where(causal[d], _dot_nt(qd_ref[d, rows, kl], keys), 0.0)
                    att = a if att is None else att + a
                v = v_ref[rows, vl]
                zv = jnp.zeros((LANES, GLA_DV), BF16)
                v_bd = jnp.concatenate([jnp.concatenate([v[:, :GLA_DV], zv], axis=1),
                                        jnp.concatenate([zv, v[:, GLA_DV:]], axis=1)], axis=0)
                o_ref[rows, vl] = _dot(att.astype(BF16), v_bd)
        yield
        key_row = lax.broadcasted_iota(I32, (pair_k, 1), 0)
        val_col = lax.broadcasted_iota(I32, (1, pair_v), 1)
        blockdiag = (key_row >= GLA_DK) == (val_col >= GLA_DV)
        for d in range(2):
            tile = i if d == 0 else n_tiles - 1 - i
            chunks = range(tile_chunks)
            for pair in range(n_pairs):
                kr = slice(pair * pair_k, (pair + 1) * pair_k)
                vl = slice(pair * pair_v, (pair + 1) * pair_v)
                s = s_ref[d, pair]
                for c4 in (chunks if d == 0 else reversed(chunks)):
                    blk, half = divmod(c4, 2)
                    kst = kst_ref[d, tile, kr, blk * LANES:(blk + 1) * LANES]
                    kst = jnp.where(upper_lane if half else ~upper_lane, kst, jnp.zeros_like(kst))
                    kv = jnp.where(blockdiag, _dot(kst, v_ref[_tile_rows(tile, blk * LANES, LANES), vl]), 0.0)
                    sst_ref[pair, tile * tile_chunks + c4, d * pair_k:(d + 1) * pair_k, :] = s.astype(BF16)
                    s = dect_ref[d, tile, kr, c4:c4 + 1] * s + kv
                s_ref[d, pair] = s
        yield

    yield from _for_row_tiles(seq_len, scan_tile, independent=False)

    if has_state_out:
        for d in range(2):
            for pair in range(n_pairs):
                s = s_ref[d, pair]
                sout_refs[d][2 * pair] = s[0:GLA_DK, 0:GLA_DV]
                sout_refs[d][2 * pair + 1] = s[GLA_DK:, GLA_DV:]

    def stage3(i):
        for pair in range(n_pairs):
            kl = slice(pair * pair_k, (pair + 1) * pair_k)
            vl = slice(pair * pair_v, (pair + 1) * pair_v)
            for c4 in range(tile_chunks):
                crow = _tile_rows(i, c4 * c, c)
                q2 = jnp.concatenate([qd_ref[0, crow, kl], qd_ref[1, crow, kl]], axis=1)
                o_ref[crow, vl] = o_ref[crow, vl] + _dot(q2, sst_ref[pair, i * tile_chunks + c4])
        yield
        rows = _tile_rows(i)
        heads = []
        for h in range(GLA_HEADS):
            hl = slice(h * GLA_DV, (h + 1) * GLA_DV)
            oh = o_ref[rows, hl]
            r = lax.rsqrt(jnp.mean(oh * oh, axis=-1, keepdims=True) + EPS)
            heads.append(oh * r * ggla_ref[:, hl])
        y_b = jnp.concatenate(heads, axis=1) * _silu(og_ref[rows, :])
        y = jnp.concatenate([ya_ref[rows, :], y_b.astype(BF16)], axis=1)
        x1_ref[rows, :] = x_ref[rows, :] + m[2] * _dot(y, wout_ref[...])
        yield

    yield from _for_row_tiles(seq_len, stage3, independent=True)


def _mixer_call(x, mod, states, weights, *, period, has_state_out, seqs_per_step):
    n_seq, seq_len, _ = x.shape
    sps = seqs_per_step
    has_state_in = states is not None
    stages_w_in = weights[1].dtype == F32
    kernel = functools.partial(_mixer_kernel, seqs_per_step=sps, seq_len=seq_len, period=period,
                               has_state_in=has_state_in, has_state_out=has_state_out, stages_w_in=stages_w_in)
    state_spec = pl.BlockSpec((sps, None, GLA_HEADS, GLA_DK, GLA_DV), lambda b: (b, 0, 0, 0, 0))
    const2 = lambda b: (0, 0)
    in_specs = [pl.BlockSpec((sps, seq_len, D_MODEL), lambda b: (b, 0, 0)),
                _mod_spec(mod, sps)]
    args = [x, mod]
    if has_state_in:
        in_specs += [state_spec, state_spec]
        args += list(states)
    in_specs += [pl.BlockSpec(w.shape, lambda b, nd=w.ndim: (0,) * nd, pipeline_mode=pl.Buffered(1))
                 for w in weights]
    args += list(weights)
    out_specs = [pl.BlockSpec((sps, seq_len, D_MODEL), lambda b: (b, 0, 0))]
    out_shape = [jax.ShapeDtypeStruct((n_seq, seq_len, D_MODEL), F32)]
    if has_state_out:
        out_specs += [state_spec, state_spec]
        out_shape += [jax.ShapeDtypeStruct((n_seq, 1, GLA_HEADS, GLA_DK, GLA_DV), F32)] * 2
    if stages_w_in:
        out_specs.append(pl.BlockSpec((D_MODEL, P_PAD), const2))
        out_shape.append(jax.ShapeDtypeStruct((D_MODEL, P_PAD), BF16))
    n_tiles = seq_len // ROW_TILE
    n_pairs = GLA_HEADS // 2
    per_seq = lambda shape, dtype: pltpu.VMEM((sps,) + shape, dtype)
    scratch = [per_seq((seq_len, GLA_DV_TOT), F32),
               per_seq((2, seq_len, GLA_DK_TOT), BF16),
               per_seq((2, seq_len, GLA_DK_TOT), BF16),
               per_seq((2, n_tiles, GLA_DK_TOT, ROW_TILE), BF16),
               per_seq((2, n_tiles, GLA_DK_TOT, LANES), F32),
               per_seq((seq_len, GLA_DV_TOT), BF16),
               per_seq((2, n_pairs, 2 * GLA_DK, 2 * GLA_DV), F32),
               per_seq((n_pairs, seq_len // GLA_CHUNK, 4 * GLA_DK, 2 * GLA_DV), BF16),
               per_seq((seq_len, GLA_DV_TOT), F32),
               per_seq((seq_len, D_CONV), BF16)]
    return pl.pallas_call(
        kernel,
        grid=(n_seq // sps,),
        in_specs=in_specs,
        out_specs=out_specs,
        out_shape=out_shape,
        scratch_shapes=scratch,
        compiler_params=_params(1),
        name="mixer",
    )(*args)


SUBLANES = 8


ROW_TOKENS = LANES // N_EXPERTS


def _route_kernel(x1_ref, mod_ref, g2_ref, wr2_ref, *refs, seqs_per_step, **static):
    programs = [_route_sequence(x1_ref.at[j], _mod_rows(mod_ref, j), g2_ref, wr2_ref, *[r.at[j] for r in refs],
                                **static)
                for j in range(seqs_per_step)]
    for _ in _staggered(programs):
        pass


def _route_sequence(x1_ref, m, g2_ref, wr2_ref, xs_ref, rank_ref, gate_ref,
                    pt_ref, h2_ref, spread_ref, bar_ref, cnt_ref, win_ref, *, seq_len, cap):
    n_rows = seq_len // ROW_TOKENS
    for t in range(seq_len // ROW_TILE):
        rows = slice(t * ROW_TILE, (t + 1) * ROW_TILE)
        h2 = _modulated_norm(x1_ref[rows, :], g2_ref[...], m[4], m[3])
        hi = h2.astype(BF16)
        h2_ref[rows, :] = hi
        yield
        by_hi = _dot_nt(wr2_ref[...], hi)
        pt_ref[:, rows] = by_hi[:N_EXPERTS] + by_hi[N_EXPERTS:]
    yield

    logits = pt_ref[...]
    ex = jnp.exp(logits - jnp.max(logits, axis=0, keepdims=True))
    probs = ex / jnp.sum(ex, axis=0, keepdims=True)
    pt_ref[...] = probs
    pad = jnp.zeros((LANES - N_EXPERTS, seq_len), F32)
    p_tok = jnp.transpose(jnp.concatenate([probs, pad], axis=0))

    spread = p_tok
    sh = N_EXPERTS
    while sh < LANES:
        spread = spread + pltpu.roll(spread, sh, 1)
        sh *= 2
    spread_ref[...] = spread
    sub_j = lax.broadcasted_iota(I32, (SUBLANES, 1), 0)
    lane_g = lax.broadcasted_iota(I32, (1, LANES), 1) >> (N_EXPERTS.bit_length() - 1)
    lane_j = (ROW_TOKENS - lane_g) & (ROW_TOKENS - 1)
    own_group = jnp.where(lane_j == sub_j, spread.reshape(n_rows, SUBLANES, LANES), 0.0)
    bar = jnp.sum(own_group, axis=1, keepdims=True)
    bar_ref[...] = jnp.broadcast_to(bar, (n_rows, SUBLANES, LANES))
    cnt_ref[...] = jnp.zeros((n_rows, SUBLANES, LANES), F32)

    rows_per_block = LANES // ROW_TOKENS

    def count_block(g_s, g_t, relation):
        s_blk = spread_ref[g_s * LANES:(g_s + 1) * LANES, :]
        wins = [jnp.zeros((SUBLANES, LANES), F32)] * rows_per_block
        for tl in range(rows_per_block):
            r = g_t * rows_per_block + tl
            bar_r = bar_ref[r]
            acc = cnt_ref[r]
            for sl in range(rows_per_block):
                s_vreg = s_blk[sl * SUBLANES:(sl + 1) * SUBLANES]
                if relation == "before" or sl < tl:
                    won = jnp.where(s_vreg >= bar_r, 1.0, 0.0)
                    acc = acc + won
                    wins[sl] = wins[sl] + won
                elif sl == tl:
                    acc = acc + jnp.where(s_vreg > bar_r, 1.0, 0.0) \
                        + jnp.where((s_vreg == bar_r) & (sub_j < lane_j), 1.0, 0.0)
            cnt_ref[r] = acc
        w_rows = slice(g_s * rows_per_block, (g_s + 1) * rows_per_block)
        win_ref[w_rows] = win_ref[w_rows] + jnp.stack(wins)

    n_grp = seq_len // LANES
    win_ref[...] = jnp.zeros((n_rows, SUBLANES, LANES), F32)
    yield
    for g_t in range(n_grp):
        for g_s in range(g_t + 1):
            count_block(g_s, g_t, "before" if g_s < g_t else "same")
            yield

    counts = jnp.sum(cnt_ref[...], axis=1, keepdims=True)
    counts = jnp.broadcast_to(counts, (n_rows, SUBLANES, LANES)).reshape(seq_len, LANES)
    rank_tok = pltpu.roll(counts, 0, 1, stride=N_EXPERTS, stride_axis=0)
    wins = win_ref[...].reshape(seq_len, LANES)
    sh = N_EXPERTS
    while sh < LANES:
        wins = wins + pltpu.roll(wins, sh, 1)
        sh *= 2
    tok = lax.broadcasted_iota(I32, (seq_len, 1), 0)
    later = ((n_rows - 1 - (tok >> (ROW_TOKENS.bit_length() - 1))) * ROW_TOKENS).astype(F32)
    rank_tok = rank_tok + (later - wins)
    rank_ref[...] = jnp.transpose(rank_tok)[:N_EXPERTS].astype(I32)
    yield

    group = DISPATCH_ROWS // cap
    slot = lax.broadcasted_iota(I32, (cap, 1), 0)
    half = D_MODEL // 2
    for gi in range(N_EXPERTS // group):
        picks = []
        for e in range(gi * group, (gi + 1) * group):
            oh = rank_ref[e:e + 1, :] == slot
            gate = jnp.sum(jnp.where(oh, pt_ref[e:e + 1, :], 0.0), axis=1, keepdims=True)
            gate_ref[e * cap:(e + 1) * cap, :] = jnp.broadcast_to(gate, (cap, LANES))
            picks.append(oh)
        ohb = jnp.where(jnp.concatenate(picks, axis=0), 1.0, 0.0).astype(BF16)
        yield
        out_rows = slice(gi * DISPATCH_ROWS, (gi + 1) * DISPATCH_ROWS)
        xs_ref[out_rows, :half] = _dot(ohb, h2_ref[:, :half]).astype(BF16)
        xs_ref[out_rows, half:] = _dot(ohb, h2_ref[:, half:]).astype(BF16)
        yield


def _route_call(x1, mod, g2, wr_both, *, seqs_per_step):
    n_seq, seq_len, _ = x1.shape
    sps = seqs_per_step
    cap = EC_CAPACITY_FACTOR * seq_len // N_EXPERTS
    kernel = functools.partial(_route_kernel, seqs_per_step=sps, seq_len=seq_len, cap=cap)
    const2 = lambda b: (0, 0)
    per_seq = lambda shape, dtype: pltpu.VMEM((sps,) + shape, dtype)
    rank_rows = (seq_len // ROW_TOKENS, SUBLANES, LANES)
    return pl.pallas_call(
        kernel,
        grid=(n_seq // sps,),
        in_specs=[pl.BlockSpec((sps, seq_len, D_MODEL), lambda b: (b, 0, 0)),
                  _mod_spec(mod, sps),
                  pl.BlockSpec((1, D_MODEL), const2),
                  pl.BlockSpec((2 * N_EXPERTS, D_MODEL), const2)],
        out_specs=[pl.BlockSpec((sps, N_EXPERTS * cap, D_MODEL), lambda b: (b, 0, 0)),
                   pl.BlockSpec((sps, N_EXPERTS, seq_len), lambda b: (b, 0, 0)),
                   pl.BlockSpec((sps, N_EXPERTS * cap, LANES), lambda b: (b, 0, 0))],
        out_shape=[jax.ShapeDtypeStruct((n_seq, N_EXPERTS * cap, D_MODEL), BF16),
                   jax.ShapeDtypeStruct((n_seq, N_EXPERTS, seq_len), I32),
                   jax.ShapeDtypeStruct((n_seq, N_EXPERTS * cap, LANES), F32)],
        scratch_shapes=[per_seq((N_EXPERTS, seq_len), F32),
                        per_seq((seq_len, D_MODEL), BF16),
                        per_seq((seq_len, LANES), F32),
                        per_seq(rank_rows, F32),
                        per_seq(rank_rows, F32),
                        per_seq(rank_rows, F32)],
        compiler_params=_params(1),
        name="route",
    )(x1, mod, g2, wr_both)


def _experts_kernel(xc_ref, xl_ref, gc_ref, gl_ref, wg_ref, wu_ref, wd_ref, yc_ref, yl_ref,
                    wgb_ref, wub_ref, wdb_ref, a_ref):
    wgb_ref[...] = wg_ref[...].astype(BF16)
    wub_ref[...] = wu_ref[...].astype(BF16)
    wdb_ref[...] = wd_ref[...].astype(BF16)
    f_tile = 2 * LANES

    def run(x_ref, g_ref, y_ref):
        n_seq, cap, _ = x_ref.shape
        seqs = DISPATCH_ROWS // cap
        for s0 in range(0, n_seq, seqs):
            x = x_ref[s0:s0 + seqs].reshape(DISPATCH_ROWS, D_MODEL)
            for f0 in range(0, D_EXPERT, f_tile):
                cols = slice(f0, f0 + f_tile)
                a_ref[:, cols] = (_silu(_dot(x, wgb_ref[:, cols])) * _dot(x, wub_ref[:, cols])).astype(BF16)
            gate = g_ref[s0:s0 + seqs].reshape(DISPATCH_ROWS, LANES)
            y = _dot(a_ref[...], wdb_ref[...]) * jnp.concatenate([gate] * (D_MODEL // LANES), axis=1)
            y_ref[s0:s0 + seqs] = y.astype(BF16).reshape(seqs, cap, D_MODEL)

    run(xc_ref, gc_ref, yc_ref)
    run(xl_ref, gl_ref, yl_ref)


def _experts_call(xs_ctx, xs_lat, gates_ctx, gates_lat, w_gate, w_up, w_down):
    def slot_spec(a):
        n_seq, _, cap, width = a.shape
        return pl.BlockSpec((n_seq, None, cap, width), lambda e: (0, e, 0, 0))

    w_spec = pl.BlockSpec((None, D_MODEL, D_EXPERT), lambda e: (e, 0, 0))
    return pl.pallas_call(
        _experts_kernel,
        grid=(N_EXPERTS,),
        in_specs=[slot_spec(xs_ctx), slot_spec(xs_lat), slot_spec(gates_ctx), slot_spec(gates_lat),
                  w_spec, w_spec, pl.BlockSpec((None, D_EXPERT, D_MODEL), lambda e: (e, 0, 0))],
        out_specs=[slot_spec(xs_ctx), slot_spec(xs_lat)],
        out_shape=[jax.ShapeDtypeStruct(xs_ctx.shape, BF16), jax.ShapeDtypeStruct(xs_lat.shape, BF16)],
        scratch_shapes=[pltpu.VMEM((D_MODEL, D_EXPERT), BF16),
                        pltpu.VMEM((D_MODEL, D_EXPERT), BF16),
                        pltpu.VMEM((D_EXPERT, D_MODEL), BF16),
                        pltpu.VMEM((DISPATCH_ROWS, D_EXPERT), BF16)],
        compiler_params=_params(1),
        name="experts",
    )(xs_ctx, xs_lat, gates_ctx, gates_lat, w_gate, w_up, w_down)


def _combine_tokens(x1_ref, y_ref, rank_ref, gate2, gf_ref, o_ref, oh_ref):
    n_tok = x1_ref.shape[0]
    cap = y_ref.shape[0] // N_EXPERTS
    slot = lax.broadcasted_iota(I32, (cap, 1), 0)
    for e in range(N_EXPERTS):
        oh_ref[e * cap:(e + 1) * cap, :] = jnp.where(rank_ref[e:e + 1, :] == slot, 1.0, 0.0).astype(BF16)
    for t in range(n_tok // ROW_TILE):
        rows = slice(t * ROW_TILE, (t + 1) * ROW_TILE)
        moe = _dot_tn(oh_ref[:, rows], y_ref[...])
        x2 = x1_ref[rows, :] + gate2 * moe
        r = lax.rsqrt(jnp.mean(x2 * x2, axis=-1, keepdims=True) + EPS)
        o_ref[rows, :] = (x2 * r) * gf_ref[...]


def _combine_kernel(x1c_ref, yc_ref, rkc_ref, modc_ref, x1l_ref, yl_ref, rkl_ref, modl_ref, gf_ref,
                    oc_ref, ol_ref, ohc_ref, ohl_ref):
    for j in range(x1c_ref.shape[0]):
        _combine_tokens(x1c_ref.at[j], yc_ref.at[j], rkc_ref.at[j], _mod_rows(modc_ref, j)[5], gf_ref,
                        oc_ref.at[j], ohc_ref.at[j])
    _combine_tokens(x1l_ref.at[0], yl_ref.at[0], rkl_ref.at[0], _mod_rows(modl_ref, 0)[5], gf_ref,
                    ol_ref.at[0], ohl_ref)


def _combine_call(ctx, lat, g_final, *, ctx_seqs_per_step):
    x1c, yc, rkc, modc = ctx
    x1l, yl, rkl, modl = lat
    n_ctx, ctx_len, _ = x1c.shape
    n_lat, lat_len, _ = x1l.shape
    sps = ctx_seqs_per_step
    n_steps = n_ctx // sps
    slabs = n_steps // n_lat
    slab = lat_len // slabs
    assert n_steps == n_lat * slabs and slab % ROW_TILE == 0 and modl.shape[0] == n_lat
    ctx_blk = lambda shape: pl.BlockSpec((sps,) + shape, lambda b: (b, 0, 0))
    lat_seq = lambda b: (b // slabs, 0, 0)
    return pl.pallas_call(
        _combine_kernel,
        grid=(n_steps,),
        in_specs=[ctx_blk((ctx_len, D_MODEL)), ctx_blk(yc.shape[1:]), ctx_blk((N_EXPERTS, ctx_len)),
                  _mod_spec(modc, sps),
                  pl.BlockSpec((1, slab, D_MODEL), lambda b: (b // slabs, b % slabs, 0)),
                  pl.BlockSpec((1,) + yl.shape[1:], lat_seq),
                  pl.BlockSpec((1, N_EXPERTS, slab), lambda b: (b // slabs, 0, b % slabs)),
                  pl.BlockSpec((1,) + modl.shape[1:], lat_seq),
                  pl.BlockSpec((1, D_MODEL), lambda b: (0, 0))],
        out_specs=[ctx_blk((ctx_len, D_MODEL)),
                   pl.BlockSpec((1, slab, D_MODEL), lambda b: (b // slabs, b % slabs, 0))],
        out_shape=[jax.ShapeDtypeStruct(x1c.shape, F32), jax.ShapeDtypeStruct(x1l.shape, F32)],
        scratch_shapes=[pltpu.VMEM((sps, yc.shape[1], ctx_len), BF16),
                        pltpu.VMEM((yl.shape[1], slab), BF16)],
        compiler_params=_params(1),
        name="combine",
    )(x1c, yc, rkc, modc, x1l, yl, rkl, modl, g_final)


def kernel(x_prompt, x_sample, state_gla_fwd, state_gla_bwd, c, c_ctx, w_mod, b_mod, g_norm1, g_norm2,
           w_in, w_conv, b_conv, w_a_up_f, b_a_f, w_a_up_b, b_a_b, g_gla_norm, w_out, w_router,
           w_gate, w_up, w_down, g_final):
    assert w_mod.shape[0] == 1, "single trunk layer"
    n_ctx, ctx_len, _ = x_prompt.shape
    n_lat, lat_len, _ = x_sample.shape
    ctx_cap = EC_CAPACITY_FACTOR * ctx_len // N_EXPERTS
    lat_cap = EC_CAPACITY_FACTOR * lat_len // N_EXPERTS

    c_rows = jnp.concatenate([c_ctx[None, :], c, jnp.zeros((8 - 1 - n_lat, D_MODEL), F32)], axis=0)
    mod_ctx, mod_lat = _mod_call(c_rows, n_lat, w_mod[0], b_mod)

    mixer_weights = [g_norm1, jnp.transpose(w_in[0]), w_conv, b_conv, w_a_up_f, b_a_f, w_a_up_b, b_a_b,
                     g_gla_norm[0].reshape(1, GLA_DV_TOT), w_out[0].astype(BF16)]
    assert len(mixer_weights) == N_MIXER_WEIGHTS
    wr_t = jnp.transpose(w_router[0])
    wr_hi = wr_t.astype(BF16)
    wr_lo = (wr_t - wr_hi.astype(F32)).astype(BF16)

    x1_ctx, new_f, new_b, w_in_bf = _mixer_call(x_prompt, mod_ctx, None, mixer_weights,
                                                period=ctx_len, has_state_out=True, seqs_per_step=4)
    mixer_weights[1] = w_in_bf
    (x1_lat,) = _mixer_call(x_sample, mod_lat, (state_gla_fwd, state_gla_bwd), mixer_weights,
                            period=GRID_W, has_state_out=False, seqs_per_step=1)

    wr_both = jnp.concatenate([wr_hi, wr_lo], axis=0)
    xs_ctx, rank_ctx, gates_ctx = _route_call(x1_ctx, mod_ctx, g_norm2, wr_both, seqs_per_step=4)
    xs_lat, rank_lat, gates_lat = _route_call(x1_lat, mod_lat, g_norm2, wr_both, seqs_per_step=1)

    per_expert = lambda a, n, cap: a.reshape(n, N_EXPERTS, cap, a.shape[-1])
    y_ctx, y_lat = _experts_call(per_expert(xs_ctx, n_ctx, ctx_cap), per_expert(xs_lat, n_lat, lat_cap),
                                 per_expert(gates_ctx, n_ctx, ctx_cap), per_expert(gates_lat, n_lat, lat_cap),
                                 w_gate[0], w_up[0], w_down[0])

    g_fin = g_final[None, :]
    y_prompt, y_sample = _combine_call((x1_ctx, y_ctx.reshape(xs_ctx.shape), rank_ctx, mod_ctx),
                                       (x1_lat, y_lat.reshape(xs_lat.shape), rank_lat, mod_lat),
                                       g_fin, ctx_seqs_per_step=4)
    return y_prompt, y_sample, new_f, new_b
```

```python
import functools

import jax
import jax.numpy as jnp
import numpy as np
from jax import lax
from jax.experimental import pallas as pl
from jax.experimental.pallas import tpu as pltpu

F32 = jnp.float32
BF16 = jnp.bfloat16
I32 = jnp.int32

D_MODEL = 1024
D_CONV = D_MODEL // 2
GRID_W = 64
GLA_HEADS = 4
GLA_DK = 64
GLA_DV = 128
GLA_DK_TOT = GLA_HEADS * GLA_DK
GLA_DV_TOT = GLA_HEADS * GLA_DV
GLA_LOW_RANK = 16
GLA_TAU = 16.0
GLA_CHUNK = 64
N_EXPERTS = 16
EC_CAPACITY_FACTOR = 2
D_EXPERT = 1024
N_MOD = 6
EPS = 1e-6
LOG2_E = 1.4426950408889634

OFF_XB = 0
OFF_XC = D_CONV
OFF_XV = 2 * D_CONV
OFF_Q = 3 * D_CONV
OFF_K = OFF_Q + GLA_DK_TOT
OFF_V = OFF_K + GLA_DK_TOT
OFF_OG = OFF_V + GLA_DV_TOT
OFF_ALOW = OFF_OG + GLA_DV_TOT
P_TOT = OFF_ALOW + 2 * GLA_LOW_RANK

LANES = 128
P_PAD = -(-P_TOT // LANES) * LANES
ROW_TILE = 256
N_MIXER_WEIGHTS = 10
DISPATCH_ROWS = 512
VMEM_LIMIT = 56 * 1024 * 1024


def _dot(a, b):
    return jnp.dot(a, b, preferred_element_type=F32)


def _dot_nt(a, b):
    return lax.dot_general(a, b, (((1,), (1,)), ((), ())), preferred_element_type=F32)


def _dot_tn(a, b):
    return lax.dot_general(a, b, (((0,), (0,)), ((), ())), preferred_element_type=F32)


def _split(a):
    hi = a.astype(BF16)
    lo = (a - hi.astype(F32)).astype(BF16)
    return hi, lo


def _silu(x):
    return x * jax.nn.sigmoid(x)


def _modulated_norm(x, g, scale, shift):
    r = lax.rsqrt(jnp.mean(x * x, axis=-1, keepdims=True) + EPS)
    return (x * r) * (g * (1.0 + scale)) + shift


def _params(n_axes):
    return pltpu.CompilerParams(dimension_semantics=("arbitrary",) * n_axes,
                                vmem_limit_bytes=VMEM_LIMIT)


def _mod_kernel(cctx_ref, c_ref, w_ref, b_ref, ctx_ref, lat_ref, acc_ref):
    rows = acc_ref.shape[0]
    n_lat = c_ref.shape[0]
    fill = jnp.zeros((rows - 1 - n_lat, c_ref.shape[1]), F32)
    s = _silu(jnp.concatenate([cctx_ref[...], c_ref[...], fill], axis=0))
    s_hi, s_lo = _split(jnp.concatenate([s, s], axis=0))
    upper = lax.broadcasted_iota(I32, (2 * rows, 1), 0) < rows
    w_hi, w_lo = _split(w_ref[...])
    by_hi = _dot(jnp.where(upper, s_hi, s_lo), w_hi)
    part = by_hi[:rows] + by_hi[rows:] + _dot(s_hi[:rows], w_lo)

    @pl.when(pl.program_id(0) == 0)
    def _():
        acc_ref[...] = part + b_ref[...]

    @pl.when(pl.program_id(0) != 0)
    def _():
        acc_ref[...] = acc_ref[...] + part

    @pl.when(pl.program_id(0) == pl.num_programs(0) - 1)
    def _():
        ctx_ref[0] = acc_ref[0:1, :]
        for i in range(n_lat):
            lat_ref[i] = acc_ref[1 + i:2 + i, :]


def _mod_call(c_ctx, c, w_mod, b_mod):
    n_lat, d = c.shape
    n = w_mod.shape[1]
    rows = -(-(1 + n_lat) // 8) * 8
    tk = D_MODEL // 4
    return pl.pallas_call(
        _mod_kernel,
        grid=(d // tk,),
        in_specs=[pl.BlockSpec((1, tk), lambda k: (0, k)),
                  pl.BlockSpec((n_lat, tk), lambda k: (0, k)),
                  pl.BlockSpec((tk, n), lambda k: (k, 0)),
                  pl.BlockSpec((1, n), lambda k: (0, 0))],
        out_specs=[pl.BlockSpec((1, 1, n), lambda k: (0, 0, 0)),
                   pl.BlockSpec((n_lat, 1, n), lambda k: (0, 0, 0))],
        out_shape=[jax.ShapeDtypeStruct((1, 1, n), F32), jax.ShapeDtypeStruct((n_lat, 1, n), F32)],
        scratch_shapes=[pltpu.VMEM((rows, n), F32)],
        compiler_params=_params(1),
        name="mod",
    )(c_ctx, c, w_mod, b_mod)


def _mod_rows(mod_ref, j):
    row = mod_ref[j % mod_ref.shape[0]]
    return [row[:, i * D_MODEL:(i + 1) * D_MODEL] for i in range(N_MOD)]


def _mod_spec(mod, seqs_per_step):
    if mod.shape[0] == 1:
        return pl.BlockSpec(mod.shape, lambda b: (0, 0, 0))
    return pl.BlockSpec((seqs_per_step,) + mod.shape[1:], lambda b: (b, 0, 0))


def _staggered(programs):
    programs = list(programs)
    started = 0
    while programs:
        started = min(started + 1, len(programs))
        running = [p for p in programs[:started] if next(p, "done") != "done"]
        programs = running + programs[started:]
        started = len(running)
        yield


def _for_row_tiles(seq_len, phases, independent):
    n = seq_len // ROW_TILE
    if independent:
        yield from _staggered(phases(i) for i in range(n))
    else:
        for i in range(n):
            yield from phases(i)


def _tile_rows(tile, offset=0, size=ROW_TILE):
    if isinstance(tile, int):
        return pl.ds(tile * ROW_TILE + offset, size)
    return pl.ds(pl.multiple_of(tile * ROW_TILE + offset, size), size)


def _mixer_kernel(*refs, seqs_per_step, has_state_in, has_state_out, stages_w_in, **static):
    refs = list(refs)
    n_in = 2 + (2 if has_state_in else 0)
    per_seq_in, refs = [refs[0]] + refs[2:n_in], [refs[1]] + refs[n_in:]
    mod_ref, weights, refs = refs[0], refs[1:1 + N_MIXER_WEIGHTS], refs[1 + N_MIXER_WEIGHTS:]
    n_out = 1 + (2 if has_state_out else 0)
    per_seq_out, refs = refs[:n_out], refs[n_out:]
    if stages_w_in:
        win_f32_ref, win_bf_ref, scratch = weights[1], refs[0], refs[1:]
        weights = weights[:1] + [win_bf_ref] + weights[2:]

        @pl.when(pl.program_id(0) == 0)
        def _():
            for c0 in range(0, P_PAD, LANES):
                n = min(LANES, P_TOT - c0)
                cols = win_f32_ref[c0:c0 + n, :]
                if n < LANES:
                    cols = jnp.concatenate([cols, jnp.zeros((LANES - n, D_MODEL), F32)], axis=0)
                win_bf_ref[:, c0:c0 + LANES] = jnp.transpose(cols).astype(BF16)
    else:
        scratch = refs
    programs = []
    for j in range(seqs_per_step):
        ins = [r.at[j] for r in per_seq_in]
        outs = [r.at[j] for r in per_seq_out]
        programs.append(_mixer_sequence(ins[0], _mod_rows(mod_ref, j), ins[1:], weights, outs[0], outs[1:],
                                        [r.at[j] for r in scratch], **static))
    for _ in _staggered(programs):
        pass


def _mixer_sequence(x_ref, m, s0_refs, weights, x1_ref, sout_refs, scratch, *, seq_len, period):
    has_state_in = bool(s0_refs)
    has_state_out = bool(sout_refs)
    g1_ref, win_ref, wconv_ref, bconv_ref, wupf_ref, bupf_ref, wupb_ref, bupb_ref, ggla_ref, wout_ref = weights
    og_ref, qd_ref, kd_ref, kst_ref, dect_ref, v_ref, s_ref, sst_ref, o_ref, ya_ref = scratch

    c = GLA_CHUNK
    tile_chunks = ROW_TILE // c
    n_tiles = seq_len // ROW_TILE
    n_pairs = GLA_HEADS // 2
    pair_k = 2 * GLA_DK
    pair_v = 2 * GLA_DV

    def stage1(ti):
        rows = _tile_rows(ti)
        h = _modulated_norm(x_ref[rows, :], g1_ref[...], m[1], m[0]).astype(BF16)
        row_i = lax.broadcasted_iota(I32, (ROW_TILE, 1), 0)
        yield
        p_gate = _dot(h, win_ref[:, OFF_OG:P_PAD])
        og_ref[rows, :] = p_gate[:, :GLA_DV_TOT]
        yield
        zero_up = jnp.zeros((GLA_LOW_RANK, GLA_DK_TOT), F32)
        w_up = jnp.concatenate([jnp.concatenate([wupf_ref[0], zero_up], axis=1),
                                jnp.concatenate([zero_up, wupb_ref[0]], axis=1),
                                jnp.zeros((P_PAD - P_TOT, 2 * GLA_DK_TOT), F32)], axis=0).astype(BF16)
        b_up = jnp.concatenate([bupf_ref[...], bupb_ref[...]], axis=1)
        z = _dot(p_gate[:, GLA_DV_TOT:].astype(BF16), w_up) + b_up
        la = (jnp.minimum(z, 0.0) - jnp.log(1.0 + jnp.exp(-jnp.abs(z)))) * (LOG2_E / GLA_TAU)
        col_j = lax.broadcasted_iota(I32, (1, ROW_TILE), 1)
        same_chunk = (row_i & -c) == (col_j & -c)
        lower = jnp.where(same_chunk & (col_j <= row_i), 1.0, 0.0).astype(BF16)
        la_parts = jnp.concatenate(_split(la), axis=1)
        n_gate = 2 * GLA_DK_TOT
        yield
        pre = _dot(lower, la_parts)
        pre = pre[:, :n_gate] + pre[:, n_gate:]
        tot = jnp.concatenate([jnp.broadcast_to(pre[(n + 1) * c - 1:(n + 1) * c], (c, n_gate))
                               for n in range(tile_chunks)], axis=0)
        p_qkv = _dot(h, win_ref[:, OFF_Q:OFF_OG])
        yield
        q = p_qkv[:, :GLA_DK_TOT] * (GLA_DK ** -0.5)
        k = p_qkv[:, GLA_DK_TOT:2 * GLA_DK_TOT]
        v_ref[rows, :] = p_qkv[:, 2 * GLA_DK_TOT:].astype(BF16)
        for d in range(2):
            cols = slice(d * GLA_DK_TOT, (d + 1) * GLA_DK_TOT)
            if d == 0:
                bq = pre[:, cols]
                bk = tot[:, cols] - bq
            else:
                bk = pre[:, cols] - la[:, cols]
                bq = tot[:, cols] - bk
            qd_ref[d, rows, :] = (q * jnp.exp2(bq)).astype(BF16)
            kd_ref[d, rows, :] = (k * jnp.exp2(-bq)).astype(BF16)
            kst_ref[d, ti] = jnp.transpose(k * jnp.exp2(bk)).astype(BF16)
            totals = [tot[n * c:n * c + 1, cols] for n in range(tile_chunks)]
            totals.append(jnp.zeros((LANES - tile_chunks, GLA_DK_TOT), F32))
            dect_ref[d, ti] = jnp.transpose(jnp.exp2(jnp.concatenate(totals, axis=0)))
        yield
        p_conv = _dot(h, win_ref[:, :OFF_Q])
        yield
        pos = row_i & (period - 1)
        u = p_conv[:, OFF_XC:OFF_XC + D_CONV] * p_conv[:, OFF_XV:OFF_XV + D_CONV]
        u_prev = jnp.where(pos == 0, 0.0, pltpu.roll(u, 1, 0))
        u_next = jnp.where(pos == period - 1, 0.0, pltpu.roll(u, ROW_TILE - 1, 0))
        conv = u_prev * wconv_ref[0] + u * wconv_ref[1] + u_next * wconv_ref[2] + bconv_ref[...]
        ya_ref[rows, :] = (p_conv[:, OFF_XB:OFF_XB + D_CONV] * conv).astype(BF16)
        yield

    yield from _for_row_tiles(seq_len, stage1, independent=True)

    for d in range(2):
        for pair in range(n_pairs):
            if has_state_in:
                zero = jnp.zeros((GLA_DK, GLA_DV), F32)
                top = jnp.concatenate([s0_refs[d][2 * pair], zero], axis=1)
                bot = jnp.concatenate([zero, s0_refs[d][2 * pair + 1]], axis=1)
                s_ref[d, pair] = jnp.concatenate([top, bot], axis=0)
            else:
                s_ref[d, pair] = jnp.zeros((pair_k, pair_v), F32)

    def scan_tile(i):
        upper_lane = lax.broadcasted_iota(I32, (1, LANES), 1) >= GLA_DK
        qi = lax.broadcasted_iota(I32, (LANES, 1), 0)
        kj = lax.broadcasted_iota(I32, (1, 2 * LANES), 1) & (LANES - 1)
        same_chunk = (qi & c) == (kj & c)
        causal = (same_chunk & (kj <= qi), same_chunk & (kj >= qi))
        for pair in range(n_pairs):
            kl = slice(pair * pair_k, (pair + 1) * pair_k)
            vl = slice(pair * pair_v, (pair + 1) * pair_v)
            for blk in range(ROW_TILE // LANES):
                rows = _tile_rows(i, blk * LANES, LANES)
                att = None
                for d in range(2):
                    kd = kd_ref[d, rows, kl]
                    zk = jnp.zeros_like(kd)
                    keys = jnp.concatenate([jnp.where(upper_lane, zk, kd), jnp.where(upper_lane, kd, zk)], axis=0)
                    a = jnp.where(causal[d], _dot_nt(qd_ref[d, rows, kl], keys), 0.0)
                    att = a if att is None else att + a
                v = v_ref[rows, vl]
                zv = jnp.zeros((LANES, GLA_DV), BF16)
                v_bd = jnp.concatenate([jnp.concatenate([v[:, :GLA_DV], zv], axis=1),
                                        jnp.concatenate([zv, v[:, GLA_DV:]], axis=1)], axis=0)
                o_ref[rows, vl] = _dot(att.astype(BF16), v_bd)
        yield
        key_row = lax.broadcasted_iota(I32, (pair_k, 1), 0)
        val_col = lax.broadcasted_iota(I32, (1, pair_v), 1)
        blockdiag = (key_row >= GLA_DK) == (val_col >= GLA_DV)
        for d in range(2):
            tile = i if d == 0 else n_tiles - 1 - i
            chunks = range(tile_chunks)
            for pair in range(n_pairs):
                kr = slice(pair * pair_k, (pair + 1) * pair_k)
                vl = slice(pair * pair_v, (pair + 1) * pair_v)
                s = s_ref[d, pair]
                for c4 in (chunks if d == 0 else reversed(chunks)):
                    blk, half = divmod(c4, 2)
                    kst = kst_ref[d, tile, kr, blk * LANES:(blk + 1) * LANES]
                    kst = jnp.where(upper_lane if half else ~upper_lane, kst, jnp.zeros_like(kst))
                    kv = jnp.where(blockdiag, _dot(kst, v_ref[_tile_rows(tile, blk * LANES, LANES), vl]), 0.0)
                    sst_ref[pair, tile * tile_chunks + c4, d * pair_k:(d + 1) * pair_k, :] = s.astype(BF16)
                    s = dect_ref[d, tile, kr, c4:c4 + 1] * s + kv
                s_ref[d, pair] = s
        yield

    yield from _for_row_tiles(seq_len, scan_tile, independent=False)

    if has_state_out:
        for d in range(2):
            for pair in range(n_pairs):
                s = s_ref[d, pair]
                sout_refs[d][2 * pair] = s[0:GLA_DK, 0:GLA_DV]
                sout_refs[d][2 * pair + 1] = s[GLA_DK:, GLA_DV:]

    def stage3(i):
        for pair in range(n_pairs):
            kl = slice(pair * pair_k, (pair + 1) * pair_k)
            vl = slice(pair * pair_v, (pair + 1) * pair_v)
            for c4 in range(tile_chunks):
                crow = _tile_rows(i, c4 * c, c)
                q2 = jnp.concatenate([qd_ref[0, crow, kl], qd_ref[1, crow, kl]], axis=1)
                o_ref[crow, vl] = o_ref[crow, vl] + _dot(q2, sst_ref[pair, i * tile_chunks + c4])
        yield
        rows = _tile_rows(i)
        heads = []
        for h in range(GLA_HEADS):
            hl = slice(h * GLA_DV, (h + 1) * GLA_DV)
            oh = o_ref[rows, hl]
            r = lax.rsqrt(jnp.mean(oh * oh, axis=-1, keepdims=True) + EPS)
            heads.append(oh * r * ggla_ref[:, hl])
        y_b = jnp.concatenate(heads, axis=1) * _silu(og_ref[rows, :])
        y = jnp.concatenate([ya_ref[rows, :], y_b.astype(BF16)], axis=1)
        x1_ref[rows, :] = x_ref[rows, :] + m[2] * _dot(y, wout_ref[...])
        yield

    yield from _for_row_tiles(seq_len, stage3, independent=True)


def _mixer_call(x, mod, states, weights, *, period, has_state_out, seqs_per_step):
    n_seq, seq_len, _ = x.shape
    sps = seqs_per_step
    has_state_in = states is not None
    stages_w_in = weights[1].dtype == F32
    kernel = functools.partial(_mixer_kernel, seqs_per_step=sps, seq_len=seq_len, period=period,
                               has_state_in=has_state_in, has_state_out=has_state_out, stages_w_in=stages_w_in)
    state_spec = pl.BlockSpec((sps, None, GLA_HEADS, GLA_DK, GLA_DV), lambda b: (b, 0, 0, 0, 0))
    const2 = lambda b: (0, 0)
    in_specs = [pl.BlockSpec((sps, seq_len, D_MODEL), lambda b: (b, 0, 0)),
                _mod_spec(mod, sps)]
    args = [x, mod]
    if has_state_in:
        in_specs += [state_spec, state_spec]
        args += list(states)
    in_specs += [pl.BlockSpec(w.shape, lambda b, nd=w.ndim: (0,) * nd, pipeline_mode=pl.Buffered(1))
                 for w in weights]
    args += list(weights)
    out_specs = [pl.BlockSpec((sps, seq_len, D_MODEL), lambda b: (b, 0, 0))]
    out_shape = [jax.ShapeDtypeStruct((n_seq, seq_len, D_MODEL), F32)]
    if has_state_out:
        out_specs += [state_spec, state_spec]
        out_shape += [jax.ShapeDtypeStruct((n_seq, 1, GLA_HEADS, GLA_DK, GLA_DV), F32)] * 2
    if stages_w_in:
        out_specs.append(pl.BlockSpec((D_MODEL, P_PAD), const2))
        out_shape.append(jax.ShapeDtypeStruct((D_MODEL, P_PAD), BF16))
    n_tiles = seq_len // ROW_TILE
    n_pairs = GLA_HEADS // 2
    per_seq = lambda shape, dtype: pltpu.VMEM((sps,) + shape, dtype)
    scratch = [per_seq((seq_len, GLA_DV_TOT), F32),
               per_seq((2, seq_len, GLA_DK_TOT), BF16),
               per_seq((2, seq_len, GLA_DK_TOT), BF16),
               per_seq((2, n_tiles, GLA_DK_TOT, ROW_TILE), BF16),
               per_seq((2, n_tiles, GLA_DK_TOT, LANES), F32),
               per_seq((seq_len, GLA_DV_TOT), BF16),
               per_seq((2, n_pairs, 2 * GLA_DK, 2 * GLA_DV), F32),
               per_seq((n_pairs, seq_len // GLA_CHUNK, 4 * GLA_DK, 2 * GLA_DV), BF16),
               per_seq((seq_len, GLA_DV_TOT), F32),
               per_seq((seq_len, D_CONV), BF16)]
    return pl.pallas_call(
        kernel,
        grid=(n_seq // sps,),
        in_specs=in_specs,
        out_specs=out_specs,
        out_shape=out_shape,
        scratch_shapes=scratch,
        compiler_params=_params(1),
        name="mixer",
    )(*args)


SUBLANES = 8


ROW_TOKENS = LANES // N_EXPERTS


def _route_kernel(x1_ref, mod_ref, g2_ref, wr2_ref, *refs, seqs_per_step, **static):
    programs = [_route_sequence(x1_ref.at[j], _mod_rows(mod_ref, j), g2_ref, wr2_ref, *[r.at[j] for r in refs],
                                **static)
                for j in range(seqs_per_step)]
    for _ in _staggered(programs):
        pass


def _route_sequence(x1_ref, m, g2_ref, wr2_ref, xs_ref, rank_ref, gate_ref,
                    pt_ref, h2_ref, spread_ref, bar_ref, cnt_ref, win_ref, *, seq_len, cap):
    n_rows = seq_len // ROW_TOKENS
    for t in range(seq_len // ROW_TILE):
        rows = slice(t * ROW_TILE, (t + 1) * ROW_TILE)
        h2 = _modulated_norm(x1_ref[rows, :], g2_ref[...], m[4], m[3])
        hi = h2.astype(BF16)
        h2_ref[rows, :] = hi
        yield
        by_hi = _dot_nt(wr2_ref[...], hi)
        pt_ref[:, rows] = by_hi[:N_EXPERTS] + by_hi[N_EXPERTS:]
    yield

    logits = pt_ref[...]
    ex = jnp.exp(logits - jnp.max(logits, axis=0, keepdims=True))
    probs = ex / jnp.sum(ex, axis=0, keepdims=True)
    pt_ref[...] = probs
    pad = jnp.zeros((LANES - N_EXPERTS, seq_len), F32)
    p_tok = jnp.transpose(jnp.concatenate([probs, pad], axis=0))

    spread = p_tok
    sh = N_EXPERTS
    while sh < LANES:
        spread = spread + pltpu.roll(spread, sh, 1)
        sh *= 2
    spread_ref[...] = spread
    sub_j = lax.broadcasted_iota(I32, (SUBLANES, 1), 0)
    lane_g = lax.broadcasted_iota(I32, (1, LANES), 1) >> (N_EXPERTS.bit_length() - 1)
    lane_j = (ROW_TOKENS - lane_g) & (ROW_TOKENS - 1)
    own_group = jnp.where(lane_j == sub_j, spread.reshape(n_rows, SUBLANES, LANES), 0.0)
    bar = jnp.sum(own_group, axis=1, keepdims=True)
    bar_ref[...] = jnp.broadcast_to(bar, (n_rows, SUBLANES, LANES))
    cnt_ref[...] = jnp.zeros((n_rows, SUBLANES, LANES), F32)

    rows_per_block = LANES // ROW_TOKENS

    def count_block(g_s, g_t, relation):
        s_blk = spread_ref[g_s * LANES:(g_s + 1) * LANES, :]
        wins = [jnp.zeros((SUBLANES, LANES), F32)] * rows_per_block
        for tl in range(rows_per_block):
            r = g_t * rows_per_block + tl
            bar_r = bar_ref[r]
            acc = cnt_ref[r]
            for sl in range(rows_per_block):
                s_vreg = s_blk[sl * SUBLANES:(sl + 1) * SUBLANES]
                if relation == "before" or sl < tl:
                    won = jnp.where(s_vreg >= bar_r, 1.0, 0.0)
                    acc = acc + won
                    wins[sl] = wins[sl] + won
                elif sl == tl:
                    acc = acc + jnp.where(s_vreg > bar_r, 1.0, 0.0) \
                        + jnp.where((s_vreg == bar_r) & (sub_j < lane_j), 1.0, 0.0)
            cnt_ref[r] = acc
        w_rows = slice(g_s * rows_per_block, (g_s + 1) * rows_per_block)
        win_ref[w_rows] = win_ref[w_rows] + jnp.stack(wins)

    n_grp = seq_len // LANES
    win_ref[...] = jnp.zeros((n_rows, SUBLANES, LANES), F32)
    yield
    for g_t in range(n_grp):
        for g_s in range(g_t + 1):
            count_block(g_s, g_t, "before" if g_s < g_t else "same")
            yield

    counts = jnp.sum(cnt_ref[...], axis=1, keepdims=True)
    counts = jnp.broadcast_to(counts, (n_rows, SUBLANES, LANES)).reshape(seq_len, LANES)
    rank_tok = pltpu.roll(counts, 0, 1, stride=N_EXPERTS, stride_axis=0)
    wins = win_ref[...].reshape(seq_len, LANES)
    sh = N_EXPERTS
    while sh < LANES:
        wins = wins + pltpu.roll(wins, sh, 1)
        sh *= 2
    tok = lax.broadcasted_iota(I32, (seq_len, 1), 0)
    later = ((n_rows - 1 - (tok >> (ROW_TOKENS.bit_length() - 1))) * ROW_TOKENS).astype(F32)
    rank_tok = rank_tok + (later - wins)
    rank_ref[...] = jnp.transpose(rank_tok)[:N_EXPERTS].astype(I32)
    yield

    group = DISPATCH_ROWS // cap
    slot = lax.broadcasted_iota(I32, (cap, 1), 0)
    half = D_MODEL // 2
    for gi in range(N_EXPERTS // group):
        picks = []
        for e in range(gi * group, (gi + 1) * group):
            oh = rank_ref[e:e + 1, :] == slot
            gate = jnp.sum(jnp.where(oh, pt_ref[e:e + 1, :], 0.0), axis=1, keepdims=True)
            gate_ref[e * cap:(e + 1) * cap, :] = jnp.broadcast_to(gate, (cap, LANES))
            picks.append(oh)
        ohb = jnp.where(jnp.concatenate(picks, axis=0), 1.0, 0.0).astype(BF16)
        yield
        out_rows = slice(gi * DISPATCH_ROWS, (gi + 1) * DISPATCH_ROWS)
        xs_ref[out_rows, :half] = _dot(ohb, h2_ref[:, :half]).astype(BF16)
        xs_ref[out_rows, half:] = _dot(ohb, h2_ref[:, half:]).astype(BF16)
        yield


def _route_call(x1, mod, g2, wr_both, *, seqs_per_step):
    n_seq, seq_len, _ = x1.shape
    sps = seqs_per_step
    cap = EC_CAPACITY_FACTOR * seq_len // N_EXPERTS
    kernel = functools.partial(_route_kernel, seqs_per_step=sps, seq_len=seq_len, cap=cap)
    const2 = lambda b: (0, 0)
    per_seq = lambda shape, dtype: pltpu.VMEM((sps,) + shape, dtype)
    rank_rows = (seq_len // ROW_TOKENS, SUBLANES, LANES)
    return pl.pallas_call(
        kernel,
        grid=(n_seq // sps,),
        in_specs=[pl.BlockSpec((sps, seq_len, D_MODEL), lambda b: (b, 0, 0)),
                  _mod_spec(mod, sps),
                  pl.BlockSpec((1, D_MODEL), const2),
                  pl.BlockSpec((2 * N_EXPERTS, D_MODEL), const2)],
        out_specs=[pl.BlockSpec((sps, N_EXPERTS * cap, D_MODEL), lambda b: (b, 0, 0)),
                   pl.BlockSpec((sps, N_EXPERTS, seq_len), lambda b: (b, 0, 0)),
                   pl.BlockSpec((sps, N_EXPERTS * cap, LANES), lambda b: (b, 0, 0))],
        out_shape=[jax.ShapeDtypeStruct((n_seq, N_EXPERTS * cap, D_MODEL), BF16),
                   jax.ShapeDtypeStruct((n_seq, N_EXPERTS, seq_len), I32),
                   jax.ShapeDtypeStruct((n_seq, N_EXPERTS * cap, LANES), F32)],
        scratch_shapes=[per_seq((N_EXPERTS, seq_len), F32),
                        per_seq((seq_len, D_MODEL), BF16),
                        per_seq((seq_len, LANES), F32),
                        per_seq(rank_rows, F32),
                        per_seq(rank_rows, F32),
                        per_seq(rank_rows, F32)],
        compiler_params=_params(1),
        name="route",
    )(x1, mod, g2, wr_both)


def _experts_kernel(xc_ref, xl_ref, gc_ref, gl_ref, wg_ref, wu_ref, wd_ref, yc_ref, yl_ref,
                    wgb_ref, wub_ref, wdb_ref, a_ref):
    wgb_ref[...] = wg_ref[...].astype(BF16)
    wub_ref[...] = wu_ref[...].astype(BF16)
    wdb_ref[...] = wd_ref[...].astype(BF16)
    f_tile = 2 * LANES

    def run(x_ref, g_ref, y_ref):
        n_seq, cap, _ = x_ref.shape
        seqs = DISPATCH_ROWS // cap
        for s0 in range(0, n_seq, seqs):
            x = x_ref[s0:s0 + seqs].reshape(DISPATCH_ROWS, D_MODEL)
            for f0 in range(0, D_EXPERT, f_tile):
                cols = slice(f0, f0 + f_tile)
                a_ref[:, cols] = (_silu(_dot(x, wgb_ref[:, cols])) * _dot(x, wub_ref[:, cols])).astype(BF16)
            gate = g_ref[s0:s0 + seqs].reshape(DISPATCH_ROWS, LANES)
            y = _dot(a_ref[...], wdb_ref[...]) * jnp.concatenate([gate] * (D_MODEL // LANES), axis=1)
            y_ref[s0:s0 + seqs] = y.astype(BF16).reshape(seqs, cap, D_MODEL)

    run(xc_ref, gc_ref, yc_ref)
    run(xl_ref, gl_ref, yl_ref)


def _experts_call(xs_ctx, xs_lat, gates_ctx, gates_lat, w_gate, w_up, w_down):
    def slot_spec(a):
        n_seq, _, cap, width = a.shape
        return pl.BlockSpec((n_seq, None, cap, width), lambda e: (0, e, 0, 0))

    w_spec = pl.BlockSpec((None, D_MODEL, D_EXPERT), lambda e: (e, 0, 0))
    return pl.pallas_call(
        _experts_kernel,
        grid=(N_EXPERTS,),
        in_specs=[slot_spec(xs_ctx), slot_spec(xs_lat), slot_spec(gates_ctx), slot_spec(gates_lat),
                  w_spec, w_spec, pl.BlockSpec((None, D_EXPERT, D_MODEL), lambda e: (e, 0, 0))],
        out_specs=[slot_spec(xs_ctx), slot_spec(xs_lat)],
        out_shape=[jax.ShapeDtypeStruct(xs_ctx.shape, BF16), jax.ShapeDtypeStruct(xs_lat.shape, BF16)],
        scratch_shapes=[pltpu.VMEM((D_MODEL, D_EXPERT), BF16),
                        pltpu.VMEM((D_MODEL, D_EXPERT), BF16),
                        pltpu.VMEM((D_EXPERT, D_MODEL), BF16),
                        pltpu.VMEM((DISPATCH_ROWS, D_EXPERT), BF16)],
        compiler_params=_params(1),
        name="experts",
    )(xs_ctx, xs_lat, gates_ctx, gates_lat, w_gate, w_up, w_down)


def _combine_tokens(x1_ref, y_ref, rank_ref, gate2, gf_ref, o_ref, oh_ref):
    n_tok = x1_ref.shape[0]
    cap = y_ref.shape[0] // N_EXPERTS
    slot = lax.broadcasted_iota(I32, (cap, 1), 0)
    for e in range(N_EXPERTS):
        oh_ref[e * cap:(e + 1) * cap, :] = jnp.where(rank_ref[e:e + 1, :] == slot, 1.0, 0.0).astype(BF16)
    for t in range(n_tok // ROW_TILE):
        rows = slice(t * ROW_TILE, (t + 1) * ROW_TILE)
        moe = _dot_tn(oh_ref[:, rows], y_ref[...])
        x2 = x1_ref[rows, :] + gate2 * moe
        r = lax.rsqrt(jnp.mean(x2 * x2, axis=-1, keepdims=True) + EPS)
        o_ref[rows, :] = (x2 * r) * gf_ref[...]


def _combine_kernel(x1c_ref, yc_ref, rkc_ref, modc_ref, x1l_ref, yl_ref, rkl_ref, modl_ref, gf_ref,
                    oc_ref, ol_ref, ohc_ref, ohl_ref):
    for j in range(x1c_ref.shape[0]):
        _combine_tokens(x1c_ref.at[j], yc_ref.at[j], rkc_ref.at[j], _mod_rows(modc_ref, j)[5], gf_ref,
                        oc_ref.at[j], ohc_ref.at[j])
    _combine_tokens(x1l_ref.at[0], yl_ref.at[0], rkl_ref.at[0], _mod_rows(modl_ref, 0)[5], gf_ref,
                    ol_ref.at[0], ohl_ref)


def _combine_call(ctx, lat, g_final, *, ctx_seqs_per_step):
    x1c, yc, rkc, modc = ctx
    x1l, yl, rkl, modl = lat
    n_ctx, ctx_len, _ = x1c.shape
    n_lat, lat_len, _ = x1l.shape
    sps = ctx_seqs_per_step
    n_steps = n_ctx // sps
    slabs = n_steps // n_lat
    slab = lat_len // slabs
    assert n_steps == n_lat * slabs and slab % ROW_TILE == 0 and modl.shape[0] == n_lat
    ctx_blk = lambda shape: pl.BlockSpec((sps,) + shape, lambda b: (b, 0, 0))
    lat_seq = lambda b: (b // slabs, 0, 0)
    return pl.pallas_call(
        _combine_kernel,
        grid=(n_steps,),
        in_specs=[ctx_blk((ctx_len, D_MODEL)), ctx_blk(yc.shape[1:]), ctx_blk((N_EXPERTS, ctx_len)),
                  _mod_spec(modc, sps),
                  pl.BlockSpec((1, slab, D_MODEL), lambda b: (b // slabs, b % slabs, 0)),
                  pl.BlockSpec((1,) + yl.shape[1:], lat_seq),
                  pl.BlockSpec((1, N_EXPERTS, slab), lambda b: (b // slabs, 0, b % slabs)),
                  pl.BlockSpec((1,) + modl.shape[1:], lat_seq),
                  pl.BlockSpec((1, D_MODEL), lambda b: (0, 0))],
        out_specs=[ctx_blk((ctx_len, D_MODEL)),
                   pl.BlockSpec((1, slab, D_MODEL), lambda b: (b // slabs, b % slabs, 0))],
        out_shape=[jax.ShapeDtypeStruct(x1c.shape, F32), jax.ShapeDtypeStruct(x1l.shape, F32)],
        scratch_shapes=[pltpu.VMEM((sps, yc.shape[1], ctx_len), BF16),
                        pltpu.VMEM((yl.shape[1], slab), BF16)],
        compiler_params=_params(1),
        name="combine",
    )(x1c, yc, rkc, modc, x1l, yl, rkl, modl, g_final)


def kernel(x_prompt, x_sample, state_gla_fwd, state_gla_bwd, c, c_ctx, w_mod, b_mod, g_norm1, g_norm2,
           w_in, w_conv, b_conv, w_a_up_f, b_a_f, w_a_up_b, b_a_b, g_gla_norm, w_out, w_router,
           w_gate, w_up, w_down, g_final):
    assert w_mod.shape[0] == 1, "single trunk layer"
    n_ctx, ctx_len, _ = x_prompt.shape
    n_lat, lat_len, _ = x_sample.shape
    ctx_cap = EC_CAPACITY_FACTOR * ctx_len // N_EXPERTS
    lat_cap = EC_CAPACITY_FACTOR * lat_len // N_EXPERTS

    mod_ctx, mod_lat = _mod_call(c_ctx[None, :], c, w_mod[0], b_mod)

    mixer_weights = [g_norm1, jnp.transpose(w_in[0]), jnp.transpose(w_conv, (1, 0, 2)), b_conv,
                     w_a_up_f, b_a_f, w_a_up_b, b_a_b,
                     g_gla_norm[0].reshape(1, GLA_DV_TOT), w_out[0].astype(BF16)]
    assert len(mixer_weights) == N_MIXER_WEIGHTS
    wr_t = jnp.transpose(jnp.concatenate([w_router[0], w_router[0]], axis=1))
    wr_hi = wr_t.astype(BF16)
    wr_lo = (wr_t - wr_hi.astype(F32)).astype(BF16)
    wr_both = jnp.where(np.arange(2 * N_EXPERTS)[:, None] < N_EXPERTS, wr_hi, wr_lo)

    x1_ctx, new_f, new_b, w_in_bf = _mixer_call(x_prompt, mod_ctx, None, mixer_weights,
                                                period=ctx_len, has_state_out=True, seqs_per_step=4)
    mixer_weights[1] = w_in_bf
    (x1_lat,) = _mixer_call(x_sample, mod_lat, (state_gla_fwd, state_gla_bwd), mixer_weights,
                            period=GRID_W, has_state_out=False, seqs_per_step=1)

    xs_ctx, rank_ctx, gates_ctx = _route_call(x1_ctx, mod_ctx, g_norm2, wr_both, seqs_per_step=4)
    xs_lat, rank_lat, gates_lat = _route_call(x1_lat, mod_lat, g_norm2, wr_both, seqs_per_step=1)

    per_expert = lambda a, n, cap: a.reshape(n, N_EXPERTS, cap, a.shape[-1])
    y_ctx, y_lat = _experts_call(per_expert(xs_ctx, n_ctx, ctx_cap), per_expert(xs_lat, n_lat, lat_cap),
                                 per_expert(gates_ctx, n_ctx, ctx_cap), per_expert(gates_lat, n_lat, lat_cap),
                                 w_gate[0], w_up[0], w_down[0])

    g_fin = g_final[None, :]
    y_prompt, y_sample = _combine_call((x1_ctx, y_ctx.reshape(xs_ctx.shape), rank_ctx, mod_ctx),
                                       (x1_lat, y_lat.reshape(xs_lat.shape), rank_lat, mod_lat),
                                       g_fin, ctx_seqs_per_step=4)
    return y_prompt, y_sample, new_f, new_b
```

```python
import functools

import jax
import jax.numpy as jnp
import numpy as np
from jax import lax
from jax.experimental import pallas as pl
from jax.experimental.pallas import tpu as pltpu

F32 = jnp.float32
BF16 = jnp.bfloat16
I32 = jnp.int32

D_MODEL = 1024
D_CONV = D_MODEL // 2
GRID_W = 64
GLA_HEADS = 4
GLA_DK = 64
GLA_DV = 128
GLA_DK_TOT = GLA_HEADS * GLA_DK
GLA_DV_TOT = GLA_HEADS * GLA_DV
GLA_LOW_RANK = 16
GLA_TAU = 16.0
GLA_CHUNK = 64
N_EXPERTS = 16
EC_CAPACITY_FACTOR = 2
D_EXPERT = 1024
N_MOD = 6
EPS = 1e-6
LOG2_E = 1.4426950408889634

OFF_XB = 0
OFF_XC = D_CONV
OFF_XV = 2 * D_CONV
OFF_Q = 3 * D_CONV
OFF_K = OFF_Q + GLA_DK_TOT
OFF_V = OFF_K + GLA_DK_TOT
OFF_OG = OFF_V + GLA_DV_TOT
OFF_ALOW = OFF_OG + GLA_DV_TOT
P_TOT = OFF_ALOW + 2 * GLA_LOW_RANK

LANES = 128
P_PAD = -(-P_TOT // LANES) * LANES
ROW_TILE = 256
N_MIXER_WEIGHTS = 10
WIN_CHUNK = 512
DISPATCH_ROWS = 512
VMEM_LIMIT = 56 * 1024 * 1024


def _dot(a, b):
    return jnp.dot(a, b, preferred_element_type=F32)


def _dot_nt(a, b):
    return lax.dot_general(a, b, (((1,), (1,)), ((), ())), preferred_element_type=F32)


def _dot_tn(a, b):
    return lax.dot_general(a, b, (((0,), (0,)), ((), ())), preferred_element_type=F32)


def _split(a):
    hi = a.astype(BF16)
    lo = (a - hi.astype(F32)).astype(BF16)
    return hi, lo


def _silu(x):
    return x * jax.nn.sigmoid(x)


def _modulated_norm(x, g, scale, shift):
    r = lax.rsqrt(jnp.mean(x * x, axis=-1, keepdims=True) + EPS)
    return (x * r) * (g * (1.0 + scale)) + shift


def _params(n_axes):
    return pltpu.CompilerParams(dimension_semantics=("arbitrary",) * n_axes,
                                vmem_limit_bytes=VMEM_LIMIT)


def _mod_kernel(cctx_ref, c_ref, w_ref, b_ref, ctx_ref, lat_ref, acc_ref):
    rows = acc_ref.shape[0]
    n_lat = c_ref.shape[0]
    fill = jnp.zeros((rows - 1 - n_lat, c_ref.shape[1]), F32)
    s = _silu(jnp.concatenate([cctx_ref[...], c_ref[...], fill], axis=0))
    s_hi, s_lo = _split(jnp.concatenate([s, s], axis=0))
    upper = lax.broadcasted_iota(I32, (2 * rows, 1), 0) < rows
    w_hi, w_lo = _split(w_ref[...])
    by_hi = _dot(jnp.where(upper, s_hi, s_lo), w_hi)
    part = by_hi[:rows] + by_hi[rows:] + _dot(s_hi[:rows], w_lo)

    @pl.when(pl.program_id(0) == 0)
    def _():
        acc_ref[...] = part + b_ref[...]

    @pl.when(pl.program_id(0) != 0)
    def _():
        acc_ref[...] = acc_ref[...] + part

    @pl.when(pl.program_id(0) == pl.num_programs(0) - 1)
    def _():
        ctx_ref[0] = acc_ref[0:1, :]
        for i in range(n_lat):
            lat_ref[i] = acc_ref[1 + i:2 + i, :]


def _mod_call(c_ctx, c, w_mod, b_mod):
    n_lat, d = c.shape
    n = w_mod.shape[1]
    rows = -(-(1 + n_lat) // 8) * 8
    tk = D_MODEL // 4
    return pl.pallas_call(
        _mod_kernel,
        grid=(d // tk,),
        in_specs=[pl.BlockSpec((1, tk), lambda k: (0, k)),
                  pl.BlockSpec((n_lat, tk), lambda k: (0, k)),
                  pl.BlockSpec((tk, n), lambda k: (k, 0)),
                  pl.BlockSpec((1, n), lambda k: (0, 0))],
        out_specs=[pl.BlockSpec((1, 1, n), lambda k: (0, 0, 0)),
                   pl.BlockSpec((n_lat, 1, n), lambda k: (0, 0, 0))],
        out_shape=[jax.ShapeDtypeStruct((1, 1, n), F32), jax.ShapeDtypeStruct((n_lat, 1, n), F32)],
        scratch_shapes=[pltpu.VMEM((rows, n), F32)],
        compiler_params=_params(1),
        name="mod",
    )(c_ctx, c, w_mod, b_mod)


def _mod_rows(mod_ref, j):
    row = mod_ref[j % mod_ref.shape[0]]
    return [row[:, i * D_MODEL:(i + 1) * D_MODEL] for i in range(N_MOD)]


def _mod_spec(mod, seqs_per_step):
    if mod.shape[0] == 1:
        return pl.BlockSpec(mod.shape, lambda b: (0, 0, 0))
    return pl.BlockSpec((seqs_per_step,) + mod.shape[1:], lambda b: (b, 0, 0))


def _staggered(programs):
    programs = list(programs)
    started = 0
    while programs:
        started = min(started + 1, len(programs))
        running = [p for p in programs[:started] if next(p, "done") != "done"]
        programs = running + programs[started:]
        started = len(running)
        yield


def _for_row_tiles(seq_len, phases, independent):
    n = seq_len // ROW_TILE
    if independent:
        yield from _staggered(phases(i) for i in range(n))
    else:
        for i in range(n):
            yield from phases(i)


def _tile_rows(tile, offset=0, size=ROW_TILE):
    if isinstance(tile, int):
        return pl.ds(tile * ROW_TILE + offset, size)
    return pl.ds(pl.multiple_of(tile * ROW_TILE + offset, size), size)


def _mixer_kernel(*refs, seqs_per_step, has_state_in, has_state_out, stages_w_in, **static):
    refs = list(refs)
    n_in = 2 + (2 if has_state_in else 0)
    per_seq_in, refs = [refs[0]] + refs[2:n_in], [refs[1]] + refs[n_in:]
    mod_ref, weights, refs = refs[0], refs[1:1 + N_MIXER_WEIGHTS], refs[1 + N_MIXER_WEIGHTS:]
    n_out = 1 + (2 if has_state_out else 0)
    per_seq_out, refs = refs[:n_out], refs[n_out:]
    if stages_w_in:
        win_hbm_ref, win_bf_ref, scratch = weights[1], refs[0], refs[1:-2]
        win_f32_ref, win_sem_ref = refs[-2:]
        weights = weights[:1] + [win_bf_ref] + weights[2:]

        @pl.when(pl.program_id(0) == 0)
        def _():
            chunks = [(r0, min(WIN_CHUNK, P_TOT - r0)) for r0 in range(0, P_TOT, WIN_CHUNK)]
            copies = [pltpu.make_async_copy(win_hbm_ref.at[pl.ds(r0, n), :], win_f32_ref.at[pl.ds(r0, n), :],
                                            win_sem_ref.at[i]) for i, (r0, n) in enumerate(chunks)]
            for copy in copies:
                copy.start()
            for (r0, n_rows), copy in zip(chunks, copies):
                copy.wait()
                for c0 in range(r0, r0 + n_rows, LANES):
                    n = min(LANES, P_TOT - c0)
                    cols = win_f32_ref[c0:c0 + n, :]
                    if n < LANES:
                        cols = jnp.concatenate([cols, jnp.zeros((LANES - n, D_MODEL), F32)], axis=0)
                    win_bf_ref[:, c0:c0 + LANES] = jnp.transpose(cols).astype(BF16)
    else:
        scratch = refs
    programs = []
    for j in range(seqs_per_step):
        ins = [r.at[j] for r in per_seq_in]
        outs = [r.at[j] for r in per_seq_out]
        programs.append(_mixer_sequence(ins[0], _mod_rows(mod_ref, j), ins[1:], weights, outs[0], outs[1:],
                                        [r.at[j] for r in scratch], **static))
    for _ in _staggered(programs):
        pass


def _mixer_sequence(x_ref, m, s0_refs, weights, x1_ref, sout_refs, scratch, *, seq_len, period):
    has_state_in = bool(s0_refs)
    has_state_out = bool(sout_refs)
    g1_ref, win_ref, wconv_ref, bconv_ref, wupf_ref, bupf_ref, wupb_ref, bupb_ref, ggla_ref, wout_ref = weights
    og_ref, qd_ref, kd_ref, kst_ref, dect_ref, v_ref, s_ref, sst_ref, o_ref, ya_ref = scratch

    c = GLA_CHUNK
    tile_chunks = ROW_TILE // c
    n_tiles = seq_len // ROW_TILE
    n_pairs = GLA_HEADS // 2
    pair_k = 2 * GLA_DK
    pair_v = 2 * GLA_DV

    def stage1(ti):
        rows = _tile_rows(ti)
        h = _modulated_norm(x_ref[rows, :], g1_ref[...], m[1], m[0]).astype(BF16)
        row_i = lax.broadcasted_iota(I32, (ROW_TILE, 1), 0)
        yield
        p_gate = _dot(h, win_ref[:, OFF_OG:P_PAD])
        og_ref[rows, :] = p_gate[:, :GLA_DV_TOT]
        yield
        zero_up = jnp.zeros((GLA_LOW_RANK, GLA_DK_TOT), F32)
        w_up = jnp.concatenate([jnp.concatenate([wupf_ref[0], zero_up], axis=1),
                                jnp.concatenate([zero_up, wupb_ref[0]], axis=1),
                                jnp.zeros((P_PAD - P_TOT, 2 * GLA_DK_TOT), F32)], axis=0).astype(BF16)
        b_up = jnp.concatenate([bupf_ref[...], bupb_ref[...]], axis=1)
        z = _dot(p_gate[:, GLA_DV_TOT:].astype(BF16), w_up) + b_up
        la = (jnp.minimum(z, 0.0) - jnp.log(1.0 + jnp.exp(-jnp.abs(z)))) * (LOG2_E / GLA_TAU)
        col_j = lax.broadcasted_iota(I32, (1, ROW_TILE), 1)
        same_chunk = (row_i & -c) == (col_j & -c)
        lower = jnp.where(same_chunk & (col_j <= row_i), 1.0, 0.0).astype(BF16)
        la_parts = jnp.concatenate(_split(la), axis=1)
        n_gate = 2 * GLA_DK_TOT
        yield
        pre = _dot(lower, la_parts)
        pre = pre[:, :n_gate] + pre[:, n_gate:]
        tot = jnp.concatenate([jnp.broadcast_to(pre[(n + 1) * c - 1:(n + 1) * c], (c, n_gate))
                               for n in range(tile_chunks)], axis=0)
        p_qkv = _dot(h, win_ref[:, OFF_Q:OFF_OG])
        yield
        q = p_qkv[:, :GLA_DK_TOT] * (GLA_DK ** -0.5)
        k = p_qkv[:, GLA_DK_TOT:2 * GLA_DK_TOT]
        v_ref[rows, :] = p_qkv[:, 2 * GLA_DK_TOT:].astype(BF16)
        for d in range(2):
            cols = slice(d * GLA_DK_TOT, (d + 1) * GLA_DK_TOT)
            if d == 0:
                bq = pre[:, cols]
                bk = tot[:, cols] - bq
            else:
                bk = pre[:, cols] - la[:, cols]
                bq = tot[:, cols] - bk
            qd_ref[d, rows, :] = (q * jnp.exp2(bq)).astype(BF16)
            kd_ref[d, rows, :] = (k * jnp.exp2(-bq)).astype(BF16)
            kst_ref[d, ti] = jnp.transpose(k * jnp.exp2(bk)).astype(BF16)
            totals = [tot[n * c:n * c + 1, cols] for n in range(tile_chunks)]
            totals.append(jnp.zeros((LANES - tile_chunks, GLA_DK_TOT), F32))
            dect_ref[d, ti] = jnp.transpose(jnp.exp2(jnp.concatenate(totals, axis=0)))
        yield
        p_conv = _dot(h, win_ref[:, :OFF_Q])
        yield
        pos = row_i & (period - 1)
        u = p_conv[:, OFF_XC:OFF_XC + D_CONV] * p_conv[:, OFF_XV:OFF_XV + D_CONV]
        u_prev = jnp.where(pos == 0, 0.0, pltpu.roll(u, 1, 0))
        u_next = jnp.where(pos == period - 1, 0.0, pltpu.roll(u, ROW_TILE - 1, 0))
        conv = u_prev * wconv_ref[0] + u * wconv_ref[1] + u_next * wconv_ref[2] + bconv_ref[...]
        ya_ref[rows, :] = (p_conv[:, OFF_XB:OFF_XB + D_CONV] * conv).astype(BF16)
        yield

    yield from _for_row_tiles(seq_len, stage1, independent=True)

    for d in range(2):
        for pair in range(n_pairs):
            if has_state_in:
                zero = jnp.zeros((GLA_DK, GLA_DV), F32)
                top = jnp.concatenate([s0_refs[d][2 * pair], zero], axis=1)
                bot = jnp.concatenate([zero, s0_refs[d][2 * pair + 1]], axis=1)
                s_ref[d, pair] = jnp.concatenate([top, bot], axis=0)
            else:
                s_ref[d, pair] = jnp.zeros((pair_k, pair_v), F32)

    def scan_tile(i):
        upper_lane = lax.broadcasted_iota(I32, (1, LANES), 1) >= GLA_DK
        qi = lax.broadcasted_iota(I32, (LANES, 1), 0)
        kj = lax.broadcasted_iota(I32, (1, 2 * LANES), 1) & (LANES - 1)
        same_chunk = (qi & c) == (kj & c)
        causal = (same_chunk & (kj <= qi), same_chunk & (kj >= qi))
        for pair in range(n_pairs):
            kl = slice(pair * pair_k, (pair + 1) * pair_k)
            vl = slice(pair * pair_v, (pair + 1) * pair_v)
            for blk in range(ROW_TILE // LANES):
                rows = _tile_rows(i, blk * LANES, LANES)
                att = None
                for d in range(2):
                    kd = kd_ref[d, rows, kl]
                    zk = jnp.zeros_like(kd)
                    keys = jnp.concatenate([jnp.where(upper_lane, zk, kd), jnp.where(upper_lane, kd, zk)], axis=0)
                    a = jnp.where(causal[d], _dot_nt(qd_ref[d, rows, kl], keys), 0.0)
                    att = a if att is None else att + a
                v = v_ref[rows, vl]
                zv = jnp.zeros((LANES, GLA_DV), BF16)
                v_bd = jnp.concatenate([jnp.concatenate([v[:, :GLA_DV], zv], axis=1),
                                        jnp.concatenate([zv, v[:, GLA_DV:]], axis=1)], axis=0)
                o_ref[rows, vl] = _dot(att.astype(BF16), v_bd)
        yield
        key_row = lax.broadcasted_iota(I32, (pair_k, 1), 0)
        val_col = lax.broadcasted_iota(I32, (1, pair_v), 1)
        blockdiag = (key_row >= GLA_DK) == (val_col >= GLA_DV)
        for d in range(2):
            tile = i if d == 0 else n_tiles - 1 - i
            chunks = range(tile_chunks)
            for pair in range(n_pairs):
                kr = slice(pair * pair_k, (pair + 1) * pair_k)
                vl = slice(pair * pair_v, (pair + 1) * pair_v)
                s = s_ref[d, pair]
                for c4 in (chunks if d == 0 else reversed(chunks)):
                    blk, half = divmod(c4, 2)
                    kst = kst_ref[d, tile, kr, blk * LANES:(blk + 1) * LANES]
                    kst = jnp.where(upper_lane if half else ~upper_lane, kst, jnp.zeros_like(kst))
                    kv = jnp.where(blockdiag, _dot(kst, v_ref[_tile_rows(tile, blk * LANES, LANES), vl]), 0.0)
                    sst_ref[pair, tile * tile_chunks + c4, d * pair_k:(d + 1) * pair_k, :] = s.astype(BF16)
                    s = dect_ref[d, tile, kr, c4:c4 + 1] * s + kv
                s_ref[d, pair] = s
        yield

    yield from _for_row_tiles(seq_len, scan_tile, independent=False)

    if has_state_out:
        for d in range(2):
            for pair in range(n_pairs):
                s = s_ref[d, pair]
                sout_refs[d][2 * pair] = s[0:GLA_DK, 0:GLA_DV]
                sout_refs[d][2 * pair + 1] = s[GLA_DK:, GLA_DV:]

    def stage3(i):
        for pair in range(n_pairs):
            kl = slice(pair * pair_k, (pair + 1) * pair_k)
            vl = slice(pair * pair_v, (pair + 1) * pair_v)
            for c4 in range(tile_chunks):
                crow = _tile_rows(i, c4 * c, c)
                q2 = jnp.concatenate([qd_ref[0, crow, kl], qd_ref[1, crow, kl]], axis=1)
                o_ref[crow, vl] = o_ref[crow, vl] + _dot(q2, sst_ref[pair, i * tile_chunks + c4])
        yield
        rows = _tile_rows(i)
        heads = []
        for h in range(GLA_HEADS):
            hl = slice(h * GLA_DV, (h + 1) * GLA_DV)
            oh = o_ref[rows, hl]
            r = lax.rsqrt(jnp.mean(oh * oh, axis=-1, keepdims=True) + EPS)
            heads.append(oh * r * ggla_ref[:, hl])
        y_b = jnp.concatenate(heads, axis=1) * _silu(og_ref[rows, :])
        y = jnp.concatenate([ya_ref[rows, :], y_b.astype(BF16)], axis=1)
        x1_ref[rows, :] = x_ref[rows, :] + m[2] * _dot(y, wout_ref[...])
        yield

    yield from _for_row_tiles(seq_len, stage3, independent=True)


def _mixer_call(x, mod, states, weights, *, period, has_state_out, seqs_per_step):
    n_seq, seq_len, _ = x.shape
    sps = seqs_per_step
    has_state_in = states is not None
    stages_w_in = weights[1].dtype == F32
    kernel = functools.partial(_mixer_kernel, seqs_per_step=sps, seq_len=seq_len, period=period,
                               has_state_in=has_state_in, has_state_out=has_state_out, stages_w_in=stages_w_in)
    state_spec = pl.BlockSpec((sps, None, GLA_HEADS, GLA_DK, GLA_DV), lambda b: (b, 0, 0, 0, 0))
    const2 = lambda b: (0, 0)
    in_specs = [pl.BlockSpec((sps, seq_len, D_MODEL), lambda b: (b, 0, 0)),
                _mod_spec(mod, sps)]
    args = [x, mod]
    if has_state_in:
        in_specs += [state_spec, state_spec]
        args += list(states)
    weight_specs = [pl.BlockSpec(w.shape, lambda b, nd=w.ndim: (0,) * nd, pipeline_mode=pl.Buffered(1))
                    for w in weights]
    if stages_w_in:
        weight_specs[1] = pl.BlockSpec(memory_space=pl.ANY)
    in_specs += weight_specs
    args += list(weights)
    out_specs = [pl.BlockSpec((sps, seq_len, D_MODEL), lambda b: (b, 0, 0))]
    out_shape = [jax.ShapeDtypeStruct((n_seq, seq_len, D_MODEL), F32)]
    if has_state_out:
        out_specs += [state_spec, state_spec]
        out_shape += [jax.ShapeDtypeStruct((n_seq, 1, GLA_HEADS, GLA_DK, GLA_DV), F32)] * 2
    if stages_w_in:
        out_specs.append(pl.BlockSpec((D_MODEL, P_PAD), const2))
        out_shape.append(jax.ShapeDtypeStruct((D_MODEL, P_PAD), BF16))
    n_tiles = seq_len // ROW_TILE
    n_pairs = GLA_HEADS // 2
    per_seq = lambda shape, dtype: pltpu.VMEM((sps,) + shape, dtype)
    scratch = [per_seq((seq_len, GLA_DV_TOT), F32),
               per_seq((2, seq_len, GLA_DK_TOT), BF16),
               per_seq((2, seq_len, GLA_DK_TOT), BF16),
               per_seq((2, n_tiles, GLA_DK_TOT, ROW_TILE), BF16),
               per_seq((2, n_tiles, GLA_DK_TOT, LANES), F32),
               per_seq((seq_len, GLA_DV_TOT), BF16),
               per_seq((2, n_pairs, 2 * GLA_DK, 2 * GLA_DV), F32),
               per_seq((n_pairs, seq_len // GLA_CHUNK, 4 * GLA_DK, 2 * GLA_DV), BF16),
               per_seq((seq_len, GLA_DV_TOT), F32),
               per_seq((seq_len, D_CONV), BF16)]
    if stages_w_in:
        scratch += [pltpu.VMEM(weights[1].shape, F32),
                    pltpu.SemaphoreType.DMA((-(-weights[1].shape[0] // WIN_CHUNK),))]
    return pl.pallas_call(
        kernel,
        grid=(n_seq // sps,),
        in_specs=in_specs,
        out_specs=out_specs,
        out_shape=out_shape,
        scratch_shapes=scratch,
        compiler_params=_params(1),
        name="mixer",
    )(*args)


SUBLANES = 8


ROW_TOKENS = LANES // N_EXPERTS


def _route_kernel(x1_ref, mod_ref, g2_ref, wr2_ref, *refs, seqs_per_step, **static):
    programs = [_route_sequence(x1_ref.at[j], _mod_rows(mod_ref, j), g2_ref, wr2_ref, *[r.at[j] for r in refs],
                                **static)
                for j in range(seqs_per_step)]
    for _ in _staggered(programs):
        pass


def _route_sequence(x1_ref, m, g2_ref, wr2_ref, xs_ref, rank_ref, gate_ref,
                    pt_ref, h2_ref, spread_ref, bar_ref, cnt_ref, win_ref, *, seq_len, cap):
    n_rows = seq_len // ROW_TOKENS
    for t in range(seq_len // ROW_TILE):
        rows = slice(t * ROW_TILE, (t + 1) * ROW_TILE)
        h2 = _modulated_norm(x1_ref[rows, :], g2_ref[...], m[4], m[3])
        hi = h2.astype(BF16)
        h2_ref[rows, :] = hi
        yield
        by_hi = _dot_nt(wr2_ref[...], hi)
        pt_ref[:, rows] = by_hi[:N_EXPERTS] + by_hi[N_EXPERTS:]
    yield

    logits = pt_ref[...]
    ex = jnp.exp(logits - jnp.max(logits, axis=0, keepdims=True))
    probs = ex / jnp.sum(ex, axis=0, keepdims=True)
    pt_ref[...] = probs
    pad = jnp.zeros((LANES - N_EXPERTS, seq_len), F32)
    p_tok = jnp.transpose(jnp.concatenate([probs, pad], axis=0))

    spread = p_tok
    sh = N_EXPERTS
    while sh < LANES:
        spread = spread + pltpu.roll(spread, sh, 1)
        sh *= 2
    spread_ref[...] = spread
    sub_j = lax.broadcasted_iota(I32, (SUBLANES, 1), 0)
    lane_g = lax.broadcasted_iota(I32, (1, LANES), 1) >> (N_EXPERTS.bit_length() - 1)
    lane_j = (ROW_TOKENS - lane_g) & (ROW_TOKENS - 1)
    own_group = jnp.where(lane_j == sub_j, spread.reshape(n_rows, SUBLANES, LANES), 0.0)
    bar = jnp.sum(own_group, axis=1, keepdims=True)
    bar_ref[...] = jnp.broadcast_to(bar, (n_rows, SUBLANES, LANES))
    cnt_ref[...] = jnp.zeros((n_rows, SUBLANES, LANES), F32)

    rows_per_block = LANES // ROW_TOKENS

    def count_block(g_s, g_t, relation):
        s_blk = spread_ref[g_s * LANES:(g_s + 1) * LANES, :]
        wins = [jnp.zeros((SUBLANES, LANES), F32)] * rows_per_block
        for tl in range(rows_per_block):
            r = g_t * rows_per_block + tl
            bar_r = bar_ref[r]
            acc = cnt_ref[r]
            for sl in range(rows_per_block):
                s_vreg = s_blk[sl * SUBLANES:(sl + 1) * SUBLANES]
                if relation == "before" or sl < tl:
                    won = jnp.where(s_vreg >= bar_r, 1.0, 0.0)
                    acc = acc + won
                    wins[sl] = wins[sl] + won
                elif sl == tl:
                    acc = acc + jnp.where(s_vreg > bar_r, 1.0, 0.0) \
                        + jnp.where((s_vreg == bar_r) & (sub_j < lane_j), 1.0, 0.0)
            cnt_ref[r] = acc
        w_rows = slice(g_s * rows_per_block, (g_s + 1) * rows_per_block)
        win_ref[w_rows] = win_ref[w_rows] + jnp.stack(wins)

    n_grp = seq_len // LANES
    win_ref[...] = jnp.zeros((n_rows, SUBLANES, LANES), F32)
    yield
    for g_t in range(n_grp):
        for g_s in range(g_t + 1):
            count_block(g_s, g_t, "before" if g_s < g_t else "same")
            yield

    counts = jnp.sum(cnt_ref[...], axis=1, keepdims=True)
    counts = jnp.broadcast_to(counts, (n_rows, SUBLANES, LANES)).reshape(seq_len, LANES)
    rank_tok = pltpu.roll(counts, 0, 1, stride=N_EXPERTS, stride_axis=0)
    wins = win_ref[...].reshape(seq_len, LANES)
    sh = N_EXPERTS
    while sh < LANES:
        wins = wins + pltpu.roll(wins, sh, 1)
        sh *= 2
    tok = lax.broadcasted_iota(I32, (seq_len, 1), 0)
    later = ((n_rows - 1 - (tok >> (ROW_TOKENS.bit_length() - 1))) * ROW_TOKENS).astype(F32)
    rank_tok = rank_tok + (later - wins)
    rank_ref[...] = jnp.transpose(rank_tok)[:N_EXPERTS].astype(I32)
    yield

    group = DISPATCH_ROWS // cap
    slot = lax.broadcasted_iota(I32, (cap, 1), 0)
    half = D_MODEL // 2
    for gi in range(N_EXPERTS // group):
        picks = []
        for e in range(gi * group, (gi + 1) * group):
            oh = rank_ref[e:e + 1, :] == slot
            gate = jnp.sum(jnp.where(oh, pt_ref[e:e + 1, :], 0.0), axis=1, keepdims=True)
            gate_ref[e * cap:(e + 1) * cap, :] = jnp.broadcast_to(gate, (cap, LANES))
            picks.append(oh)
        ohb = jnp.where(jnp.concatenate(picks, axis=0), 1.0, 0.0).astype(BF16)
        yield
        out_rows = slice(gi * DISPATCH_ROWS, (gi + 1) * DISPATCH_ROWS)
        xs_ref[out_rows, :half] = _dot(ohb, h2_ref[:, :half]).astype(BF16)
        xs_ref[out_rows, half:] = _dot(ohb, h2_ref[:, half:]).astype(BF16)
        yield


def _route_call(x1, mod, g2, wr_both, *, seqs_per_step):
    n_seq, seq_len, _ = x1.shape
    sps = seqs_per_step
    cap = EC_CAPACITY_FACTOR * seq_len // N_EXPERTS
    kernel = functools.partial(_route_kernel, seqs_per_step=sps, seq_len=seq_len, cap=cap)
    const2 = lambda b: (0, 0)
    per_seq = lambda shape, dtype: pltpu.VMEM((sps,) + shape, dtype)
    rank_rows = (seq_len // ROW_TOKENS, SUBLANES, LANES)
    return pl.pallas_call(
        kernel,
        grid=(n_seq // sps,),
        in_specs=[pl.BlockSpec((sps, seq_len, D_MODEL), lambda b: (b, 0, 0)),
                  _mod_spec(mod, sps),
                  pl.BlockSpec((1, D_MODEL), const2),
                  pl.BlockSpec((2 * N_EXPERTS, D_MODEL), const2)],
        out_specs=[pl.BlockSpec((sps, N_EXPERTS * cap, D_MODEL), lambda b: (b, 0, 0)),
                   pl.BlockSpec((sps, N_EXPERTS, seq_len), lambda b: (b, 0, 0)),
                   pl.BlockSpec((sps, N_EXPERTS * cap, LANES), lambda b: (b, 0, 0))],
        out_shape=[jax.ShapeDtypeStruct((n_seq, N_EXPERTS * cap, D_MODEL), BF16),
                   jax.ShapeDtypeStruct((n_seq, N_EXPERTS, seq_len), I32),
                   jax.ShapeDtypeStruct((n_seq, N_EXPERTS * cap, LANES), F32)],
        scratch_shapes=[per_seq((N_EXPERTS, seq_len), F32),
                        per_seq((seq_len, D_MODEL), BF16),
                        per_seq((seq_len, LANES), F32),
                        per_seq(rank_rows, F32),
                        per_seq(rank_rows, F32),
                        per_seq(rank_rows, F32)],
        compiler_params=_params(1),
        name="route",
    )(x1, mod, g2, wr_both)


def _experts_kernel(xc_ref, xl_ref, gc_ref, gl_ref, wg_ref, wu_ref, wd_ref, yc_ref, yl_ref,
                    wgb_ref, wub_ref, wdb_ref, a_ref):
    wgb_ref[...] = wg_ref[...].astype(BF16)
    wub_ref[...] = wu_ref[...].astype(BF16)
    wdb_ref[...] = wd_ref[...].astype(BF16)
    f_tile = 2 * LANES

    def run(x_ref, g_ref, y_ref):
        n_seq, cap, _ = x_ref.shape
        seqs = DISPATCH_ROWS // cap
        for s0 in range(0, n_seq, seqs):
            x = x_ref[s0:s0 + seqs].reshape(DISPATCH_ROWS, D_MODEL)
            for f0 in range(0, D_EXPERT, f_tile):
                cols = slice(f0, f0 + f_tile)
                a_ref[:, cols] = (_silu(_dot(x, wgb_ref[:, cols])) * _dot(x, wub_ref[:, cols])).astype(BF16)
            gate = g_ref[s0:s0 + seqs].reshape(DISPATCH_ROWS, LANES)
            y = _dot(a_ref[...], wdb_ref[...]) * jnp.concatenate([gate] * (D_MODEL // LANES), axis=1)
            y_ref[s0:s0 + seqs] = y.astype(BF16).reshape(seqs, cap, D_MODEL)

    run(xc_ref, gc_ref, yc_ref)
    run(xl_ref, gl_ref, yl_ref)


def _experts_call(xs_ctx, xs_lat, gates_ctx, gates_lat, w_gate, w_up, w_down):
    def slot_spec(a):
        n_seq, _, cap, width = a.shape
        return pl.BlockSpec((n_seq, None, cap, width), lambda e: (0, e, 0, 0))

    w_spec = pl.BlockSpec((None, D_MODEL, D_EXPERT), lambda e: (e, 0, 0))
    return pl.pallas_call(
        _experts_kernel,
        grid=(N_EXPERTS,),
        in_specs=[slot_spec(xs_ctx), slot_spec(xs_lat), slot_spec(gates_ctx), slot_spec(gates_lat),
                  w_spec, w_spec, pl.BlockSpec((None, D_EXPERT, D_MODEL), lambda e: (e, 0, 0))],
        out_specs=[slot_spec(xs_ctx), slot_spec(xs_lat)],
        out_shape=[jax.ShapeDtypeStruct(xs_ctx.shape, BF16), jax.ShapeDtypeStruct(xs_lat.shape, BF16)],
        scratch_shapes=[pltpu.VMEM((D_MODEL, D_EXPERT), BF16),
                        pltpu.VMEM((D_MODEL, D_EXPERT), BF16),
                        pltpu.VMEM((D_EXPERT, D_MODEL), BF16),
                        pltpu.VMEM((DISPATCH_ROWS, D_EXPERT), BF16)],
        compiler_params=_params(1),
        name="experts",
    )(xs_ctx, xs_lat, gates_ctx, gates_lat, w_gate, w_up, w_down)


def _combine_tokens(x1_ref, y_ref, rank_ref, gate2, gf_ref, o_ref, oh_ref):
    n_tok = x1_ref.shape[0]
    cap = y_ref.shape[0] // N_EXPERTS
    slot = lax.broadcasted_iota(I32, (cap, 1), 0)
    for e in range(N_EXPERTS):
        oh_ref[e * cap:(e + 1) * cap, :] = jnp.where(rank_ref[e:e + 1, :] == slot, 1.0, 0.0).astype(BF16)
    for t in range(n_tok // ROW_TILE):
        rows = slice(t * ROW_TILE, (t + 1) * ROW_TILE)
        moe = _dot_tn(oh_ref[:, rows], y_ref[...])
        x2 = x1_ref[rows, :] + gate2 * moe
        r = lax.rsqrt(jnp.mean(x2 * x2, axis=-1, keepdims=True) + EPS)
        o_ref[rows, :] = (x2 * r) * gf_ref[...]


def _combine_kernel(x1c_ref, yc_ref, rkc_ref, modc_ref, x1l_ref, yl_ref, rkl_ref, modl_ref, gf_ref,
                    oc_ref, ol_ref, ohc_ref, ohl_ref):
    for j in range(x1c_ref.shape[0]):
        _combine_tokens(x1c_ref.at[j], yc_ref.at[j], rkc_ref.at[j], _mod_rows(modc_ref, j)[5], gf_ref,
                        oc_ref.at[j], ohc_ref.at[j])
    _combine_tokens(x1l_ref.at[0], yl_ref.at[0], rkl_ref.at[0], _mod_rows(modl_ref, 0)[5], gf_ref,
                    ol_ref.at[0], ohl_ref)


def _combine_call(ctx, lat, g_final, *, ctx_seqs_per_step):
    x1c, yc, rkc, modc = ctx
    x1l, yl, rkl, modl = lat
    n_ctx, ctx_len, _ = x1c.shape
    n_lat, lat_len, _ = x1l.shape
    sps = ctx_seqs_per_step
    n_steps = n_ctx // sps
    slabs = n_steps // n_lat
    slab = lat_len // slabs
    assert n_steps == n_lat * slabs and slab % ROW_TILE == 0 and modl.shape[0] == n_lat
    ctx_blk = lambda shape: pl.BlockSpec((sps,) + shape, lambda b: (b, 0, 0))
    lat_seq = lambda b: (b // slabs, 0, 0)
    return pl.pallas_call(
        _combine_kernel,
        grid=(n_steps,),
        in_specs=[ctx_blk((ctx_len, D_MODEL)), ctx_blk(yc.shape[1:]), ctx_blk((N_EXPERTS, ctx_len)),
                  _mod_spec(modc, sps),
                  pl.BlockSpec((1, slab, D_MODEL), lambda b: (b // slabs, b % slabs, 0)),
                  pl.BlockSpec((1,) + yl.shape[1:], lat_seq),
                  pl.BlockSpec((1, N_EXPERTS, slab), lambda b: (b // slabs, 0, b % slabs)),
                  pl.BlockSpec((1,) + modl.shape[1:], lat_seq),
                  pl.BlockSpec((1, D_MODEL), lambda b: (0, 0))],
        out_specs=[ctx_blk((ctx_len, D_MODEL)),
                   pl.BlockSpec((1, slab, D_MODEL), lambda b: (b // slabs, b % slabs, 0))],
        out_shape=[jax.ShapeDtypeStruct(x1c.shape, F32), jax.ShapeDtypeStruct(x1l.shape, F32)],
        scratch_shapes=[pltpu.VMEM((sps, yc.shape[1], ctx_len), BF16),
                        pltpu.VMEM((yl.shape[1], slab), BF16)],
        compiler_params=_params(1),
        name="combine",
    )(x1c, yc, rkc, modc, x1l, yl, rkl, modl, g_final)


def kernel(x_prompt, x_sample, state_gla_fwd, state_gla_bwd, c, c_ctx, w_mod, b_mod, g_norm1, g_norm2,
           w_in, w_conv, b_conv, w_a_up_f, b_a_f, w_a_up_b, b_a_b, g_gla_norm, w_out, w_router,
           w_gate, w_up, w_down, g_final):
    assert w_mod.shape[0] == 1, "single trunk layer"
    n_ctx, ctx_len, _ = x_prompt.shape
    n_lat, lat_len, _ = x_sample.shape
    ctx_cap = EC_CAPACITY_FACTOR * ctx_len // N_EXPERTS
    lat_cap = EC_CAPACITY_FACTOR * lat_len // N_EXPERTS

    mod_ctx, mod_lat = _mod_call(c_ctx[None, :], c, w_mod[0], b_mod)

    mixer_weights = [g_norm1, jnp.transpose(w_in[0]), jnp.transpose(w_conv, (1, 0, 2)), b_conv,
                     w_a_up_f, b_a_f, w_a_up_b, b_a_b,
                     g_gla_norm[0].reshape(1, GLA_DV_TOT), w_out[0].astype(BF16)]
    assert len(mixer_weights) == N_MIXER_WEIGHTS
    wr_t = jnp.transpose(jnp.concatenate([w_router[0], w_router[0]], axis=1))
    wr_hi = wr_t.astype(BF16)
    wr_lo = (wr_t - wr_hi.astype(F32)).astype(BF16)
    wr_both = jnp.where(np.arange(2 * N_EXPERTS)[:, None] < N_EXPERTS, wr_hi, wr_lo)

    x1_ctx, new_f, new_b, w_in_bf = _mixer_call(x_prompt, mod_ctx, None, mixer_weights,
                                                period=ctx_len, has_state_out=True, seqs_per_step=4)
    mixer_weights[1] = w_in_bf
    (x1_lat,) = _mixer_call(x_sample, mod_lat, (state_gla_fwd, state_gla_bwd), mixer_weights,
                            period=GRID_W, has_state_out=False, seqs_per_step=1)

    xs_ctx, rank_ctx, gates_ctx = _route_call(x1_ctx, mod_ctx, g_norm2, wr_both, seqs_per_step=4)
    xs_lat, rank_lat, gates_lat = _route_call(x1_lat, mod_lat, g_norm2, wr_both, seqs_per_step=1)

    per_expert = lambda a, n, cap: a.reshape(n, N_EXPERTS, cap, a.shape[-1])
    y_ctx, y_lat = _experts_call(per_expert(xs_ctx, n_ctx, ctx_cap), per_expert(xs_lat, n_lat, lat_cap),
                                 per_expert(gates_ctx, n_ctx, ctx_cap), per_expert(gates_lat, n_lat, lat_cap),
                                 w_gate[0], w_up[0], w_down[0])

    g_fin = g_final[None, :]
    y_prompt, y_sample = _combine_call((x1_ctx, y_ctx.reshape(xs_ctx.shape), rank_ctx, mod_ctx),
                                       (x1_lat, y_lat.reshape(xs_lat.shape), rank_lat, mod_lat),
                                       g_fin, ctx_seqs_per_step=4)
    return y_prompt, y_sample, new_f, new_b
```

```python
import functools

import jax
import jax.numpy as jnp
import numpy as np
from jax import lax
from jax.experimental import pallas as pl
from jax.experimental.pallas import tpu as pltpu

F32 = jnp.float32
BF16 = jnp.bfloat16
I32 = jnp.int32

D_MODEL = 1024
D_CONV = D_MODEL // 2
GRID_W = 64
GLA_HEADS = 4
GLA_DK = 64
GLA_DV = 128
GLA_DK_TOT = GLA_HEADS * GLA_DK
GLA_DV_TOT = GLA_HEADS * GLA_DV
GLA_LOW_RANK = 16
GLA_TAU = 16.0
GLA_CHUNK = 64
N_EXPERTS = 16
EC_CAPACITY_FACTOR = 2
D_EXPERT = 1024
N_MOD = 6
EPS = 1e-6
LOG2_E = 1.4426950408889634

OFF_XB = 0
OFF_XC = D_CONV
OFF_XV = 2 * D_CONV
OFF_Q = 3 * D_CONV
OFF_K = OFF_Q + GLA_DK_TOT
OFF_V = OFF_K + GLA_DK_TOT
OFF_OG = OFF_V + GLA_DV_TOT
OFF_ALOW = OFF_OG + GLA_DV_TOT
P_TOT = OFF_ALOW + 2 * GLA_LOW_RANK

LANES = 128
P_PAD = -(-P_TOT // LANES) * LANES
ROW_TILE = 256
N_MIXER_WEIGHTS = 10
WIN_CHUNK = 512
WIN_GROUPS = ((OFF_OG, P_PAD), (OFF_Q, OFF_OG), (0, OFF_Q))
DISPATCH_ROWS = 512
VMEM_LIMIT = 56 * 1024 * 1024


def _dot(a, b):
    return jnp.dot(a, b, preferred_element_type=F32)


def _dot_nt(a, b):
    return lax.dot_general(a, b, (((1,), (1,)), ((), ())), preferred_element_type=F32)


def _dot_tn(a, b):
    return lax.dot_general(a, b, (((0,), (0,)), ((), ())), preferred_element_type=F32)


def _split(a):
    hi = a.astype(BF16)
    lo = (a - hi.astype(F32)).astype(BF16)
    return hi, lo


def _silu(x):
    return x * jax.nn.sigmoid(x)


def _modulated_norm(x, g, scale, shift):
    r = lax.rsqrt(jnp.mean(x * x, axis=-1, keepdims=True) + EPS)
    return (x * r) * (g * (1.0 + scale)) + shift


def _params(n_axes):
    return pltpu.CompilerParams(dimension_semantics=("arbitrary",) * n_axes,
                                vmem_limit_bytes=VMEM_LIMIT)


def _mod_kernel(cctx_ref, c_ref, w_ref, b_ref, ctx_ref, lat_ref, acc_ref):
    rows = acc_ref.shape[0]
    n_lat = c_ref.shape[0]
    fill = jnp.zeros((rows - 1 - n_lat, c_ref.shape[1]), F32)
    s = _silu(jnp.concatenate([cctx_ref[...], c_ref[...], fill], axis=0))
    s_hi, s_lo = _split(jnp.concatenate([s, s], axis=0))
    upper = lax.broadcasted_iota(I32, (2 * rows, 1), 0) < rows
    w_hi, w_lo = _split(w_ref[...])
    by_hi = _dot(jnp.where(upper, s_hi, s_lo), w_hi)
    part = by_hi[:rows] + by_hi[rows:] + _dot(s_hi[:rows], w_lo)

    @pl.when(pl.program_id(0) == 0)
    def _():
        acc_ref[...] = part + b_ref[...]

    @pl.when(pl.program_id(0) != 0)
    def _():
        acc_ref[...] = acc_ref[...] + part

    @pl.when(pl.program_id(0) == pl.num_programs(0) - 1)
    def _():
        ctx_ref[0] = acc_ref[0:1, :]
        for i in range(n_lat):
            lat_ref[i] = acc_ref[1 + i:2 + i, :]


def _mod_call(c_ctx, c, w_mod, b_mod):
    n_lat, d = c.shape
    n = w_mod.shape[1]
    rows = -(-(1 + n_lat) // 8) * 8
    tk = D_MODEL // 4
    return pl.pallas_call(
        _mod_kernel,
        grid=(d // tk,),
        in_specs=[pl.BlockSpec((1, tk), lambda k: (0, k)),
                  pl.BlockSpec((n_lat, tk), lambda k: (0, k)),
                  pl.BlockSpec((tk, n), lambda k: (k, 0)),
                  pl.BlockSpec((1, n), lambda k: (0, 0))],
        out_specs=[pl.BlockSpec((1, 1, n), lambda k: (0, 0, 0)),
                   pl.BlockSpec((n_lat, 1, n), lambda k: (0, 0, 0))],
        out_shape=[jax.ShapeDtypeStruct((1, 1, n), F32), jax.ShapeDtypeStruct((n_lat, 1, n), F32)],
        scratch_shapes=[pltpu.VMEM((rows, n), F32)],
        compiler_params=_params(1),
        name="mod",
    )(c_ctx, c, w_mod, b_mod)


def _mod_rows(mod_ref, j):
    row = mod_ref[j % mod_ref.shape[0]]
    return [row[:, i * D_MODEL:(i + 1) * D_MODEL] for i in range(N_MOD)]


def _mod_spec(mod, seqs_per_step):
    if mod.shape[0] == 1:
        return pl.BlockSpec(mod.shape, lambda b: (0, 0, 0))
    return pl.BlockSpec((seqs_per_step,) + mod.shape[1:], lambda b: (b, 0, 0))


def _staggered(programs):
    programs = list(programs)
    started = 0
    while programs:
        started = min(started + 1, len(programs))
        running = [p for p in programs[:started] if next(p, "done") != "done"]
        programs = running + programs[started:]
        started = len(running)
        yield


def _for_row_tiles(seq_len, phases, independent):
    n = seq_len // ROW_TILE
    if independent:
        yield from _staggered(phases(i) for i in range(n))
    else:
        for i in range(n):
            yield from phases(i)


def _tile_rows(tile, offset=0, size=ROW_TILE):
    if isinstance(tile, int):
        return pl.ds(tile * ROW_TILE + offset, size)
    return pl.ds(pl.multiple_of(tile * ROW_TILE + offset, size), size)


def _mixer_kernel(*refs, seqs_per_step, has_state_in, has_state_out, stages_w_in, **static):
    refs = list(refs)
    n_in = 2 + (2 if has_state_in else 0)
    per_seq_in, refs = [refs[0]] + refs[2:n_in], [refs[1]] + refs[n_in:]
    mod_ref, weights, refs = refs[0], refs[1:1 + N_MIXER_WEIGHTS], refs[1 + N_MIXER_WEIGHTS:]
    n_out = 1 + (2 if has_state_out else 0)
    per_seq_out, refs = refs[:n_out], refs[n_out:]
    if stages_w_in:
        win_hbm_ref, win_bf_ref, scratch = weights[1], refs[0], refs[1:-2]
        win_f32_ref, win_sem_ref = refs[-2:]
        weights = weights[:1] + [win_bf_ref] + weights[2:]

        @pl.when(pl.program_id(0) == 0)
        def _():
            chunks = [(r0, min(WIN_CHUNK, P_TOT - r0)) for r0 in range(0, P_TOT, WIN_CHUNK)]
            copies = [pltpu.make_async_copy(win_hbm_ref.at[pl.ds(r0, n), :], win_f32_ref.at[pl.ds(r0, n), :],
                                            win_sem_ref.at[i]) for i, (r0, n) in enumerate(chunks)]
            for copy in copies:
                copy.start()
            for (r0, n_rows), copy in zip(chunks, copies):
                copy.wait()
                for c0 in range(r0, r0 + n_rows, LANES):
                    n = min(LANES, P_TOT - c0)
                    cols = win_f32_ref[c0:c0 + n, :]
                    if n < LANES:
                        cols = jnp.concatenate([cols, jnp.zeros((LANES - n, D_MODEL), F32)], axis=0)
                    win_bf_ref[:, c0:c0 + LANES] = jnp.transpose(cols).astype(BF16)
        win_ready = None
    else:
        win_hbm_ref, scratch = weights[1], refs[:-2]
        win_vmem_ref, win_sem_ref = refs[-2:]
        weights = weights[:1] + [win_vmem_ref] + weights[2:]
        copies = [pltpu.make_async_copy(win_hbm_ref.at[:, pl.ds(c0, c1 - c0)], win_vmem_ref.at[:, pl.ds(c0, c1 - c0)],
                                        win_sem_ref.at[i]) for i, (c0, c1) in enumerate(WIN_GROUPS)]

        @pl.when(pl.program_id(0) == 0)
        def _():
            for copy in copies:
                copy.start()

        def win_ready(group):
            @pl.when(pl.program_id(0) == 0)
            def _():
                copies[group].wait()
    programs = []
    for j in range(seqs_per_step):
        ins = [r.at[j] for r in per_seq_in]
        outs = [r.at[j] for r in per_seq_out]
        programs.append(_mixer_sequence(ins[0], _mod_rows(mod_ref, j), ins[1:], weights, outs[0], outs[1:],
                                        [r.at[j] for r in scratch], win_ready=win_ready if j == 0 else None,
                                        **static))
    for _ in _staggered(programs):
        pass


def _mixer_sequence(x_ref, m, s0_refs, weights, x1_ref, sout_refs, scratch, *, seq_len, period, win_ready=None):
    def need_group(ti, group):
        if win_ready is not None and ti == 0:
            win_ready(group)
    has_state_in = bool(s0_refs)
    has_state_out = bool(sout_refs)
    g1_ref, win_ref, wconv_ref, bconv_ref, wupf_ref, bupf_ref, wupb_ref, bupb_ref, ggla_ref, wout_ref = weights
    og_ref, qd_ref, kd_ref, kst_ref, dect_ref, v_ref, s_ref, sst_ref, o_ref, ya_ref = scratch

    c = GLA_CHUNK
    tile_chunks = ROW_TILE // c
    n_tiles = seq_len // ROW_TILE
    n_pairs = GLA_HEADS // 2
    pair_k = 2 * GLA_DK
    pair_v = 2 * GLA_DV

    def stage1(ti):
        rows = _tile_rows(ti)
        h = _modulated_norm(x_ref[rows, :], g1_ref[...], m[1], m[0]).astype(BF16)
        row_i = lax.broadcasted_iota(I32, (ROW_TILE, 1), 0)
        yield
        need_group(ti, 0)
        p_gate = _dot(h, win_ref[:, OFF_OG:P_PAD])
        og_ref[rows, :] = p_gate[:, :GLA_DV_TOT]
        yield
        zero_up = jnp.zeros((GLA_LOW_RANK, GLA_DK_TOT), F32)
        w_up = jnp.concatenate([jnp.concatenate([wupf_ref[0], zero_up], axis=1),
                                jnp.concatenate([zero_up, wupb_ref[0]], axis=1),
                                jnp.zeros((P_PAD - P_TOT, 2 * GLA_DK_TOT), F32)], axis=0).astype(BF16)
        b_up = jnp.concatenate([bupf_ref[...], bupb_ref[...]], axis=1)
        z = _dot(p_gate[:, GLA_DV_TOT:].astype(BF16), w_up) + b_up
        la = (jnp.minimum(z, 0.0) - jnp.log(1.0 + jnp.exp(-jnp.abs(z)))) * (LOG2_E / GLA_TAU)
        col_j = lax.broadcasted_iota(I32, (1, ROW_TILE), 1)
        same_chunk = (row_i & -c) == (col_j & -c)
        lower = jnp.where(same_chunk & (col_j <= row_i), 1.0, 0.0).astype(BF16)
        la_parts = jnp.concatenate(_split(la), axis=1)
        n_gate = 2 * GLA_DK_TOT
        yield
        pre = _dot(lower, la_parts)
        pre = pre[:, :n_gate] + pre[:, n_gate:]
        tot = jnp.concatenate([jnp.broadcast_to(pre[(n + 1) * c - 1:(n + 1) * c], (c, n_gate))
                               for n in range(tile_chunks)], axis=0)
        need_group(ti, 1)
        p_qkv = _dot(h, win_ref[:, OFF_Q:OFF_OG])
        yield
        q = p_qkv[:, :GLA_DK_TOT] * (GLA_DK ** -0.5)
        k = p_qkv[:, GLA_DK_TOT:2 * GLA_DK_TOT]
        v_ref[rows, :] = p_qkv[:, 2 * GLA_DK_TOT:].astype(BF16)
        for d in range(2):
            cols = slice(d * GLA_DK_TOT, (d + 1) * GLA_DK_TOT)
            if d == 0:
                bq = pre[:, cols]
                bk = tot[:, cols] - bq
            else:
                bk = pre[:, cols] - la[:, cols]
                bq = tot[:, cols] - bk
            qd_ref[d, rows, :] = (q * jnp.exp2(bq)).astype(BF16)
            kd_ref[d, rows, :] = (k * jnp.exp2(-bq)).astype(BF16)
            kst_ref[d, ti] = jnp.transpose(k * jnp.exp2(bk)).astype(BF16)
            totals = [tot[n * c:n * c + 1, cols] for n in range(tile_chunks)]
            totals.append(jnp.zeros((LANES - tile_chunks, GLA_DK_TOT), F32))
            dect_ref[d, ti] = jnp.transpose(jnp.exp2(jnp.concatenate(totals, axis=0)))
        yield
        need_group(ti, 2)
        p_conv = _dot(h, win_ref[:, :OFF_Q])
        yield
        pos = row_i & (period - 1)
        u = p_conv[:, OFF_XC:OFF_XC + D_CONV] * p_conv[:, OFF_XV:OFF_XV + D_CONV]
        u_prev = jnp.where(pos == 0, 0.0, pltpu.roll(u, 1, 0))
        u_next = jnp.where(pos == period - 1, 0.0, pltpu.roll(u, ROW_TILE - 1, 0))
        conv = u_prev * wconv_ref[0] + u * wconv_ref[1] + u_next * wconv_ref[2] + bconv_ref[...]
        ya_ref[rows, :] = (p_conv[:, OFF_XB:OFF_XB + D_CONV] * conv).astype(BF16)
        yield

    yield from _for_row_tiles(seq_len, stage1, independent=True)

    for d in range(2):
        for pair in range(n_pairs):
            if has_state_in:
                zero = jnp.zeros((GLA_DK, GLA_DV), F32)
                top = jnp.concatenate([s0_refs[d][2 * pair], zero], axis=1)
                bot = jnp.concatenate([zero, s0_refs[d][2 * pair + 1]], axis=1)
                s_ref[d, pair] = jnp.concatenate([top, bot], axis=0)
            else:
                s_ref[d, pair] = jnp.zeros((pair_k, pair_v), F32)

    def scan_tile(i):
        upper_lane = lax.broadcasted_iota(I32, (1, LANES), 1) >= GLA_DK
        qi = lax.broadcasted_iota(I32, (LANES, 1), 0)
        kj = lax.broadcasted_iota(I32, (1, 2 * LANES), 1) & (LANES - 1)
        same_chunk = (qi & c) == (kj & c)
        causal = (same_chunk & (kj <= qi), same_chunk & (kj >= qi))
        for pair in range(n_pairs):
            kl = slice(pair * pair_k, (pair + 1) * pair_k)
            vl = slice(pair * pair_v, (pair + 1) * pair_v)
            for blk in range(ROW_TILE // LANES):
                rows = _tile_rows(i, blk * LANES, LANES)
                att = None
                for d in range(2):
                    kd = kd_ref[d, rows, kl]
                    zk = jnp.zeros_like(kd)
                    keys = jnp.concatenate([jnp.where(upper_lane, zk, kd), jnp.where(upper_lane, kd, zk)], axis=0)
                    a = jnp.where(causal[d], _dot_nt(qd_ref[d, rows, kl], keys), 0.0)
                    att = a if att is None else att + a
                v = v_ref[rows, vl]
                zv = jnp.zeros((LANES, GLA_DV), BF16)
                v_bd = jnp.concatenate([jnp.concatenate([v[:, :GLA_DV], zv], axis=1),
                                        jnp.concatenate([zv, v[:, GLA_DV:]], axis=1)], axis=0)
                o_ref[rows, vl] = _dot(att.astype(BF16), v_bd)
        yield
        key_row = lax.broadcasted_iota(I32, (pair_k, 1), 0)
        val_col = lax.broadcasted_iota(I32, (1, pair_v), 1)
        blockdiag = (key_row >= GLA_DK) == (val_col >= GLA_DV)
        for d in range(2):
            tile = i if d == 0 else n_tiles - 1 - i
            chunks = range(tile_chunks)
            for pair in range(n_pairs):
                kr = slice(pair * pair_k, (pair + 1) * pair_k)
                vl = slice(pair * pair_v, (pair + 1) * pair_v)
                s = s_ref[d, pair]
                for c4 in (chunks if d == 0 else reversed(chunks)):
                    blk, half = divmod(c4, 2)
                    kst = kst_ref[d, tile, kr, blk * LANES:(blk + 1) * LANES]
                    kst = jnp.where(upper_lane if half else ~upper_lane, kst, jnp.zeros_like(kst))
                    kv = jnp.where(blockdiag, _dot(kst, v_ref[_tile_rows(tile, blk * LANES, LANES), vl]), 0.0)
                    sst_ref[pair, tile * tile_chunks + c4, d * pair_k:(d + 1) * pair_k, :] = s.astype(BF16)
                    s = dect_ref[d, tile, kr, c4:c4 + 1] * s + kv
                s_ref[d, pair] = s
        yield

    yield from _for_row_tiles(seq_len, scan_tile, independent=False)

    if has_state_out:
        for d in range(2):
            for pair in range(n_pairs):
                s = s_ref[d, pair]
                sout_refs[d][2 * pair] = s[0:GLA_DK, 0:GLA_DV]
                sout_refs[d][2 * pair + 1] = s[GLA_DK:, GLA_DV:]

    def stage3(i):
        for pair in range(n_pairs):
            kl = slice(pair * pair_k, (pair + 1) * pair_k)
            vl = slice(pair * pair_v, (pair + 1) * pair_v)
            for c4 in range(tile_chunks):
                crow = _tile_rows(i, c4 * c, c)
                q2 = jnp.concatenate([qd_ref[0, crow, kl], qd_ref[1, crow, kl]], axis=1)
                o_ref[crow, vl] = o_ref[crow, vl] + _dot(q2, sst_ref[pair, i * tile_chunks + c4])
        yield
        rows = _tile_rows(i)
        heads = []
        for h in range(GLA_HEADS):
            hl = slice(h * GLA_DV, (h + 1) * GLA_DV)
            oh = o_ref[rows, hl]
            r = lax.rsqrt(jnp.mean(oh * oh, axis=-1, keepdims=True) + EPS)
            heads.append(oh * r * ggla_ref[:, hl])
        y_b = jnp.concatenate(heads, axis=1) * _silu(og_ref[rows, :])
        y = jnp.concatenate([ya_ref[rows, :], y_b.astype(BF16)], axis=1)
        x1_ref[rows, :] = x_ref[rows, :] + m[2] * _dot(y, wout_ref[...])
        yield

    yield from _for_row_tiles(seq_len, stage3, independent=True)


def _mixer_call(x, mod, states, weights, *, period, has_state_out, seqs_per_step):
    n_seq, seq_len, _ = x.shape
    sps = seqs_per_step
    has_state_in = states is not None
    stages_w_in = weights[1].dtype == F32
    kernel = functools.partial(_mixer_kernel, seqs_per_step=sps, seq_len=seq_len, period=period,
                               has_state_in=has_state_in, has_state_out=has_state_out, stages_w_in=stages_w_in)
    state_spec = pl.BlockSpec((sps, None, GLA_HEADS, GLA_DK, GLA_DV), lambda b: (b, 0, 0, 0, 0))
    const2 = lambda b: (0, 0)
    in_specs = [pl.BlockSpec((sps, seq_len, D_MODEL), lambda b: (b, 0, 0)),
                _mod_spec(mod, sps)]
    args = [x, mod]
    if has_state_in:
        in_specs += [state_spec, state_spec]
        args += list(states)
    weight_specs = [pl.BlockSpec(w.shape, lambda b, nd=w.ndim: (0,) * nd, pipeline_mode=pl.Buffered(1))
                    for w in weights]
    weight_specs[1] = pl.BlockSpec(memory_space=pl.ANY)
    in_specs += weight_specs
    args += list(weights)
    out_specs = [pl.BlockSpec((sps, seq_len, D_MODEL), lambda b: (b, 0, 0))]
    out_shape = [jax.ShapeDtypeStruct((n_seq, seq_len, D_MODEL), F32)]
    if has_state_out:
        out_specs += [state_spec, state_spec]
        out_shape += [jax.ShapeDtypeStruct((n_seq, 1, GLA_HEADS, GLA_DK, GLA_DV), F32)] * 2
    if stages_w_in:
        out_specs.append(pl.BlockSpec((D_MODEL, P_PAD), const2))
        out_shape.append(jax.ShapeDtypeStruct((D_MODEL, P_PAD), BF16))
    n_tiles = seq_len // ROW_TILE
    n_pairs = GLA_HEADS // 2
    per_seq = lambda shape, dtype: pltpu.VMEM((sps,) + shape, dtype)
    scratch = [per_seq((seq_len, GLA_DV_TOT), F32),
               per_seq((2, seq_len, GLA_DK_TOT), BF16),
               per_seq((2, seq_len, GLA_DK_TOT), BF16),
               per_seq((2, n_tiles, GLA_DK_TOT, ROW_TILE), BF16),
               per_seq((2, n_tiles, GLA_DK_TOT, LANES), F32),
               per_seq((seq_len, GLA_DV_TOT), BF16),
               per_seq((2, n_pairs, 2 * GLA_DK, 2 * GLA_DV), F32),
               per_seq((n_pairs, seq_len // GLA_CHUNK, 4 * GLA_DK, 2 * GLA_DV), BF16),
               per_seq((seq_len, GLA_DV_TOT), F32),
               per_seq((seq_len, D_CONV), BF16)]
    if stages_w_in:
        scratch += [pltpu.VMEM(weights[1].shape, F32),
                    pltpu.SemaphoreType.DMA((-(-weights[1].shape[0] // WIN_CHUNK),))]
    else:
        scratch += [pltpu.VMEM(weights[1].shape, BF16),
                    pltpu.SemaphoreType.DMA((len(WIN_GROUPS),))]
    return pl.pallas_call(
        kernel,
        grid=(n_seq // sps,),
        in_specs=in_specs,
        out_specs=out_specs,
        out_shape=out_shape,
        scratch_shapes=scratch,
        compiler_params=_params(1),
        name="mixer",
    )(*args)


SUBLANES = 8


ROW_TOKENS = LANES // N_EXPERTS


def _route_kernel(x1_ref, mod_ref, g2_ref, wr2_ref, *refs, seqs_per_step, **static):
    programs = [_route_sequence(x1_ref.at[j], _mod_rows(mod_ref, j), g2_ref, wr2_ref, *[r.at[j] for r in refs],
                                **static)
                for j in range(seqs_per_step)]
    for _ in _staggered(programs):
        pass


def _route_sequence(x1_ref, m, g2_ref, wr2_ref, xs_ref, rank_ref, gate_ref,
                    pt_ref, h2_ref, spread_ref, bar_ref, cnt_ref, win_ref, *, seq_len, cap):
    n_rows = seq_len // ROW_TOKENS
    for t in range(seq_len // ROW_TILE):
        rows = slice(t * ROW_TILE, (t + 1) * ROW_TILE)
        h2 = _modulated_norm(x1_ref[rows, :], g2_ref[...], m[4], m[3])
        hi = h2.astype(BF16)
        h2_ref[rows, :] = hi
        yield
        by_hi = _dot_nt(wr2_ref[...], hi)
        pt_ref[:, rows] = by_hi[:N_EXPERTS] + by_hi[N_EXPERTS:]
    yield

    logits = pt_ref[...]
    ex = jnp.exp(logits - jnp.max(logits, axis=0, keepdims=True))
    probs = ex / jnp.sum(ex, axis=0, keepdims=True)
    pt_ref[...] = probs
    pad = jnp.zeros((LANES - N_EXPERTS, seq_len), F32)
    p_tok = jnp.transpose(jnp.concatenate([probs, pad], axis=0))

    spread = p_tok
    sh = N_EXPERTS
    while sh < LANES:
        spread = spread + pltpu.roll(spread, sh, 1)
        sh *= 2
    spread_ref[...] = spread
    sub_j = lax.broadcasted_iota(I32, (SUBLANES, 1), 0)
    lane_g = lax.broadcasted_iota(I32, (1, LANES), 1) >> (N_EXPERTS.bit_length() - 1)
    lane_j = (ROW_TOKENS - lane_g) & (ROW_TOKENS - 1)
    own_group = jnp.where(lane_j == sub_j, spread.reshape(n_rows, SUBLANES, LANES), 0.0)
    bar = jnp.sum(own_group, axis=1, keepdims=True)
    bar_ref[...] = jnp.broadcast_to(bar, (n_rows, SUBLANES, LANES))
    cnt_ref[...] = jnp.zeros((n_rows, SUBLANES, LANES), F32)

    rows_per_block = LANES // ROW_TOKENS

    def count_block(g_s, g_t, relation):
        s_blk = spread_ref[g_s * LANES:(g_s + 1) * LANES, :]
        wins = [jnp.zeros((SUBLANES, LANES), F32)] * rows_per_block
        for tl in range(rows_per_block):
            r = g_t * rows_per_block + tl
            bar_r = bar_ref[r]
            acc = cnt_ref[r]
            for sl in range(rows_per_block):
                s_vreg = s_blk[sl * SUBLANES:(sl + 1) * SUBLANES]
                if relation == "before" or sl < tl:
                    won = jnp.where(s_vreg >= bar_r, 1.0, 0.0)
                    acc = acc + won
                    wins[sl] = wins[sl] + won
                elif sl == tl:
                    acc = acc + jnp.where(s_vreg > bar_r, 1.0, 0.0) \
                        + jnp.where((s_vreg == bar_r) & (sub_j < lane_j), 1.0, 0.0)
            cnt_ref[r] = acc
        w_rows = slice(g_s * rows_per_block, (g_s + 1) * rows_per_block)
        win_ref[w_rows] = win_ref[w_rows] + jnp.stack(wins)

    n_grp = seq_len // LANES
    win_ref[...] = jnp.zeros((n_rows, SUBLANES, LANES), F32)
    yield
    for g_t in range(n_grp):
        for g_s in range(g_t + 1):
            count_block(g_s, g_t, "before" if g_s < g_t else "same")
            yield

    counts = jnp.sum(cnt_ref[...], axis=1, keepdims=True)
    counts = jnp.broadcast_to(counts, (n_rows, SUBLANES, LANES)).reshape(seq_len, LANES)
    rank_tok = pltpu.roll(counts, 0, 1, stride=N_EXPERTS, stride_axis=0)
    wins = win_ref[...].reshape(seq_len, LANES)
    sh = N_EXPERTS
    while sh < LANES:
        wins = wins + pltpu.roll(wins, sh, 1)
        sh *= 2
    tok = lax.broadcasted_iota(I32, (seq_len, 1), 0)
    later = ((n_rows - 1 - (tok >> (ROW_TOKENS.bit_length() - 1))) * ROW_TOKENS).astype(F32)
    rank_tok = rank_tok + (later - wins)
    rank_ref[...] = jnp.transpose(rank_tok)[:N_EXPERTS].astype(I32)
    yield

    group = DISPATCH_ROWS // cap
    slot = lax.broadcasted_iota(I32, (cap, 1), 0)
    half = D_MODEL // 2
    for gi in range(N_EXPERTS // group):
        picks = []
        for e in range(gi * group, (gi + 1) * group):
            oh = rank_ref[e:e + 1, :] == slot
            gate = jnp.sum(jnp.where(oh, pt_ref[e:e + 1, :], 0.0), axis=1, keepdims=True)
            gate_ref[e * cap:(e + 1) * cap, :] = jnp.broadcast_to(gate, (cap, LANES))
            picks.append(oh)
        ohb = jnp.where(jnp.concatenate(picks, axis=0), 1.0, 0.0).astype(BF16)
        yield
        out_rows = slice(gi * DISPATCH_ROWS, (gi + 1) * DISPATCH_ROWS)
        xs_ref[out_rows, :half] = _dot(ohb, h2_ref[:, :half]).astype(BF16)
        xs_ref[out_rows, half:] = _dot(ohb, h2_ref[:, half:]).astype(BF16)
        yield


def _route_call(x1, mod, g2, wr_both, *, seqs_per_step):
    n_seq, seq_len, _ = x1.shape
    sps = seqs_per_step
    cap = EC_CAPACITY_FACTOR * seq_len // N_EXPERTS
    kernel = functools.partial(_route_kernel, seqs_per_step=sps, seq_len=seq_len, cap=cap)
    const2 = lambda b: (0, 0)
    per_seq = lambda shape, dtype: pltpu.VMEM((sps,) + shape, dtype)
    rank_rows = (seq_len // ROW_TOKENS, SUBLANES, LANES)
    return pl.pallas_call(
        kernel,
        grid=(n_seq // sps,),
        in_specs=[pl.BlockSpec((sps, seq_len, D_MODEL), lambda b: (b, 0, 0)),
                  _mod_spec(mod, sps),
                  pl.BlockSpec((1, D_MODEL), const2),
                  pl.BlockSpec((2 * N_EXPERTS, D_MODEL), const2)],
        out_specs=[pl.BlockSpec((sps, N_EXPERTS * cap, D_MODEL), lambda b: (b, 0, 0)),
                   pl.BlockSpec((sps, N_EXPERTS, seq_len), lambda b: (b, 0, 0)),
                   pl.BlockSpec((sps, N_EXPERTS * cap, LANES), lambda b: (b, 0, 0))],
        out_shape=[jax.ShapeDtypeStruct((n_seq, N_EXPERTS * cap, D_MODEL), BF16),
                   jax.ShapeDtypeStruct((n_seq, N_EXPERTS, seq_len), I32),
                   jax.ShapeDtypeStruct((n_seq, N_EXPERTS * cap, LANES), F32)],
        scratch_shapes=[per_seq((N_EXPERTS, seq_len), F32),
                        per_seq((seq_len, D_MODEL), BF16),
                        per_seq((seq_len, LANES), F32),
                        per_seq(rank_rows, F32),
                        per_seq(rank_rows, F32),
                        per_seq(rank_rows, F32)],
        compiler_params=_params(1),
        name="route",
    )(x1, mod, g2, wr_both)


def _experts_kernel(xc_ref, xl_ref, gc_ref, gl_ref, wg_ref, wu_ref, wd_ref, yc_ref, yl_ref,
                    wgb_ref, wub_ref, wdb_ref, a_ref):
    wgb_ref[...] = wg_ref[...].astype(BF16)
    wub_ref[...] = wu_ref[...].astype(BF16)
    wdb_ref[...] = wd_ref[...].astype(BF16)
    f_tile = 2 * LANES

    def run(x_ref, g_ref, y_ref):
        n_seq, cap, _ = x_ref.shape
        seqs = DISPATCH_ROWS // cap
        for s0 in range(0, n_seq, seqs):
            x = x_ref[s0:s0 + seqs].reshape(DISPATCH_ROWS, D_MODEL)
            for f0 in range(0, D_EXPERT, f_tile):
                cols = slice(f0, f0 + f_tile)
                a_ref[:, cols] = (_silu(_dot(x, wgb_ref[:, cols])) * _dot(x, wub_ref[:, cols])).astype(BF16)
            gate = g_ref[s0:s0 + seqs].reshape(DISPATCH_ROWS, LANES)
            y = _dot(a_ref[...], wdb_ref[...]) * jnp.concatenate([gate] * (D_MODEL // LANES), axis=1)
            y_ref[s0:s0 + seqs] = y.astype(BF16).reshape(seqs, cap, D_MODEL)

    run(xc_ref, gc_ref, yc_ref)
    run(xl_ref, gl_ref, yl_ref)


def _experts_call(xs_ctx, xs_lat, gates_ctx, gates_lat, w_gate, w_up, w_down):
    def slot_spec(a):
        n_seq, _, cap, width = a.shape
        return pl.BlockSpec((n_seq, None, cap, width), lambda e: (0, e, 0, 0))

    w_spec = pl.BlockSpec((None, D_MODEL, D_EXPERT), lambda e: (e, 0, 0))
    return pl.pallas_call(
        _experts_kernel,
        grid=(N_EXPERTS,),
        in_specs=[slot_spec(xs_ctx), slot_spec(xs_lat), slot_spec(gates_ctx), slot_spec(gates_lat),
                  w_spec, w_spec, pl.BlockSpec((None, D_EXPERT, D_MODEL), lambda e: (e, 0, 0))],
        out_specs=[slot_spec(xs_ctx), slot_spec(xs_lat)],
        out_shape=[jax.ShapeDtypeStruct(xs_ctx.shape, BF16), jax.ShapeDtypeStruct(xs_lat.shape, BF16)],
        scratch_shapes=[pltpu.VMEM((D_MODEL, D_EXPERT), BF16),
                        pltpu.VMEM((D_MODEL, D_EXPERT), BF16),
                        pltpu.VMEM((D_EXPERT, D_MODEL), BF16),
                        pltpu.VMEM((DISPATCH_ROWS, D_EXPERT), BF16)],
        compiler_params=_params(1),
        name="experts",
    )(xs_ctx, xs_lat, gates_ctx, gates_lat, w_gate, w_up, w_down)


def _combine_tokens(x1_ref, y_ref, rank_ref, gate2, gf_ref, o_ref, oh_ref):
    n_tok = x1_ref.shape[0]
    cap = y_ref.shape[0] // N_EXPERTS
    slot = lax.broadcasted_iota(I32, (cap, 1), 0)
    for e in range(N_EXPERTS):
        oh_ref[e * cap:(e + 1) * cap, :] = jnp.where(rank_ref[e:e + 1, :] == slot, 1.0, 0.0).astype(BF16)
    for t in range(n_tok // ROW_TILE):
        rows = slice(t * ROW_TILE, (t + 1) * ROW_TILE)
        moe = _dot_tn(oh_ref[:, rows], y_ref[...])
        x2 = x1_ref[rows, :] + gate2 * moe
        r = lax.rsqrt(jnp.mean(x2 * x2, axis=-1, keepdims=True) + EPS)
        o_ref[rows, :] = (x2 * r) * gf_ref[...]


def _combine_kernel(x1c_ref, yc_ref, rkc_ref, modc_ref, x1l_ref, yl_ref, rkl_ref, modl_ref, gf_ref,
                    oc_ref, ol_ref, ohc_ref, ohl_ref):
    for j in range(x1c_ref.shape[0]):
        _combine_tokens(x1c_ref.at[j], yc_ref.at[j], rkc_ref.at[j], _mod_rows(modc_ref, j)[5], gf_ref,
                        oc_ref.at[j], ohc_ref.at[j])
    _combine_tokens(x1l_ref.at[0], yl_ref.at[0], rkl_ref.at[0], _mod_rows(modl_ref, 0)[5], gf_ref,
                    ol_ref.at[0], ohl_ref)


def _combine_call(ctx, lat, g_final, *, ctx_seqs_per_step):
    x1c, yc, rkc, modc = ctx
    x1l, yl, rkl, modl = lat
    n_ctx, ctx_len, _ = x1c.shape
    n_lat, lat_len, _ = x1l.shape
    sps = ctx_seqs_per_step
    n_steps = n_ctx // sps
    slabs = n_steps // n_lat
    slab = lat_len // slabs
    assert n_steps == n_lat * slabs and slab % ROW_TILE == 0 and modl.shape[0] == n_lat
    ctx_blk = lambda shape: pl.BlockSpec((sps,) + shape, lambda b: (b, 0, 0))
    lat_seq = lambda b: (b // slabs, 0, 0)
    return pl.pallas_call(
        _combine_kernel,
        grid=(n_steps,),
        in_specs=[ctx_blk((ctx_len, D_MODEL)), ctx_blk(yc.shape[1:]), ctx_blk((N_EXPERTS, ctx_len)),
                  _mod_spec(modc, sps),
                  pl.BlockSpec((1, slab, D_MODEL), lambda b: (b // slabs, b % slabs, 0)),
                  pl.BlockSpec((1,) + yl.shape[1:], lat_seq),
                  pl.BlockSpec((1, N_EXPERTS, slab), lambda b: (b // slabs, 0, b % slabs)),
                  pl.BlockSpec((1,) + modl.shape[1:], lat_seq),
                  pl.BlockSpec((1, D_MODEL), lambda b: (0, 0))],
        out_specs=[ctx_blk((ctx_len, D_MODEL)),
                   pl.BlockSpec((1, slab, D_MODEL), lambda b: (b // slabs, b % slabs, 0))],
        out_shape=[jax.ShapeDtypeStruct(x1c.shape, F32), jax.ShapeDtypeStruct(x1l.shape, F32)],
        scratch_shapes=[pltpu.VMEM((sps, yc.shape[1], ctx_len), BF16),
                        pltpu.VMEM((yl.shape[1], slab), BF16)],
        compiler_params=_params(1),
        name="combine",
    )(x1c, yc, rkc, modc, x1l, yl, rkl, modl, g_final)


def kernel(x_prompt, x_sample, state_gla_fwd, state_gla_bwd, c, c_ctx, w_mod, b_mod, g_norm1, g_norm2,
           w_in, w_conv, b_conv, w_a_up_f, b_a_f, w_a_up_b, b_a_b, g_gla_norm, w_out, w_router,
           w_gate, w_up, w_down, g_final):
    assert w_mod.shape[0] == 1, "single trunk layer"
    n_ctx, ctx_len, _ = x_prompt.shape
    n_lat, lat_len, _ = x_sample.shape
    ctx_cap = EC_CAPACITY_FACTOR * ctx_len // N_EXPERTS
    lat_cap = EC_CAPACITY_FACTOR * lat_len // N_EXPERTS

    mod_ctx, mod_lat = _mod_call(c_ctx[None, :], c, w_mod[0], b_mod)

    mixer_weights = [g_norm1, jnp.transpose(w_in[0]), jnp.transpose(w_conv, (1, 0, 2)), b_conv,
                     w_a_up_f, b_a_f, w_a_up_b, b_a_b,
                     g_gla_norm[0].reshape(1, GLA_DV_TOT), w_out[0].astype(BF16)]
    assert len(mixer_weights) == N_MIXER_WEIGHTS
    wr_t = jnp.transpose(jnp.concatenate([w_router[0], w_router[0]], axis=1))
    wr_hi = wr_t.astype(BF16)
    wr_lo = (wr_t - wr_hi.astype(F32)).astype(BF16)
    wr_both = jnp.where(np.arange(2 * N_EXPERTS)[:, None] < N_EXPERTS, wr_hi, wr_lo)

    x1_ctx, new_f, new_b, w_in_bf = _mixer_call(x_prompt, mod_ctx, None, mixer_weights,
                                                period=ctx_len, has_state_out=True, seqs_per_step=4)
    mixer_weights[1] = w_in_bf
    (x1_lat,) = _mixer_call(x_sample, mod_lat, (state_gla_fwd, state_gla_bwd), mixer_weights,
                            period=GRID_W, has_state_out=False, seqs_per_step=1)

    xs_ctx, rank_ctx, gates_ctx = _route_call(x1_ctx, mod_ctx, g_norm2, wr_both, seqs_per_step=4)
    xs_lat, rank_lat, gates_lat = _route_call(x1_lat, mod_lat, g_norm2, wr_both, seqs_per_step=1)

    per_expert = lambda a, n, cap: a.reshape(n, N_EXPERTS, cap, a.shape[-1])
    y_ctx, y_lat = _experts_call(per_expert(xs_ctx, n_ctx, ctx_cap), per_expert(xs_lat, n_lat, lat_cap),
                                 per_expert(gates_ctx, n_ctx, ctx_cap), per_expert(gates_lat, n_lat, lat_cap),
                                 w_gate[0], w_up[0], w_down[0])

    g_fin = g_final[None, :]
    y_prompt, y_sample = _combine_call((x1_ctx, y_ctx.reshape(xs_ctx.shape), rank_ctx, mod_ctx),
                                       (x1_lat, y_lat.reshape(xs_lat.shape), rank_lat, mod_lat),
                                       g_fin, ctx_seqs_per_step=4)
    return y_prompt, y_sample, new_f, new_b
```

```python
import functools

import jax
import jax.numpy as jnp
import numpy as np
from jax import lax
from jax.experimental import pallas as pl
from jax.experimental.pallas import tpu as pltpu

F32 = jnp.float32
BF16 = jnp.bfloat16
I32 = jnp.int32

D_MODEL = 1024
D_CONV = D_MODEL // 2
GRID_W = 64
GLA_HEADS = 4
GLA_DK = 64
GLA_DV = 128
GLA_DK_TOT = GLA_HEADS * GLA_DK
GLA_DV_TOT = GLA_HEADS * GLA_DV
GLA_LOW_RANK = 16
GLA_TAU = 16.0
GLA_CHUNK = 64
N_EXPERTS = 16
EC_CAPACITY_FACTOR = 2
D_EXPERT = 1024
N_MOD = 6
EPS = 1e-6
LOG2_E = 1.4426950408889634

OFF_XB = 0
OFF_XC = D_CONV
OFF_XV = 2 * D_CONV
OFF_Q = 3 * D_CONV
OFF_K = OFF_Q + GLA_DK_TOT
OFF_V = OFF_K + GLA_DK_TOT
OFF_OG = OFF_V + GLA_DV_TOT
OFF_ALOW = OFF_OG + GLA_DV_TOT
P_TOT = OFF_ALOW + 2 * GLA_LOW_RANK

LANES = 128
P_PAD = -(-P_TOT // LANES) * LANES
ROW_TILE = 256
N_MIXER_WEIGHTS = 10
WIN_CHUNK = 512
DISPATCH_ROWS = 512
VMEM_LIMIT = 56 * 1024 * 1024


def _dot(a, b):
    return jnp.dot(a, b, preferred_element_type=F32)


def _dot_nt(a, b):
    return lax.dot_general(a, b, (((1,), (1,)), ((), ())), preferred_element_type=F32)


def _dot_tn(a, b):
    return lax.dot_general(a, b, (((0,), (0,)), ((), ())), preferred_element_type=F32)


def _split(a):
    hi = a.astype(BF16)
    lo = (a - hi.astype(F32)).astype(BF16)
    return hi, lo


def _silu(x):
    return x * jax.nn.sigmoid(x)


def _modulated_norm(x, g, scale, shift):
    r = lax.rsqrt(jnp.mean(x * x, axis=-1, keepdims=True) + EPS)
    return (x * r) * (g * (1.0 + scale)) + shift


def _params(n_axes):
    return pltpu.CompilerParams(dimension_semantics=("arbitrary",) * n_axes,
                                vmem_limit_bytes=VMEM_LIMIT)


def _mod_kernel(cctx_ref, c_ref, w_ref, b_ref, ctx_ref, lat_ref, acc_ref):
    rows = acc_ref.shape[0]
    n_lat = c_ref.shape[0]
    fill = jnp.zeros((rows - 1 - n_lat, c_ref.shape[1]), F32)
    s = _silu(jnp.concatenate([cctx_ref[...], c_ref[...], fill], axis=0))
    s_hi, s_lo = _split(jnp.concatenate([s, s], axis=0))
    upper = lax.broadcasted_iota(I32, (2 * rows, 1), 0) < rows
    w_hi, w_lo = _split(w_ref[...])
    by_hi = _dot(jnp.where(upper, s_hi, s_lo), w_hi)
    part = by_hi[:rows] + by_hi[rows:] + _dot(s_hi[:rows], w_lo)

    @pl.when(pl.program_id(0) == 0)
    def _():
        acc_ref[...] = part + b_ref[...]

    @pl.when(pl.program_id(0) != 0)
    def _():
        acc_ref[...] = acc_ref[...] + part

    @pl.when(pl.program_id(0) == pl.num_programs(0) - 1)
    def _():
        ctx_ref[0] = acc_ref[0:1, :]
        for i in range(n_lat):
            lat_ref[i] = acc_ref[1 + i:2 + i, :]


def _mod_call(c_ctx, c, w_mod, b_mod):
    n_lat, d = c.shape
    n = w_mod.shape[1]
    rows = -(-(1 + n_lat) // 8) * 8
    tk = D_MODEL // 4
    return pl.pallas_call(
        _mod_kernel,
        grid=(d // tk,),
        in_specs=[pl.BlockSpec((1, tk), lambda k: (0, k)),
                  pl.BlockSpec((n_lat, tk), lambda k: (0, k)),
                  pl.BlockSpec((tk, n), lambda k: (k, 0)),
                  pl.BlockSpec((1, n), lambda k: (0, 0))],
        out_specs=[pl.BlockSpec((1, 1, n), lambda k: (0, 0, 0)),
                   pl.BlockSpec((n_lat, 1, n), lambda k: (0, 0, 0))],
        out_shape=[jax.ShapeDtypeStruct((1, 1, n), F32), jax.ShapeDtypeStruct((n_lat, 1, n), F32)],
        scratch_shapes=[pltpu.VMEM((rows, n), F32)],
        compiler_params=_params(1),
        name="mod",
    )(c_ctx, c, w_mod, b_mod)


def _mod_rows(mod_ref, j):
    row = mod_ref[j % mod_ref.shape[0]]
    return [row[:, i * D_MODEL:(i + 1) * D_MODEL] for i in range(N_MOD)]


def _mod_spec(mod, seqs_per_step):
    if mod.shape[0] == 1:
        return pl.BlockSpec(mod.shape, lambda b: (0, 0, 0))
    return pl.BlockSpec((seqs_per_step,) + mod.shape[1:], lambda b: (b, 0, 0))


def _staggered(programs):
    programs = list(programs)
    started = 0
    while programs:
        started = min(started + 1, len(programs))
        running = [p for p in programs[:started] if next(p, "done") != "done"]
        programs = running + programs[started:]
        started = len(running)
        yield


def _for_row_tiles(seq_len, phases, independent):
    n = seq_len // ROW_TILE
    if independent:
        yield from _staggered(phases(i) for i in range(n))
    else:
        for i in range(n):
            yield from phases(i)


def _tile_rows(tile, offset=0, size=ROW_TILE):
    if isinstance(tile, int):
        return pl.ds(tile * ROW_TILE + offset, size)
    return pl.ds(pl.multiple_of(tile * ROW_TILE + offset, size), size)


def _mixer_kernel(*refs, seqs_per_step, has_state_in, has_state_out, stages_w_in, **static):
    refs = list(refs)
    n_in = 2 + (2 if has_state_in else 0)
    per_seq_in, refs = [refs[0]] + refs[2:n_in], [refs[1]] + refs[n_in:]
    mod_ref, weights, refs = refs[0], refs[1:1 + N_MIXER_WEIGHTS], refs[1 + N_MIXER_WEIGHTS:]
    n_out = 1 + (2 if has_state_out else 0)
    per_seq_out, refs = refs[:n_out], refs[n_out:]
    if stages_w_in:
        win_hbm_ref, win_bf_ref, scratch = weights[1], refs[0], refs[1:-2]
        win_f32_ref, win_sem_ref = refs[-2:]
        weights = weights[:1] + [win_bf_ref] + weights[2:]

        @pl.when(pl.program_id(0) == 0)
        def _():
            chunks = [(r0, min(WIN_CHUNK, P_TOT - r0)) for r0 in range(0, P_TOT, WIN_CHUNK)]
            copies = [pltpu.make_async_copy(win_hbm_ref.at[pl.ds(r0, n), :], win_f32_ref.at[pl.ds(r0, n), :],
                                            win_sem_ref.at[i]) for i, (r0, n) in enumerate(chunks)]
            for i, copy in enumerate(copies):
                copy.start(priority=i % 2)
            for (r0, n_rows), copy in zip(chunks, copies):
                copy.wait()
                for c0 in range(r0, r0 + n_rows, LANES):
                    n = min(LANES, P_TOT - c0)
                    cols = win_f32_ref[c0:c0 + n, :]
                    if n < LANES:
                        cols = jnp.concatenate([cols, jnp.zeros((LANES - n, D_MODEL), F32)], axis=0)
                    win_bf_ref[:, c0:c0 + LANES] = jnp.transpose(cols).astype(BF16)
    else:
        scratch = refs
    programs = []
    for j in range(seqs_per_step):
        ins = [r.at[j] for r in per_seq_in]
        outs = [r.at[j] for r in per_seq_out]
        programs.append(_mixer_sequence(ins[0], _mod_rows(mod_ref, j), ins[1:], weights, outs[0], outs[1:],
                                        [r.at[j] for r in scratch], **static))
    for _ in _staggered(programs):
        pass


def _mixer_sequence(x_ref, m, s0_refs, weights, x1_ref, sout_refs, scratch, *, seq_len, period):
    has_state_in = bool(s0_refs)
    has_state_out = bool(sout_refs)
    g1_ref, win_ref, wconv_ref, bconv_ref, wupf_ref, bupf_ref, wupb_ref, bupb_ref, ggla_ref, wout_ref = weights
    og_ref, qd_ref, kd_ref, kst_ref, dect_ref, v_ref, s_ref, sst_ref, o_ref, ya_ref = scratch

    c = GLA_CHUNK
    tile_chunks = ROW_TILE // c
    n_tiles = seq_len // ROW_TILE
    n_pairs = GLA_HEADS // 2
    pair_k = 2 * GLA_DK
    pair_v = 2 * GLA_DV

    def stage1(ti):
        rows = _tile_rows(ti)
        h = _modulated_norm(x_ref[rows, :], g1_ref[...], m[1], m[0]).astype(BF16)
        row_i = lax.broadcasted_iota(I32, (ROW_TILE, 1), 0)
        yield
        p_gate = _dot(h, win_ref[:, OFF_OG:P_PAD])
        og_ref[rows, :] = p_gate[:, :GLA_DV_TOT]
        yield
        zero_up = jnp.zeros((GLA_LOW_RANK, GLA_DK_TOT), F32)
        w_up = jnp.concatenate([jnp.concatenate([wupf_ref[0], zero_up], axis=1),
                                jnp.concatenate([zero_up, wupb_ref[0]], axis=1),
                                jnp.zeros((P_PAD - P_TOT, 2 * GLA_DK_TOT), F32)], axis=0).astype(BF16)
        b_up = jnp.concatenate([bupf_ref[...], bupb_ref[...]], axis=1)
        z = _dot(p_gate[:, GLA_DV_TOT:].astype(BF16), w_up) + b_up
        la = (jnp.minimum(z, 0.0) - jnp.log(1.0 + jnp.exp(-jnp.abs(z)))) * (LOG2_E / GLA_TAU)
        col_j = lax.broadcasted_iota(I32, (1, ROW_TILE), 1)
        same_chunk = (row_i & -c) == (col_j & -c)
        lower = jnp.where(same_chunk & (col_j <= row_i), 1.0, 0.0).astype(BF16)
        la_parts = jnp.concatenate(_split(la), axis=1)
        n_gate = 2 * GLA_DK_TOT
        yield
        pre = _dot(lower, la_parts)
        pre = pre[:, :n_gate] + pre[:, n_gate:]
        tot = jnp.concatenate([jnp.broadcast_to(pre[(n + 1) * c - 1:(n + 1) * c], (c, n_gate))
                               for n in range(tile_chunks)], axis=0)
        p_qkv = _dot(h, win_ref[:, OFF_Q:OFF_OG])
        yield
        q = p_qkv[:, :GLA_DK_TOT] * (GLA_DK ** -0.5)
        k = p_qkv[:, GLA_DK_TOT:2 * GLA_DK_TOT]
        v_ref[rows, :] = p_qkv[:, 2 * GLA_DK_TOT:].astype(BF16)
        for d in range(2):
            cols = slice(d * GLA_DK_TOT, (d + 1) * GLA_DK_TOT)
            if d == 0:
                bq = pre[:, cols]
                bk = tot[:, cols] - bq
            else:
                bk = pre[:, cols] - la[:, cols]
                bq = tot[:, cols] - bk
            qd_ref[d, rows, :] = (q * jnp.exp2(bq)).astype(BF16)
            kd_ref[d, rows, :] = (k * jnp.exp2(-bq)).astype(BF16)
            kst_ref[d, ti] = jnp.transpose(k * jnp.exp2(bk)).astype(BF16)
            totals = [tot[n * c:n * c + 1, cols] for n in range(tile_chunks)]
            totals.append(jnp.zeros((LANES - tile_chunks, GLA_DK_TOT), F32))
            dect_ref[d, ti] = jnp.transpose(jnp.exp2(jnp.concatenate(totals, axis=0)))
        yield
        p_conv = _dot(h, win_ref[:, :OFF_Q])
        yield
        pos = row_i & (period - 1)
        u = p_conv[:, OFF_XC:OFF_XC + D_CONV] * p_conv[:, OFF_XV:OFF_XV + D_CONV]
        u_prev = jnp.where(pos == 0, 0.0, pltpu.roll(u, 1, 0))
        u_next = jnp.where(pos == period - 1, 0.0, pltpu.roll(u, ROW_TILE - 1, 0))
        conv = u_prev * wconv_ref[0] + u * wconv_ref[1] + u_next * wconv_ref[2] + bconv_ref[...]
        ya_ref[rows, :] = (p_conv[:, OFF_XB:OFF_XB + D_CONV] * conv).astype(BF16)
        yield

    yield from _for_row_tiles(seq_len, stage1, independent=True)

    for d in range(2):
        for pair in range(n_pairs):
            if has_state_in:
                zero = jnp.zeros((GLA_DK, GLA_DV), F32)
                top = jnp.concatenate([s0_refs[d][2 * pair], zero], axis=1)
                bot = jnp.concatenate([zero, s0_refs[d][2 * pair + 1]], axis=1)
                s_ref[d, pair] = jnp.concatenate([top, bot], axis=0)
            else:
                s_ref[d, pair] = jnp.zeros((pair_k, pair_v), F32)

    def scan_tile(i):
        upper_lane = lax.broadcasted_iota(I32, (1, LANES), 1) >= GLA_DK
        qi = lax.broadcasted_iota(I32, (LANES, 1), 0)
        kj = lax.broadcasted_iota(I32, (1, 2 * LANES), 1) & (LANES - 1)
        same_chunk = (qi & c) == (kj & c)
        causal = (same_chunk & (kj <= qi), same_chunk & (kj >= qi))
        for pair in range(n_pairs):
            kl = slice(pair * pair_k, (pair + 1) * pair_k)
            vl = slice(pair * pair_v, (pair + 1) * pair_v)
            for blk in range(ROW_TILE // LANES):
                rows = _tile_rows(i, blk * LANES, LANES)
                att = None
                for d in range(2):
                    kd = kd_ref[d, rows, kl]
                    zk = jnp.zeros_like(kd)
                    keys = jnp.concatenate([jnp.where(upper_lane, zk, kd), jnp.where(upper_lane, kd, zk)], axis=0)
                    a = jnp.where(causal[d], _dot_nt(qd_ref[d, rows, kl], keys), 0.0)
                    att = a if att is None else att + a
                v = v_ref[rows, vl]
                zv = jnp.zeros((LANES, GLA_DV), BF16)
                v_bd = jnp.concatenate([jnp.concatenate([v[:, :GLA_DV], zv], axis=1),
                                        jnp.concatenate([zv, v[:, GLA_DV:]], axis=1)], axis=0)
                o_ref[rows, vl] = _dot(att.astype(BF16), v_bd)
        yield
        key_row = lax.broadcasted_iota(I32, (pair_k, 1), 0)
        val_col = lax.broadcasted_iota(I32, (1, pair_v), 1)
        blockdiag = (key_row >= GLA_DK) == (val_col >= GLA_DV)
        for d in range(2):
            tile = i if d == 0 else n_tiles - 1 - i
            chunks = range(tile_chunks)
            for pair in range(n_pairs):
                kr = slice(pair * pair_k, (pair + 1) * pair_k)
                vl = slice(pair * pair_v, (pair + 1) * pair_v)
                s = s_ref[d, pair]
                for c4 in (chunks if d == 0 else reversed(chunks)):
                    blk, half = divmod(c4, 2)
                    kst = kst_ref[d, tile, kr, blk * LANES:(blk + 1) * LANES]
                    kst = jnp.where(upper_lane if half else ~upper_lane, kst, jnp.zeros_like(kst))
                    kv = jnp.where(blockdiag, _dot(kst, v_ref[_tile_rows(tile, blk * LANES, LANES), vl]), 0.0)
                    sst_ref[pair, tile * tile_chunks + c4, d * pair_k:(d + 1) * pair_k, :] = s.astype(BF16)
                    s = dect_ref[d, tile, kr, c4:c4 + 1] * s + kv
                s_ref[d, pair] = s
        yield

    yield from _for_row_tiles(seq_len, scan_tile, independent=False)

    if has_state_out:
        for d in range(2):
            for pair in range(n_pairs):
                s = s_ref[d, pair]
                sout_refs[d][2 * pair] = s[0:GLA_DK, 0:GLA_DV]
                sout_refs[d][2 * pair + 1] = s[GLA_DK:, GLA_DV:]

    def stage3(i):
        for pair in range(n_pairs):
            kl = slice(pair * pair_k, (pair + 1) * pair_k)
            vl = slice(pair * pair_v, (pair + 1) * pair_v)
            for c4 in range(tile_chunks):
                crow = _tile_rows(i, c4 * c, c)
                q2 = jnp.concatenate([qd_ref[0, crow, kl], qd_ref[1, crow, kl]], axis=1)
                o_ref[crow, vl] = o_ref[crow, vl] + _dot(q2, sst_ref[pair, i * tile_chunks + c4])
        yield
        rows = _tile_rows(i)
        heads = []
        for h in range(GLA_HEADS):
            hl = slice(h * GLA_DV, (h + 1) * GLA_DV)
            oh = o_ref[rows, hl]
            r = lax.rsqrt(jnp.mean(oh * oh, axis=-1, keepdims=True) + EPS)
            heads.append(oh * r * ggla_ref[:, hl])
        y_b = jnp.concatenate(heads, axis=1) * _silu(og_ref[rows, :])
        y = jnp.concatenate([ya_ref[rows, :], y_b.astype(BF16)], axis=1)
        x1_ref[rows, :] = x_ref[rows, :] + m[2] * _dot(y, wout_ref[...])
        yield

    yield from _for_row_tiles(seq_len, stage3, independent=True)


def _mixer_call(x, mod, states, weights, *, period, has_state_out, seqs_per_step):
    n_seq, seq_len, _ = x.shape
    sps = seqs_per_step
    has_state_in = states is not None
    stages_w_in = weights[1].dtype == F32
    kernel = functools.partial(_mixer_kernel, seqs_per_step=sps, seq_len=seq_len, period=period,
                               has_state_in=has_state_in, has_state_out=has_state_out, stages_w_in=stages_w_in)
    state_spec = pl.BlockSpec((sps, None, GLA_HEADS, GLA_DK, GLA_DV), lambda b: (b, 0, 0, 0, 0))
    const2 = lambda b: (0, 0)
    in_specs = [pl.BlockSpec((sps, seq_len, D_MODEL), lambda b: (b, 0, 0)),
                _mod_spec(mod, sps)]
    args = [x, mod]
    if has_state_in:
        in_specs += [state_spec, state_spec]
        args += list(states)
    weight_specs = [pl.BlockSpec(w.shape, lambda b, nd=w.ndim: (0,) * nd, pipeline_mode=pl.Buffered(1))
                    for w in weights]
    if stages_w_in:
        weight_specs[1] = pl.BlockSpec(memory_space=pl.ANY)
    in_specs += weight_specs
    args += list(weights)
    out_specs = [pl.BlockSpec((sps, seq_len, D_MODEL), lambda b: (b, 0, 0))]
    out_shape = [jax.ShapeDtypeStruct((n_seq, seq_len, D_MODEL), F32)]
    if has_state_out:
        out_specs += [state_spec, state_spec]
        out_shape += [jax.ShapeDtypeStruct((n_seq, 1, GLA_HEADS, GLA_DK, GLA_DV), F32)] * 2
    if stages_w_in:
        out_specs.append(pl.BlockSpec((D_MODEL, P_PAD), const2))
        out_shape.append(jax.ShapeDtypeStruct((D_MODEL, P_PAD), BF16))
    n_tiles = seq_len // ROW_TILE
    n_pairs = GLA_HEADS // 2
    per_seq = lambda shape, dtype: pltpu.VMEM((sps,) + shape, dtype)
    scratch = [per_seq((seq_len, GLA_DV_TOT), F32),
               per_seq((2, seq_len, GLA_DK_TOT), BF16),
               per_seq((2, seq_len, GLA_DK_TOT), BF16),
               per_seq((2, n_tiles, GLA_DK_TOT, ROW_TILE), BF16),
               per_seq((2, n_tiles, GLA_DK_TOT, LANES), F32),
               per_seq((seq_len, GLA_DV_TOT), BF16),
               per_seq((2, n_pairs, 2 * GLA_DK, 2 * GLA_DV), F32),
               per_seq((n_pairs, seq_len // GLA_CHUNK, 4 * GLA_DK, 2 * GLA_DV), BF16),
               per_seq((seq_len, GLA_DV_TOT), F32),
               per_seq((seq_len, D_CONV), BF16)]
    if stages_w_in:
        scratch += [pltpu.VMEM(weights[1].shape, F32),
                    pltpu.SemaphoreType.DMA((-(-weights[1].shape[0] // WIN_CHUNK),))]
    return pl.pallas_call(
        kernel,
        grid=(n_seq // sps,),
        in_specs=in_specs,
        out_specs=out_specs,
        out_shape=out_shape,
        scratch_shapes=scratch,
        compiler_params=_params(1),
        name="mixer",
    )(*args)


SUBLANES = 8


ROW_TOKENS = LANES // N_EXPERTS


def _route_kernel(x1_ref, mod_ref, g2_ref, wr2_ref, *refs, seqs_per_step, **static):
    programs = [_route_sequence(x1_ref.at[j], _mod_rows(mod_ref, j), g2_ref, wr2_ref, *[r.at[j] for r in refs],
                                **static)
                for j in range(seqs_per_step)]
    for _ in _staggered(programs):
        pass


def _route_sequence(x1_ref, m, g2_ref, wr2_ref, xs_ref, rank_ref, gate_ref,
                    pt_ref, h2_ref, spread_ref, bar_ref, cnt_ref, win_ref, *, seq_len, cap):
    n_rows = seq_len // ROW_TOKENS
    for t in range(seq_len // ROW_TILE):
        rows = slice(t * ROW_TILE, (t + 1) * ROW_TILE)
        h2 = _modulated_norm(x1_ref[rows, :], g2_ref[...], m[4], m[3])
        hi = h2.astype(BF16)
        h2_ref[rows, :] = hi
        yield
        by_hi = _dot_nt(wr2_ref[...], hi)
        pt_ref[:, rows] = by_hi[:N_EXPERTS] + by_hi[N_EXPERTS:]
    yield

    logits = pt_ref[...]
    ex = jnp.exp(logits - jnp.max(logits, axis=0, keepdims=True))
    probs = ex / jnp.sum(ex, axis=0, keepdims=True)
    pt_ref[...] = probs
    pad = jnp.zeros((LANES - N_EXPERTS, seq_len), F32)
    p_tok = jnp.transpose(jnp.concatenate([probs, pad], axis=0))

    spread = p_tok
    sh = N_EXPERTS
    while sh < LANES:
        spread = spread + pltpu.roll(spread, sh, 1)
        sh *= 2
    spread_ref[...] = spread
    sub_j = lax.broadcasted_iota(I32, (SUBLANES, 1), 0)
    lane_g = lax.broadcasted_iota(I32, (1, LANES), 1) >> (N_EXPERTS.bit_length() - 1)
    lane_j = (ROW_TOKENS - lane_g) & (ROW_TOKENS - 1)
    own_group = jnp.where(lane_j == sub_j, spread.reshape(n_rows, SUBLANES, LANES), 0.0)
    bar = jnp.sum(own_group, axis=1, keepdims=True)
    bar_ref[...] = jnp.broadcast_to(bar, (n_rows, SUBLANES, LANES))
    cnt_ref[...] = jnp.zeros((n_rows, SUBLANES, LANES), F32)

    rows_per_block = LANES // ROW_TOKENS

    def count_block(g_s, g_t, relation):
        s_blk = spread_ref[g_s * LANES:(g_s + 1) * LANES, :]
        wins = [jnp.zeros((SUBLANES, LANES), F32)] * rows_per_block
        for tl in range(rows_per_block):
            r = g_t * rows_per_block + tl
            bar_r = bar_ref[r]
            acc = cnt_ref[r]
            for sl in range(rows_per_block):
                s_vreg = s_blk[sl * SUBLANES:(sl + 1) * SUBLANES]
                if relation == "before" or sl < tl:
                    won = jnp.where(s_vreg >= bar_r, 1.0, 0.0)
                    acc = acc + won
                    wins[sl] = wins[sl] + won
                elif sl == tl:
                    acc = acc + jnp.where(s_vreg > bar_r, 1.0, 0.0) \
                        + jnp.where((s_vreg == bar_r) & (sub_j < lane_j), 1.0, 0.0)
            cnt_ref[r] = acc
        w_rows = slice(g_s * rows_per_block, (g_s + 1) * rows_per_block)
        win_ref[w_rows] = win_ref[w_rows] + jnp.stack(wins)

    n_grp = seq_len // LANES
    win_ref[...] = jnp.zeros((n_rows, SUBLANES, LANES), F32)
    yield
    for g_t in range(n_grp):
        for g_s in range(g_t + 1):
            count_block(g_s, g_t, "before" if g_s < g_t else "same")
            yield

    counts = jnp.sum(cnt_ref[...], axis=1, keepdims=True)
    counts = jnp.broadcast_to(counts, (n_rows, SUBLANES, LANES)).reshape(seq_len, LANES)
    rank_tok = pltpu.roll(counts, 0, 1, stride=N_EXPERTS, stride_axis=0)
    wins = win_ref[...].reshape(seq_len, LANES)
    sh = N_EXPERTS
    while sh < LANES:
        wins = wins + pltpu.roll(wins, sh, 1)
        sh *= 2
    tok = lax.broadcasted_iota(I32, (seq_len, 1), 0)
    later = ((n_rows - 1 - (tok >> (ROW_TOKENS.bit_length() - 1))) * ROW_TOKENS).astype(F32)
    rank_tok = rank_tok + (later - wins)
    rank_ref[...] = jnp.transpose(rank_tok)[:N_EXPERTS].astype(I32)
    yield

    group = DISPATCH_ROWS // cap
    slot = lax.broadcasted_iota(I32, (cap, 1), 0)
    half = D_MODEL // 2
    for gi in range(N_EXPERTS // group):
        picks = []
        for e in range(gi * group, (gi + 1) * group):
            oh = rank_ref[e:e + 1, :] == slot
            gate = jnp.sum(jnp.where(oh, pt_ref[e:e + 1, :], 0.0), axis=1, keepdims=True)
            gate_ref[e * cap:(e + 1) * cap, :] = jnp.broadcast_to(gate, (cap, LANES))
            picks.append(oh)
        ohb = jnp.where(jnp.concatenate(picks, axis=0), 1.0, 0.0).astype(BF16)
        yield
        out_rows = slice(gi * DISPATCH_ROWS, (gi + 1) * DISPATCH_ROWS)
        xs_ref[out_rows, :half] = _dot(ohb, h2_ref[:, :half]).astype(BF16)
        xs_ref[out_rows, half:] = _dot(ohb, h2_ref[:, half:]).astype(BF16)
        yield


def _route_call(x1, mod, g2, wr_both, *, seqs_per_step):
    n_seq, seq_len, _ = x1.shape
    sps = seqs_per_step
    cap = EC_CAPACITY_FACTOR * seq_len // N_EXPERTS
    kernel = functools.partial(_route_kernel, seqs_per_step=sps, seq_len=seq_len, cap=cap)
    const2 = lambda b: (0, 0)
    per_seq = lambda shape, dtype: pltpu.VMEM((sps,) + shape, dtype)
    rank_rows = (seq_len // ROW_TOKENS, SUBLANES, LANES)
    return pl.pallas_call(
        kernel,
        grid=(n_seq // sps,),
        in_specs=[pl.BlockSpec((sps, seq_len, D_MODEL), lambda b: (b, 0, 0)),
                  _mod_spec(mod, sps),
                  pl.BlockSpec((1, D_MODEL), const2),
                  pl.BlockSpec((2 * N_EXPERTS, D_MODEL), const2)],
        out_specs=[pl.BlockSpec((sps, N_EXPERTS * cap, D_MODEL), lambda b: (b, 0, 0)),
                   pl.BlockSpec((sps, N_EXPERTS, seq_len), lambda b: (b, 0, 0)),
                   pl.BlockSpec((sps, N_EXPERTS * cap, LANES), lambda b: (b, 0, 0))],
        out_shape=[jax.ShapeDtypeStruct((n_seq, N_EXPERTS * cap, D_MODEL), BF16),
                   jax.ShapeDtypeStruct((n_seq, N_EXPERTS, seq_len), I32),
                   jax.ShapeDtypeStruct((n_seq, N_EXPERTS * cap, LANES), F32)],
        scratch_shapes=[per_seq((N_EXPERTS, seq_len), F32),
                        per_seq((seq_len, D_MODEL), BF16),
                        per_seq((seq_len, LANES), F32),
                        per_seq(rank_rows, F32),
                        per_seq(rank_rows, F32),
                        per_seq(rank_rows, F32)],
        compiler_params=_params(1),
        name="route",
    )(x1, mod, g2, wr_both)


def _experts_kernel(xc_ref, xl_ref, gc_ref, gl_ref, wg_ref, wu_ref, wd_ref, yc_ref, yl_ref,
                    wgb_ref, wub_ref, wdb_ref, a_ref):
    wgb_ref[...] = wg_ref[...].astype(BF16)
    wub_ref[...] = wu_ref[...].astype(BF16)
    wdb_ref[...] = wd_ref[...].astype(BF16)
    f_tile = 2 * LANES

    def run(x_ref, g_ref, y_ref):
        n_seq, cap, _ = x_ref.shape
        seqs = DISPATCH_ROWS // cap
        for s0 in range(0, n_seq, seqs):
            x = x_ref[s0:s0 + seqs].reshape(DISPATCH_ROWS, D_MODEL)
            for f0 in range(0, D_EXPERT, f_tile):
                cols = slice(f0, f0 + f_tile)
                a_ref[:, cols] = (_silu(_dot(x, wgb_ref[:, cols])) * _dot(x, wub_ref[:, cols])).astype(BF16)
            gate = g_ref[s0:s0 + seqs].reshape(DISPATCH_ROWS, LANES)
            y = _dot(a_ref[...], wdb_ref[...]) * jnp.concatenate([gate] * (D_MODEL // LANES), axis=1)
            y_ref[s0:s0 + seqs] = y.astype(BF16).reshape(seqs, cap, D_MODEL)

    run(xc_ref, gc_ref, yc_ref)
    run(xl_ref, gl_ref, yl_ref)


def _experts_call(xs_ctx, xs_lat, gates_ctx, gates_lat, w_gate, w_up, w_down):
    def slot_spec(a):
        n_seq, _, cap, width = a.shape
        return pl.BlockSpec((n_seq, None, cap, width), lambda e: (0, e, 0, 0))

    w_spec = pl.BlockSpec((None, D_MODEL, D_EXPERT), lambda e: (e, 0, 0))
    return pl.pallas_call(
        _experts_kernel,
        grid=(N_EXPERTS,),
        in_specs=[slot_spec(xs_ctx), slot_spec(xs_lat), slot_spec(gates_ctx), slot_spec(gates_lat),
                  w_spec, w_spec, pl.BlockSpec((None, D_EXPERT, D_MODEL), lambda e: (e, 0, 0))],
        out_specs=[slot_spec(xs_ctx), slot_spec(xs_lat)],
        out_shape=[jax.ShapeDtypeStruct(xs_ctx.shape, BF16), jax.ShapeDtypeStruct(xs_lat.shape, BF16)],
        scratch_shapes=[pltpu.VMEM((D_MODEL, D_EXPERT), BF16),
                        pltpu.VMEM((D_MODEL, D_EXPERT), BF16),
                        pltpu.VMEM((D_EXPERT, D_MODEL), BF16),
                        pltpu.VMEM((DISPATCH_ROWS, D_EXPERT), BF16)],
        compiler_params=_params(1),
        name="experts",
    )(xs_ctx, xs_lat, gates_ctx, gates_lat, w_gate, w_up, w_down)


def _combine_tokens(x1_ref, y_ref, rank_ref, gate2, gf_ref, o_ref, oh_ref):
    n_tok = x1_ref.shape[0]
    cap = y_ref.shape[0] // N_EXPERTS
    slot = lax.broadcasted_iota(I32, (cap, 1), 0)
    for e in range(N_EXPERTS):
        oh_ref[e * cap:(e + 1) * cap, :] = jnp.where(rank_ref[e:e + 1, :] == slot, 1.0, 0.0).astype(BF16)
    for t in range(n_tok // ROW_TILE):
        rows = slice(t * ROW_TILE, (t + 1) * ROW_TILE)
        moe = _dot_tn(oh_ref[:, rows], y_ref[...])
        x2 = x1_ref[rows, :] + gate2 * moe
        r = lax.rsqrt(jnp.mean(x2 * x2, axis=-1, keepdims=True) + EPS)
        o_ref[rows, :] = (x2 * r) * gf_ref[...]


def _combine_kernel(x1c_ref, yc_ref, rkc_ref, modc_ref, x1l_ref, yl_ref, rkl_ref, modl_ref, gf_ref,
                    oc_ref, ol_ref, ohc_ref, ohl_ref):
    for j in range(x1c_ref.shape[0]):
        _combine_tokens(x1c_ref.at[j], yc_ref.at[j], rkc_ref.at[j], _mod_rows(modc_ref, j)[5], gf_ref,
                        oc_ref.at[j], ohc_ref.at[j])
    _combine_tokens(x1l_ref.at[0], yl_ref.at[0], rkl_ref.at[0], _mod_rows(modl_ref, 0)[5], gf_ref,
                    ol_ref.at[0], ohl_ref)


def _combine_call(ctx, lat, g_final, *, ctx_seqs_per_step):
    x1c, yc, rkc, modc = ctx
    x1l, yl, rkl, modl = lat
    n_ctx, ctx_len, _ = x1c.shape
    n_lat, lat_len, _ = x1l.shape
    sps = ctx_seqs_per_step
    n_steps = n_ctx // sps
    slabs = n_steps // n_lat
    slab = lat_len // slabs
    assert n_steps == n_lat * slabs and slab % ROW_TILE == 0 and modl.shape[0] == n_lat
    ctx_blk = lambda shape: pl.BlockSpec((sps,) + shape, lambda b: (b, 0, 0))
    lat_seq = lambda b: (b // slabs, 0, 0)
    return pl.pallas_call(
        _combine_kernel,
        grid=(n_steps,),
        in_specs=[ctx_blk((ctx_len, D_MODEL)), ctx_blk(yc.shape[1:]), ctx_blk((N_EXPERTS, ctx_len)),
                  _mod_spec(modc, sps),
                  pl.BlockSpec((1, slab, D_MODEL), lambda b: (b // slabs, b % slabs, 0)),
                  pl.BlockSpec((1,) + yl.shape[1:], lat_seq),
                  pl.BlockSpec((1, N_EXPERTS, slab), lambda b: (b // slabs, 0, b % slabs)),
                  pl.BlockSpec((1,) + modl.shape[1:], lat_seq),
                  pl.BlockSpec((1, D_MODEL), lambda b: (0, 0))],
        out_specs=[ctx_blk((ctx_len, D_MODEL)),
                   pl.BlockSpec((1, slab, D_MODEL), lambda b: (b // slabs, b % slabs, 0))],
        out_shape=[jax.ShapeDtypeStruct(x1c.shape, F32), jax.ShapeDtypeStruct(x1l.shape, F32)],
        scratch_shapes=[pltpu.VMEM((sps, yc.shape[1], ctx_len), BF16),
                        pltpu.VMEM((yl.shape[1], slab), BF16)],
        compiler_params=_params(1),
        name="combine",
    )(x1c, yc, rkc, modc, x1l, yl, rkl, modl, g_final)


def kernel(x_prompt, x_sample, state_gla_fwd, state_gla_bwd, c, c_ctx, w_mod, b_mod, g_norm1, g_norm2,
           w_in, w_conv, b_conv, w_a_up_f, b_a_f, w_a_up_b, b_a_b, g_gla_norm, w_out, w_router,
           w_gate, w_up, w_down, g_final):
    assert w_mod.shape[0] == 1, "single trunk layer"
    n_ctx, ctx_len, _ = x_prompt.shape
    n_lat, lat_len, _ = x_sample.shape
    ctx_cap = EC_CAPACITY_FACTOR * ctx_len // N_EXPERTS
    lat_cap = EC_CAPACITY_FACTOR * lat_len // N_EXPERTS

    mod_ctx, mod_lat = _mod_call(c_ctx[None, :], c, w_mod[0], b_mod)

    mixer_weights = [g_norm1, jnp.transpose(w_in[0]), jnp.transpose(w_conv, (1, 0, 2)), b_conv,
                     w_a_up_f, b_a_f, w_a_up_b, b_a_b,
                     g_gla_norm[0].reshape(1, GLA_DV_TOT), w_out[0].astype(BF16)]
    assert len(mixer_weights) == N_MIXER_WEIGHTS
    wr_t = jnp.transpose(jnp.concatenate([w_router[0], w_router[0]], axis=1))
    wr_hi = wr_t.astype(BF16)
    wr_lo = (wr_t - wr_hi.astype(F32)).astype(BF16)
    wr_both = jnp.where(np.arange(2 * N_EXPERTS)[:, None] < N_EXPERTS, wr_hi, wr_lo)

    x1_ctx, new_f, new_b, w_in_bf = _mixer_call(x_prompt, mod_ctx, None, mixer_weights,
                                                period=ctx_len, has_state_out=True, seqs_per_step=4)
    mixer_weights[1] = w_in_bf
    (x1_lat,) = _mixer_call(x_sample, mod_lat, (state_gla_fwd, state_gla_bwd), mixer_weights,
                            period=GRID_W, has_state_out=False, seqs_per_step=1)

    xs_ctx, rank_ctx, gates_ctx = _route_call(x1_ctx, mod_ctx, g_norm2, wr_both, seqs_per_step=4)
    xs_lat, rank_lat, gates_lat = _route_call(x1_lat, mod_lat, g_norm2, wr_both, seqs_per_step=1)

    per_expert = lambda a, n, cap: a.reshape(n, N_EXPERTS, cap, a.shape[-1])
    y_ctx, y_lat = _experts_call(per_expert(xs_ctx, n_ctx, ctx_cap), per_expert(xs_lat, n_lat, lat_cap),
                                 per_expert(gates_ctx, n_ctx, ctx_cap), per_expert(gates_lat, n_lat, lat_cap),
                                 w_gate[0], w_up[0], w_down[0])

    g_fin = g_final[None, :]
    y_prompt, y_sample = _combine_call((x1_ctx, y_ctx.reshape(xs_ctx.shape), rank_ctx, mod_ctx),
                                       (x1_lat, y_lat.reshape(xs_lat.shape), rank_lat, mod_lat),
                                       g_fin, ctx_seqs_per_step=4)
    return y_prompt, y_sample, new_f, new_b
```
